```python
import math
import jax
import jax.numpy as jnp
from jax import lax
import numpy as np

D_MODEL = 4096
BATCH = 1
SEQ = 8192
DEPTH = 4

GRID_W = 64
CTX_LEN = 256
EPS = 1e-6
SHORT_CONV = 3
CHUNK = 64

GDN_HEADS = 8
GDN_DK = 128
GDN_DV = 128
GDN_QK = GDN_HEADS * GDN_DK
GDN_VW = GDN_HEADS * GDN_DV
GDN_CONV_CH = 2 * GDN_QK + GDN_VW
GDN_IN = GDN_CONV_CH + GDN_VW + 4 * GDN_HEADS

GLA_HEADS = 4
GLA_DK = 128
GLA_DV = 256
GLA_QK = GLA_HEADS * GLA_DK
GLA_VW = GLA_HEADS * GLA_DV
GLA_GATE_RANK = 16
GLA_GATE_NORM = 16.0
GLA_IN = 2 * GLA_QK + 2 * GLA_VW + 2 * GLA_GATE_RANK

HY_WIDTH = 1024
HY_ORDER = 2
HY_EMB = 33
HY_FILTER_HIDDEN = 64
HY_FILTER_SCALE = 0.03
HY_FAST_DECAY = 0.3
HY_SLOW_DECAY = 1.5
HY_TARGET = 1e-2
HY_IN = (HY_ORDER + 1) * HY_WIDTH

N_BRANCH = 3
BRANCH_WIDTH = 1024
MERGE_RANK = 256
REC_IN = GDN_IN + GLA_IN
D_IN = REC_IN + HY_IN + MERGE_RANK

N_EXPERTS = 16
N_GROUPS = 4
TOP_K = 2
D_EXPERT = 256

kernel_name = 'hybrid_gdn_gla_hyena_moe_prefix_dit'


def rmsnorm(x, gain):
    xf = x.astype(jnp.float32)
    y = xf * lax.rsqrt(jnp.mean(xf * xf, axis=-1, keepdims=True) + EPS)
    return (y * gain.astype(jnp.float32)).astype(x.dtype)


def modulate(h, shift, scale):
    return h * (1.0 + scale) + shift


def l2norm(x):
    xf = x.astype(jnp.float32)
    return xf * lax.rsqrt(jnp.sum(xf * xf, axis=-1, keepdims=True) + EPS)


def short_conv(x, w):
    width = w.shape[0]
    pad = width // 2
    n = x.shape[1]
    xp = jnp.pad(x, ((0, 0), (pad, pad), (0, 0)))
    y = xp[:, 0:n] * w[0]
    for j in range(1, width):
        y = y + xp[:, j:j + n] * w[j]
    return y


def split_heads(x, n_heads):
    b, n, f = x.shape
    return x.reshape(b, n, n_heads, f // n_heads).transpose(0, 2, 1, 3)


def to_chunks(a):
    return a.reshape(a.shape[:2] + (a.shape[2] // CHUNK, CHUNK) + a.shape[3:])


def from_chunks(a):
    return a.reshape(a.shape[:2] + (a.shape[2] * a.shape[3],) + a.shape[4:])


def raster_to_columns(z, rows):
    b, n, f = z.shape
    return z.reshape(b, rows, GRID_W, f).transpose(0, 2, 1, 3).reshape(b, n, f)


def columns_to_raster(z, rows):
    b, n, f = z.shape
    return z.reshape(b, GRID_W, rows, f).transpose(0, 2, 1, 3).reshape(b, n, f)


def gated_delta_rule(q, k, v, g, beta, s0):
    f32 = jnp.float32
    dv = v.shape[-1]
    q = to_chunks(q.astype(f32)) * (q.shape[-1] ** -0.5)
    k = to_chunks(k.astype(f32))
    v = to_chunks(v.astype(f32))
    beta = to_chunks(beta.astype(f32))[..., None]
    gc = jnp.cumsum(to_chunks(g.astype(f32)), axis=-1)
    lower = jnp.tril(jnp.ones((CHUNK, CHUNK), bool))
    strict = jnp.tril(jnp.ones((CHUNK, CHUNK), f32), -1)
    gamma = jnp.exp(jnp.where(lower, gc[..., :, None] - gc[..., None, :], -jnp.inf))
    kk = jnp.einsum('bhncd,bhnsd->bhncs', k * beta, k) * gamma
    a_mat = jnp.eye(CHUNK, dtype=f32) + kk * strict
    rhs = jnp.concatenate([v * beta, k * beta * jnp.exp(gc)[..., None]], axis=-1)
    sol = lax.linalg.triangular_solve(a_mat, rhs, left_side=True, lower=True, unit_diagonal=True)
    u, w = sol[..., :dv], sol[..., dv:]
    aqk = jnp.einsum('bhncd,bhnsd->bhncs', q, k) * gamma
    qg = q * jnp.exp(gc)[..., None]
    kg = k * jnp.exp(gc[..., -1:] - gc)[..., None]
    glast = jnp.exp(gc[..., -1])[..., None, None]

    def step(s, xs):
        u_n, w_n, aqk_n, qg_n, kg_n, gl_n = xs
        v_new = u_n - jnp.einsum('bhcd,bhde->bhce', w_n, s)
        o_n = jnp.einsum('bhcd,bhde->bhce', qg_n, s) + jnp.einsum('bhcs,bhse->bhce', aqk_n, v_new)
        s = gl_n * s + jnp.einsum('bhcd,bhce->bhde', kg_n, v_new)
        return s, o_n

    xs = tuple(jnp.moveaxis(t, 2, 0) for t in (u, w, aqk, qg, kg, glast))
    s_fin, o = lax.scan(step, s0.astype(f32), xs)
    return from_chunks(jnp.moveaxis(o, 0, 2)), s_fin


def gla_recurrence(q, k, v, g, s0):
    f32 = jnp.float32
    q = to_chunks(q.astype(f32)) * (q.shape[-1] ** -0.5)
    k = to_chunks(k.astype(f32))
    v = to_chunks(v.astype(f32))
    b = jnp.cumsum(to_chunks(g.astype(f32)), axis=3)
    b_last = b[..., -1:, :]
    qe = q * jnp.exp(b)
    lower = jnp.tril(jnp.ones((CHUNK, CHUNK), bool))
    a_intra = jnp.where(lower, jnp.einsum('bhncd,bhnsd->bhncs', qe, k * jnp.exp(-b)), 0.0)
    o_intra = jnp.einsum('bhncs,bhnse->bhnce', a_intra, v)
    ds = jnp.einsum('bhncd,bhnce->bhnde', k * jnp.exp(b_last - b), v)
    decay = jnp.exp(b_last[..., 0, :])[..., None]

    def step(s, xs):
        dec_n, ds_n = xs
        return dec_n * s + ds_n, s

    s_fin, s_prev = lax.scan(step, s0.astype(f32), (jnp.moveaxis(decay, 2, 0), jnp.moveaxis(ds, 2, 0)))
    o = o_intra + jnp.einsum('bhncd,nbhde->bhnce', qe, s_prev)
    return from_chunks(o), s_fin


def two_way_prefix_scan(rule, ctx_fwd, ctx_bwd, lat_fwd, lat_bwd, s0):
    flip = lambda args: tuple(jnp.flip(a, axis=2) for a in args)
    o_cf, s_cf = rule(*ctx_fwd, s0)
    o_lf, _ = rule(*lat_fwd, s_cf)
    o_cb, s_cb = rule(*flip(ctx_bwd), s0)
    o_lb, _ = rule(*flip(lat_bwd), s_cb)
    return o_cf + jnp.flip(o_cb, axis=2), o_lf + jnp.flip(o_lb, axis=2)


def gated_head_norm(o, gate, norm_w):
    b, h, n, d = o.shape
    o = rmsnorm(o.transpose(0, 2, 1, 3), norm_w)
    g = jax.nn.silu(gate.astype(jnp.float32)).reshape(b, n, h, d)
    return (o * g).reshape(b, n, h * d).astype(gate.dtype)


def gdn_prepare(z, conv_w, a_log, dt_bias):
    f32 = jnp.float32
    b, n, _ = z.shape
    qkv, gate, beta_logit, a_logit = jnp.split(
        z, [GDN_CONV_CH, GDN_CONV_CH + GDN_VW, GDN_CONV_CH + GDN_VW + 2 * GDN_HEADS], axis=-1)
    qkv = jax.nn.silu(short_conv(qkv, conv_w))
    q, k, v = jnp.split(qkv, [GDN_QK, 2 * GDN_QK], axis=-1)
    q = l2norm(split_heads(q, GDN_HEADS))
    k = l2norm(split_heads(k, GDN_HEADS))
    v = split_heads(v, GDN_HEADS)
    beta = jax.nn.sigmoid(beta_logit.astype(f32)).reshape(b, n, 2, GDN_HEADS).transpose(2, 0, 3, 1)
    a = a_logit.astype(f32).reshape(b, n, 2, GDN_HEADS)
    g = (-jnp.exp(a_log.astype(f32)) * jax.nn.softplus(a + dt_bias.astype(f32))).transpose(2, 0, 3, 1)
    return q, k, v, g, beta, gate


def gla_prepare(z, gate_up, gate_b):
    f32 = jnp.float32
    b, n, _ = z.shape
    q, k, v, gate, lr = jnp.split(
        z, [GLA_QK, 2 * GLA_QK, 2 * GLA_QK + GLA_VW, 2 * GLA_QK + 2 * GLA_VW], axis=-1)
    lr = lr.astype(f32).reshape(b, n, 2, GLA_GATE_RANK)
    logit = jnp.einsum('blzr,zrk->zblk', lr, gate_up.astype(f32)) + gate_b.astype(f32)[:, None, None, :]
    g = jax.nn.log_sigmoid(logit) / GLA_GATE_NORM
    return (split_heads(q, GLA_HEADS), split_heads(k, GLA_HEADS), split_heads(v, GLA_HEADS),
            split_heads(g[0], GLA_HEADS), split_heads(g[1], GLA_HEADS), gate)


def hyena_filters(n, w1, b1, w2, b2, w3, freq):
    f32 = jnp.float32
    w1, b1, w2, b2, w3, freq = (a.astype(f32) for a in (w1, b1, w2, b2, w3, freq))
    t = jnp.linspace(0.0, 1.0, n, dtype=f32)[:, None]
    bands = (HY_EMB - 1) // 2
    omega = (2.0 * math.pi / n) * jnp.arange(n, dtype=f32)[:, None]
    f = jnp.linspace(1e-4, bands - 1, bands, dtype=f32)[None, :]
    feats = jnp.concatenate([t, jnp.cos(f * omega), -jnp.sin(f * omega)], axis=-1)
    hid = jnp.sin(freq[0] * (feats @ w1 + b1))
    hid = jnp.sin(freq[1] * (hid @ w2 + b2))
    h = hid @ w3
    rates = jnp.abs(jnp.linspace(math.log(HY_TARGET) / HY_FAST_DECAY,
                                 math.log(HY_TARGET) / HY_SLOW_DECAY, HY_WIDTH, dtype=f32))
    window = jnp.exp(-t * rates)
    return h * jnp.tile(window, (1, 2 * HY_ORDER))


def long_conv_bidir(u, h_fwd, h_bwd):
    n = u.shape[1]
    taps = jnp.concatenate([h_fwd, jnp.zeros_like(h_fwd[:1]), jnp.flip(h_bwd[1:], axis=0)], axis=0)
    uf = jnp.fft.rfft(u.astype(jnp.float32), n=2 * n, axis=1)
    tf = jnp.fft.rfft(taps.astype(jnp.float32), n=2 * n, axis=0)
    y = jnp.fft.irfft(uf * tf[None], n=2 * n, axis=1)[:, :n]
    return y.astype(u.dtype)


def hyena_sequence(z, conv_w, conv_b, filt, skip):
    n = z.shape[1]
    z = short_conv(z, conv_w) + conv_b
    v, x1, x2 = jnp.split(z, HY_ORDER + 1, axis=-1)
    h = filt.reshape(n, HY_ORDER, 2, HY_WIDTH)
    y = v
    for i, gate in enumerate((x1, x2)):
        y = gate * (long_conv_bidir(y, h[:, i, 0], h[:, i, 1]) + skip[i] * y)
    return y


def merge_branches(outs, z_gate, merge_up, merge_b, w_branch, w_out):
    o = jnp.stack(outs, axis=0)
    y = jnp.einsum('nblw,nwd->nbld', o, w_branch)
    gate = jax.nn.sigmoid(jnp.einsum('blr,nrd->nbld', z_gate, merge_up) + merge_b[:, None, None, :])
    return jnp.sum(gate * y, axis=0) @ w_out


def token_mixer(h_lat, h_ctx, rows, with_ctx, w_in, gdn_conv, gdn_a_log, gdn_dt_bias, gdn_norm,
                gla_gate_up, gla_gate_b, gla_norm, hy_conv_w, hy_conv_b, hy_w1, hy_b1, hy_w2, hy_b2,
                hy_w3, hy_freq, hy_skip, merge_up, merge_b, w_branch, w_out):
    batch = h_lat.shape[0]
    z_lat = h_lat @ w_in
    z_ctx = h_ctx @ (w_in if with_ctx else w_in[:, :REC_IN])
    qc, kc, vc, gc, bc, gate_c = gdn_prepare(z_ctx[..., :GDN_IN], gdn_conv, gdn_a_log, gdn_dt_bias)
    ql, kl, vl, gl, bl, gate_l = gdn_prepare(z_lat[..., :GDN_IN], gdn_conv, gdn_a_log, gdn_dt_bias)
    s0 = jnp.zeros((batch, GDN_HEADS, GDN_DK, GDN_DV), jnp.float32)
    a_ctx, a_lat = two_way_prefix_scan(
        gated_delta_rule, (qc, kc, vc, gc[0], bc[0]), (qc, kc, vc, gc[1], bc[1]),
        (ql, kl, vl, gl[0], bl[0]), (ql, kl, vl, gl[1], bl[1]), s0)
    a_lat = gated_head_norm(a_lat, gate_l, gdn_norm)
    q2c, k2c, v2c, gfc, gbc, gate2_c = gla_prepare(z_ctx[..., GDN_IN:REC_IN], gla_gate_up, gla_gate_b)
    q2l, k2l, v2l, gfl, gbl, gate2_l = gla_prepare(
        raster_to_columns(z_lat[..., GDN_IN:REC_IN], rows), gla_gate_up, gla_gate_b)
    s0 = jnp.zeros((batch, GLA_HEADS, GLA_DK, GLA_DV), jnp.float32)
    b_ctx, b_lat = two_way_prefix_scan(
        gla_recurrence, (q2c, k2c, v2c, gfc), (q2c, k2c, v2c, gbc),
        (q2l, k2l, v2l, gfl), (q2l, k2l, v2l, gbl), s0)
    b_lat = columns_to_raster(gated_head_norm(b_lat, gate2_l, gla_norm), rows)
    filt_args = (hy_w1, hy_b1, hy_w2, hy_b2, hy_w3, hy_freq)
    c_lat = hyena_sequence(z_lat[..., REC_IN:REC_IN + HY_IN], hy_conv_w, hy_conv_b,
                           hyena_filters(z_lat.shape[1], *filt_args), hy_skip)
    out_lat = merge_branches((a_lat, b_lat, c_lat), z_lat[..., REC_IN + HY_IN:],
                             merge_up, merge_b, w_branch, w_out)
    if not with_ctx:
        return out_lat, None
    a_ctx = gated_head_norm(a_ctx, gate_c, gdn_norm)
    b_ctx = gated_head_norm(b_ctx, gate2_c, gla_norm)
    c_ctx_out = hyena_sequence(z_ctx[..., REC_IN:REC_IN + HY_IN], hy_conv_w, hy_conv_b,
                               hyena_filters(z_ctx.shape[1], *filt_args), hy_skip)
    out_ctx = merge_branches((a_ctx, b_ctx, c_ctx_out), z_ctx[..., REC_IN + HY_IN:],
                             merge_up, merge_b, w_branch, w_out)
    return out_lat, out_ctx


def moe(h, w_router, router_bias, w1, w3, w2):
    f32 = jnp.float32
    scores = jax.nn.sigmoid((h @ w_router).astype(f32))
    sel = scores + router_bias.astype(f32)
    per_group = N_EXPERTS // N_GROUPS
    grp = sel.reshape(sel.shape[:-1] + (N_GROUPS, per_group))
    grp_score = jnp.sum(lax.top_k(grp, TOP_K)[0], axis=-1)
    best = jnp.argmax(grp_score, axis=-1)
    in_group = (jnp.arange(N_EXPERTS) // per_group) == best[..., None]
    _, idx = lax.top_k(jnp.where(in_group, sel, -jnp.inf), TOP_K)
    wts = jnp.take_along_axis(scores, idx, axis=-1)
    wts = wts / jnp.sum(wts, axis=-1, keepdims=True)
    gate = jnp.sum(jax.nn.one_hot(idx, N_EXPERTS, dtype=f32) * wts[..., None], axis=-2)
    act = jax.nn.silu(jnp.einsum('bld,edf->blef', h, w1)) * jnp.einsum('bld,edf->blef', h, w3)
    return jnp.einsum('blef,efd->bld', act * gate.astype(h.dtype)[..., None], w2)


def setup_inputs(seed: int = 0) -> dict:
    key = jax.random.key(seed)
    ks = iter(jax.random.split(key, 48))
    f32 = jnp.float32
    nrm = lambda shape, scale: jax.random.normal(next(ks), shape, f32) * scale
    D = D_MODEL
    dt = jnp.exp(jax.random.uniform(next(ks), (DEPTH, 2, GDN_HEADS), f32,
                                    minval=math.log(1e-3), maxval=math.log(1e-1)))
    a_log = jnp.log(jax.random.uniform(next(ks), (DEPTH, 2, GDN_HEADS), f32, minval=1.0, maxval=16.0))
    return {
        'x': nrm((BATCH, SEQ, D), 1.0),
        'c': nrm((BATCH, D), 1.0),
        'ctx': nrm((BATCH, CTX_LEN, D), 1.0),
        'c_ctx': nrm((D,), 1.0),
        'norm1_g': 1.0 + nrm((DEPTH, D), 0.1),
        'norm2_g': 1.0 + nrm((DEPTH, D), 0.1),
        'w_mod': nrm((DEPTH, D, 6 * D), D ** -0.5),
        'b_mod': nrm((DEPTH, 6 * D), 0.02),
        'w_in': nrm((DEPTH, D, D_IN), D ** -0.5),
        'gdn_conv': nrm((DEPTH, SHORT_CONV, GDN_CONV_CH), SHORT_CONV ** -0.5),
        'gdn_a_log': a_log,
        'gdn_dt_bias': dt + jnp.log(-jnp.expm1(-dt)),
        'gdn_norm': 1.0 + nrm((DEPTH, GDN_DV), 0.1),
        'gla_gate_up': nrm((DEPTH, 2, GLA_GATE_RANK, GLA_QK), GLA_GATE_RANK ** -0.5),
        'gla_gate_b': nrm((DEPTH, 2, GLA_QK), 0.1),
        'gla_norm': 1.0 + nrm((DEPTH, GLA_DV), 0.1),
        'hy_conv_w': nrm((DEPTH, SHORT_CONV, HY_IN), SHORT_CONV ** -0.5),
        'hy_conv_b': nrm((DEPTH, HY_IN), 0.02),
        'hy_w1': nrm((DEPTH, HY_EMB, HY_FILTER_HIDDEN), HY_EMB ** -0.5),
        'hy_b1': nrm((DEPTH, HY_FILTER_HIDDEN), 0.1),
        'hy_w2': nrm((DEPTH, HY_FILTER_HIDDEN, HY_FILTER_HIDDEN), HY_FILTER_HIDDEN ** -0.5),
        'hy_b2': nrm((DEPTH, HY_FILTER_HIDDEN), 0.1),
        'hy_w3': nrm((DEPTH, HY_FILTER_HIDDEN, 2 * HY_ORDER * HY_WIDTH), HY_FILTER_SCALE * HY_FILTER_HIDDEN ** -0.5),
        'hy_freq': 1.0 + nrm((DEPTH, 2, HY_FILTER_HIDDEN), 0.1),
        'hy_skip': nrm((DEPTH, HY_ORDER, HY_WIDTH), 0.5),
        'merge_up': nrm((DEPTH, N_BRANCH, MERGE_RANK, D), MERGE_RANK ** -0.5),
        'merge_b': nrm((DEPTH, N_BRANCH, D), 0.1),
        'w_branch': nrm((DEPTH, N_BRANCH, BRANCH_WIDTH, D), BRANCH_WIDTH ** -0.5),
        'w_out': nrm((DEPTH, D, D), D ** -0.5),
        'w_router': nrm((D, N_EXPERTS), D ** -0.5),
        'router_bias': nrm((N_EXPERTS,), 0.01),
        'moe_w1': nrm((DEPTH, N_EXPERTS, D, D_EXPERT), D ** -0.5),
        'moe_w3': nrm((DEPTH, N_EXPERTS, D, D_EXPERT), D ** -0.5),
        'moe_w2': nrm((DEPTH, N_EXPERTS, D_EXPERT, D), D_EXPERT ** -0.5),
        'final_g': 1.0 + nrm((D,), 0.1),
    }


def reference(x, c, ctx, c_ctx, norm1_g, norm2_g, w_mod, b_mod, w_in, gdn_conv, gdn_a_log,
              gdn_dt_bias, gdn_norm, gla_gate_up, gla_gate_b, gla_norm, hy_conv_w, hy_conv_b,
              hy_w1, hy_b1, hy_w2, hy_b2, hy_w3, hy_freq, hy_skip, merge_up, merge_b, w_branch,
              w_out, w_router, router_bias, moe_w1, moe_w3, moe_w2, final_g):
    rows = x.shape[1] // GRID_W
    silu_c = jax.nn.silu(c)
    silu_cc = jax.nn.silu(c_ctx)
    lat, cx = x, ctx
    for l in range(DEPTH):
        with_ctx = l < DEPTH - 1
        m_lat = jnp.split((silu_c @ w_mod[l] + b_mod[l])[:, None, :], 6, axis=-1)
        m_ctx = jnp.split(silu_cc @ w_mod[l] + b_mod[l], 6, axis=-1)
        h_lat = modulate(rmsnorm(lat, norm1_g[l]), m_lat[0], m_lat[1])
        h_ctx = modulate(rmsnorm(cx, norm1_g[l]), m_ctx[0], m_ctx[1])
        o_lat, o_ctx = token_mixer(
            h_lat, h_ctx, rows, with_ctx, w_in[l], gdn_conv[l], gdn_a_log[l], gdn_dt_bias[l],
            gdn_norm[l], gla_gate_up[l], gla_gate_b[l], gla_norm[l], hy_conv_w[l], hy_conv_b[l],
            hy_w1[l], hy_b1[l], hy_w2[l], hy_b2[l], hy_w3[l], hy_freq[l], hy_skip[l],
            merge_up[l], merge_b[l], w_branch[l], w_out[l])
        lat = lat + m_lat[2] * o_lat
        lat = lat + m_lat[5] * moe(modulate(rmsnorm(lat, norm2_g[l]), m_lat[3], m_lat[4]),
                                   w_router, router_bias, moe_w1[l], moe_w3[l], moe_w2[l])
        if with_ctx:
            cx = cx + m_ctx[2] * o_ctx
            cx = cx + m_ctx[5] * moe(modulate(rmsnorm(cx, norm2_g[l]), m_ctx[3], m_ctx[4]),
                                     w_router, router_bias, moe_w1[l], moe_w3[l], moe_w2[l])
    return rmsnorm(lat, final_g)
```

```python
import functools
import math
from typing import NamedTuple

import numpy as np
import jax
import jax.numpy as jnp
from jax import lax
from jax.experimental import pallas as pl
from jax.experimental.pallas import tpu as pltpu

F32 = jnp.float32
BF16 = jnp.bfloat16

EPS = 1e-6
LANES = 128
V7X_VMEM_BYTES = 64 * 1024 * 1024
VMEM_LIMIT = (V7X_VMEM_BYTES * 13) // 16
SHORT_CONV = 3
GLA_CHUNK = 64
GLA_GATE_NORM = 16.0
N_EXPERTS = 16
N_GROUPS = 4
HY_EMB = 33
HY_FAST_DECAY = 0.3
HY_SLOW_DECAY = 1.5
HY_TARGET = 1e-2


class Cfg(NamedTuple):
    d: int
    seq: int
    ctx: int
    grid_w: int
    gdn_h: int
    gla_h: int
    hy_w: int
    merge_rank: int
    d_expert: int
    row_tile: int
    mm_tm: int

    @property
    def t(self):
        return self.seq + self.ctx

    @property
    def gdn_w(self):
        return self.gdn_h * 128

    @property
    def gla_qk(self):
        return self.gla_h * 128

    @property
    def gla_v(self):
        return self.gla_h * 256


def _cparams(sem):
    return pltpu.CompilerParams(dimension_semantics=sem, vmem_limit_bytes=VMEM_LIMIT)


def _split3(x):
    hi = x.astype(BF16)
    r1 = x - hi.astype(F32)
    mid = r1.astype(BF16)
    lo = (r1 - mid.astype(F32)).astype(BF16)
    return hi, mid, lo


def _dot(a, b):
    return jnp.dot(a, b, preferred_element_type=F32)


def _dot_sel(sel_bf16, x):
    hi, mid, lo = _split3(x)
    return _dot(sel_bf16, hi) + _dot(sel_bf16, mid) + _dot(sel_bf16, lo)


def _dot_x_sel(x, sel_bf16):
    hi, mid, lo = _split3(x)
    return _dot(hi, sel_bf16) + _dot(mid, sel_bf16) + _dot(lo, sel_bf16)


def _dot_hi(a, b):
    a1, a2, a3 = _split3(a)
    b1, b2, b3 = _split3(b)
    return (_dot(a1, b1) + (_dot(a1, b2) + _dot(a2, b1))
            + (_dot(a2, b2) + _dot(a1, b3) + _dot(a3, b1)))


def _silu(x):
    return x * jax.nn.sigmoid(x)


def _modvec_kernel(x_ref, w_ref, b_ref, o_ref):
    o_ref[...] = _dot(_silu(x_ref[...]), w_ref[...]) + b_ref[...]


def modvec(cvec, w_mod, b_mod, tn=512):
    depth, d, n = w_mod.shape
    return pl.pallas_call(
        _modvec_kernel,
        out_shape=jax.ShapeDtypeStruct((depth, 8, n), F32),
        grid=(depth, n // tn),
        in_specs=[pl.BlockSpec((8, d), lambda l, j: (0, 0)),
                  pl.BlockSpec((None, d, tn), lambda l, j: (l, 0, j)),
                  pl.BlockSpec((None, 1, tn), lambda l, j: (l, 0, j))],
        out_specs=pl.BlockSpec((None, 8, tn), lambda l, j: (l, 0, j)),
        compiler_params=_cparams(("parallel", "parallel")),
        name="modvec",
    )(cvec, w_mod, b_mod.reshape(depth, 1, n))


def _mod_row(mod_ref, is_ctx, idx, d):
    return mod_ref[pl.ds(is_ctx, 1), idx * d:(idx + 1) * d]


def _norm_mod(x, gain, shift, scale):
    y = x * lax.rsqrt(jnp.mean(x * x, axis=-1, keepdims=True) + EPS)
    return (y * gain) * (1.0 + scale) + shift


def _norm1_kernel(x_ref, g_ref, mod_ref, o_ref, *, d, n_lat_tiles):
    is_ctx = (pl.program_id(0) >= n_lat_tiles).astype(jnp.int32)
    h = _norm_mod(x_ref[...], g_ref[...], _mod_row(mod_ref, is_ctx, 0, d),
                  _mod_row(mod_ref, is_ctx, 1, d))
    o_ref[...] = h.astype(BF16)


def norm1(cfg, x, gain, mod):
    tr = cfg.row_tile
    t, d = x.shape
    return pl.pallas_call(
        functools.partial(_norm1_kernel, d=d, n_lat_tiles=cfg.seq // tr),
        out_shape=jax.ShapeDtypeStruct((t, d), BF16),
        grid=(t // tr,),
        in_specs=[pl.BlockSpec((tr, d), lambda i: (i, 0)),
                  pl.BlockSpec((1, d), lambda i: (0, 0)),
                  pl.BlockSpec((8, 6 * d), lambda i: (0, 0))],
        out_specs=pl.BlockSpec((tr, d), lambda i: (i, 0)),
        compiler_params=_cparams(("parallel",)),
        name="norm1",
    )(x, gain.reshape(1, d), mod)


def _route(sel_t, sc_t):
    per = N_EXPERTS // N_GROUPS
    grp_score = []
    for g in range(N_GROUPS):
        v = sel_t[g * per:(g + 1) * per]
        best = None
        for a in range(per):
            for b in range(a + 1, per):
                s = v[a] + v[b]
                best = s if best is None else jnp.maximum(best, s)
        grp_score.append(best)
    best_s, best_g = grp_score[0], jnp.zeros_like(grp_score[0])
    for g in range(1, N_GROUPS):
        better = grp_score[g] > best_s
        best_s = jnp.where(better, grp_score[g], best_s)
        best_g = jnp.where(better, float(g), best_g)
    picked = []
    for e in range(N_EXPERTS):
        g, i = divmod(e, per)
        rank = jnp.zeros_like(best_s)
        for j in range(per):
            if j == i:
                continue
            o = sel_t[g * per + j]
            ahead = (o >= sel_t[e]) if j < i else (o > sel_t[e])
            rank = rank + ahead.astype(F32)
        picked.append(jnp.where((best_g == float(g)) & (rank < 2.0), sc_t[e], 0.0))
    den = picked[0]
    for e in range(1, N_EXPERTS):
        den = den + picked[e]
    inv = 1.0 / den
    return [p * inv for p in picked]


def _norm2_kernel(x_ref, g_ref, mod_ref, wr_ref, rb_ref, ex_ref, o_ref, gate_ref, *, d,
                  n_lat_tiles):
    is_ctx = (pl.program_id(0) >= n_lat_tiles).astype(jnp.int32)
    h = _norm_mod(x_ref[...], g_ref[...], _mod_row(mod_ref, is_ctx, 3, d),
                  _mod_row(mod_ref, is_ctx, 4, d))
    o_ref[...] = h.astype(BF16)
    logits = _dot_hi(h, wr_ref[...])
    lt = jnp.transpose(logits)
    sc = jax.nn.sigmoid(lt[0:N_EXPERTS, :])
    sel = sc + rb_ref[...]
    gate_rows = _route([sel[e:e + 1, :] for e in range(N_EXPERTS)],
                       [sc[e:e + 1, :] for e in range(N_EXPERTS)])
    rows = lax.broadcasted_iota(jnp.int32, lt.shape, 0)
    gt = jnp.zeros(lt.shape, F32)
    for e in range(N_EXPERTS):
        gt = jnp.where(rows == e, gate_rows[e], gt)
    gate = jnp.transpose(gt)
    gate_ref[...] = _dot_x_sel(gate, ex_ref[...])


def norm2_route(cfg, x, gain, mod, w_router_pad, rbias_col, expand):
    tr = cfg.row_tile
    t, d = x.shape
    return pl.pallas_call(
        functools.partial(_norm2_kernel, d=d, n_lat_tiles=cfg.seq // tr),
        out_shape=(jax.ShapeDtypeStruct((t, d), BF16),
                   jax.ShapeDtypeStruct((t, N_EXPERTS * LANES), F32)),
        grid=(t // tr,),
        in_specs=[pl.BlockSpec((tr, d), lambda i: (i, 0)),
                  pl.BlockSpec((1, d), lambda i: (0, 0)),
                  pl.BlockSpec((8, 6 * d), lambda i: (0, 0)),
                  pl.BlockSpec((d, LANES), lambda i: (0, 0)),
                  pl.BlockSpec((N_EXPERTS, 1), lambda i: (0, 0)),
                  pl.BlockSpec((LANES, N_EXPERTS * LANES), lambda i: (0, 0))],
        out_specs=(pl.BlockSpec((tr, d), lambda i: (i, 0)),
                   pl.BlockSpec((tr, N_EXPERTS * LANES), lambda i: (i, 0))),
        compiler_params=_cparams(("parallel",)),
        name="norm2_route",
    )(x, gain.reshape(1, d), mod, w_router_pad, rbias_col, expand)


def _final_norm_kernel(x_ref, g_ref, o_ref):
    x = x_ref[...]
    o_ref[...] = (x * lax.rsqrt(jnp.mean(x * x, axis=-1, keepdims=True) + EPS)) * g_ref[...]


def final_norm(x, gain, n_rows, tr):
    d = x.shape[1]
    return pl.pallas_call(
        _final_norm_kernel,
        out_shape=jax.ShapeDtypeStruct((n_rows, d), F32),
        grid=(n_rows // tr,),
        in_specs=[pl.BlockSpec((tr, d), lambda i: (i, 0)),
                  pl.BlockSpec((1, d), lambda i: (0, 0))],
        out_specs=pl.BlockSpec((tr, d), lambda i: (i, 0)),
        compiler_params=_cparams(("parallel",)),
        name="final_norm",
    )(x, gain.reshape(1, d))


def _mm_plain_kernel(a_ref, w_ref, o_ref):
    o_ref[...] = _dot(a_ref[...], w_ref[...]).astype(o_ref.dtype)


def matmul_plain(a, w, layer, *, tm, tn, n_cols=None, col0=0, out_dtype=F32, name="mm"):
    t, k = a.shape
    n_cols = w.shape[2] - col0 if n_cols is None else n_cols
    off = col0 // tn
    return pl.pallas_call(
        _mm_plain_kernel,
        out_shape=jax.ShapeDtypeStruct((t, n_cols), out_dtype),
        grid=(n_cols // tn, t // tm),
        in_specs=[pl.BlockSpec((tm, k), lambda j, i: (i, 0)),
                  pl.BlockSpec((None, k, tn), lambda j, i: (layer, 0, j + off))],
        out_specs=pl.BlockSpec((tm, tn), lambda j, i: (i, j)),
        compiler_params=_cparams(("parallel", "parallel")),
        name=name,
    )(a, w)


def _mm_resid_kernel(a_ref, w_ref, r_ref, mod_ref, o_ref, *, tm, n_lat):
    row = pl.program_id(1) * tm + lax.broadcasted_iota(jnp.int32, (tm, 1), 0)
    gate = jnp.where(row < n_lat, mod_ref[0:1, :], mod_ref[1:2, :])
    o_ref[...] = r_ref[...] + gate * _dot(a_ref[...], w_ref[...])


def matmul_resid(cfg, a, w, layer, resid, mod, idx, *, tn, name):
    t, k = a.shape
    d = w.shape[2]
    tm = cfg.mm_tm
    return pl.pallas_call(
        functools.partial(_mm_resid_kernel, tm=tm, n_lat=cfg.seq),
        out_shape=jax.ShapeDtypeStruct((t, d), F32),
        grid=(d // tn, t // tm),
        in_specs=[pl.BlockSpec((tm, k), lambda j, i: (i, 0)),
                  pl.BlockSpec((None, k, tn), lambda j, i: (layer, 0, j)),
                  pl.BlockSpec((tm, tn), lambda j, i: (i, j)),
                  pl.BlockSpec((8, tn), lambda j, i: (0, idx * (d // tn) + j))],
        out_specs=pl.BlockSpec((tm, tn), lambda j, i: (i, j)),
        input_output_aliases={2: 0},
        compiler_params=_cparams(("parallel", "parallel")),
        name=name,
    )(a, w, resid, mod)


def _mm_moe_act_kernel(a_ref, w_ref, g_ref, o_ref, *, de, n_e):
    acc = _dot(a_ref[...], w_ref[...])
    for e in range(n_e):
        up = acc[:, (2 * e) * de:(2 * e + 1) * de]
        lin = acc[:, (2 * e + 1) * de:(2 * e + 2) * de]
        g = g_ref[:, e * LANES:(e + 1) * LANES]
        g = jnp.concatenate([g] * (de // LANES), axis=1)
        o_ref[:, e * de:(e + 1) * de] = (_silu(up) * lin * g).astype(o_ref.dtype)


def matmul_moe_act(cfg, h, w13, layer, gate_rep, *, n_e=2):
    t, k = h.shape
    de = cfg.d_expert
    tm = cfg.mm_tm
    tn = n_e * 2 * de
    return pl.pallas_call(
        functools.partial(_mm_moe_act_kernel, de=de, n_e=n_e),
        out_shape=jax.ShapeDtypeStruct((t, N_EXPERTS * de), BF16),
        grid=(N_EXPERTS // n_e, t // tm),
        in_specs=[pl.BlockSpec((tm, k), lambda j, i: (i, 0)),
                  pl.BlockSpec((None, k, tn), lambda j, i: (layer, 0, j)),
                  pl.BlockSpec((tm, n_e * LANES), lambda j, i: (i, j))],
        out_specs=pl.BlockSpec((tm, n_e * de), lambda j, i: (i, j)),
        compiler_params=_cparams(("parallel", "parallel")),
        name="moe_up",
    )(h, w13, gate_rep)


def _merge_kernel(a0_ref, a1_ref, a2_ref, zg_ref, wb_ref, mu_ref, mb_ref, o_ref):
    zg = zg_ref[...].astype(BF16)
    acc = None
    for n, a_ref in enumerate((a0_ref, a1_ref, a2_ref)):
        y = _dot(a_ref[...], wb_ref[n])
        gate = jax.nn.sigmoid(_dot(zg, mu_ref[n]) + mb_ref[n])
        acc = gate * y if acc is None else acc + gate * y
    o_ref[...] = acc.astype(o_ref.dtype)


def merge_branches(cfg, outs, z_small, w_branch, merge_up, merge_b, layer, *, tn):
    t, bw = outs[0].shape
    d = w_branch.shape[3]
    r = cfg.merge_rank
    tm = cfg.mm_tm
    a_spec = pl.BlockSpec((tm, bw), lambda j, i: (i, 0))
    return pl.pallas_call(
        _merge_kernel,
        out_shape=jax.ShapeDtypeStruct((t, d), BF16),
        grid=(d // tn, t // tm),
        in_specs=[a_spec, a_spec, a_spec,
                  pl.BlockSpec((tm, r), lambda j, i: (i, 0)),
                  pl.BlockSpec((None, 3, bw, tn), lambda j, i: (layer, 0, 0, j)),
                  pl.BlockSpec((None, 3, r, tn), lambda j, i: (layer, 0, 0, j)),
                  pl.BlockSpec((None, 3, 1, tn), lambda j, i: (layer, 0, 0, j))],
        out_specs=pl.BlockSpec((tm, tn), lambda j, i: (i, j)),
        compiler_params=_cparams(("parallel", "parallel")),
        name="merge",
    )(outs[0], outs[1], outs[2], z_small, w_branch, merge_up, merge_b)


def _halo_specs(tr, width, col_block, n_row_blocks8):
    r8 = tr // 8
    main = pl.BlockSpec((tr, width), lambda i: (i, col_block))
    prev = pl.BlockSpec((8, width), lambda i: (jnp.maximum(i * r8 - 1, 0), col_block))
    nxt = pl.BlockSpec((8, width), lambda i: (jnp.minimum((i + 1) * r8, n_row_blocks8 - 1), col_block))
    return main, prev, nxt


def _conv3(x, prev_ref, next_ref, w_ref, n_lat_tiles):
    i = pl.program_id(0)
    tr = x.shape[0]
    has_prev = jnp.logical_and(i != 0, i != n_lat_tiles).astype(F32)
    has_next = jnp.logical_and(i != n_lat_tiles - 1, i != n_lat_tiles).astype(F32)
    row = lax.broadcasted_iota(jnp.int32, (tr, 1), 0)
    x_prev = jnp.where(row == 0, prev_ref[7:8, :] * has_prev, pltpu.roll(x, 1, axis=0))
    x_next = jnp.where(row == tr - 1, next_ref[0:1, :] * has_next, pltpu.roll(x, tr - 1, axis=0))
    return x_prev * w_ref[0:1, :] + x * w_ref[1:2, :] + x_next * w_ref[2:3, :]


def _tri_masks(n):
    r = lax.broadcasted_iota(jnp.int32, (n, n), 0)
    c = lax.broadcasted_iota(jnp.int32, (n, n), 1)
    return r, c


def _gdn_prep_kernel(z_ref, zp_ref, zn_ref, s_ref, cw_ref, alog_ref, dtb_ref,
                     q_ref, k_ref, v_ref, bb_ref, gcb_ref, gct_ref, *, n_heads, n_lat_tiles):
    tr = z_ref.shape[0]
    hw = n_heads * 128
    y = _silu(_conv3(z_ref[...], zp_ref, zn_ref, cw_ref, n_lat_tiles))
    for h in range(n_heads):
        q = y[:, h * 128:(h + 1) * 128]
        k = y[:, hw + h * 128:hw + (h + 1) * 128]
        q = q * (lax.rsqrt(jnp.sum(q * q, axis=-1, keepdims=True) + EPS) * (128.0 ** -0.5))
        k = k * lax.rsqrt(jnp.sum(k * k, axis=-1, keepdims=True) + EPS)
        q_ref[h] = q.astype(BF16)
        k_ref[h] = k.astype(BF16)
        v_ref[h] = y[:, 2 * hw + h * 128:2 * hw + (h + 1) * 128].astype(BF16)
    s = s_ref[...]
    nh2 = 2 * n_heads
    beta = jax.nn.sigmoid(s)
    g = -jnp.exp(alog_ref[...]) * jax.nn.softplus(s + dtb_ref[...])
    r, c = _tri_masks(tr)
    incl_lo = (c <= r).astype(BF16)
    incl_up = (c >= r).astype(BF16)
    lane = lax.broadcasted_iota(jnp.int32, (tr, LANES), 1)
    fwd_lane = lane < nh2 + n_heads
    gc = jnp.where(fwd_lane, _dot_sel(incl_lo, g), _dot_sel(incl_up, g))
    for ch in range(nh2):
        bb_ref[ch] = jnp.broadcast_to(beta[:, ch:ch + 1], (tr, LANES))
        gcb_ref[ch] = jnp.broadcast_to(gc[:, nh2 + ch:nh2 + ch + 1], (tr, LANES))
    gct_ref[...] = jnp.transpose(gc)[nh2:2 * nh2, :]


def gdn_prep(cfg, z_big, z_small, conv_w, alog_row, dtb_row):
    tr = cfg.row_tile
    t = z_big.shape[0]
    nh = cfg.gdn_h
    hw = nh * 128
    n_tiles = t // tr
    main, prev, nxt = _halo_specs(tr, 3 * hw, 0, t // 8)
    head_out = jax.ShapeDtypeStruct((nh, t, 128), BF16)
    head_spec = pl.BlockSpec((nh, tr, 128), lambda i: (0, i, 0))
    col_out = jax.ShapeDtypeStruct((2 * nh, t, LANES), F32)
    col_spec = pl.BlockSpec((2 * nh, tr, LANES), lambda i: (0, i, 0))
    return pl.pallas_call(
        functools.partial(_gdn_prep_kernel, n_heads=nh, n_lat_tiles=cfg.seq // tr),
        out_shape=(head_out, head_out, head_out, col_out, col_out,
                   jax.ShapeDtypeStruct((n_tiles, 2 * nh, tr), F32)),
        grid=(n_tiles,),
        in_specs=[main, prev, nxt,
                  pl.BlockSpec((tr, LANES), lambda i: (i, cfg.merge_rank // LANES)),
                  pl.BlockSpec((SHORT_CONV, 3 * hw), lambda i: (0, 0)),
                  pl.BlockSpec((1, LANES), lambda i: (0, 0)),
                  pl.BlockSpec((1, LANES), lambda i: (0, 0))],
        out_specs=(head_spec, head_spec, head_spec, col_spec, col_spec,
                   pl.BlockSpec((None, 2 * nh, tr), lambda i: (i, 0, 0))),
        compiler_params=_cparams(("parallel",)),
        name="gdn_prep",
    )(z_big, z_big, z_big, z_small, conv_w, alog_row, dtb_row)


def _unit_tri_inverse(n_mat, r, c):
    n = n_mat.shape[0]
    eye = (r == c).astype(F32)

    def same_block(b):
        sh = int(math.log2(b))
        return (r >> sh) == (c >> sh)

    pf = jnp.where(same_block(8), -n_mat, 0.0)
    p = pf.astype(BF16)
    p2b = _dot(p, p).astype(BF16)
    p4b = _dot(p2b, p2b).astype(BF16)
    t = eye + pf
    t = t + _dot(t.astype(BF16), p2b)
    t = t + _dot(t.astype(BF16), p4b)
    b = 8
    while b < n:
        off = jnp.where(jnp.logical_and(same_block(2 * b), jnp.logical_not(same_block(b))),
                        n_mat, 0.0).astype(BF16)
        tb = t.astype(BF16)
        t = t - _dot(_dot(tb, off).astype(BF16), tb)
        b *= 2
    return t, eye


def _gdn_chunk_kernel(q_ref, k_ref, v_ref, bf_ref, bb_ref, gf_ref, gb_ref, gct_ref,
                      uw_ref, aq_ref, qk_ref, gl_ref, *, n_heads):
    h = pl.program_id(1)
    tr = q_ref.shape[0]
    q = q_ref[...].astype(F32)
    k_b = k_ref[...]
    k = k_b.astype(F32)
    v = v_ref[...].astype(F32)
    r, c = _tri_masks(tr)
    nt = (((1,), (1,)), ((), ()))
    a_qk = lax.dot_general(q_ref[...], k_b, nt, preferred_element_type=F32)
    for d, (beta_ref, gcb_ref) in enumerate(((bf_ref, gf_ref), (bb_ref, gb_ref))):
        beta = beta_ref[...]
        gc = gcb_ref[...]
        gc_row = gct_ref[pl.ds(d * n_heads + h, 1), :]
        incl = (c <= r) if d == 0 else (c >= r)
        strict = (c < r) if d == 0 else (c > r)
        gc_col = jnp.concatenate([gc] * (tr // LANES), axis=1)
        gam = jnp.exp(jnp.where(incl, gc_col - gc_row, -jnp.inf))
        kb = k * beta
        a_kk = lax.dot_general(kb.astype(BF16), k_b, nt, preferred_element_type=F32)
        n_mat = jnp.where(strict, a_kk * gam, 0.0)
        t_inv, eye = _unit_tri_inverse(n_mat, r, c)
        e = jnp.exp(gc)
        rhs = jnp.concatenate([v * beta, kb * e], axis=1)
        sol = rhs + _dot((t_inv - eye).astype(BF16), rhs.astype(BF16))
        g_last = gc[tr - 1:tr, :] if d == 0 else gc[0:1, :]
        uw_ref[d] = sol.astype(BF16)
        aq_ref[d] = (a_qk * gam).astype(BF16)
        qk_ref[d] = jnp.concatenate([q * e, k * jnp.exp(g_last - gc)], axis=1).astype(BF16)
        gl_ref[d] = jnp.broadcast_to(jnp.exp(g_last), (8, LANES))


def gdn_chunks(cfg, qn, kn, vs, bb, gcb, gct):
    tr = cfg.row_tile
    nh, t, _ = qn.shape
    n_tiles = t // tr
    head = pl.BlockSpec((None, tr, 128), lambda i, h: (h, i, 0))
    head_b = pl.BlockSpec((None, tr, 128), lambda i, h: (h + nh, i, 0))
    big = jax.ShapeDtypeStruct((2, nh, t, 2 * 128), BF16)
    aq = jax.ShapeDtypeStruct((2, nh, t, tr), BF16)
    return pl.pallas_call(
        functools.partial(_gdn_chunk_kernel, n_heads=nh),
        out_shape=(big, aq, big, jax.ShapeDtypeStruct((2, nh, n_tiles * 8, LANES), F32)),
        grid=(n_tiles, nh),
        in_specs=[head, head, head, head, head_b, head, head_b,
                  pl.BlockSpec((None, 2 * nh, tr), lambda i, h: (i, 0, 0))],
        out_specs=(pl.BlockSpec((2, None, tr, 256), lambda i, h: (0, h, i, 0)),
                   pl.BlockSpec((2, None, tr, tr), lambda i, h: (0, h, i, 0)),
                   pl.BlockSpec((2, None, tr, 256), lambda i, h: (0, h, i, 0)),
                   pl.BlockSpec((2, None, 8, LANES), lambda i, h: (0, h, i, 0))),
        compiler_params=_cparams(("parallel", "parallel")),
        name="gdn_chunks",
    )(qn, kn, vs, bb, bb, gcb, gcb, gct)


def _gdn_scan_kernel(uwf_ref, aqf_ref, qkf_ref, glf_ref, uwb_ref, aqb_ref, qkb_ref, glb_ref,
                     of_ref, ob_ref, s_ref, *, n_heads):
    @pl.when(pl.program_id(0) == 0)
    def _():
        s_ref[...] = jnp.zeros_like(s_ref)

    tn = (((0,), (0,)), ((), ()))
    for d, (uw_ref, aq_ref, qk_ref, gl_ref, o_ref) in enumerate(
            ((uwf_ref, aqf_ref, qkf_ref, glf_ref, of_ref), (uwb_ref, aqb_ref, qkb_ref, glb_ref, ob_ref))):
        for h in range(n_heads):
            st = s_ref[d, h]
            st_b = st.astype(BF16)
            uw = uw_ref[h]
            qk = qk_ref[h]
            v_new = uw[:, 0:128].astype(F32) - _dot(uw[:, 128:256], st_b)
            v_new_b = v_new.astype(BF16)
            o_ref[:, h * 128:(h + 1) * 128] = _dot(qk[:, 0:128], st_b) + _dot(aq_ref[h], v_new_b)
            s_ref[d, h] = gl_ref[h][0:1, :] * st + lax.dot_general(
                qk[:, 128:256], v_new_b, tn, preferred_element_type=F32)


def gdn_scan(cfg, uw, aq, qk, gl):
    tr = cfg.row_tile
    _, nh, t, _ = uw.shape
    n_tiles = t // tr
    last = n_tiles - 1

    def fwd(s):
        return jnp.where(s == 0, last, s - 1)

    def bwd(s):
        return jnp.where(s == 0, last, last - s)

    def specs(d, order):
        return [pl.BlockSpec((None, nh, tr, 256), lambda s: (d, 0, order(s), 0)),
                pl.BlockSpec((None, nh, tr, tr), lambda s: (d, 0, order(s), 0)),
                pl.BlockSpec((None, nh, tr, 256), lambda s: (d, 0, order(s), 0)),
                pl.BlockSpec((None, nh, 8, LANES), lambda s: (d, 0, order(s), 0))]

    out = jax.ShapeDtypeStruct((t, nh * 128), F32)
    return pl.pallas_call(
        functools.partial(_gdn_scan_kernel, n_heads=nh),
        out_shape=(out, out),
        grid=(n_tiles,),
        in_specs=specs(0, fwd) + specs(1, bwd),
        out_specs=(pl.BlockSpec((tr, nh * 128), lambda s: (fwd(s), 0)),
                   pl.BlockSpec((tr, nh * 128), lambda s: (bwd(s), 0))),
        scratch_shapes=[pltpu.VMEM((2, nh, 128, 128), F32)],
        compiler_params=_cparams(("arbitrary",)),
        name="gdn_scan",
    )(uw, aq, qk, gl, uw, aq, qk, gl)


def _head_norm_kernel(of_ref, ob_ref, gate_ref, w_ref, o_ref, *, n_heads, dv):
    for h in range(n_heads):
        sl = slice(h * dv, (h + 1) * dv)
        o = of_ref[:, sl] + ob_ref[:, sl]
        o = o * lax.rsqrt(jnp.mean(o * o, axis=-1, keepdims=True) + EPS) * w_ref[...]
        o_ref[:, sl] = (o * _silu(gate_ref[:, sl])).astype(o_ref.dtype)


def head_norm(o_f, o_b, z_big, gate_col_block, norm_w, n_heads, dv, tr, gate_row0=0):
    t, w = o_f.shape
    spec = pl.BlockSpec((tr, w), lambda i: (i, 0))
    rb0 = gate_row0 // tr
    return pl.pallas_call(
        functools.partial(_head_norm_kernel, n_heads=n_heads, dv=dv),
        out_shape=jax.ShapeDtypeStruct((t, w), BF16),
        grid=(t // tr,),
        in_specs=[spec, spec, pl.BlockSpec((tr, w), lambda i: (i + rb0, gate_col_block)),
                  pl.BlockSpec((1, dv), lambda i: (0, 0))],
        out_specs=spec,
        compiler_params=_cparams(("parallel",)),
        name="head_norm",
    )(o_f, o_b, z_big, norm_w.reshape(1, dv))


GLA_TILE = 128
GLA_LR_LANE0 = 32


def _gla_direction(d, qk_ref, v_ref, sm_ref, wup_ref, gb_ref, o_ref, st_ref, n_heads):
    tr = GLA_TILE
    qkw = n_heads * 128
    r, c = _tri_masks(tr)
    sh = int(math.log2(GLA_CHUNK))
    same = (r >> sh) == (c >> sh)
    mask = jnp.logical_and(same, (c <= r) if d == 0 else (c >= r))
    logit = _dot_hi(sm_ref[...], wup_ref[:, d * qkw:(d + 1) * qkw]) + gb_ref[:, d * qkw:(d + 1) * qkw]
    g = jax.nn.log_sigmoid(logit) * (1.0 / GLA_GATE_NORM)
    b = _dot_sel(jnp.where(mask, 1.0, 0.0).astype(BF16), g)
    nt = (((1,), (1,)), ((), ()))
    tn = (((0,), (0,)), ((), ()))
    chunks = range(tr // GLA_CHUNK) if d == 0 else range(tr // GLA_CHUNK - 1, -1, -1)
    for h in range(n_heads):
        bq = b[:, h * 128:(h + 1) * 128]
        q = qk_ref[:, h * 128:(h + 1) * 128].astype(F32)
        k = qk_ref[:, qkw + h * 128:qkw + (h + 1) * 128].astype(F32)
        v_b = v_ref[:, h * 256:(h + 1) * 256].astype(BF16)
        qe = (q * jnp.exp(bq) * (128.0 ** -0.5)).astype(BF16)
        kinv = (k * jnp.exp(-bq)).astype(BF16)
        a = jnp.where(mask, lax.dot_general(qe, kinv, nt, preferred_element_type=F32), 0.0)
        o_intra = _dot(a.astype(BF16), v_b)
        for ci in chunks:
            lo = ci * GLA_CHUNK
            last = lo + GLA_CHUNK - 1 if d == 0 else lo
            b_last = bq[last:last + 1, :]
            kdec = (k[lo:lo + GLA_CHUNK] * jnp.exp(b_last - bq[lo:lo + GLA_CHUNK])).astype(BF16)
            ds = lax.dot_general(kdec, v_b[lo:lo + GLA_CHUNK], tn, preferred_element_type=F32)
            st = st_ref[d, h]
            o_ref[lo:lo + GLA_CHUNK, h * 256:(h + 1) * 256] = (
                o_intra[lo:lo + GLA_CHUNK] + _dot(qe[lo:lo + GLA_CHUNK], st.astype(BF16)))
            dec = jnp.transpose(jnp.broadcast_to(jnp.exp(b_last), (128, 128)))
            st_ref[d, h] = jnp.concatenate([dec, dec], axis=1) * st + ds


def _gla_kernel(qkf_ref, vf_ref, smf_ref, qkb_ref, vb_ref, smb_ref, wup_ref, gb_ref, s0_ref,
                of_ref, ob_ref, sfin_ref, st_ref, *, n_heads):
    @pl.when(pl.program_id(0) == 0)
    def _():
        st_ref[...] = s0_ref[...]

    _gla_direction(0, qkf_ref, vf_ref, smf_ref, wup_ref, gb_ref, of_ref, st_ref, n_heads)
    _gla_direction(1, qkb_ref, vb_ref, smb_ref, wup_ref, gb_ref, ob_ref, st_ref, n_heads)

    @pl.when(pl.program_id(0) == pl.num_programs(0) - 1)
    def _():
        sfin_ref[...] = st_ref[...]


def _gla_call(cfg, n_steps, arrays, spec_fn, out_rows_shape, out_spec_fn, wup, gbias, s0, name):
    nh = cfg.gla_h
    fwd = lambda s: s
    bwd = lambda s: n_steps - 1 - s
    full = lambda shape: pl.BlockSpec(shape, lambda s: (0,) * len(shape))
    z_view, zs_view = arrays
    st_shape = (2, nh, 128, 256)
    out = jax.ShapeDtypeStruct(out_rows_shape, F32)
    return pl.pallas_call(
        functools.partial(_gla_kernel, n_heads=nh),
        out_shape=(out, out, jax.ShapeDtypeStruct(st_shape, F32)),
        grid=(n_steps,),
        in_specs=spec_fn(fwd) + spec_fn(bwd) + [full(wup.shape), full(gbias.shape), full(st_shape)],
        out_specs=(out_spec_fn(fwd), out_spec_fn(bwd), full(st_shape)),
        scratch_shapes=[pltpu.VMEM(st_shape, F32)],
        compiler_params=_cparams(("arbitrary",)),
        name=name,
    )(z_view, z_view, zs_view, z_view, z_view, zs_view, wup, gbias, s0)


def gla(cfg, z_big, z_small, col0, wup, gbias):
    nh = cfg.gla_h
    qk2, vw = 2 * nh * 128, nh * 256
    zw, sw = z_big.shape[1], z_small.shape[1]
    w = cfg.grid_w
    rows = cfg.seq // w
    assert rows == GLA_TILE and cfg.ctx % GLA_TILE == 0
    assert zw % qk2 == 0 and zw % vw == 0 and col0 % qk2 == 0 and (col0 + qk2) % vw == 0
    small_blk = cfg.merge_rank // LANES
    s0 = jnp.zeros((2, nh, 128, 256), F32)

    rb0 = cfg.seq // GLA_TILE
    ctx_specs = lambda order: [
        pl.BlockSpec((GLA_TILE, qk2), lambda s: (rb0 + order(s), col0 // qk2)),
        pl.BlockSpec((GLA_TILE, vw), lambda s: (rb0 + order(s), (col0 + qk2) // vw)),
        pl.BlockSpec((GLA_TILE, LANES), lambda s: (rb0 + order(s), small_blk))]
    ctx_out = lambda order: pl.BlockSpec((GLA_TILE, vw), lambda s: (order(s), 0))
    ocf, ocb, s_ctx = _gla_call(cfg, cfg.ctx // GLA_TILE, (z_big, z_small), ctx_specs,
                                (cfg.ctx, vw), ctx_out, wup, gbias, s0, "gla_ctx")

    t = z_big.shape[0]
    zv = z_big.reshape(t // w, w * zw)
    zsv = z_small.reshape(t // w, w * sw)
    lat_specs = lambda order: [
        pl.BlockSpec((GLA_TILE, qk2), lambda s: (0, (order(s) * zw + col0) // qk2)),
        pl.BlockSpec((GLA_TILE, vw), lambda s: (0, (order(s) * zw + col0 + qk2) // vw)),
        pl.BlockSpec((GLA_TILE, LANES), lambda s: (0, order(s) * (sw // LANES) + small_blk))]
    lat_out = lambda order: pl.BlockSpec((GLA_TILE, vw), lambda s: (0, order(s)))
    olf, olb, _ = _gla_call(cfg, w, (zv, zsv), lat_specs, (rows, w * vw), lat_out,
                            wup, gbias, s_ctx, "gla_lat")
    return (olf.reshape(cfg.seq, vw), olb.reshape(cfg.seq, vw)), (ocf, ocb)


FFT_G = 8


def _hy_prep_kernel(z_ref, zp_ref, zn_ref, w_ref, b_ref, o_ref, *, n_lat_tiles):
    o_ref[...] = _conv3(z_ref[...], zp_ref, zn_ref, w_ref, n_lat_tiles) + b_ref[...]


def hyena_prep(cfg, z_big, col0, conv_w, conv_b):
    tr, hw = cfg.row_tile, cfg.hy_w
    t = z_big.shape[0]
    r8, nb8, cb0 = tr // 8, t // 8, col0 // hw
    assert col0 % hw == 0
    return pl.pallas_call(
        functools.partial(_hy_prep_kernel, n_lat_tiles=cfg.seq // tr),
        out_shape=jax.ShapeDtypeStruct((3, t, hw), F32),
        grid=(t // tr, 3),
        in_specs=[pl.BlockSpec((tr, hw), lambda i, p: (i, cb0 + p)),
                  pl.BlockSpec((8, hw), lambda i, p: (jnp.maximum(i * r8 - 1, 0), cb0 + p)),
                  pl.BlockSpec((8, hw), lambda i, p: (jnp.minimum((i + 1) * r8, nb8 - 1), cb0 + p)),
                  pl.BlockSpec((SHORT_CONV, hw), lambda i, p: (0, p)),
                  pl.BlockSpec((1, hw), lambda i, p: (0, p))],
        out_specs=pl.BlockSpec((None, tr, hw), lambda i, p: (p, i, 0)),
        compiler_params=_cparams(("parallel", "parallel")),
        name="hyena_prep",
    )(z_big, z_big, z_big, conv_w, conv_b.reshape(1, 3 * hw))


def _hy_taps_kernel(f_ref, w1_ref, b1_ref, w2_ref, b2_ref, fr_ref, w3_ref, rate_ref, o_ref, *,
                    length, tt):
    feats = f_ref[...]
    hid = jnp.sin(fr_ref[0:1, :] * (_dot_hi(feats, w1_ref[...]) + b1_ref[...]))
    hid = jnp.sin(fr_ref[1:2, :] * (_dot_hi(hid, w2_ref[...]) + b2_ref[...]))
    h = _dot_hi(hid, w3_ref[...])
    n = pl.program_id(0) * tt + lax.broadcasted_iota(jnp.int32, (tt, 1), 0)
    taps = h * jnp.exp(-feats[:, 0:1] * rate_ref[...])
    o_ref[...] = jnp.where(n == length, 0.0, taps).astype(o_ref.dtype)


def hyena_taps(feats, w1p, b1p, w2p, b2p, freqp, w3p, rates, order, length, hw):
    tt = min(512, length)
    n_half = length // tt
    return pl.pallas_call(
        functools.partial(_hy_taps_kernel, length=length, tt=tt),
        out_shape=jax.ShapeDtypeStruct((2 * length, hw), BF16),
        grid=(2 * n_half,),
        in_specs=[pl.BlockSpec((tt, LANES), lambda j: (j, 0)),
                  pl.BlockSpec((LANES, LANES), lambda j: (0, 0)),
                  pl.BlockSpec((1, LANES), lambda j: (0, 0)),
                  pl.BlockSpec((LANES, LANES), lambda j: (0, 0)),
                  pl.BlockSpec((1, LANES), lambda j: (0, 0)),
                  pl.BlockSpec((2, LANES), lambda j: (0, 0)),
                  pl.BlockSpec((LANES, hw), lambda j: (0, 2 * order + j // n_half)),
                  pl.BlockSpec((1, hw), lambda j: (0, 0))],
        out_specs=pl.BlockSpec((tt, hw), lambda j: (j, 0)),
        compiler_params=_cparams(("parallel",)),
        name="hyena_taps",
    )(feats, w1p, b1p, w2p, b2p, freqp, w3p, rates)


def _fft1_kernel(x_ref, t_ref, yr_ref, yi_ref, *, c, p):
    for g in range(FFT_G):
        y = _dot(t_ref[g], x_ref[:, g * c:(g + 1) * c].astype(BF16))
        yr_ref[g] = y[0:p].astype(yr_ref.dtype)
        yi_ref[g] = y[p:2 * p].astype(yi_ref.dtype)


def fft_step1(x, table, n_rows, c):
    p = table.shape[1] // 2
    xv = x.reshape(x.shape[0] // LANES, LANES * c)
    out = jax.ShapeDtypeStruct((LANES, p, c), BF16)
    return pl.pallas_call(
        functools.partial(_fft1_kernel, c=c, p=p),
        out_shape=(out, out),
        grid=(LANES // FFT_G,),
        in_specs=[pl.BlockSpec((n_rows, FFT_G * c), lambda j: (0, j)),
                  pl.BlockSpec((FFT_G, 2 * p, n_rows), lambda j: (j, 0, 0))],
        out_specs=(pl.BlockSpec((FFT_G, p, c), lambda j: (j, 0, 0)),) * 2,
        compiler_params=_cparams(("parallel",)),
        name="fft_step1",
    )(xv, table)


def _fft2_kernel(yr_ref, yi_ref, m_ref, zr_ref, zi_ref, *, c):
    for g in range(FFT_G):
        s = jnp.concatenate([yr_ref[:, g * c:(g + 1) * c], yi_ref[:, g * c:(g + 1) * c]], axis=0)
        z = _dot(m_ref[...], s)
        zr_ref[g] = z[0:LANES].astype(zr_ref.dtype)
        zi_ref[g] = z[LANES:2 * LANES].astype(zi_ref.dtype)


def fft_step2(yr, yi, m2):
    _, p, c = yr.shape
    out = jax.ShapeDtypeStruct((p, LANES, c), BF16)
    spec = pl.BlockSpec((LANES, FFT_G * c), lambda j: (0, j))
    return pl.pallas_call(
        functools.partial(_fft2_kernel, c=c),
        out_shape=(out, out),
        grid=(p // FFT_G,),
        in_specs=[spec, spec, pl.BlockSpec((2 * LANES, 2 * LANES), lambda j: (0, 0))],
        out_specs=(pl.BlockSpec((FFT_G, LANES, c), lambda j: (j, 0, 0)),) * 2,
        compiler_params=_cparams(("parallel",)),
        name="fft_step2",
    )(yr.reshape(LANES, p * c), yi.reshape(LANES, p * c), m2)


def _ifft1_kernel(zr_ref, zi_ref, hr_ref, hi_ref, m_ref, vr_ref, vi_ref):
    for g in range(FFT_G):
        zr, zi = zr_ref[g].astype(F32), zi_ref[g].astype(F32)
        hr, hi = hr_ref[g].astype(F32), hi_ref[g].astype(F32)
        s = jnp.concatenate([zr * hr - zi * hi, zr * hi + zi * hr], axis=0).astype(BF16)
        v = _dot(m_ref[...], s)
        vr_ref[g] = v[0:LANES].astype(vr_ref.dtype)
        vi_ref[g] = v[LANES:2 * LANES].astype(vi_ref.dtype)


def ifft_step1(zr, zi, hr, hi, ma):
    p, _, c = zr.shape
    out = jax.ShapeDtypeStruct((p, LANES, c), BF16)
    spec = pl.BlockSpec((FFT_G, LANES, c), lambda j: (j, 0, 0))
    return pl.pallas_call(
        _ifft1_kernel,
        out_shape=(out, out),
        grid=(p // FFT_G,),
        in_specs=[spec, spec, spec, spec, pl.BlockSpec((2 * LANES, 2 * LANES), lambda j: (0, 0))],
        out_specs=(spec, spec),
        compiler_params=_cparams(("parallel",)),
        name="ifft_step1",
    )(zr, zi, hr, hi, ma)


def _ifft2_kernel(vr_ref, vi_ref, t_ref, x_ref, u_ref, sk_ref, o_ref, *, c):
    for g in range(FFT_G):
        sl = slice(g * c, (g + 1) * c)
        s = jnp.concatenate([vr_ref[:, sl], vi_ref[:, sl]], axis=0)
        y = _dot(t_ref[g], s)
        u = u_ref[:, sl]
        o_ref[:, sl] = x_ref[:, sl] * (y + sk_ref[...] * u)


def ifft_step2_gate(vr, vi, table, gate_x, u, skip_row):
    p, _, c = vr.shape
    half = p // 2
    xv = gate_x.reshape(gate_x.shape[0] // LANES, LANES * c)
    uv = u.reshape(u.shape[0] // LANES, LANES * c)
    vspec = pl.BlockSpec((p, FFT_G * c), lambda j: (0, j))
    tspec = pl.BlockSpec((half, FFT_G * c), lambda j: (0, j))
    out = pl.pallas_call(
        functools.partial(_ifft2_kernel, c=c),
        out_shape=jax.ShapeDtypeStruct((half, LANES * c), F32),
        grid=(LANES // FFT_G,),
        in_specs=[vspec, vspec, pl.BlockSpec((FFT_G, half, 2 * p), lambda j: (j, 0, 0)),
                  tspec, tspec, pl.BlockSpec((1, c), lambda j: (0, 0))],
        out_specs=tspec,
        compiler_params=_cparams(("parallel",)),
        name="ifft_step2",
    )(vr.reshape(p, LANES * c), vi.reshape(p, LANES * c), table, xv, uv, skip_row)
    return out.reshape(half * LANES, c)


def _hy_ctx_kernel(v_ref, x1_ref, x2_ref, t0_ref, t1_ref, sk_ref, f_ref, g_ref, o_ref, *, n):
    def conv(u, taps_ref):
        us = _dot(f_ref[:, 0:n], u.astype(BF16))
        hs = _dot(f_ref[...], taps_ref[...])
        ur, ui, hr, hi = us[0:2 * n], us[2 * n:4 * n], hs[0:2 * n], hs[2 * n:4 * n]
        prod = jnp.concatenate([ur * hr - ui * hi, ur * hi + ui * hr], axis=0).astype(BF16)
        return _dot(g_ref[...], prod)

    v = v_ref[...]
    y = x1_ref[...] * (conv(v, t0_ref) + sk_ref[0:1, :] * v)
    o_ref[...] = x2_ref[...] * (conv(y, t1_ref) + sk_ref[1:2, :] * y)


def hyena_ctx(cfg, vxx, taps0, taps1, skip, fmat, gmat):
    n, hw = cfg.ctx, cfg.hy_w
    cb = 256
    rb = cfg.seq // n
    part = lambda p: pl.BlockSpec((None, n, cb), lambda j: (p, rb, j))
    tspec = pl.BlockSpec((2 * n, cb), lambda j: (0, j))
    return pl.pallas_call(
        functools.partial(_hy_ctx_kernel, n=n),
        out_shape=jax.ShapeDtypeStruct((n, hw), F32),
        grid=(hw // cb,),
        in_specs=[part(0), part(1), part(2), tspec, tspec,
                  pl.BlockSpec((2, cb), lambda j: (0, j)),
                  pl.BlockSpec((4 * n, 2 * n), lambda j: (0, 0)),
                  pl.BlockSpec((n, 4 * n), lambda j: (0, 0))],
        out_specs=pl.BlockSpec((n, cb), lambda j: (0, j)),
        compiler_params=_cparams(("parallel",)),
        name="hyena_ctx",
    )(vxx, vxx, vxx, taps0, taps1, skip, fmat, gmat)


class HyenaConsts(NamedTuple):
    feats: jax.Array
    rates: jax.Array
    t1_data: jax.Array
    t1_taps: jax.Array
    m2: jax.Array
    ma: jax.Array
    tb: jax.Array
    feats_ctx: jax.Array
    f_ctx: jax.Array
    g_ctx: jax.Array


def _features(length):
    n = np.arange(2 * length, dtype=np.float64)
    pos = np.where(n < length, n, 2 * length - n)
    bands = (HY_EMB - 1) // 2
    f = np.linspace(1e-4, bands - 1, bands)
    omega = (2.0 * math.pi / length) * pos
    feats = np.zeros((2 * length, LANES), np.float64)
    feats[:, 0] = pos / (length - 1)
    feats[:, 1:1 + bands] = np.cos(omega[:, None] * f[None, :])
    feats[:, 1 + bands:1 + 2 * bands] = -np.sin(omega[:, None] * f[None, :])
    return jnp.asarray(feats, F32)


def hyena_consts(cfg):
    length, n, hw = cfg.seq, cfg.ctx, cfg.hy_w
    big_n = 2 * length
    p = big_n // LANES
    rates = np.abs(np.linspace(math.log(HY_TARGET) / HY_FAST_DECAY, math.log(HY_TARGET) / HY_SLOW_DECAY, hw))
    b = np.arange(LANES)[:, None, None]
    k1 = np.arange(p)[None, :, None]
    a = np.arange(p)[None, None, :]
    ang = 2.0 * math.pi * ((k1 * (LANES * a + b)) % big_n) / big_n
    t1 = np.concatenate([np.cos(ang), -np.sin(ang)], axis=1)
    tb = np.concatenate([np.cos(ang), -np.sin(ang)], axis=1).transpose(0, 2, 1)[:, :p // 2, :] / big_n
    k2 = np.arange(LANES)
    a128 = 2.0 * math.pi * ((k2[:, None] * k2[None, :]) % LANES) / LANES
    fr, fi = np.cos(a128), -np.sin(a128)
    m2 = np.block([[fr, -fi], [fi, fr]])
    ma = np.block([[fr, fi], [-fi, fr]])
    kk = np.arange(2 * n)
    ac = 2.0 * math.pi * ((kk[:, None] * kk[None, :]) % (2 * n)) / (2 * n)
    f_ctx = np.concatenate([np.cos(ac), -np.sin(ac)], axis=0)
    g_ctx = np.concatenate([np.cos(ac), -np.sin(ac)], axis=1)[:n] / (2 * n)
    bf = lambda x: jnp.asarray(x, BF16)
    return HyenaConsts(_features(length), jnp.asarray(rates[None, :], F32), bf(t1[:, :, :p // 2]), bf(t1),
                       bf(m2), bf(ma), bf(tb), _features(n), bf(f_ctx), bf(g_ctx))


def hyena(cfg, hc, vxx, filt_w, skip, with_ctx=True):
    length, n, hw = cfg.seq, cfg.ctx, cfg.hy_w
    half = length // LANES
    y_lat = vxx[0]
    for order in range(2):
        taps = hyena_taps(hc.feats, *filt_w, hc.rates, order, length, hw)
        hr, hi = fft_step2(*fft_step1(taps, hc.t1_taps, 2 * half, hw), hc.m2)
        zr, zi = fft_step2(*fft_step1(y_lat, hc.t1_data, half, hw), hc.m2)
        vr, vi = ifft_step1(zr, zi, hr, hi, hc.ma)
        y_lat = ifft_step2_gate(vr, vi, hc.tb, vxx[1 + order], y_lat, skip[order:order + 1])
    if not with_ctx:
        return y_lat, jnp.zeros((n, hw), F32)
    taps_c = [hyena_taps(hc.feats_ctx, *filt_w, hc.rates, order, n, hw) for order in range(2)]
    y_ctx = hyena_ctx(cfg, vxx, taps_c[0], taps_c[1], skip, hc.f_ctx, hc.g_ctx)
    return y_lat, y_ctx


def expert_expand():
    m = np.zeros((LANES, N_EXPERTS * LANES), np.float32)
    for e in range(N_EXPERTS):
        m[e, e * LANES:(e + 1) * LANES] = 1.0
    return jnp.asarray(m, BF16)


def prep_w13(w1, w3):
    l, e, d, f = w1.shape
    w = jnp.concatenate([jnp.transpose(w1, (0, 2, 1, 3)), jnp.transpose(w3, (0, 2, 1, 3))], axis=3)
    return w.reshape(l, d, e * 2 * f).astype(BF16)


def prep_gla_gate(gate_up, gate_b):
    l, _, r, qk = gate_up.shape
    w = jnp.zeros((l, LANES, 2 * qk), F32)
    for z in range(2):
        w = w.at[:, GLA_LR_LANE0 + z * r:GLA_LR_LANE0 + (z + 1) * r, z * qk:(z + 1) * qk].set(gate_up[:, z])
    return w, gate_b.reshape(l, 1, 2 * qk)


def prep_hyena_filter(w1, b1, w2, b2, w3, freq):
    e, hdim = w1.shape
    w1p = jnp.zeros((LANES, LANES), F32).at[:e, :hdim].set(w1)
    w2p = jnp.zeros((LANES, LANES), F32).at[:hdim, :hdim].set(w2)
    w3p = jnp.zeros((LANES, w3.shape[1]), F32).at[:hdim].set(w3)
    pad = lambda v: jnp.zeros((1, LANES), F32).at[0, :hdim].set(v)
    freqp = jnp.zeros((2, LANES), F32).at[:, :hdim].set(freq)
    return w1p, pad(b1), w2p, pad(b2), freqp, w3p


def prep_w2(w2):
    l, e, f, d = w2.shape
    return w2.reshape(l, e * f, d).astype(BF16)


def kernel(x, c, ctx, c_ctx, norm1_g, norm2_g, w_mod, b_mod, w_in, gdn_conv, gdn_a_log, gdn_dt_bias, gdn_norm, gla_gate_up, gla_gate_b, gla_norm, hy_conv_w, hy_conv_b, hy_w1, hy_b1, hy_w2, hy_b2, hy_w3, hy_freq, hy_skip, merge_up, merge_b, w_branch, w_out, w_router, router_bias, moe_w1, moe_w3, moe_w2, final_g):
    cfg = Cfg(d=4096, seq=8192, ctx=256, grid_w=64, gdn_h=8, gla_h=4, hy_w=1024, merge_rank=256,
              d_expert=256, row_tile=256, mm_tm=768)
    return forward(cfg, x, c, ctx, c_ctx, norm1_g, norm2_g, w_mod, b_mod, w_in, gdn_conv, gdn_a_log,
                   gdn_dt_bias, gdn_norm, gla_gate_up, gla_gate_b, gla_norm, hy_conv_w, hy_conv_b, hy_w1,
                   hy_b1, hy_w2, hy_b2, hy_w3, hy_freq, hy_skip, merge_up, merge_b, w_branch, w_out,
                   w_router, router_bias, moe_w1, moe_w3, moe_w2, final_g)


def prep_w_in(cfg, w_in):
    gw = cfg.gdn_w
    gdn_small = 4 * cfg.gdn_h
    gdn_in = 4 * gw + gdn_small
    gla_main = 2 * cfg.gla_qk + 2 * cfg.gla_v
    gla_in = gla_main + 32
    rec = gdn_in + gla_in
    hy_in = 3 * cfg.hy_w
    l, d, _ = w_in.shape
    big = jnp.concatenate([w_in[:, :, :4 * gw], w_in[:, :, gdn_in:gdn_in + gla_main],
                           w_in[:, :, rec:rec + hy_in]], axis=2).astype(BF16)
    zeros = lambda n: jnp.zeros((l, d, n), w_in.dtype)
    small = jnp.concatenate([w_in[:, :, rec + hy_in:rec + hy_in + cfg.merge_rank],
                             w_in[:, :, 4 * gw:gdn_in], zeros(GLA_LR_LANE0 - gdn_small),
                             w_in[:, :, gdn_in + gla_main:gdn_in + gla_in], zeros(LANES - GLA_LR_LANE0 - 32)],
                            axis=2).astype(BF16)
    return big, small


def forward(cfg, x, c, ctx, c_ctx, norm1_g, norm2_g, w_mod, b_mod, w_in, gdn_conv, gdn_a_log, gdn_dt_bias,
            gdn_norm, gla_gate_up, gla_gate_b, gla_norm, hy_conv_w, hy_conv_b, hy_w1, hy_b1, hy_w2, hy_b2,
            hy_w3, hy_freq, hy_skip, merge_up, merge_b, w_branch, w_out, w_router, router_bias, moe_w1,
            moe_w3, moe_w2, final_g):
    d, tr, tm = cfg.d, cfg.row_tile, cfg.mm_tm
    depth = w_in.shape[0]
    nh = cfg.gdn_h
    tn = min(1024, d)
    w_big, w_small = prep_w_in(cfg, w_in)
    w13 = prep_w13(moe_w1, moe_w3)
    w2f = prep_w2(moe_w2)
    wb, mu, wo = w_branch.astype(BF16), merge_up.astype(BF16), w_out.astype(BF16)
    mb = merge_b.reshape(depth, 3, 1, d)
    wr_pad = jnp.zeros((d, LANES), F32).at[:, :N_EXPERTS].set(w_router)
    rb_col = router_bias.reshape(N_EXPERTS, 1)
    expand = expert_expand()
    wup, gbias = prep_gla_gate(gla_gate_up, gla_gate_b)
    lane0 = 2 * nh
    alog_rows = jnp.zeros((depth, 1, LANES), F32).at[:, 0, lane0:2 * lane0].set(gdn_a_log.reshape(depth, -1))
    dtb_rows = jnp.zeros((depth, 1, LANES), F32).at[:, 0, lane0:2 * lane0].set(gdn_dt_bias.reshape(depth, -1))
    hc = hyena_consts(cfg)
    gla_col0 = 4 * cfg.gdn_w
    hy_col0 = gla_col0 + 2 * cfg.gla_qk + 2 * cfg.gla_v
    gla_gate_blk = (gla_col0 + 2 * cfg.gla_qk + cfg.gla_v) // cfg.gla_v

    lat = jnp.concatenate([x[0], ctx[0]], axis=0)
    cvec = jnp.zeros((8, d), F32).at[0].set(c[0]).at[1].set(c_ctx)
    mods = modvec(cvec, w_mod, b_mod)

    for l in range(depth):
        with_ctx = l < depth - 1
        mod = mods[l]
        h = norm1(cfg, lat, norm1_g[l], mod)
        z_big = matmul_plain(h, w_big, l, tm=tm, tn=tn, name="w_in_big")
        z_small = matmul_plain(h, w_small, l, tm=tm, tn=w_small.shape[2], name="w_in_small")
        qn, kn, vs, bb, gcb, gct = gdn_prep(cfg, z_big, z_small, gdn_conv[l], alog_rows[l], dtb_rows[l])
        o_f, o_b = gdn_scan(cfg, *gdn_chunks(cfg, qn, kn, vs, bb, gcb, gct))
        a_all = head_norm(o_f, o_b, z_big, 3, gdn_norm[l], nh, 128, tr)
        (olf, olb), (ocf, ocb) = gla(cfg, z_big, z_small, gla_col0, wup[l], gbias[l])
        b_lat = head_norm(olf, olb, z_big, gla_gate_blk, gla_norm[l], cfg.gla_h, 256, tr)
        b_ctx = head_norm(ocf, ocb, z_big, gla_gate_blk, gla_norm[l], cfg.gla_h, 256, tr, gate_row0=cfg.seq)
        b_all = jnp.concatenate([b_lat, b_ctx], axis=0)
        vxx = hyena_prep(cfg, z_big, hy_col0, hy_conv_w[l], hy_conv_b[l])
        filt_w = prep_hyena_filter(hy_w1[l], hy_b1[l], hy_w2[l], hy_b2[l], hy_w3[l], hy_freq[l])
        c_lat, c_ctx_out = hyena(cfg, hc, vxx, filt_w, hy_skip[l], with_ctx)
        c_all = jnp.concatenate([c_lat, c_ctx_out], axis=0).astype(BF16)
        s = merge_branches(cfg, (a_all, b_all, c_all), z_small, wb, mu, mb, l, tn=tn)
        lat = matmul_resid(cfg, s, wo, l, lat, mod, 2, tn=tn, name="w_out")
        h2, gate_rep = norm2_route(cfg, lat, norm2_g[l], mod, wr_pad, rb_col, expand)
        act = matmul_moe_act(cfg, h2, w13, l, gate_rep)
        lat = matmul_resid(cfg, act, w2f, l, lat, mod, 5, tn=tn, name="moe_down")
    return final_norm(lat, final_g, cfg.seq, tr)[None]
```

```python
import functools
import math
from typing import NamedTuple

import numpy as np
import jax
import jax.numpy as jnp
from jax import lax
from jax.experimental import pallas as pl
from jax.experimental.pallas import tpu as pltpu

F32 = jnp.float32
BF16 = jnp.bfloat16

EPS = 1e-6
LANES = 128
V7X_VMEM_BYTES = 64 * 1024 * 1024
VMEM_LIMIT = (V7X_VMEM_BYTES * 13) // 16
SHORT_CONV = 3
GLA_CHUNK = 64
GLA_GATE_NORM = 16.0
N_EXPERTS = 16
N_GROUPS = 4
HY_EMB = 33
HY_FAST_DECAY = 0.3
HY_SLOW_DECAY = 1.5
HY_TARGET = 1e-2


class Cfg(NamedTuple):
    d: int
    seq: int
    ctx: int
    grid_w: int
    gdn_h: int
    gla_h: int
    hy_w: int
    merge_rank: int
    d_expert: int
    row_tile: int
    mm_tm: int

    @property
    def t(self):
        return self.seq + self.ctx

    @property
    def gdn_w(self):
        return self.gdn_h * 128

    @property
    def gla_qk(self):
        return self.gla_h * 128

    @property
    def gla_v(self):
        return self.gla_h * 256


def _cparams(sem):
    return pltpu.CompilerParams(dimension_semantics=sem, vmem_limit_bytes=VMEM_LIMIT)


def _split3(x):
    hi = x.astype(BF16)
    r1 = x - hi.astype(F32)
    mid = r1.astype(BF16)
    lo = (r1 - mid.astype(F32)).astype(BF16)
    return hi, mid, lo


def _dot(a, b):
    return jnp.dot(a, b, preferred_element_type=F32)


def _dot_sel(sel_bf16, x):
    hi, mid, lo = _split3(x)
    return _dot(sel_bf16, hi) + _dot(sel_bf16, mid) + _dot(sel_bf16, lo)


def _dot_x_sel(x, sel_bf16):
    hi, mid, lo = _split3(x)
    return _dot(hi, sel_bf16) + _dot(mid, sel_bf16) + _dot(lo, sel_bf16)


def _dot_hi(a, b):
    a1, a2, a3 = _split3(a)
    b1, b2, b3 = _split3(b)
    return (_dot(a1, b1) + (_dot(a1, b2) + _dot(a2, b1))
            + (_dot(a2, b2) + _dot(a1, b3) + _dot(a3, b1)))


def _dot_3(a, b):
    a1 = a.astype(BF16)
    a2 = (a - a1.astype(F32)).astype(BF16)
    b1 = b.astype(BF16)
    b2 = (b - b1.astype(F32)).astype(BF16)
    return _dot(a1, b1) + (_dot(a1, b2) + _dot(a2, b1))


def _silu(x):
    return x * jax.nn.sigmoid(x)


def _modvec_kernel(x_ref, w_ref, b_ref, o_ref):
    o_ref[...] = _dot(_silu(x_ref[...]), w_ref[...]) + b_ref[...]


def modvec(cvec, w_mod, b_mod, tn=512):
    depth, d, n = w_mod.shape
    return pl.pallas_call(
        _modvec_kernel,
        out_shape=jax.ShapeDtypeStruct((depth, 8, n), F32),
        grid=(depth, n // tn),
        in_specs=[pl.BlockSpec((8, d), lambda l, j: (0, 0)),
                  pl.BlockSpec((None, d, tn), lambda l, j: (l, 0, j)),
                  pl.BlockSpec((None, 1, tn), lambda l, j: (l, 0, j))],
        out_specs=pl.BlockSpec((None, 8, tn), lambda l, j: (l, 0, j)),
        compiler_params=_cparams(("parallel", "parallel")),
        name="modvec",
    )(cvec, w_mod, b_mod.reshape(depth, 1, n))


def _mod_row(mod_ref, is_ctx, idx, d):
    return mod_ref[pl.ds(is_ctx, 1), idx * d:(idx + 1) * d]


def _norm_mod(x, gain, shift, scale):
    y = x * lax.rsqrt(jnp.mean(x * x, axis=-1, keepdims=True) + EPS)
    return (y * gain) * (1.0 + scale) + shift


def _norm1_kernel(x_ref, g_ref, mod_ref, o_ref, *, d, n_lat_tiles):
    is_ctx = (pl.program_id(0) >= n_lat_tiles).astype(jnp.int32)
    h = _norm_mod(x_ref[...], g_ref[...], _mod_row(mod_ref, is_ctx, 0, d),
                  _mod_row(mod_ref, is_ctx, 1, d))
    o_ref[...] = h.astype(BF16)


def norm1(cfg, x, gain, mod):
    tr = cfg.row_tile
    t, d = x.shape
    return pl.pallas_call(
        functools.partial(_norm1_kernel, d=d, n_lat_tiles=cfg.seq // tr),
        out_shape=jax.ShapeDtypeStruct((t, d), BF16),
        grid=(t // tr,),
        in_specs=[pl.BlockSpec((tr, d), lambda i: (i, 0)),
                  pl.BlockSpec((1, d), lambda i: (0, 0)),
                  pl.BlockSpec((8, 6 * d), lambda i: (0, 0))],
        out_specs=pl.BlockSpec((tr, d), lambda i: (i, 0)),
        compiler_params=_cparams(("parallel",)),
        name="norm1",
    )(x, gain.reshape(1, d), mod)


def _route(sel_t, sc_t):
    per = N_EXPERTS // N_GROUPS
    grp_score = []
    for g in range(N_GROUPS):
        v = sel_t[g * per:(g + 1) * per]
        best = None
        for a in range(per):
            for b in range(a + 1, per):
                s = v[a] + v[b]
                best = s if best is None else jnp.maximum(best, s)
        grp_score.append(best)
    best_s, best_g = grp_score[0], jnp.zeros_like(grp_score[0])
    for g in range(1, N_GROUPS):
        better = grp_score[g] > best_s
        best_s = jnp.where(better, grp_score[g], best_s)
        best_g = jnp.where(better, float(g), best_g)
    picked = []
    for e in range(N_EXPERTS):
        g, i = divmod(e, per)
        rank = jnp.zeros_like(best_s)
        for j in range(per):
            if j == i:
                continue
            o = sel_t[g * per + j]
            ahead = (o >= sel_t[e]) if j < i else (o > sel_t[e])
            rank = rank + ahead.astype(F32)
        picked.append(jnp.where((best_g == float(g)) & (rank < 2.0), sc_t[e], 0.0))
    den = picked[0]
    for e in range(1, N_EXPERTS):
        den = den + picked[e]
    inv = 1.0 / den
    return [p * inv for p in picked]


def _norm2_kernel(x_ref, g_ref, mod_ref, wr_ref, rb_ref, ex_ref, o_ref, gate_ref, *, d,
                  n_lat_tiles):
    is_ctx = (pl.program_id(0) >= n_lat_tiles).astype(jnp.int32)
    h = _norm_mod(x_ref[...], g_ref[...], _mod_row(mod_ref, is_ctx, 3, d),
                  _mod_row(mod_ref, is_ctx, 4, d))
    o_ref[...] = h.astype(BF16)
    logits = _dot_3(h, wr_ref[...])
    lt = jnp.transpose(logits)
    sc = jax.nn.sigmoid(lt[0:N_EXPERTS, :])
    sel = sc + rb_ref[...]
    gate_rows = _route([sel[e:e + 1, :] for e in range(N_EXPERTS)],
                       [sc[e:e + 1, :] for e in range(N_EXPERTS)])
    rows = lax.broadcasted_iota(jnp.int32, lt.shape, 0)
    gt = jnp.zeros(lt.shape, F32)
    for e in range(N_EXPERTS):
        gt = jnp.where(rows == e, gate_rows[e], gt)
    gate = jnp.transpose(gt)
    gate_ref[...] = _dot_x_sel(gate, ex_ref[...])


def norm2_route(cfg, x, gain, mod, w_router_pad, rbias_col, expand):
    tr = cfg.row_tile
    t, d = x.shape
    return pl.pallas_call(
        functools.partial(_norm2_kernel, d=d, n_lat_tiles=cfg.seq // tr),
        out_shape=(jax.ShapeDtypeStruct((t, d), BF16),
                   jax.ShapeDtypeStruct((t, N_EXPERTS * LANES), F32)),
        grid=(t // tr,),
        in_specs=[pl.BlockSpec((tr, d), lambda i: (i, 0)),
                  pl.BlockSpec((1, d), lambda i: (0, 0)),
                  pl.BlockSpec((8, 6 * d), lambda i: (0, 0)),
                  pl.BlockSpec((d, LANES), lambda i: (0, 0)),
                  pl.BlockSpec((N_EXPERTS, 1), lambda i: (0, 0)),
                  pl.BlockSpec((LANES, N_EXPERTS * LANES), lambda i: (0, 0))],
        out_specs=(pl.BlockSpec((tr, d), lambda i: (i, 0)),
                   pl.BlockSpec((tr, N_EXPERTS * LANES), lambda i: (i, 0))),
        compiler_params=_cparams(("parallel",)),
        name="norm2_route",
    )(x, gain.reshape(1, d), mod, w_router_pad, rbias_col, expand)


def _final_norm_kernel(x_ref, g_ref, o_ref):
    x = x_ref[...]
    o_ref[...] = (x * lax.rsqrt(jnp.mean(x * x, axis=-1, keepdims=True) + EPS)) * g_ref[...]


def final_norm(x, gain, n_rows, tr):
    d = x.shape[1]
    return pl.pallas_call(
        _final_norm_kernel,
        out_shape=jax.ShapeDtypeStruct((n_rows, d), F32),
        grid=(n_rows // tr,),
        in_specs=[pl.BlockSpec((tr, d), lambda i: (i, 0)),
                  pl.BlockSpec((1, d), lambda i: (0, 0))],
        out_specs=pl.BlockSpec((tr, d), lambda i: (i, 0)),
        compiler_params=_cparams(("parallel",)),
        name="final_norm",
    )(x, gain.reshape(1, d))


def _mm_plain_kernel(a_ref, w_ref, o_ref):
    o_ref[...] = _dot(a_ref[...], w_ref[...]).astype(o_ref.dtype)


def matmul_plain(a, w, layer, *, tm, tn, n_cols=None, col0=0, out_dtype=F32, name="mm"):
    t, k = a.shape
    n_cols = w.shape[2] - col0 if n_cols is None else n_cols
    off = col0 // tn
    return pl.pallas_call(
        _mm_plain_kernel,
        out_shape=jax.ShapeDtypeStruct((t, n_cols), out_dtype),
        grid=(n_cols // tn, t // tm),
        in_specs=[pl.BlockSpec((tm, k), lambda j, i: (i, 0)),
                  pl.BlockSpec((None, k, tn), lambda j, i: (layer, 0, j + off))],
        out_specs=pl.BlockSpec((tm, tn), lambda j, i: (i, j)),
        compiler_params=_cparams(("parallel", "parallel")),
        name=name,
    )(a, w)


def _mm_resid_kernel(a_ref, w_ref, r_ref, mod_ref, o_ref, *, tm, n_lat):
    row = pl.program_id(1) * tm + lax.broadcasted_iota(jnp.int32, (tm, 1), 0)
    gate = jnp.where(row < n_lat, mod_ref[0:1, :], mod_ref[1:2, :])
    o_ref[...] = r_ref[...] + gate * _dot(a_ref[...], w_ref[...])


def matmul_resid(cfg, a, w, layer, resid, mod, idx, *, tn, name):
    t, k = a.shape
    d = w.shape[2]
    tm = cfg.mm_tm
    return pl.pallas_call(
        functools.partial(_mm_resid_kernel, tm=tm, n_lat=cfg.seq),
        out_shape=jax.ShapeDtypeStruct((t, d), F32),
        grid=(d // tn, t // tm),
        in_specs=[pl.BlockSpec((tm, k), lambda j, i: (i, 0)),
                  pl.BlockSpec((None, k, tn), lambda j, i: (layer, 0, j)),
                  pl.BlockSpec((tm, tn), lambda j, i: (i, j)),
                  pl.BlockSpec((8, tn), lambda j, i: (0, idx * (d // tn) + j))],
        out_specs=pl.BlockSpec((tm, tn), lambda j, i: (i, j)),
        input_output_aliases={2: 0},
        compiler_params=_cparams(("parallel", "parallel")),
        name=name,
    )(a, w, resid, mod)


def _mm_moe_act_kernel(a_ref, w1_ref, w3_ref, g_ref, o_ref, *, de, n_e):
    a = a_ref[...]
    for e in range(n_e):
        up = _dot(a, w1_ref[e])
        lin = _dot(a, w3_ref[e])
        g = g_ref[:, e * LANES:(e + 1) * LANES]
        g = jnp.concatenate([g] * (de // LANES), axis=1)
        o_ref[:, e * de:(e + 1) * de] = (_silu(up) * lin * g).astype(o_ref.dtype)


def matmul_moe_act(cfg, h, w1, w3, layer, gate_rep, *, n_e=2):
    t, k = h.shape
    de = cfg.d_expert
    tm = cfg.mm_tm
    wspec = pl.BlockSpec((None, n_e, k, de), lambda j, i: (layer, j, 0, 0))
    return pl.pallas_call(
        functools.partial(_mm_moe_act_kernel, de=de, n_e=n_e),
        out_shape=jax.ShapeDtypeStruct((t, N_EXPERTS * de), BF16),
        grid=(N_EXPERTS // n_e, t // tm),
        in_specs=[pl.BlockSpec((tm, k), lambda j, i: (i, 0)), wspec, wspec,
                  pl.BlockSpec((tm, n_e * LANES), lambda j, i: (i, j))],
        out_specs=pl.BlockSpec((tm, n_e * de), lambda j, i: (i, j)),
        compiler_params=_cparams(("parallel", "parallel")),
        name="moe_up",
    )(h, w1, w3, gate_rep)


def _merge_kernel(a0_ref, a1_ref, a2_ref, zg_ref, wb_ref, mu_ref, mb_ref, o_ref):
    zg = zg_ref[...].astype(BF16)
    acc = None
    for n, a_ref in enumerate((a0_ref, a1_ref, a2_ref)):
        y = _dot(a_ref[...], wb_ref[n])
        gate = jax.nn.sigmoid(_dot(zg, mu_ref[n]) + mb_ref[n])
        acc = gate * y if acc is None else acc + gate * y
    o_ref[...] = acc.astype(o_ref.dtype)


def merge_branches(cfg, outs, z_small, w_branch, merge_up, merge_b, layer, *, tn):
    t, bw = outs[0].shape
    d = w_branch.shape[3]
    r = cfg.merge_rank
    tm = cfg.mm_tm
    a_spec = pl.BlockSpec((tm, bw), lambda j, i: (i, 0))
    return pl.pallas_call(
        _merge_kernel,
        out_shape=jax.ShapeDtypeStruct((t, d), BF16),
        grid=(d // tn, t // tm),
        in_specs=[a_spec, a_spec, a_spec,
                  pl.BlockSpec((tm, r), lambda j, i: (i, 0)),
                  pl.BlockSpec((None, 3, bw, tn), lambda j, i: (layer, 0, 0, j)),
                  pl.BlockSpec((None, 3, r, tn), lambda j, i: (layer, 0, 0, j)),
                  pl.BlockSpec((None, 3, 1, tn), lambda j, i: (layer, 0, 0, j))],
        out_specs=pl.BlockSpec((tm, tn), lambda j, i: (i, j)),
        compiler_params=_cparams(("parallel", "parallel")),
        name="merge",
    )(outs[0], outs[1], outs[2], z_small, w_branch, merge_up, merge_b)


HALO_ROWS = 16


def _halo_specs(tr, width, col_block, n_rows):
    rb = tr // HALO_ROWS
    last = n_rows // HALO_ROWS - 1
    main = pl.BlockSpec((tr, width), lambda i: (i, col_block))
    prev = pl.BlockSpec((HALO_ROWS, width), lambda i: (jnp.maximum(i * rb - 1, 0), col_block))
    nxt = pl.BlockSpec((HALO_ROWS, width), lambda i: (jnp.minimum((i + 1) * rb, last), col_block))
    return main, prev, nxt


def _conv3(z_ref, prev_ref, next_ref, w_ref, n_lat_tiles):
    i = pl.program_id(0)
    x = z_ref[...].astype(F32)
    tr = x.shape[0]
    has_prev = jnp.logical_and(i != 0, i != n_lat_tiles).astype(F32)
    has_next = jnp.logical_and(i != n_lat_tiles - 1, i != n_lat_tiles).astype(F32)
    row = lax.broadcasted_iota(jnp.int32, (tr, 1), 0)
    halo_prev = prev_ref[HALO_ROWS - 1:HALO_ROWS, :].astype(F32) * has_prev
    halo_next = next_ref[0:1, :].astype(F32) * has_next
    x_prev = jnp.where(row == 0, halo_prev, pltpu.roll(x, 1, axis=0))
    x_next = jnp.where(row == tr - 1, halo_next, pltpu.roll(x, tr - 1, axis=0))
    return x_prev * w_ref[0:1, :] + x * w_ref[1:2, :] + x_next * w_ref[2:3, :]


def _tri_masks(n):
    r = lax.broadcasted_iota(jnp.int32, (n, n), 0)
    c = lax.broadcasted_iota(jnp.int32, (n, n), 1)
    return r, c


def _gdn_prep_kernel(z_ref, zp_ref, zn_ref, s_ref, cw_ref, alog_ref, dtb_ref,
                     q_ref, k_ref, v_ref, bb_ref, gcb_ref, gct_ref, *, n_heads, n_lat_tiles):
    tr = z_ref.shape[0]
    hw = n_heads * 128
    y = _silu(_conv3(z_ref, zp_ref, zn_ref, cw_ref, n_lat_tiles))
    for h in range(n_heads):
        q = y[:, h * 128:(h + 1) * 128]
        k = y[:, hw + h * 128:hw + (h + 1) * 128]
        q = q * (lax.rsqrt(jnp.sum(q * q, axis=-1, keepdims=True) + EPS) * (128.0 ** -0.5))
        k = k * lax.rsqrt(jnp.sum(k * k, axis=-1, keepdims=True) + EPS)
        q_ref[h] = q.astype(BF16)
        k_ref[h] = k.astype(BF16)
        v_ref[h] = y[:, 2 * hw + h * 128:2 * hw + (h + 1) * 128].astype(BF16)
    s = s_ref[...]
    nh2 = 2 * n_heads
    beta = jax.nn.sigmoid(s)
    g = -jnp.exp(alog_ref[...]) * jax.nn.softplus(s + dtb_ref[...])
    r, c = _tri_masks(tr)
    incl_lo = (c <= r).astype(BF16)
    incl_up = (c >= r).astype(BF16)
    lane = lax.broadcasted_iota(jnp.int32, (tr, LANES), 1)
    fwd_lane = lane < nh2 + n_heads
    gc = jnp.where(fwd_lane, _dot_sel(incl_lo, g), _dot_sel(incl_up, g))
    for ch in range(nh2):
        bb_ref[ch] = jnp.broadcast_to(beta[:, ch:ch + 1], (tr, LANES))
        gcb_ref[ch] = jnp.broadcast_to(gc[:, nh2 + ch:nh2 + ch + 1], (tr, LANES))
    gct_ref[...] = jnp.transpose(gc)[nh2:2 * nh2, :]


def gdn_prep(cfg, z_big, z_small, conv_w, alog_row, dtb_row):
    tr = cfg.row_tile
    t = z_big.shape[0]
    nh = cfg.gdn_h
    hw = nh * 128
    n_tiles = t // tr
    main, prev, nxt = _halo_specs(tr, 3 * hw, 0, t)
    head_out = jax.ShapeDtypeStruct((nh, t, 128), BF16)
    head_spec = pl.BlockSpec((nh, tr, 128), lambda i: (0, i, 0))
    col_out = jax.ShapeDtypeStruct((2 * nh, t, LANES), F32)
    col_spec = pl.BlockSpec((2 * nh, tr, LANES), lambda i: (0, i, 0))
    return pl.pallas_call(
        functools.partial(_gdn_prep_kernel, n_heads=nh, n_lat_tiles=cfg.seq // tr),
        out_shape=(head_out, head_out, head_out, col_out, col_out,
                   jax.ShapeDtypeStruct((n_tiles, 2 * nh, tr), F32)),
        grid=(n_tiles,),
        in_specs=[main, prev, nxt,
                  pl.BlockSpec((tr, LANES), lambda i: (i, cfg.merge_rank // LANES)),
                  pl.BlockSpec((SHORT_CONV, 3 * hw), lambda i: (0, 0)),
                  pl.BlockSpec((1, LANES), lambda i: (0, 0)),
                  pl.BlockSpec((1, LANES), lambda i: (0, 0))],
        out_specs=(head_spec, head_spec, head_spec, col_spec, col_spec,
                   pl.BlockSpec((None, 2 * nh, tr), lambda i: (i, 0, 0))),
        compiler_params=_cparams(("parallel",)),
        name="gdn_prep",
    )(z_big, z_big, z_big, z_small, conv_w, alog_row, dtb_row)


def _unit_tri_inverses(n_mats, r, c):
    n = n_mats[0].shape[0]
    eye = (r == c).astype(F32)

    def same_block(b):
        sh = int(math.log2(b))
        return (r >> sh) == (c >> sh)

    blk = same_block(8)
    pfs = [jnp.where(blk, -m, 0.0) for m in n_mats]
    ps = [pf.astype(BF16) for pf in pfs]
    p2 = [_dot(p, p).astype(BF16) for p in ps]
    p4 = [_dot(x, x).astype(BF16) for x in p2]
    ts = [eye + pf for pf in pfs]
    ts = [t + _dot(t.astype(BF16), x) for t, x in zip(ts, p2)]
    ts = [t + _dot(t.astype(BF16), x) for t, x in zip(ts, p4)]
    b = 8
    while b < n:
        sel = jnp.logical_and(same_block(2 * b), jnp.logical_not(same_block(b)))
        offs = [jnp.where(sel, m, 0.0).astype(BF16) for m in n_mats]
        tbs = [t.astype(BF16) for t in ts]
        xs = [_dot(tb, off).astype(BF16) for tb, off in zip(tbs, offs)]
        ts = [t - _dot(x, tb) for t, x, tb in zip(ts, xs, tbs)]
        b *= 2
    return ts, eye


GDN_HEADS_PER_STEP = 2


def _gdn_chunk_kernel(q_ref, k_ref, v_ref, bf_ref, bb_ref, gf_ref, gb_ref, gct_ref,
                      uw_ref, aq_ref, qk_ref, gl_ref, *, n_heads):
    hps = q_ref.shape[0]
    h0 = pl.program_id(1) * hps
    tr = q_ref.shape[1]
    r, c = _tri_masks(tr)
    nt = (((1,), (1,)), ((), ()))
    masks = (((c <= r), (c < r)), ((c >= r), (c > r)))
    chains = [(j, d) for j in range(hps) for d in range(2)]
    k_b = [k_ref[j] for j in range(hps)]
    k_f = [x.astype(F32) for x in k_b]
    a_qk = [lax.dot_general(q_ref[j], k_b[j], nt, preferred_element_type=F32) for j in range(hps)]
    beta, gc, gam, kb, n_mats = {}, {}, {}, {}, []
    for j, d in chains:
        beta[j, d] = (bf_ref, bb_ref)[d][j]
        gc[j, d] = (gf_ref, gb_ref)[d][j]
        gc_row = gct_ref[pl.ds(d * n_heads + h0 + j, 1), :]
        gc_col = jnp.concatenate([gc[j, d]] * (tr // LANES), axis=1)
        gam[j, d] = jnp.exp(jnp.where(masks[d][0], gc_col - gc_row, -jnp.inf))
        kb[j, d] = k_f[j] * beta[j, d]
        a_kk = lax.dot_general(kb[j, d].astype(BF16), k_b[j], nt, preferred_element_type=F32)
        n_mats.append(jnp.where(masks[d][1], a_kk * gam[j, d], 0.0))
    t_invs, eye = _unit_tri_inverses(n_mats, r, c)
    for (j, d), t_inv in zip(chains, t_invs):
        e = jnp.exp(gc[j, d])
        rhs = jnp.concatenate([v_ref[j].astype(F32) * beta[j, d], kb[j, d] * e], axis=1)
        sol = rhs + _dot((t_inv - eye).astype(BF16), rhs.astype(BF16))
        g_last = gc[j, d][tr - 1:tr, :] if d == 0 else gc[j, d][0:1, :]
        uw_ref[d, j] = sol.astype(BF16)
        aq_ref[d, j] = (a_qk[j] * gam[j, d]).astype(BF16)
        qk_ref[d, j] = jnp.concatenate([q_ref[j].astype(F32) * e,
                                        k_f[j] * jnp.exp(g_last - gc[j, d])], axis=1).astype(BF16)
        gl_ref[d, j] = jnp.broadcast_to(jnp.exp(g_last), (8, LANES))


def gdn_chunks(cfg, qn, kn, vs, bb, gcb, gct):
    tr = cfg.row_tile
    nh, t, _ = qn.shape
    hps = GDN_HEADS_PER_STEP
    assert nh % hps == 0
    n_tiles = t // tr
    head = pl.BlockSpec((hps, tr, 128), lambda i, h: (h, i, 0))
    head_b = pl.BlockSpec((hps, tr, 128), lambda i, h: (h + nh // hps, i, 0))
    big = jax.ShapeDtypeStruct((2, nh, t, 2 * 128), BF16)
    aq = jax.ShapeDtypeStruct((2, nh, t, tr), BF16)
    return pl.pallas_call(
        functools.partial(_gdn_chunk_kernel, n_heads=nh),
        out_shape=(big, aq, big, jax.ShapeDtypeStruct((2, nh, n_tiles * 8, LANES), F32)),
        grid=(n_tiles, nh // hps),
        in_specs=[head, head, head, head, head_b, head, head_b,
                  pl.BlockSpec((None, 2 * nh, tr), lambda i, h: (i, 0, 0))],
        out_specs=(pl.BlockSpec((2, hps, tr, 256), lambda i, h: (0, h, i, 0)),
                   pl.BlockSpec((2, hps, tr, tr), lambda i, h: (0, h, i, 0)),
                   pl.BlockSpec((2, hps, tr, 256), lambda i, h: (0, h, i, 0)),
                   pl.BlockSpec((2, hps, 8, LANES), lambda i, h: (0, h, i, 0))),
        compiler_params=_cparams(("parallel", "parallel")),
        name="gdn_chunks",
    )(qn, kn, vs, bb, bb, gcb, gcb, gct)


def _gdn_scan_kernel(uwf_ref, aqf_ref, qkf_ref, glf_ref, uwb_ref, aqb_ref, qkb_ref, glb_ref,
                     of_ref, ob_ref, s_ref, *, n_heads):
    @pl.when(pl.program_id(0) == 0)
    def _():
        s_ref[...] = jnp.zeros_like(s_ref)

    tn = (((0,), (0,)), ((), ()))
    for d, (uw_ref, aq_ref, qk_ref, gl_ref, o_ref) in enumerate(
            ((uwf_ref, aqf_ref, qkf_ref, glf_ref, of_ref), (uwb_ref, aqb_ref, qkb_ref, glb_ref, ob_ref))):
        for h in range(n_heads):
            st = s_ref[d, h]
            st_b = st.astype(BF16)
            uw = uw_ref[h]
            qk = qk_ref[h]
            v_new = uw[:, 0:128].astype(F32) - _dot(uw[:, 128:256], st_b)
            v_new_b = v_new.astype(BF16)
            o_ref[:, h * 128:(h + 1) * 128] = (
                _dot(qk[:, 0:128], st_b) + _dot(aq_ref[h], v_new_b)).astype(o_ref.dtype)
            s_ref[d, h] = gl_ref[h][0:1, :] * st + lax.dot_general(
                qk[:, 128:256], v_new_b, tn, preferred_element_type=F32)


def gdn_scan(cfg, uw, aq, qk, gl):
    tr = cfg.row_tile
    _, nh, t, _ = uw.shape
    n_tiles = t // tr
    last = n_tiles - 1

    def fwd(s):
        return jnp.where(s == 0, last, s - 1)

    def bwd(s):
        return jnp.where(s == 0, last, last - s)

    def specs(d, order):
        return [pl.BlockSpec((None, nh, tr, 256), lambda s: (d, 0, order(s), 0)),
                pl.BlockSpec((None, nh, tr, tr), lambda s: (d, 0, order(s), 0)),
                pl.BlockSpec((None, nh, tr, 256), lambda s: (d, 0, order(s), 0)),
                pl.BlockSpec((None, nh, 8, LANES), lambda s: (d, 0, order(s), 0))]

    out = jax.ShapeDtypeStruct((t, nh * 128), BF16)
    return pl.pallas_call(
        functools.partial(_gdn_scan_kernel, n_heads=nh),
        out_shape=(out, out),
        grid=(n_tiles,),
        in_specs=specs(0, fwd) + specs(1, bwd),
        out_specs=(pl.BlockSpec((tr, nh * 128), lambda s: (fwd(s), 0)),
                   pl.BlockSpec((tr, nh * 128), lambda s: (bwd(s), 0))),
        scratch_shapes=[pltpu.VMEM((2, nh, 128, 128), F32)],
        compiler_params=_cparams(("arbitrary",)),
        name="gdn_scan",
    )(uw, aq, qk, gl, uw, aq, qk, gl)


def _head_norm_kernel(of_ref, ob_ref, gate_ref, w_ref, o_ref, *, n_heads, dv):
    for h in range(n_heads):
        sl = slice(h * dv, (h + 1) * dv)
        o = of_ref[:, sl].astype(F32) + ob_ref[:, sl].astype(F32)
        o = o * lax.rsqrt(jnp.mean(o * o, axis=-1, keepdims=True) + EPS) * w_ref[...]
        o_ref[:, sl] = (o * _silu(gate_ref[:, sl].astype(F32))).astype(o_ref.dtype)


def head_norm(o_f, o_b, z_big, gate_col_block, norm_w, n_heads, dv, tr, gate_row0=0):
    t, w = o_f.shape
    spec = pl.BlockSpec((tr, w), lambda i: (i, 0))
    rb0 = gate_row0 // tr
    return pl.pallas_call(
        functools.partial(_head_norm_kernel, n_heads=n_heads, dv=dv),
        out_shape=jax.ShapeDtypeStruct((t, w), BF16),
        grid=(t // tr,),
        in_specs=[spec, spec, pl.BlockSpec((tr, w), lambda i: (i + rb0, gate_col_block)),
                  pl.BlockSpec((1, dv), lambda i: (0, 0))],
        out_specs=spec,
        compiler_params=_cparams(("parallel",)),
        name="head_norm",
    )(o_f, o_b, z_big, norm_w.reshape(1, dv))


GLA_TILE = 128
GLA_LR_LANE0 = 32


def _gla_direction(d, qk_ref, v_ref, sm_ref, wup_ref, gb_ref, o_ref, st_ref, n_heads):
    tr = GLA_TILE
    qkw = n_heads * 128
    r, c = _tri_masks(tr)
    sh = int(math.log2(GLA_CHUNK))
    same = (r >> sh) == (c >> sh)
    mask = jnp.logical_and(same, (c <= r) if d == 0 else (c >= r))
    logit = _dot_hi(sm_ref[...], wup_ref[:, d * qkw:(d + 1) * qkw]) + gb_ref[:, d * qkw:(d + 1) * qkw]
    g = jax.nn.log_sigmoid(logit) * (1.0 / GLA_GATE_NORM)
    b = _dot_sel(jnp.where(mask, 1.0, 0.0).astype(BF16), g)
    nt = (((1,), (1,)), ((), ()))
    tn = (((0,), (0,)), ((), ()))
    chunks = range(tr // GLA_CHUNK) if d == 0 else range(tr // GLA_CHUNK - 1, -1, -1)
    for h in range(n_heads):
        bq = b[:, h * 128:(h + 1) * 128]
        q = qk_ref[:, h * 128:(h + 1) * 128].astype(F32)
        k = qk_ref[:, qkw + h * 128:qkw + (h + 1) * 128].astype(F32)
        v_b = v_ref[:, h * 256:(h + 1) * 256].astype(BF16)
        qe = (q * jnp.exp(bq) * (128.0 ** -0.5)).astype(BF16)
        kinv = (k * jnp.exp(-bq)).astype(BF16)
        a = jnp.where(mask, lax.dot_general(qe, kinv, nt, preferred_element_type=F32), 0.0)
        o_intra = _dot(a.astype(BF16), v_b)
        for ci in chunks:
            lo = ci * GLA_CHUNK
            last = lo + GLA_CHUNK - 1 if d == 0 else lo
            b_last = bq[last:last + 1, :]
            kdec = (k[lo:lo + GLA_CHUNK] * jnp.exp(b_last - bq[lo:lo + GLA_CHUNK])).astype(BF16)
            ds = lax.dot_general(kdec, v_b[lo:lo + GLA_CHUNK], tn, preferred_element_type=F32)
            st = st_ref[d, h]
            o_ref[lo:lo + GLA_CHUNK, h * 256:(h + 1) * 256] = (
                o_intra[lo:lo + GLA_CHUNK] + _dot(qe[lo:lo + GLA_CHUNK], st.astype(BF16))).astype(o_ref.dtype)
            dec = jnp.transpose(jnp.broadcast_to(jnp.exp(b_last), (128, 128)))
            st_ref[d, h] = jnp.concatenate([dec, dec], axis=1) * st + ds


def _gla_kernel(qkf_ref, vf_ref, smf_ref, qkb_ref, vb_ref, smb_ref, wup_ref, gb_ref, s0_ref,
                of_ref, ob_ref, sfin_ref, st_ref, *, n_heads):
    @pl.when(pl.program_id(0) == 0)
    def _():
        st_ref[...] = s0_ref[...]

    _gla_direction(0, qkf_ref, vf_ref, smf_ref, wup_ref, gb_ref, of_ref, st_ref, n_heads)
    _gla_direction(1, qkb_ref, vb_ref, smb_ref, wup_ref, gb_ref, ob_ref, st_ref, n_heads)

    @pl.when(pl.program_id(0) == pl.num_programs(0) - 1)
    def _():
        sfin_ref[...] = st_ref[...]


def _gla_call(cfg, n_steps, arrays, spec_fn, out_rows_shape, out_spec_fn, wup, gbias, s0, name):
    nh = cfg.gla_h
    fwd = lambda s: s
    bwd = lambda s: n_steps - 1 - s
    full = lambda shape: pl.BlockSpec(shape, lambda s: (0,) * len(shape))
    z_view, zs_view = arrays
    st_shape = (2, nh, 128, 256)
    out = jax.ShapeDtypeStruct(out_rows_shape, BF16)
    return pl.pallas_call(
        functools.partial(_gla_kernel, n_heads=nh),
        out_shape=(out, out, jax.ShapeDtypeStruct(st_shape, F32)),
        grid=(n_steps,),
        in_specs=spec_fn(fwd) + spec_fn(bwd) + [full(wup.shape), full(gbias.shape), full(st_shape)],
        out_specs=(out_spec_fn(fwd), out_spec_fn(bwd), full(st_shape)),
        scratch_shapes=[pltpu.VMEM(st_shape, F32)],
        compiler_params=_cparams(("arbitrary",)),
        name=name,
    )(z_view, z_view, zs_view, z_view, z_view, zs_view, wup, gbias, s0)


def gla(cfg, z_big, z_small, col0, wup, gbias):
    nh = cfg.gla_h
    qk2, vw = 2 * nh * 128, nh * 256
    zw, sw = z_big.shape[1], z_small.shape[1]
    w = cfg.grid_w
    rows = cfg.seq // w
    assert rows == GLA_TILE and cfg.ctx % GLA_TILE == 0
    assert zw % qk2 == 0 and zw % vw == 0 and col0 % qk2 == 0 and (col0 + qk2) % vw == 0
    small_blk = cfg.merge_rank // LANES
    s0 = jnp.zeros((2, nh, 128, 256), F32)

    rb0 = cfg.seq // GLA_TILE
    ctx_specs = lambda order: [
        pl.BlockSpec((GLA_TILE, qk2), lambda s: (rb0 + order(s), col0 // qk2)),
        pl.BlockSpec((GLA_TILE, vw), lambda s: (rb0 + order(s), (col0 + qk2) // vw)),
        pl.BlockSpec((GLA_TILE, LANES), lambda s: (rb0 + order(s), small_blk))]
    ctx_out = lambda order: pl.BlockSpec((GLA_TILE, vw), lambda s: (order(s), 0))
    ocf, ocb, s_ctx = _gla_call(cfg, cfg.ctx // GLA_TILE, (z_big, z_small), ctx_specs,
                                (cfg.ctx, vw), ctx_out, wup, gbias, s0, "gla_ctx")

    t = z_big.shape[0]
    zv = z_big.reshape(t // w, w * zw)
    zsv = z_small.reshape(t // w, w * sw)
    lat_specs = lambda order: [
        pl.BlockSpec((GLA_TILE, qk2), lambda s: (0, (order(s) * zw + col0) // qk2)),
        pl.BlockSpec((GLA_TILE, vw), lambda s: (0, (order(s) * zw + col0 + qk2) // vw)),
        pl.BlockSpec((GLA_TILE, LANES), lambda s: (0, order(s) * (sw // LANES) + small_blk))]
    lat_out = lambda order: pl.BlockSpec((GLA_TILE, vw), lambda s: (0, order(s)))
    olf, olb, _ = _gla_call(cfg, w, (zv, zsv), lat_specs, (rows, w * vw), lat_out,
                            wup, gbias, s_ctx, "gla_lat")
    return (olf.reshape(cfg.seq, vw), olb.reshape(cfg.seq, vw)), (ocf, ocb)


FFT_G = 8


def _hy_prep_kernel(z_ref, zp_ref, zn_ref, w_ref, b_ref, v_ref, x1_ref, x2_ref, *, n_lat_tiles, hw):
    y = _conv3(z_ref, zp_ref, zn_ref, w_ref, n_lat_tiles) + b_ref[...]
    for p, o_ref in enumerate((v_ref, x1_ref, x2_ref)):
        o_ref[...] = y[:, p * hw:(p + 1) * hw]


def hyena_prep(cfg, z_hy, conv_w, conv_b):
    tr, hw = cfg.row_tile, cfg.hy_w
    t = z_hy.shape[0]
    main, prev, nxt = _halo_specs(tr, 3 * hw, 0, t)
    out = jax.ShapeDtypeStruct((t, hw), F32)
    ospec = pl.BlockSpec((tr, hw), lambda i: (i, 0))
    return pl.pallas_call(
        functools.partial(_hy_prep_kernel, n_lat_tiles=cfg.seq // tr, hw=hw),
        out_shape=(out, out, out),
        grid=(t // tr,),
        in_specs=[main, prev, nxt,
                  pl.BlockSpec((SHORT_CONV, 3 * hw), lambda i: (0, 0)),
                  pl.BlockSpec((1, 3 * hw), lambda i: (0, 0))],
        out_specs=(ospec, ospec, ospec),
        compiler_params=_cparams(("parallel",)),
        name="hyena_prep",
    )(z_hy, z_hy, z_hy, conv_w, conv_b.reshape(1, 3 * hw))


def _hy_taps_kernel(f_ref, w1_ref, b1_ref, w2_ref, b2_ref, fr_ref, w30_ref, w31_ref, rate_ref,
                    o0_ref, o1_ref, *, length, tt):
    feats = f_ref[...]
    hid = jnp.sin(fr_ref[0:1, :] * (_dot_3(feats, w1_ref[...]) + b1_ref[...]))
    hid = jnp.sin(fr_ref[1:2, :] * (_dot_3(hid, w2_ref[...]) + b2_ref[...]))
    n = pl.program_id(0) * tt + lax.broadcasted_iota(jnp.int32, (tt, 1), 0)
    window = jnp.where(n == length, 0.0, jnp.exp(-feats[:, 0:1] * rate_ref[...]))
    for w3_ref, o_ref in ((w30_ref, o0_ref), (w31_ref, o1_ref)):
        o_ref[...] = (_dot_3(hid, w3_ref[...]) * window).astype(o_ref.dtype)


def hyena_taps(feats, w1p, b1p, w2p, b2p, freqp, w3p, rates, length, hw):
    tt = min(512, length)
    n_half = length // tt
    out = jax.ShapeDtypeStruct((2 * length, hw), BF16)
    ospec = pl.BlockSpec((tt, hw), lambda j: (j, 0))
    return pl.pallas_call(
        functools.partial(_hy_taps_kernel, length=length, tt=tt),
        out_shape=(out, out),
        grid=(2 * n_half,),
        in_specs=[pl.BlockSpec((tt, LANES), lambda j: (j, 0)),
                  pl.BlockSpec((LANES, LANES), lambda j: (0, 0)),
                  pl.BlockSpec((1, LANES), lambda j: (0, 0)),
                  pl.BlockSpec((LANES, LANES), lambda j: (0, 0)),
                  pl.BlockSpec((1, LANES), lambda j: (0, 0)),
                  pl.BlockSpec((2, LANES), lambda j: (0, 0)),
                  pl.BlockSpec((LANES, hw), lambda j: (0, j // n_half)),
                  pl.BlockSpec((LANES, hw), lambda j: (0, 2 + j // n_half)),
                  pl.BlockSpec((1, hw), lambda j: (0, 0))],
        out_specs=(ospec, ospec),
        compiler_params=_cparams(("parallel",)),
        name="hyena_taps",
    )(feats, w1p, b1p, w2p, b2p, freqp, w3p, w3p, rates)


def _fft1_kernel(x_ref, t_ref, yr_ref, yi_ref, *, c, p):
    for g in range(FFT_G):
        y = _dot(t_ref[g], x_ref[:, g * c:(g + 1) * c].astype(BF16))
        yr_ref[g] = y[0:p].astype(yr_ref.dtype)
        yi_ref[g] = y[p:2 * p].astype(yi_ref.dtype)


def fft_step1(x, table, n_rows, c):
    p = table.shape[1] // 2
    xv = x.reshape(x.shape[0] // LANES, LANES * c)
    out = jax.ShapeDtypeStruct((LANES, p, c), BF16)
    return pl.pallas_call(
        functools.partial(_fft1_kernel, c=c, p=p),
        out_shape=(out, out),
        grid=(LANES // FFT_G,),
        in_specs=[pl.BlockSpec((n_rows, FFT_G * c), lambda j: (0, j)),
                  pl.BlockSpec((FFT_G, 2 * p, n_rows), lambda j: (j, 0, 0))],
        out_specs=(pl.BlockSpec((FFT_G, p, c), lambda j: (j, 0, 0)),) * 2,
        compiler_params=_cparams(("parallel",)),
        name="fft_step1",
    )(xv, table)


def _fft2_kernel(yr_ref, yi_ref, m_ref, zr_ref, zi_ref, *, c):
    for g in range(FFT_G):
        s = jnp.concatenate([yr_ref[:, g * c:(g + 1) * c], yi_ref[:, g * c:(g + 1) * c]], axis=0)
        z = _dot(m_ref[...], s)
        zr_ref[g] = z[0:LANES].astype(zr_ref.dtype)
        zi_ref[g] = z[LANES:2 * LANES].astype(zi_ref.dtype)


def fft_step2(yr, yi, m2):
    _, p, c = yr.shape
    out = jax.ShapeDtypeStruct((p, LANES, c), BF16)
    spec = pl.BlockSpec((LANES, FFT_G * c), lambda j: (0, j))
    return pl.pallas_call(
        functools.partial(_fft2_kernel, c=c),
        out_shape=(out, out),
        grid=(p // FFT_G,),
        in_specs=[spec, spec, pl.BlockSpec((2 * LANES, 2 * LANES), lambda j: (0, 0))],
        out_specs=(pl.BlockSpec((FFT_G, LANES, c), lambda j: (j, 0, 0)),) * 2,
        compiler_params=_cparams(("parallel",)),
        name="fft_step2",
    )(yr.reshape(LANES, p * c), yi.reshape(LANES, p * c), m2)


def _ifft1_kernel(zr_ref, zi_ref, hr_ref, hi_ref, m_ref, vr_ref, vi_ref):
    for g in range(FFT_G):
        zr, zi = zr_ref[g].astype(F32), zi_ref[g].astype(F32)
        hr, hi = hr_ref[g].astype(F32), hi_ref[g].astype(F32)
        s = jnp.concatenate([zr * hr - zi * hi, zr * hi + zi * hr], axis=0).astype(BF16)
        v = _dot(m_ref[...], s)
        vr_ref[g] = v[0:LANES].astype(vr_ref.dtype)
        vi_ref[g] = v[LANES:2 * LANES].astype(vi_ref.dtype)


def ifft_step1(zr, zi, hr, hi, ma):
    p, _, c = zr.shape
    out = jax.ShapeDtypeStruct((p, LANES, c), BF16)
    spec = pl.BlockSpec((FFT_G, LANES, c), lambda j: (j, 0, 0))
    return pl.pallas_call(
        _ifft1_kernel,
        out_shape=(out, out),
        grid=(p // FFT_G,),
        in_specs=[spec, spec, spec, spec, pl.BlockSpec((2 * LANES, 2 * LANES), lambda j: (0, 0))],
        out_specs=(spec, spec),
        compiler_params=_cparams(("parallel",)),
        name="ifft_step1",
    )(zr, zi, hr, hi, ma)


def _ifft2_kernel(vr_ref, vi_ref, t_ref, x_ref, u_ref, sk_ref, o_ref, *, c):
    for g in range(FFT_G):
        sl = slice(g * c, (g + 1) * c)
        s = jnp.concatenate([vr_ref[:, sl], vi_ref[:, sl]], axis=0)
        y = _dot(t_ref[g], s)
        u = u_ref[:, sl]
        o_ref[:, sl] = x_ref[:, sl] * (y + sk_ref[...] * u)


def ifft_step2_gate(vr, vi, table, gate_x, u, skip_row):
    p, _, c = vr.shape
    half = p // 2
    xv = gate_x.reshape(gate_x.shape[0] // LANES, LANES * c)
    uv = u.reshape(u.shape[0] // LANES, LANES * c)
    vspec = pl.BlockSpec((p, FFT_G * c), lambda j: (0, j))
    tspec = pl.BlockSpec((half, FFT_G * c), lambda j: (0, j))
    out = pl.pallas_call(
        functools.partial(_ifft2_kernel, c=c),
        out_shape=jax.ShapeDtypeStruct((half, LANES * c), F32),
        grid=(LANES // FFT_G,),
        in_specs=[vspec, vspec, pl.BlockSpec((FFT_G, half, 2 * p), lambda j: (j, 0, 0)),
                  tspec, tspec, pl.BlockSpec((1, c), lambda j: (0, 0))],
        out_specs=tspec,
        compiler_params=_cparams(("parallel",)),
        name="ifft_step2",
    )(vr.reshape(p, LANES * c), vi.reshape(p, LANES * c), table, xv, uv, skip_row)
    return out.reshape(half * LANES, c)


def _hy_ctx_kernel(v_ref, x1_ref, x2_ref, t0_ref, t1_ref, sk_ref, f_ref, g_ref, o_ref, *, n):
    def conv(u, taps_ref):
        us = _dot(f_ref[:, 0:n], u.astype(BF16))
        hs = _dot(f_ref[...], taps_ref[...])
        ur, ui, hr, hi = us[0:2 * n], us[2 * n:4 * n], hs[0:2 * n], hs[2 * n:4 * n]
        prod = jnp.concatenate([ur * hr - ui * hi, ur * hi + ui * hr], axis=0).astype(BF16)
        return _dot(g_ref[...], prod)

    v = v_ref[...]
    y = x1_ref[...] * (conv(v, t0_ref) + sk_ref[0:1, :] * v)
    o_ref[...] = x2_ref[...] * (conv(y, t1_ref) + sk_ref[1:2, :] * y)


def hyena_ctx(cfg, vxx, taps0, taps1, skip, fmat, gmat):
    n, hw = cfg.ctx, cfg.hy_w
    cb = 256
    rb = cfg.seq // n
    part = pl.BlockSpec((n, cb), lambda j: (rb, j))
    tspec = pl.BlockSpec((2 * n, cb), lambda j: (0, j))
    return pl.pallas_call(
        functools.partial(_hy_ctx_kernel, n=n),
        out_shape=jax.ShapeDtypeStruct((n, hw), F32),
        grid=(hw // cb,),
        in_specs=[part, part, part, tspec, tspec,
                  pl.BlockSpec((2, cb), lambda j: (0, j)),
                  pl.BlockSpec((4 * n, 2 * n), lambda j: (0, 0)),
                  pl.BlockSpec((n, 4 * n), lambda j: (0, 0))],
        out_specs=pl.BlockSpec((n, cb), lambda j: (0, j)),
        compiler_params=_cparams(("parallel",)),
        name="hyena_ctx",
    )(vxx[0], vxx[1], vxx[2], taps0, taps1, skip, fmat, gmat)


class HyenaConsts(NamedTuple):
    feats: jax.Array
    rates: jax.Array
    t1_data: jax.Array
    t1_taps: jax.Array
    m2: jax.Array
    ma: jax.Array
    tb: jax.Array
    feats_ctx: jax.Array
    f_ctx: jax.Array
    g_ctx: jax.Array


def _features(length):
    n = np.arange(2 * length, dtype=np.float64)
    pos = np.where(n < length, n, 2 * length - n)
    bands = (HY_EMB - 1) // 2
    f = np.linspace(1e-4, bands - 1, bands)
    omega = (2.0 * math.pi / length) * pos
    feats = np.zeros((2 * length, LANES), np.float64)
    feats[:, 0] = pos / (length - 1)
    feats[:, 1:1 + bands] = np.cos(omega[:, None] * f[None, :])
    feats[:, 1 + bands:1 + 2 * bands] = -np.sin(omega[:, None] * f[None, :])
    return jnp.asarray(feats, F32)


def hyena_consts(cfg):
    length, n, hw = cfg.seq, cfg.ctx, cfg.hy_w
    big_n = 2 * length
    p = big_n // LANES
    rates = np.abs(np.linspace(math.log(HY_TARGET) / HY_FAST_DECAY, math.log(HY_TARGET) / HY_SLOW_DECAY, hw))
    b = np.arange(LANES)[:, None, None]
    k1 = np.arange(p)[None, :, None]
    a = np.arange(p)[None, None, :]
    ang = 2.0 * math.pi * ((k1 * (LANES * a + b)) % big_n) / big_n
    t1 = np.concatenate([np.cos(ang), -np.sin(ang)], axis=1)
    tb = np.concatenate([np.cos(ang), -np.sin(ang)], axis=1).transpose(0, 2, 1)[:, :p // 2, :] / big_n
    k2 = np.arange(LANES)
    a128 = 2.0 * math.pi * ((k2[:, None] * k2[None, :]) % LANES) / LANES
    fr, fi = np.cos(a128), -np.sin(a128)
    m2 = np.block([[fr, -fi], [fi, fr]])
    ma = np.block([[fr, fi], [-fi, fr]])
    kk = np.arange(2 * n)
    ac = 2.0 * math.pi * ((kk[:, None] * kk[None, :]) % (2 * n)) / (2 * n)
    f_ctx = np.concatenate([np.cos(ac), -np.sin(ac)], axis=0)
    g_ctx = np.concatenate([np.cos(ac), -np.sin(ac)], axis=1)[:n] / (2 * n)
    bf = lambda x: jnp.asarray(x, BF16)
    return HyenaConsts(_features(length), jnp.asarray(rates[None, :], F32), bf(t1[:, :, :p // 2]), bf(t1),
                       bf(m2), bf(ma), bf(tb), _features(n), bf(f_ctx), bf(g_ctx))


def hyena(cfg, hc, vxx, filt_w, skip, with_ctx=True):
    length, n, hw = cfg.seq, cfg.ctx, cfg.hy_w
    half = length // LANES
    y_lat = vxx[0]
    taps = hyena_taps(hc.feats, *filt_w, hc.rates, length, hw)
    for order in range(2):
        hr, hi = fft_step2(*fft_step1(taps[order], hc.t1_taps, 2 * half, hw), hc.m2)
        zr, zi = fft_step2(*fft_step1(y_lat, hc.t1_data, half, hw), hc.m2)
        vr, vi = ifft_step1(zr, zi, hr, hi, hc.ma)
        y_lat = ifft_step2_gate(vr, vi, hc.tb, vxx[1 + order], y_lat, skip[order:order + 1])
    if not with_ctx:
        return y_lat, jnp.zeros((n, hw), F32)
    taps_c = hyena_taps(hc.feats_ctx, *filt_w, hc.rates, n, hw)
    y_ctx = hyena_ctx(cfg, vxx, taps_c[0], taps_c[1], skip, hc.f_ctx, hc.g_ctx)
    return y_lat, y_ctx


def expert_expand():
    m = np.zeros((LANES, N_EXPERTS * LANES), np.float32)
    for e in range(N_EXPERTS):
        m[e, e * LANES:(e + 1) * LANES] = 1.0
    return jnp.asarray(m, BF16)


def prep_gla_gate(gate_up, gate_b):
    l, _, r, qk = gate_up.shape
    w = jnp.zeros((l, LANES, 2 * qk), F32)
    for z in range(2):
        w = w.at[:, GLA_LR_LANE0 + z * r:GLA_LR_LANE0 + (z + 1) * r, z * qk:(z + 1) * qk].set(gate_up[:, z])
    return w, gate_b.reshape(l, 1, 2 * qk)


def prep_hyena_filter(w1, b1, w2, b2, w3, freq):
    e, hdim = w1.shape
    w1p = jnp.zeros((LANES, LANES), F32).at[:e, :hdim].set(w1)
    w2p = jnp.zeros((LANES, LANES), F32).at[:hdim, :hdim].set(w2)
    w3p = jnp.zeros((LANES, w3.shape[1]), F32).at[:hdim].set(w3)
    pad = lambda v: jnp.zeros((1, LANES), F32).at[0, :hdim].set(v)
    freqp = jnp.zeros((2, LANES), F32).at[:, :hdim].set(freq)
    return w1p, pad(b1), w2p, pad(b2), freqp, w3p


def prep_w2(w2):
    l, e, f, d = w2.shape
    return w2.reshape(l, e * f, d).astype(BF16)


def kernel(x, c, ctx, c_ctx, norm1_g, norm2_g, w_mod, b_mod, w_in, gdn_conv, gdn_a_log, gdn_dt_bias, gdn_norm, gla_gate_up, gla_gate_b, gla_norm, hy_conv_w, hy_conv_b, hy_w1, hy_b1, hy_w2, hy_b2, hy_w3, hy_freq, hy_skip, merge_up, merge_b, w_branch, w_out, w_router, router_bias, moe_w1, moe_w3, moe_w2, final_g):
    cfg = Cfg(d=4096, seq=8192, ctx=256, grid_w=64, gdn_h=8, gla_h=4, hy_w=1024, merge_rank=256,
              d_expert=256, row_tile=256, mm_tm=768)
    return forward(cfg, x, c, ctx, c_ctx, norm1_g, norm2_g, w_mod, b_mod, w_in, gdn_conv, gdn_a_log,
                   gdn_dt_bias, gdn_norm, gla_gate_up, gla_gate_b, gla_norm, hy_conv_w, hy_conv_b, hy_w1,
                   hy_b1, hy_w2, hy_b2, hy_w3, hy_freq, hy_skip, merge_up, merge_b, w_branch, w_out,
                   w_router, router_bias, moe_w1, moe_w3, moe_w2, final_g)


def prep_w_in(cfg, w_in):
    gw = cfg.gdn_w
    gdn_small = 4 * cfg.gdn_h
    gdn_in = 4 * gw + gdn_small
    gla_main = 2 * cfg.gla_qk + 2 * cfg.gla_v
    gla_in = gla_main + 32
    rec = gdn_in + gla_in
    hy_in = 3 * cfg.hy_w
    l, d, _ = w_in.shape
    big = jnp.concatenate([w_in[:, :, :4 * gw], w_in[:, :, gdn_in:gdn_in + gla_main],
                           w_in[:, :, rec:rec + hy_in]], axis=2).astype(BF16)
    zeros = lambda n: jnp.zeros((l, d, n), w_in.dtype)
    small = jnp.concatenate([w_in[:, :, rec + hy_in:rec + hy_in + cfg.merge_rank],
                             w_in[:, :, 4 * gw:gdn_in], zeros(GLA_LR_LANE0 - gdn_small),
                             w_in[:, :, gdn_in + gla_main:gdn_in + gla_in], zeros(LANES - GLA_LR_LANE0 - 32)],
                            axis=2).astype(BF16)
    return big, small


def forward(cfg, x, c, ctx, c_ctx, norm1_g, norm2_g, w_mod, b_mod, w_in, gdn_conv, gdn_a_log, gdn_dt_bias,
            gdn_norm, gla_gate_up, gla_gate_b, gla_norm, hy_conv_w, hy_conv_b, hy_w1, hy_b1, hy_w2, hy_b2,
            hy_w3, hy_freq, hy_skip, merge_up, merge_b, w_branch, w_out, w_router, router_bias, moe_w1,
            moe_w3, moe_w2, final_g):
    d, tr, tm = cfg.d, cfg.row_tile, cfg.mm_tm
    depth = w_in.shape[0]
    nh = cfg.gdn_h
    tn = min(1024, d)
    w_big, w_small = prep_w_in(cfg, w_in)
    w1b, w3b = moe_w1.astype(BF16), moe_w3.astype(BF16)
    w2f = prep_w2(moe_w2)
    wb, mu, wo = w_branch.astype(BF16), merge_up.astype(BF16), w_out.astype(BF16)
    mb = merge_b.reshape(depth, 3, 1, d)
    wr_pad = jnp.zeros((d, LANES), F32).at[:, :N_EXPERTS].set(w_router)
    rb_col = router_bias.reshape(N_EXPERTS, 1)
    expand = expert_expand()
    wup, gbias = prep_gla_gate(gla_gate_up, gla_gate_b)
    lane0 = 2 * nh
    alog_rows = jnp.zeros((depth, 1, LANES), F32).at[:, 0, lane0:2 * lane0].set(gdn_a_log.reshape(depth, -1))
    dtb_rows = jnp.zeros((depth, 1, LANES), F32).at[:, 0, lane0:2 * lane0].set(gdn_dt_bias.reshape(depth, -1))
    hc = hyena_consts(cfg)
    gdn_cols = 4 * cfg.gdn_w
    gla_cols = 2 * cfg.gla_qk + 2 * cfg.gla_v
    gla_gate_blk = (2 * cfg.gla_qk + cfg.gla_v) // cfg.gla_v
    z_proj = functools.partial(matmul_plain, tm=tm, tn=tn, out_dtype=BF16)

    lat = jnp.concatenate([x[0], ctx[0]], axis=0)
    cvec = jnp.zeros((8, d), F32).at[0].set(c[0]).at[1].set(c_ctx)
    mods = modvec(cvec, w_mod, b_mod)

    for l in range(depth):
        with_ctx = l < depth - 1
        mod = mods[l]
        h = norm1(cfg, lat, norm1_g[l], mod)
        z_gdn = z_proj(h, w_big, l, n_cols=gdn_cols, col0=0, name="w_in_gdn")
        z_gla = z_proj(h, w_big, l, n_cols=gla_cols, col0=gdn_cols, name="w_in_gla")
        z_hy = z_proj(h, w_big, l, n_cols=3 * cfg.hy_w, col0=gdn_cols + gla_cols, name="w_in_hy")
        z_small = matmul_plain(h, w_small, l, tm=tm, tn=w_small.shape[2], name="w_in_small")
        qn, kn, vs, bb, gcb, gct = gdn_prep(cfg, z_gdn, z_small, gdn_conv[l], alog_rows[l], dtb_rows[l])
        o_f, o_b = gdn_scan(cfg, *gdn_chunks(cfg, qn, kn, vs, bb, gcb, gct))
        a_all = head_norm(o_f, o_b, z_gdn, 3, gdn_norm[l], nh, 128, tr)
        (olf, olb), (ocf, ocb) = gla(cfg, z_gla, z_small, 0, wup[l], gbias[l])
        b_lat = head_norm(olf, olb, z_gla, gla_gate_blk, gla_norm[l], cfg.gla_h, 256, tr)
        b_ctx = head_norm(ocf, ocb, z_gla, gla_gate_blk, gla_norm[l], cfg.gla_h, 256, tr, gate_row0=cfg.seq)
        b_all = jnp.concatenate([b_lat, b_ctx], axis=0)
        vxx = hyena_prep(cfg, z_hy, hy_conv_w[l], hy_conv_b[l])
        filt_w = prep_hyena_filter(hy_w1[l], hy_b1[l], hy_w2[l], hy_b2[l], hy_w3[l], hy_freq[l])
        c_lat, c_ctx_out = hyena(cfg, hc, vxx, filt_w, hy_skip[l], with_ctx)
        c_all = jnp.concatenate([c_lat, c_ctx_out], axis=0).astype(BF16)
        s = merge_branches(cfg, (a_all, b_all, c_all), z_small, wb, mu, mb, l, tn=tn)
        lat = matmul_resid(cfg, s, wo, l, lat, mod, 2, tn=tn, name="w_out")
        h2, gate_rep = norm2_route(cfg, lat, norm2_g[l], mod, wr_pad, rb_col, expand)
        act = matmul_moe_act(cfg, h2, w1b, w3b, l, gate_rep)
        lat = matmul_resid(cfg, act, w2f, l, lat, mod, 5, tn=tn, name="moe_down")
    return final_norm(lat, final_g, cfg.seq, tr)[None]
```

```python
import functools
import math
from typing import NamedTuple

import numpy as np
import jax
import jax.numpy as jnp
from jax import lax
from jax.experimental import pallas as pl
from jax.experimental.pallas import tpu as pltpu

F32 = jnp.float32
BF16 = jnp.bfloat16

EPS = 1e-6
LANES = 128
V7X_VMEM_BYTES = 64 * 1024 * 1024
VMEM_LIMIT = (V7X_VMEM_BYTES * 13) // 16
SHORT_CONV = 3
GLA_CHUNK = 64
GLA_GATE_NORM = 16.0
N_EXPERTS = 16
N_GROUPS = 4
HY_EMB = 33
HY_FAST_DECAY = 0.3
HY_SLOW_DECAY = 1.5
HY_TARGET = 1e-2


class Cfg(NamedTuple):
    d: int
    seq: int
    ctx: int
    grid_w: int
    gdn_h: int
    gla_h: int
    hy_w: int
    merge_rank: int
    d_expert: int
    row_tile: int
    mm_tm: int

    @property
    def t(self):
        return self.seq + self.ctx

    @property
    def gdn_w(self):
        return self.gdn_h * 128

    @property
    def gla_qk(self):
        return self.gla_h * 128

    @property
    def gla_v(self):
        return self.gla_h * 256


def _cparams(sem):
    return pltpu.CompilerParams(dimension_semantics=sem, vmem_limit_bytes=VMEM_LIMIT)


def _split3(x):
    hi = x.astype(BF16)
    r1 = x - hi.astype(F32)
    mid = r1.astype(BF16)
    lo = (r1 - mid.astype(F32)).astype(BF16)
    return hi, mid, lo


def _dot(a, b):
    return jnp.dot(a, b, preferred_element_type=F32)


def _dot_sel(sel_bf16, x):
    hi, mid, lo = _split3(x)
    return _dot(sel_bf16, hi) + _dot(sel_bf16, mid) + _dot(sel_bf16, lo)


def _dot_x_sel(x, sel_bf16):
    hi, mid, lo = _split3(x)
    return _dot(hi, sel_bf16) + _dot(mid, sel_bf16) + _dot(lo, sel_bf16)


def _dot_hi(a, b):
    a1, a2, a3 = _split3(a)
    b1, b2, b3 = _split3(b)
    return (_dot(a1, b1) + (_dot(a1, b2) + _dot(a2, b1))
            + (_dot(a2, b2) + _dot(a1, b3) + _dot(a3, b1)))


def _dot_3(a, b):
    a1 = a.astype(BF16)
    a2 = (a - a1.astype(F32)).astype(BF16)
    b1 = b.astype(BF16)
    b2 = (b - b1.astype(F32)).astype(BF16)
    return _dot(a1, b1) + (_dot(a1, b2) + _dot(a2, b1))


def _silu(x):
    return x * jax.nn.sigmoid(x)


def _modvec_kernel(x_ref, w_ref, b_ref, o_ref):
    o_ref[...] = _dot(_silu(x_ref[...]), w_ref[...]) + b_ref[...]


def modvec(cvec, w_mod, b_mod, tn=512):
    depth, d, n = w_mod.shape
    return pl.pallas_call(
        _modvec_kernel,
        out_shape=jax.ShapeDtypeStruct((depth, 8, n), F32),
        grid=(depth, n // tn),
        in_specs=[pl.BlockSpec((8, d), lambda l, j: (0, 0)),
                  pl.BlockSpec((None, d, tn), lambda l, j: (l, 0, j)),
                  pl.BlockSpec((None, 1, tn), lambda l, j: (l, 0, j))],
        out_specs=pl.BlockSpec((None, 8, tn), lambda l, j: (l, 0, j)),
        compiler_params=_cparams(("parallel", "parallel")),
        name="modvec",
    )(cvec, w_mod, b_mod.reshape(depth, 1, n))


def _mod_row(mod_ref, is_ctx, idx, d):
    return mod_ref[pl.ds(is_ctx, 1), idx * d:(idx + 1) * d]


def _norm_mod(x, gain, shift, scale):
    y = x * lax.rsqrt(jnp.mean(x * x, axis=-1, keepdims=True) + EPS)
    return (y * gain) * (1.0 + scale) + shift


def _norm1_kernel(x_ref, g_ref, mod_ref, o_ref, *, d, n_lat_tiles):
    is_ctx = (pl.program_id(0) >= n_lat_tiles).astype(jnp.int32)
    h = _norm_mod(x_ref[...], g_ref[...], _mod_row(mod_ref, is_ctx, 0, d),
                  _mod_row(mod_ref, is_ctx, 1, d))
    o_ref[...] = h.astype(BF16)


def norm1(cfg, x, gain, mod):
    tr = cfg.row_tile
    t, d = x.shape
    return pl.pallas_call(
        functools.partial(_norm1_kernel, d=d, n_lat_tiles=cfg.seq // tr),
        out_shape=jax.ShapeDtypeStruct((t, d), BF16),
        grid=(t // tr,),
        in_specs=[pl.BlockSpec((tr, d), lambda i: (i, 0)),
                  pl.BlockSpec((1, d), lambda i: (0, 0)),
                  pl.BlockSpec((8, 6 * d), lambda i: (0, 0))],
        out_specs=pl.BlockSpec((tr, d), lambda i: (i, 0)),
        compiler_params=_cparams(("parallel",)),
        name="norm1",
    )(x, gain.reshape(1, d), mod)


def _route(sel_t, sc_t):
    per = N_EXPERTS // N_GROUPS
    grp_score = []
    for g in range(N_GROUPS):
        v = sel_t[g * per:(g + 1) * per]
        best = None
        for a in range(per):
            for b in range(a + 1, per):
                s = v[a] + v[b]
                best = s if best is None else jnp.maximum(best, s)
        grp_score.append(best)
    best_s, best_g = grp_score[0], jnp.zeros_like(grp_score[0])
    for g in range(1, N_GROUPS):
        better = grp_score[g] > best_s
        best_s = jnp.where(better, grp_score[g], best_s)
        best_g = jnp.where(better, float(g), best_g)
    picked = []
    for e in range(N_EXPERTS):
        g, i = divmod(e, per)
        rank = jnp.zeros_like(best_s)
        for j in range(per):
            if j == i:
                continue
            o = sel_t[g * per + j]
            ahead = (o >= sel_t[e]) if j < i else (o > sel_t[e])
            rank = rank + ahead.astype(F32)
        picked.append(jnp.where((best_g == float(g)) & (rank < 2.0), sc_t[e], 0.0))
    den = picked[0]
    for e in range(1, N_EXPERTS):
        den = den + picked[e]
    inv = 1.0 / den
    return [p * inv for p in picked]


def _norm2_kernel(x_ref, g_ref, mod_ref, wr_ref, rb_ref, ex_ref, o_ref, gate_ref, *, d,
                  n_lat_tiles):
    is_ctx = (pl.program_id(0) >= n_lat_tiles).astype(jnp.int32)
    h = _norm_mod(x_ref[...], g_ref[...], _mod_row(mod_ref, is_ctx, 3, d),
                  _mod_row(mod_ref, is_ctx, 4, d))
    o_ref[...] = h.astype(BF16)
    logits = _dot_3(h, wr_ref[...])
    lt = jnp.transpose(logits)
    sc = jax.nn.sigmoid(lt[0:N_EXPERTS, :])
    sel = sc + rb_ref[...]
    gate_rows = _route([sel[e:e + 1, :] for e in range(N_EXPERTS)],
                       [sc[e:e + 1, :] for e in range(N_EXPERTS)])
    rows = lax.broadcasted_iota(jnp.int32, lt.shape, 0)
    gt = jnp.zeros(lt.shape, F32)
    for e in range(N_EXPERTS):
        gt = jnp.where(rows == e, gate_rows[e], gt)
    gate = jnp.transpose(gt)
    gate_ref[...] = _dot_x_sel(gate, ex_ref[...])


def norm2_route(cfg, x, gain, mod, w_router_pad, rbias_col, expand):
    tr = cfg.row_tile
    t, d = x.shape
    return pl.pallas_call(
        functools.partial(_norm2_kernel, d=d, n_lat_tiles=cfg.seq // tr),
        out_shape=(jax.ShapeDtypeStruct((t, d), BF16),
                   jax.ShapeDtypeStruct((t, N_EXPERTS * LANES), F32)),
        grid=(t // tr,),
        in_specs=[pl.BlockSpec((tr, d), lambda i: (i, 0)),
                  pl.BlockSpec((1, d), lambda i: (0, 0)),
                  pl.BlockSpec((8, 6 * d), lambda i: (0, 0)),
                  pl.BlockSpec((d, LANES), lambda i: (0, 0)),
                  pl.BlockSpec((N_EXPERTS, 1), lambda i: (0, 0)),
                  pl.BlockSpec((LANES, N_EXPERTS * LANES), lambda i: (0, 0))],
        out_specs=(pl.BlockSpec((tr, d), lambda i: (i, 0)),
                   pl.BlockSpec((tr, N_EXPERTS * LANES), lambda i: (i, 0))),
        compiler_params=_cparams(("parallel",)),
        name="norm2_route",
    )(x, gain.reshape(1, d), mod, w_router_pad, rbias_col, expand)


def _final_norm_kernel(x_ref, g_ref, o_ref):
    x = x_ref[...]
    o_ref[...] = (x * lax.rsqrt(jnp.mean(x * x, axis=-1, keepdims=True) + EPS)) * g_ref[...]


def final_norm(x, gain, n_rows, tr):
    d = x.shape[1]
    return pl.pallas_call(
        _final_norm_kernel,
        out_shape=jax.ShapeDtypeStruct((n_rows, d), F32),
        grid=(n_rows // tr,),
        in_specs=[pl.BlockSpec((tr, d), lambda i: (i, 0)),
                  pl.BlockSpec((1, d), lambda i: (0, 0))],
        out_specs=pl.BlockSpec((tr, d), lambda i: (i, 0)),
        compiler_params=_cparams(("parallel",)),
        name="final_norm",
    )(x, gain.reshape(1, d))


def _mm_plain_kernel(a_ref, w_ref, o_ref):
    o_ref[...] = _dot(a_ref[...], w_ref[...]).astype(o_ref.dtype)


def matmul_plain(a, w, layer, *, tm, tn, n_cols=None, col0=0, out_dtype=F32, name="mm"):
    t, k = a.shape
    n_cols = w.shape[2] - col0 if n_cols is None else n_cols
    off = col0 // tn
    return pl.pallas_call(
        _mm_plain_kernel,
        out_shape=jax.ShapeDtypeStruct((t, n_cols), out_dtype),
        grid=(n_cols // tn, t // tm),
        in_specs=[pl.BlockSpec((tm, k), lambda j, i: (i, 0)),
                  pl.BlockSpec((None, k, tn), lambda j, i: (layer, 0, j + off))],
        out_specs=pl.BlockSpec((tm, tn), lambda j, i: (i, j)),
        compiler_params=_cparams(("parallel", "parallel")),
        name=name,
    )(a, w)


def _mm_resid_kernel(a_ref, w_ref, r_ref, mod_ref, o_ref, *, tm, n_lat):
    row = pl.program_id(1) * tm + lax.broadcasted_iota(jnp.int32, (tm, 1), 0)
    gate = jnp.where(row < n_lat, mod_ref[0:1, :], mod_ref[1:2, :])
    o_ref[...] = r_ref[...] + gate * _dot(a_ref[...], w_ref[...])


def matmul_resid(cfg, a, w, layer, resid, mod, idx, *, tn, name):
    t, k = a.shape
    d = w.shape[2]
    tm = cfg.mm_tm
    return pl.pallas_call(
        functools.partial(_mm_resid_kernel, tm=tm, n_lat=cfg.seq),
        out_shape=jax.ShapeDtypeStruct((t, d), F32),
        grid=(d // tn, t // tm),
        in_specs=[pl.BlockSpec((tm, k), lambda j, i: (i, 0)),
                  pl.BlockSpec((None, k, tn), lambda j, i: (layer, 0, j)),
                  pl.BlockSpec((tm, tn), lambda j, i: (i, j)),
                  pl.BlockSpec((8, tn), lambda j, i: (0, idx * (d // tn) + j))],
        out_specs=pl.BlockSpec((tm, tn), lambda j, i: (i, j)),
        input_output_aliases={2: 0},
        compiler_params=_cparams(("parallel", "parallel")),
        name=name,
    )(a, w, resid, mod)


def _mm_moe_act_kernel(a_ref, w1_ref, w3_ref, g_ref, o_ref, *, de, n_e):
    a = a_ref[...]
    for e in range(n_e):
        up = _dot(a, w1_ref[e])
        lin = _dot(a, w3_ref[e])
        g = g_ref[:, e * LANES:(e + 1) * LANES]
        g = jnp.concatenate([g] * (de // LANES), axis=1)
        o_ref[:, e * de:(e + 1) * de] = (_silu(up) * lin * g).astype(o_ref.dtype)


def matmul_moe_act(cfg, h, w1, w3, layer, gate_rep, *, n_e=2):
    t, k = h.shape
    de = cfg.d_expert
    tm = cfg.mm_tm
    wspec = pl.BlockSpec((None, n_e, k, de), lambda j, i: (layer, j, 0, 0))
    return pl.pallas_call(
        functools.partial(_mm_moe_act_kernel, de=de, n_e=n_e),
        out_shape=jax.ShapeDtypeStruct((t, N_EXPERTS * de), BF16),
        grid=(N_EXPERTS // n_e, t // tm),
        in_specs=[pl.BlockSpec((tm, k), lambda j, i: (i, 0)), wspec, wspec,
                  pl.BlockSpec((tm, n_e * LANES), lambda j, i: (i, j))],
        out_specs=pl.BlockSpec((tm, n_e * de), lambda j, i: (i, j)),
        compiler_params=_cparams(("parallel", "parallel")),
        name="moe_up",
    )(h, w1, w3, gate_rep)


def _merge_kernel(a0_ref, a1_ref, a2_ref, zg_ref, wb_ref, mu_ref, mb_ref, o_ref):
    zg = zg_ref[...].astype(BF16)
    acc = None
    for n, a_ref in enumerate((a0_ref, a1_ref, a2_ref)):
        y = _dot(a_ref[...], wb_ref[n])
        gate = jax.nn.sigmoid(_dot(zg, mu_ref[n]) + mb_ref[n])
        acc = gate * y if acc is None else acc + gate * y
    o_ref[...] = acc.astype(o_ref.dtype)


def merge_branches(cfg, outs, z_small, w_branch, merge_up, merge_b, layer, *, tn):
    t, bw = outs[0].shape
    d = w_branch.shape[3]
    r = cfg.merge_rank
    tm = cfg.mm_tm
    a_spec = pl.BlockSpec((tm, bw), lambda j, i: (i, 0))
    return pl.pallas_call(
        _merge_kernel,
        out_shape=jax.ShapeDtypeStruct((t, d), BF16),
        grid=(d // tn, t // tm),
        in_specs=[a_spec, a_spec, a_spec,
                  pl.BlockSpec((tm, r), lambda j, i: (i, 0)),
                  pl.BlockSpec((None, 3, bw, tn), lambda j, i: (layer, 0, 0, j)),
                  pl.BlockSpec((None, 3, r, tn), lambda j, i: (layer, 0, 0, j)),
                  pl.BlockSpec((None, 3, 1, tn), lambda j, i: (layer, 0, 0, j))],
        out_specs=pl.BlockSpec((tm, tn), lambda j, i: (i, j)),
        compiler_params=_cparams(("parallel", "parallel")),
        name="merge",
    )(outs[0], outs[1], outs[2], z_small, w_branch, merge_up, merge_b)


HALO_ROWS = 16


def _halo_specs(tr, width, col_block, n_rows):
    rb = tr // HALO_ROWS
    last = n_rows // HALO_ROWS - 1
    main = pl.BlockSpec((tr, width), lambda i: (i, col_block))
    prev = pl.BlockSpec((HALO_ROWS, width), lambda i: (jnp.maximum(i * rb - 1, 0), col_block))
    nxt = pl.BlockSpec((HALO_ROWS, width), lambda i: (jnp.minimum((i + 1) * rb, last), col_block))
    return main, prev, nxt


def _conv3(z_ref, prev_ref, next_ref, w_ref, n_lat_tiles):
    i = pl.program_id(0)
    x = z_ref[...].astype(F32)
    tr = x.shape[0]
    has_prev = jnp.logical_and(i != 0, i != n_lat_tiles).astype(F32)
    has_next = jnp.logical_and(i != n_lat_tiles - 1, i != n_lat_tiles).astype(F32)
    row = lax.broadcasted_iota(jnp.int32, (tr, 1), 0)
    halo_prev = prev_ref[HALO_ROWS - 1:HALO_ROWS, :].astype(F32) * has_prev
    halo_next = next_ref[0:1, :].astype(F32) * has_next
    x_prev = jnp.where(row == 0, halo_prev, pltpu.roll(x, 1, axis=0))
    x_next = jnp.where(row == tr - 1, halo_next, pltpu.roll(x, tr - 1, axis=0))
    return x_prev * w_ref[0:1, :] + x * w_ref[1:2, :] + x_next * w_ref[2:3, :]


def _tri_masks(n):
    r = lax.broadcasted_iota(jnp.int32, (n, n), 0)
    c = lax.broadcasted_iota(jnp.int32, (n, n), 1)
    return r, c


def _gdn_prep_kernel(z_ref, zp_ref, zn_ref, s_ref, cw_ref, alog_ref, dtb_ref,
                     q_ref, k_ref, v_ref, bb_ref, gcb_ref, gct_ref, *, n_heads, n_lat_tiles):
    tr = z_ref.shape[0]
    hw = n_heads * 128
    y = _silu(_conv3(z_ref, zp_ref, zn_ref, cw_ref, n_lat_tiles))
    for h in range(n_heads):
        q = y[:, h * 128:(h + 1) * 128]
        k = y[:, hw + h * 128:hw + (h + 1) * 128]
        q = q * (lax.rsqrt(jnp.sum(q * q, axis=-1, keepdims=True) + EPS) * (128.0 ** -0.5))
        k = k * lax.rsqrt(jnp.sum(k * k, axis=-1, keepdims=True) + EPS)
        q_ref[h] = q.astype(BF16)
        k_ref[h] = k.astype(BF16)
        v_ref[h] = y[:, 2 * hw + h * 128:2 * hw + (h + 1) * 128].astype(BF16)
    s = s_ref[...]
    nh2 = 2 * n_heads
    beta = jax.nn.sigmoid(s)
    g = -jnp.exp(alog_ref[...]) * jax.nn.softplus(s + dtb_ref[...])
    r, c = _tri_masks(tr)
    incl_lo = (c <= r).astype(BF16)
    incl_up = (c >= r).astype(BF16)
    lane = lax.broadcasted_iota(jnp.int32, (tr, LANES), 1)
    fwd_lane = lane < nh2 + n_heads
    gc = jnp.where(fwd_lane, _dot_sel(incl_lo, g), _dot_sel(incl_up, g))
    for ch in range(nh2):
        bb_ref[ch] = jnp.broadcast_to(beta[:, ch:ch + 1], (tr, LANES))
        gcb_ref[ch] = jnp.broadcast_to(gc[:, nh2 + ch:nh2 + ch + 1], (tr, LANES))
    gct_ref[...] = jnp.transpose(gc)[nh2:2 * nh2, :]


def gdn_prep(cfg, z_big, z_small, conv_w, alog_row, dtb_row):
    tr = cfg.row_tile
    t = z_big.shape[0]
    nh = cfg.gdn_h
    hw = nh * 128
    n_tiles = t // tr
    main, prev, nxt = _halo_specs(tr, 3 * hw, 0, t)
    head_out = jax.ShapeDtypeStruct((nh, t, 128), BF16)
    head_spec = pl.BlockSpec((nh, tr, 128), lambda i: (0, i, 0))
    col_out = jax.ShapeDtypeStruct((2 * nh, t, LANES), F32)
    col_spec = pl.BlockSpec((2 * nh, tr, LANES), lambda i: (0, i, 0))
    return pl.pallas_call(
        functools.partial(_gdn_prep_kernel, n_heads=nh, n_lat_tiles=cfg.seq // tr),
        out_shape=(head_out, head_out, head_out, col_out, col_out,
                   jax.ShapeDtypeStruct((n_tiles, 2 * nh, tr), F32)),
        grid=(n_tiles,),
        in_specs=[main, prev, nxt,
                  pl.BlockSpec((tr, LANES), lambda i: (i, cfg.merge_rank // LANES)),
                  pl.BlockSpec((SHORT_CONV, 3 * hw), lambda i: (0, 0)),
                  pl.BlockSpec((1, LANES), lambda i: (0, 0)),
                  pl.BlockSpec((1, LANES), lambda i: (0, 0))],
        out_specs=(head_spec, head_spec, head_spec, col_spec, col_spec,
                   pl.BlockSpec((None, 2 * nh, tr), lambda i: (i, 0, 0))),
        compiler_params=_cparams(("parallel",)),
        name="gdn_prep",
    )(z_big, z_big, z_big, z_small, conv_w, alog_row, dtb_row)


def _unit_tri_inverses(n_mats, r, c):
    n = n_mats[0].shape[0]
    eye = (r == c).astype(F32)

    def same_block(b):
        sh = int(math.log2(b))
        return (r >> sh) == (c >> sh)

    blk = same_block(8)
    pfs = [jnp.where(blk, -m, 0.0) for m in n_mats]
    ps = [pf.astype(BF16) for pf in pfs]
    p2 = [_dot(p, p).astype(BF16) for p in ps]
    p4 = [_dot(x, x).astype(BF16) for x in p2]
    ts = [eye + pf for pf in pfs]
    ts = [t + _dot(t.astype(BF16), x) for t, x in zip(ts, p2)]
    ts = [t + _dot(t.astype(BF16), x) for t, x in zip(ts, p4)]
    b = 8
    while b < n:
        sel = jnp.logical_and(same_block(2 * b), jnp.logical_not(same_block(b)))
        offs = [jnp.where(sel, m, 0.0).astype(BF16) for m in n_mats]
        tbs = [t.astype(BF16) for t in ts]
        xs = [_dot(tb, off).astype(BF16) for tb, off in zip(tbs, offs)]
        ts = [t - _dot(x, tb) for t, x, tb in zip(ts, xs, tbs)]
        b *= 2
    return ts, eye


GDN_HEADS_PER_STEP = 4


def _gdn_chunk_kernel(q_ref, k_ref, v_ref, bf_ref, bb_ref, gf_ref, gb_ref, gct_ref,
                      uw_ref, aq_ref, qk_ref, gl_ref, *, n_heads):
    hps = q_ref.shape[0]
    h0 = pl.program_id(1) * hps
    tr = q_ref.shape[1]
    r, c = _tri_masks(tr)
    nt = (((1,), (1,)), ((), ()))
    masks = (((c <= r), (c < r)), ((c >= r), (c > r)))
    chains = [(j, d) for j in range(hps) for d in range(2)]
    k_b = [k_ref[j] for j in range(hps)]
    k_f = [x.astype(F32) for x in k_b]
    a_qk = [lax.dot_general(q_ref[j], k_b[j], nt, preferred_element_type=F32) for j in range(hps)]
    beta, gc, gam, kb, n_mats = {}, {}, {}, {}, []
    for j, d in chains:
        beta[j, d] = (bf_ref, bb_ref)[d][j]
        gc[j, d] = (gf_ref, gb_ref)[d][j]
        gc_row = gct_ref[pl.ds(d * n_heads + h0 + j, 1), :]
        gc_col = jnp.concatenate([gc[j, d]] * (tr // LANES), axis=1)
        gam[j, d] = jnp.exp(jnp.where(masks[d][0], gc_col - gc_row, -jnp.inf))
        kb[j, d] = k_f[j] * beta[j, d]
        a_kk = lax.dot_general(kb[j, d].astype(BF16), k_b[j], nt, preferred_element_type=F32)
        n_mats.append(jnp.where(masks[d][1], a_kk * gam[j, d], 0.0))
    t_invs, eye = _unit_tri_inverses(n_mats, r, c)
    for (j, d), t_inv in zip(chains, t_invs):
        e = jnp.exp(gc[j, d])
        rhs = jnp.concatenate([v_ref[j].astype(F32) * beta[j, d], kb[j, d] * e], axis=1)
        sol = rhs + _dot((t_inv - eye).astype(BF16), rhs.astype(BF16))
        g_last = gc[j, d][tr - 1:tr, :] if d == 0 else gc[j, d][0:1, :]
        uw_ref[d, j] = sol.astype(BF16)
        aq_ref[d, j] = (a_qk[j] * gam[j, d]).astype(BF16)
        qk_ref[d, j] = jnp.concatenate([q_ref[j].astype(F32) * e,
                                        k_f[j] * jnp.exp(g_last - gc[j, d])], axis=1).astype(BF16)
        gl_ref[d, j] = jnp.broadcast_to(jnp.exp(g_last), (8, LANES))


def gdn_chunks(cfg, qn, kn, vs, bb, gcb, gct):
    tr = cfg.row_tile
    nh, t, _ = qn.shape
    hps = min(GDN_HEADS_PER_STEP, nh)
    assert nh % hps == 0
    n_tiles = t // tr
    head = pl.BlockSpec((hps, tr, 128), lambda i, h: (h, i, 0))
    head_b = pl.BlockSpec((hps, tr, 128), lambda i, h: (h + nh // hps, i, 0))
    big = jax.ShapeDtypeStruct((2, nh, t, 2 * 128), BF16)
    aq = jax.ShapeDtypeStruct((2, nh, t, tr), BF16)
    return pl.pallas_call(
        functools.partial(_gdn_chunk_kernel, n_heads=nh),
        out_shape=(big, aq, big, jax.ShapeDtypeStruct((2, nh, n_tiles * 8, LANES), F32)),
        grid=(n_tiles, nh // hps),
        in_specs=[head, head, head, head, head_b, head, head_b,
                  pl.BlockSpec((None, 2 * nh, tr), lambda i, h: (i, 0, 0))],
        out_specs=(pl.BlockSpec((2, hps, tr, 256), lambda i, h: (0, h, i, 0)),
                   pl.BlockSpec((2, hps, tr, tr), lambda i, h: (0, h, i, 0)),
                   pl.BlockSpec((2, hps, tr, 256), lambda i, h: (0, h, i, 0)),
                   pl.BlockSpec((2, hps, 8, LANES), lambda i, h: (0, h, i, 0))),
        compiler_params=_cparams(("parallel", "parallel")),
        name="gdn_chunks",
    )(qn, kn, vs, bb, bb, gcb, gcb, gct)


def _gdn_scan_kernel(uwf_ref, aqf_ref, qkf_ref, glf_ref, uwb_ref, aqb_ref, qkb_ref, glb_ref,
                     of_ref, ob_ref, s_ref, *, n_heads):
    @pl.when(pl.program_id(0) == 0)
    def _():
        s_ref[...] = jnp.zeros_like(s_ref)

    tn = (((0,), (0,)), ((), ()))
    for d, (uw_ref, aq_ref, qk_ref, gl_ref, o_ref) in enumerate(
            ((uwf_ref, aqf_ref, qkf_ref, glf_ref, of_ref), (uwb_ref, aqb_ref, qkb_ref, glb_ref, ob_ref))):
        for h in range(n_heads):
            st = s_ref[d, h]
            st_b = st.astype(BF16)
            uw = uw_ref[h]
            qk = qk_ref[h]
            v_new = uw[:, 0:128].astype(F32) - _dot(uw[:, 128:256], st_b)
            v_new_b = v_new.astype(BF16)
            o_ref[:, h * 128:(h + 1) * 128] = (
                _dot(qk[:, 0:128], st_b) + _dot(aq_ref[h], v_new_b)).astype(o_ref.dtype)
            s_ref[d, h] = gl_ref[h][0:1, :] * st + lax.dot_general(
                qk[:, 128:256], v_new_b, tn, preferred_element_type=F32)


def gdn_scan(cfg, uw, aq, qk, gl):
    tr = cfg.row_tile
    _, nh, t, _ = uw.shape
    n_tiles = t // tr
    last = n_tiles - 1

    def fwd(s):
        return jnp.where(s == 0, last, s - 1)

    def bwd(s):
        return jnp.where(s == 0, last, last - s)

    def specs(d, order):
        return [pl.BlockSpec((None, nh, tr, 256), lambda s: (d, 0, order(s), 0)),
                pl.BlockSpec((None, nh, tr, tr), lambda s: (d, 0, order(s), 0)),
                pl.BlockSpec((None, nh, tr, 256), lambda s: (d, 0, order(s), 0)),
                pl.BlockSpec((None, nh, 8, LANES), lambda s: (d, 0, order(s), 0))]

    out = jax.ShapeDtypeStruct((t, nh * 128), BF16)
    return pl.pallas_call(
        functools.partial(_gdn_scan_kernel, n_heads=nh),
        out_shape=(out, out),
        grid=(n_tiles,),
        in_specs=specs(0, fwd) + specs(1, bwd),
        out_specs=(pl.BlockSpec((tr, nh * 128), lambda s: (fwd(s), 0)),
                   pl.BlockSpec((tr, nh * 128), lambda s: (bwd(s), 0))),
        scratch_shapes=[pltpu.VMEM((2, nh, 128, 128), F32)],
        compiler_params=_cparams(("arbitrary",)),
        name="gdn_scan",
    )(uw, aq, qk, gl, uw, aq, qk, gl)


def _head_norm_kernel(of_ref, ob_ref, gate_ref, w_ref, o_ref, *, n_heads, dv):
    for h in range(n_heads):
        sl = slice(h * dv, (h + 1) * dv)
        o = of_ref[:, sl].astype(F32) + ob_ref[:, sl].astype(F32)
        o = o * lax.rsqrt(jnp.mean(o * o, axis=-1, keepdims=True) + EPS) * w_ref[...]
        o_ref[:, sl] = (o * _silu(gate_ref[:, sl].astype(F32))).astype(o_ref.dtype)


def head_norm(o_f, o_b, z_big, gate_col_block, norm_w, n_heads, dv, tr, gate_row0=0):
    t, w = o_f.shape
    spec = pl.BlockSpec((tr, w), lambda i: (i, 0))
    rb0 = gate_row0 // tr
    return pl.pallas_call(
        functools.partial(_head_norm_kernel, n_heads=n_heads, dv=dv),
        out_shape=jax.ShapeDtypeStruct((t, w), BF16),
        grid=(t // tr,),
        in_specs=[spec, spec, pl.BlockSpec((tr, w), lambda i: (i + rb0, gate_col_block)),
                  pl.BlockSpec((1, dv), lambda i: (0, 0))],
        out_specs=spec,
        compiler_params=_cparams(("parallel",)),
        name="head_norm",
    )(o_f, o_b, z_big, norm_w.reshape(1, dv))


GLA_TILE = 128
GLA_LR_LANE0 = 32


def _gla_direction(d, qk_ref, v_ref, sm_ref, wup_ref, gb_ref, o_ref, st_ref, n_heads):
    tr = GLA_TILE
    qkw = n_heads * 128
    r, c = _tri_masks(tr)
    sh = int(math.log2(GLA_CHUNK))
    same = (r >> sh) == (c >> sh)
    mask = jnp.logical_and(same, (c <= r) if d == 0 else (c >= r))
    logit = _dot_hi(sm_ref[...], wup_ref[:, d * qkw:(d + 1) * qkw]) + gb_ref[:, d * qkw:(d + 1) * qkw]
    g = jax.nn.log_sigmoid(logit) * (1.0 / GLA_GATE_NORM)
    b = _dot_sel(jnp.where(mask, 1.0, 0.0).astype(BF16), g)
    nt = (((1,), (1,)), ((), ()))
    tn = (((0,), (0,)), ((), ()))
    chunks = range(tr // GLA_CHUNK) if d == 0 else range(tr // GLA_CHUNK - 1, -1, -1)
    for h in range(n_heads):
        bq = b[:, h * 128:(h + 1) * 128]
        q = qk_ref[:, h * 128:(h + 1) * 128].astype(F32)
        k = qk_ref[:, qkw + h * 128:qkw + (h + 1) * 128].astype(F32)
        v_b = v_ref[:, h * 256:(h + 1) * 256].astype(BF16)
        qe = (q * jnp.exp(bq) * (128.0 ** -0.5)).astype(BF16)
        kinv = (k * jnp.exp(-bq)).astype(BF16)
        a = jnp.where(mask, lax.dot_general(qe, kinv, nt, preferred_element_type=F32), 0.0)
        o_intra = _dot(a.astype(BF16), v_b)
        for ci in chunks:
            lo = ci * GLA_CHUNK
            last = lo + GLA_CHUNK - 1 if d == 0 else lo
            b_last = bq[last:last + 1, :]
            kdec = (k[lo:lo + GLA_CHUNK] * jnp.exp(b_last - bq[lo:lo + GLA_CHUNK])).astype(BF16)
            ds = lax.dot_general(kdec, v_b[lo:lo + GLA_CHUNK], tn, preferred_element_type=F32)
            st = st_ref[d, h]
            o_ref[lo:lo + GLA_CHUNK, h * 256:(h + 1) * 256] = (
                o_intra[lo:lo + GLA_CHUNK] + _dot(qe[lo:lo + GLA_CHUNK], st.astype(BF16))).astype(o_ref.dtype)
            dec = jnp.transpose(jnp.broadcast_to(jnp.exp(b_last), (128, 128)))
            st_ref[d, h] = jnp.concatenate([dec, dec], axis=1) * st + ds


def _gla_kernel(qkf_ref, vf_ref, smf_ref, qkb_ref, vb_ref, smb_ref, wup_ref, gb_ref, s0_ref,
                of_ref, ob_ref, sfin_ref, st_ref, *, n_heads):
    @pl.when(pl.program_id(0) == 0)
    def _():
        st_ref[...] = s0_ref[...]

    _gla_direction(0, qkf_ref, vf_ref, smf_ref, wup_ref, gb_ref, of_ref, st_ref, n_heads)
    _gla_direction(1, qkb_ref, vb_ref, smb_ref, wup_ref, gb_ref, ob_ref, st_ref, n_heads)

    @pl.when(pl.program_id(0) == pl.num_programs(0) - 1)
    def _():
        sfin_ref[...] = st_ref[...]


def _gla_call(cfg, n_steps, arrays, spec_fn, out_rows_shape, out_spec_fn, wup, gbias, s0, name):
    nh = cfg.gla_h
    fwd = lambda s: s
    bwd = lambda s: n_steps - 1 - s
    full = lambda shape: pl.BlockSpec(shape, lambda s: (0,) * len(shape))
    z_view, zs_view = arrays
    st_shape = (2, nh, 128, 256)
    out = jax.ShapeDtypeStruct(out_rows_shape, BF16)
    return pl.pallas_call(
        functools.partial(_gla_kernel, n_heads=nh),
        out_shape=(out, out, jax.ShapeDtypeStruct(st_shape, F32)),
        grid=(n_steps,),
        in_specs=spec_fn(fwd) + spec_fn(bwd) + [full(wup.shape), full(gbias.shape), full(st_shape)],
        out_specs=(out_spec_fn(fwd), out_spec_fn(bwd), full(st_shape)),
        scratch_shapes=[pltpu.VMEM(st_shape, F32)],
        compiler_params=_cparams(("arbitrary",)),
        name=name,
    )(z_view, z_view, zs_view, z_view, z_view, zs_view, wup, gbias, s0)


def gla(cfg, z_big, z_small, col0, wup, gbias):
    nh = cfg.gla_h
    qk2, vw = 2 * nh * 128, nh * 256
    zw, sw = z_big.shape[1], z_small.shape[1]
    w = cfg.grid_w
    rows = cfg.seq // w
    assert rows == GLA_TILE and cfg.ctx % GLA_TILE == 0
    assert zw % qk2 == 0 and zw % vw == 0 and col0 % qk2 == 0 and (col0 + qk2) % vw == 0
    small_blk = cfg.merge_rank // LANES
    s0 = jnp.zeros((2, nh, 128, 256), F32)

    rb0 = cfg.seq // GLA_TILE
    ctx_specs = lambda order: [
        pl.BlockSpec((GLA_TILE, qk2), lambda s: (rb0 + order(s), col0 // qk2)),
        pl.BlockSpec((GLA_TILE, vw), lambda s: (rb0 + order(s), (col0 + qk2) // vw)),
        pl.BlockSpec((GLA_TILE, LANES), lambda s: (rb0 + order(s), small_blk))]
    ctx_out = lambda order: pl.BlockSpec((GLA_TILE, vw), lambda s: (order(s), 0))
    ocf, ocb, s_ctx = _gla_call(cfg, cfg.ctx // GLA_TILE, (z_big, z_small), ctx_specs,
                                (cfg.ctx, vw), ctx_out, wup, gbias, s0, "gla_ctx")

    t = z_big.shape[0]
    zv = z_big.reshape(t // w, w * zw)
    zsv = z_small.reshape(t // w, w * sw)
    lat_specs = lambda order: [
        pl.BlockSpec((GLA_TILE, qk2), lambda s: (0, (order(s) * zw + col0) // qk2)),
        pl.BlockSpec((GLA_TILE, vw), lambda s: (0, (order(s) * zw + col0 + qk2) // vw)),
        pl.BlockSpec((GLA_TILE, LANES), lambda s: (0, order(s) * (sw // LANES) + small_blk))]
    lat_out = lambda order: pl.BlockSpec((GLA_TILE, vw), lambda s: (0, order(s)))
    olf, olb, _ = _gla_call(cfg, w, (zv, zsv), lat_specs, (rows, w * vw), lat_out,
                            wup, gbias, s_ctx, "gla_lat")
    return (olf.reshape(cfg.seq, vw), olb.reshape(cfg.seq, vw)), (ocf, ocb)


FFT_G = 8


def _hy_prep_kernel(z_ref, zp_ref, zn_ref, w_ref, b_ref, v_ref, x1_ref, x2_ref, *, n_lat_tiles, hw):
    y = _conv3(z_ref, zp_ref, zn_ref, w_ref, n_lat_tiles) + b_ref[...]
    for p, o_ref in enumerate((v_ref, x1_ref, x2_ref)):
        o_ref[...] = y[:, p * hw:(p + 1) * hw]


def hyena_prep(cfg, z_hy, conv_w, conv_b):
    tr, hw = cfg.row_tile, cfg.hy_w
    t = z_hy.shape[0]
    main, prev, nxt = _halo_specs(tr, 3 * hw, 0, t)
    out = jax.ShapeDtypeStruct((t, hw), F32)
    ospec = pl.BlockSpec((tr, hw), lambda i: (i, 0))
    return pl.pallas_call(
        functools.partial(_hy_prep_kernel, n_lat_tiles=cfg.seq // tr, hw=hw),
        out_shape=(out, out, out),
        grid=(t // tr,),
        in_specs=[main, prev, nxt,
                  pl.BlockSpec((SHORT_CONV, 3 * hw), lambda i: (0, 0)),
                  pl.BlockSpec((1, 3 * hw), lambda i: (0, 0))],
        out_specs=(ospec, ospec, ospec),
        compiler_params=_cparams(("parallel",)),
        name="hyena_prep",
    )(z_hy, z_hy, z_hy, conv_w, conv_b.reshape(1, 3 * hw))


def _hy_taps_kernel(f_ref, w1_ref, b1_ref, w2_ref, b2_ref, fr_ref, w30_ref, w31_ref, rate_ref,
                    o0_ref, o1_ref, *, length, tt):
    feats = f_ref[...]
    hid = jnp.sin(fr_ref[0:1, :] * (_dot_3(feats, w1_ref[...]) + b1_ref[...]))
    hid = jnp.sin(fr_ref[1:2, :] * (_dot_3(hid, w2_ref[...]) + b2_ref[...]))
    n = pl.program_id(0) * tt + lax.broadcasted_iota(jnp.int32, (tt, 1), 0)
    window = jnp.where(n == length, 0.0, jnp.exp(-feats[:, 0:1] * rate_ref[...]))
    for w3_ref, o_ref in ((w30_ref, o0_ref), (w31_ref, o1_ref)):
        o_ref[...] = (_dot_3(hid, w3_ref[...]) * window).astype(o_ref.dtype)


def hyena_taps(feats, w1p, b1p, w2p, b2p, freqp, w3p, rates, length, hw):
    tt = min(512, length)
    n_half = length // tt
    out = jax.ShapeDtypeStruct((2 * length, hw), F32)
    ospec = pl.BlockSpec((tt, hw), lambda j: (j, 0))
    return pl.pallas_call(
        functools.partial(_hy_taps_kernel, length=length, tt=tt),
        out_shape=(out, out),
        grid=(2 * n_half,),
        in_specs=[pl.BlockSpec((tt, LANES), lambda j: (j, 0)),
                  pl.BlockSpec((LANES, LANES), lambda j: (0, 0)),
                  pl.BlockSpec((1, LANES), lambda j: (0, 0)),
                  pl.BlockSpec((LANES, LANES), lambda j: (0, 0)),
                  pl.BlockSpec((1, LANES), lambda j: (0, 0)),
                  pl.BlockSpec((2, LANES), lambda j: (0, 0)),
                  pl.BlockSpec((LANES, hw), lambda j: (0, j // n_half)),
                  pl.BlockSpec((LANES, hw), lambda j: (0, 2 + j // n_half)),
                  pl.BlockSpec((1, hw), lambda j: (0, 0))],
        out_specs=(ospec, ospec),
        compiler_params=_cparams(("parallel",)),
        name="hyena_taps",
    )(feats, w1p, b1p, w2p, b2p, freqp, w3p, w3p, rates)


FFT_BT = 8


def _fft_a_kernel(x_ref, l_ref, yr_ref, yi_ref):
    a, bt, c = x_ref.shape
    ph = yr_ref.shape[0]
    y = _dot(l_ref[...], x_ref[...].reshape(a * bt, c).astype(BF16))
    yr_ref[...] = y[0:ph * bt].reshape(ph, bt, c)
    yi_ref[...] = y[ph * bt:2 * ph * bt].reshape(ph, bt, c)


def fft_a(x, lhs, n_pages, ph):
    c = x.shape[1]
    x3 = x.reshape(x.shape[0] // LANES, LANES, c)
    out = jax.ShapeDtypeStruct((ph, LANES, c), F32)
    ospec = pl.BlockSpec((ph, FFT_BT, c), lambda j: (0, j, 0))
    return pl.pallas_call(
        _fft_a_kernel,
        out_shape=(out, out),
        grid=(LANES // FFT_BT,),
        in_specs=[pl.BlockSpec((n_pages, FFT_BT, c), lambda j: (0, j, 0)),
                  pl.BlockSpec(lhs.shape, lambda j: (0, 0))],
        out_specs=(ospec, ospec),
        compiler_params=_cparams(("parallel",)),
        name="fft_a",
    )(x3, lhs)


def _fft_b_kernel(yr_ref, yi_ref, m_ref, zr_ref, zi_ref):
    for g in range(FFT_G):
        s = jnp.concatenate([yr_ref[g], yi_ref[g]], axis=0).astype(BF16)
        z = _dot(m_ref[g], s)
        zr_ref[g] = z[0:LANES].astype(zr_ref.dtype)
        zi_ref[g] = z[LANES:2 * LANES].astype(zi_ref.dtype)


def fft_b(yr, yi, m2):
    ph, _, c = yr.shape
    out = jax.ShapeDtypeStruct((ph, LANES, c), BF16)
    spec = pl.BlockSpec((FFT_G, LANES, c), lambda j: (j, 0, 0))
    return pl.pallas_call(
        _fft_b_kernel,
        out_shape=(out, out),
        grid=(ph // FFT_G,),
        in_specs=[spec, spec, pl.BlockSpec((FFT_G, 2 * LANES, 2 * LANES), lambda j: (j, 0, 0))],
        out_specs=(spec, spec),
        compiler_params=_cparams(("parallel",)),
        name="fft_b",
    )(yr, yi, m2)


def _ifft_a_kernel(zr_ref, zi_ref, hr_ref, hi_ref, m_ref, vr_ref, vi_ref):
    for g in range(FFT_G):
        zr, zi = zr_ref[g].astype(F32), zi_ref[g].astype(F32)
        hr, hi = hr_ref[g].astype(F32), hi_ref[g].astype(F32)
        s = jnp.concatenate([zr * hr - zi * hi, zr * hi + zi * hr], axis=0).astype(BF16)
        v = _dot(m_ref[g], s)
        vr_ref[g] = v[0:LANES]
        vi_ref[g] = v[LANES:2 * LANES]


def ifft_a(zr, zi, hr, hi, ma):
    ph, _, c = zr.shape
    out = jax.ShapeDtypeStruct((ph, LANES, c), F32)
    spec = pl.BlockSpec((FFT_G, LANES, c), lambda j: (j, 0, 0))
    return pl.pallas_call(
        _ifft_a_kernel,
        out_shape=(out, out),
        grid=(ph // FFT_G,),
        in_specs=[spec, spec, spec, spec,
                  pl.BlockSpec((FFT_G, 2 * LANES, 2 * LANES), lambda j: (j, 0, 0))],
        out_specs=(spec, spec),
        compiler_params=_cparams(("parallel",)),
        name="ifft_a",
    )(zr, zi, hr, hi, ma)


def _ifft_b_kernel(vr_ref, vi_ref, l_ref, x_ref, u_ref, sk_ref, o_ref):
    ph, bt, c = vr_ref.shape
    s = jnp.concatenate([vr_ref[...].reshape(ph * bt, c), vi_ref[...].reshape(ph * bt, c)],
                        axis=0).astype(BF16)
    y = _dot(l_ref[...], s).reshape(o_ref.shape)
    o_ref[...] = x_ref[...] * (y + sk_ref[...] * u_ref[...])


def ifft_b_gate(vr, vi, lhs, gate_x, u, skip_row):
    ph, _, c = vr.shape
    a_out = lhs.shape[0] // FFT_BT
    x3 = gate_x.reshape(gate_x.shape[0] // LANES, LANES, c)
    u3 = u.reshape(u.shape[0] // LANES, LANES, c)
    vspec = pl.BlockSpec((ph, FFT_BT, c), lambda j: (0, j, 0))
    tspec = pl.BlockSpec((a_out, FFT_BT, c), lambda j: (0, j, 0))
    out = pl.pallas_call(
        _ifft_b_kernel,
        out_shape=jax.ShapeDtypeStruct((a_out, LANES, c), F32),
        grid=(LANES // FFT_BT,),
        in_specs=[vspec, vspec, pl.BlockSpec(lhs.shape, lambda j: (0, 0)), tspec, tspec,
                  pl.BlockSpec((1, 1, c), lambda j: (0, 0, 0))],
        out_specs=tspec,
        compiler_params=_cparams(("parallel",)),
        name="ifft_b",
    )(vr, vi, lhs, x3, u3, skip_row.reshape(1, 1, c))
    return out.reshape(a_out * LANES, c)


def _hy_ctx_kernel(v_ref, x1_ref, x2_ref, t0_ref, t1_ref, sk_ref, f_ref, g_ref, o_ref, *, n):
    def conv(u, taps_ref):
        us = _dot(f_ref[:, 0:n], u.astype(BF16))
        hs = _dot(f_ref[...], taps_ref[...].astype(BF16))
        ur, ui, hr, hi = us[0:2 * n], us[2 * n:4 * n], hs[0:2 * n], hs[2 * n:4 * n]
        prod = jnp.concatenate([ur * hr - ui * hi, ur * hi + ui * hr], axis=0).astype(BF16)
        return _dot(g_ref[...], prod)

    v = v_ref[...]
    y = x1_ref[...] * (conv(v, t0_ref) + sk_ref[0:1, :] * v)
    o_ref[...] = x2_ref[...] * (conv(y, t1_ref) + sk_ref[1:2, :] * y)


def hyena_ctx(cfg, vxx, taps0, taps1, skip, fmat, gmat):
    n, hw = cfg.ctx, cfg.hy_w
    cb = 256
    rb = cfg.seq // n
    part = pl.BlockSpec((n, cb), lambda j: (rb, j))
    tspec = pl.BlockSpec((2 * n, cb), lambda j: (0, j))
    return pl.pallas_call(
        functools.partial(_hy_ctx_kernel, n=n),
        out_shape=jax.ShapeDtypeStruct((n, hw), F32),
        grid=(hw // cb,),
        in_specs=[part, part, part, tspec, tspec,
                  pl.BlockSpec((2, cb), lambda j: (0, j)),
                  pl.BlockSpec((4 * n, 2 * n), lambda j: (0, 0)),
                  pl.BlockSpec((n, 4 * n), lambda j: (0, 0))],
        out_specs=pl.BlockSpec((n, cb), lambda j: (0, j)),
        compiler_params=_cparams(("parallel",)),
        name="hyena_ctx",
    )(vxx[0], vxx[1], vxx[2], taps0, taps1, skip, fmat, gmat)


class HyenaConsts(NamedTuple):
    feats: jax.Array
    rates: jax.Array
    ph: int
    la_data: jax.Array
    la_taps: jax.Array
    m2: jax.Array
    ma: jax.Array
    lb: jax.Array
    feats_ctx: jax.Array
    f_ctx: jax.Array
    g_ctx: jax.Array


def _features(length):
    n = np.arange(2 * length, dtype=np.float64)
    pos = np.where(n < length, n, 2 * length - n)
    bands = (HY_EMB - 1) // 2
    f = np.linspace(1e-4, bands - 1, bands)
    omega = (2.0 * math.pi / length) * pos
    feats = np.zeros((2 * length, LANES), np.float64)
    feats[:, 0] = pos / (length - 1)
    feats[:, 1:1 + bands] = np.cos(omega[:, None] * f[None, :])
    feats[:, 1 + bands:1 + 2 * bands] = -np.sin(omega[:, None] * f[None, :])
    return jnp.asarray(feats, F32)


def hyena_consts(cfg):
    length, n, hw = cfg.seq, cfg.ctx, cfg.hy_w
    big_n = 2 * length
    p = big_n // LANES
    rates = np.abs(np.linspace(math.log(HY_TARGET) / HY_FAST_DECAY, math.log(HY_TARGET) / HY_SLOW_DECAY, hw))
    ph = p // 2 + 8
    k1 = np.arange(ph)
    kept = (k1 <= p // 2).astype(np.float64)
    a = np.arange(p)
    ang_a = 2.0 * math.pi * ((k1[:, None] * a[None, :]) % p) / p
    dft_a = np.concatenate([np.cos(ang_a), -np.sin(ang_a)], axis=0) * np.tile(kept, 2)[:, None]
    eye = np.eye(FFT_BT)
    la_taps = np.kron(dft_a, eye)
    la_data = np.kron(dft_a[:, :p // 2], eye)
    weight = kept * np.where((k1 == 0) | (k1 == p // 2), 1.0, 2.0) / big_n
    inv_a = (np.concatenate([np.cos(ang_a), -np.sin(ang_a)], axis=0) * np.tile(weight, 2)[:, None]).T
    lb = np.kron(inv_a[:p // 2], eye)
    b = np.arange(LANES)
    ang_b = 2.0 * math.pi * (((b[:, None] * b[None, :]) % LANES) / LANES)[None] \
        + 2.0 * math.pi * (k1[:, None, None] * b[None, None, :]) / big_n
    fr, fi = np.cos(ang_b), -np.sin(ang_b)
    m2 = np.concatenate([np.concatenate([fr, -fi], axis=2), np.concatenate([fi, fr], axis=2)], axis=1)
    gr, gi = fr.transpose(0, 2, 1), -fi.transpose(0, 2, 1)
    ma = np.concatenate([np.concatenate([gr, -gi], axis=2), np.concatenate([gi, gr], axis=2)], axis=1)
    kk = np.arange(2 * n)
    ac = 2.0 * math.pi * ((kk[:, None] * kk[None, :]) % (2 * n)) / (2 * n)
    f_ctx = np.concatenate([np.cos(ac), -np.sin(ac)], axis=0)
    g_ctx = np.concatenate([np.cos(ac), -np.sin(ac)], axis=1)[:n] / (2 * n)
    bf = lambda x: jnp.asarray(x, BF16)
    return HyenaConsts(_features(length), jnp.asarray(rates[None, :], F32), ph, bf(la_data), bf(la_taps),
                       bf(m2), bf(ma), bf(lb), _features(n), bf(f_ctx), bf(g_ctx))


def hyena(cfg, hc, vxx, filt_w, skip, with_ctx=True):
    length, n, hw = cfg.seq, cfg.ctx, cfg.hy_w
    half = length // LANES
    y_lat = vxx[0]
    taps = hyena_taps(hc.feats, *filt_w, hc.rates, length, hw)
    for order in range(2):
        hr, hi = fft_b(*fft_a(taps[order], hc.la_taps, 2 * half, hc.ph), hc.m2)
        zr, zi = fft_b(*fft_a(y_lat, hc.la_data, half, hc.ph), hc.m2)
        vr, vi = ifft_a(zr, zi, hr, hi, hc.ma)
        y_lat = ifft_b_gate(vr, vi, hc.lb, vxx[1 + order], y_lat, skip[order:order + 1])
    if not with_ctx:
        return y_lat, jnp.zeros((n, hw), F32)
    taps_c = hyena_taps(hc.feats_ctx, *filt_w, hc.rates, n, hw)
    y_ctx = hyena_ctx(cfg, vxx, taps_c[0], taps_c[1], skip, hc.f_ctx, hc.g_ctx)
    return y_lat, y_ctx


def expert_expand():
    m = np.zeros((LANES, N_EXPERTS * LANES), np.float32)
    for e in range(N_EXPERTS):
        m[e, e * LANES:(e + 1) * LANES] = 1.0
    return jnp.asarray(m, BF16)


def prep_gla_gate(gate_up, gate_b):
    l, _, r, qk = gate_up.shape
    w = jnp.zeros((l, LANES, 2 * qk), F32)
    for z in range(2):
        w = w.at[:, GLA_LR_LANE0 + z * r:GLA_LR_LANE0 + (z + 1) * r, z * qk:(z + 1) * qk].set(gate_up[:, z])
    return w, gate_b.reshape(l, 1, 2 * qk)


def prep_hyena_filter(w1, b1, w2, b2, w3, freq):
    e, hdim = w1.shape
    w1p = jnp.zeros((LANES, LANES), F32).at[:e, :hdim].set(w1)
    w2p = jnp.zeros((LANES, LANES), F32).at[:hdim, :hdim].set(w2)
    w3p = jnp.zeros((LANES, w3.shape[1]), F32).at[:hdim].set(w3)
    pad = lambda v: jnp.zeros((1, LANES), F32).at[0, :hdim].set(v)
    freqp = jnp.zeros((2, LANES), F32).at[:, :hdim].set(freq)
    return w1p, pad(b1), w2p, pad(b2), freqp, w3p


def prep_w2(w2):
    l, e, f, d = w2.shape
    return w2.reshape(l, e * f, d).astype(BF16)


def kernel(x, c, ctx, c_ctx, norm1_g, norm2_g, w_mod, b_mod, w_in, gdn_conv, gdn_a_log, gdn_dt_bias, gdn_norm, gla_gate_up, gla_gate_b, gla_norm, hy_conv_w, hy_conv_b, hy_w1, hy_b1, hy_w2, hy_b2, hy_w3, hy_freq, hy_skip, merge_up, merge_b, w_branch, w_out, w_router, router_bias, moe_w1, moe_w3, moe_w2, final_g):
    cfg = Cfg(d=4096, seq=8192, ctx=256, grid_w=64, gdn_h=8, gla_h=4, hy_w=1024, merge_rank=256,
              d_expert=256, row_tile=256, mm_tm=768)
    return forward(cfg, x, c, ctx, c_ctx, norm1_g, norm2_g, w_mod, b_mod, w_in, gdn_conv, gdn_a_log,
                   gdn_dt_bias, gdn_norm, gla_gate_up, gla_gate_b, gla_norm, hy_conv_w, hy_conv_b, hy_w1,
                   hy_b1, hy_w2, hy_b2, hy_w3, hy_freq, hy_skip, merge_up, merge_b, w_branch, w_out,
                   w_router, router_bias, moe_w1, moe_w3, moe_w2, final_g)


def prep_w_in(cfg, w_in):
    gw = cfg.gdn_w
    gdn_small = 4 * cfg.gdn_h
    gdn_in = 4 * gw + gdn_small
    gla_main = 2 * cfg.gla_qk + 2 * cfg.gla_v
    gla_in = gla_main + 32
    rec = gdn_in + gla_in
    hy_in = 3 * cfg.hy_w
    l, d, _ = w_in.shape
    w_gdn = w_in[:, :, :4 * gw].astype(BF16)
    w_gla = w_in[:, :, gdn_in:gdn_in + gla_main].astype(BF16)
    w_hy = w_in[:, :, rec:rec + hy_in].astype(BF16)
    zeros = lambda n: jnp.zeros((l, d, n), BF16)
    small = jnp.concatenate([w_in[:, :, rec + hy_in:rec + hy_in + cfg.merge_rank].astype(BF16),
                             w_in[:, :, 4 * gw:gdn_in].astype(BF16), zeros(GLA_LR_LANE0 - gdn_small),
                             w_in[:, :, gdn_in + gla_main:gdn_in + gla_in].astype(BF16),
                             zeros(LANES - GLA_LR_LANE0 - 32)], axis=2)
    return w_gdn, w_gla, w_hy, small


def forward(cfg, x, c, ctx, c_ctx, norm1_g, norm2_g, w_mod, b_mod, w_in, gdn_conv, gdn_a_log, gdn_dt_bias,
            gdn_norm, gla_gate_up, gla_gate_b, gla_norm, hy_conv_w, hy_conv_b, hy_w1, hy_b1, hy_w2, hy_b2,
            hy_w3, hy_freq, hy_skip, merge_up, merge_b, w_branch, w_out, w_router, router_bias, moe_w1,
            moe_w3, moe_w2, final_g):
    d, tr, tm = cfg.d, cfg.row_tile, cfg.mm_tm
    depth = w_in.shape[0]
    nh = cfg.gdn_h
    tn = min(1024, d)
    w_gdn, w_gla, w_hy, w_small = prep_w_in(cfg, w_in)
    w1b, w3b = moe_w1.astype(BF16), moe_w3.astype(BF16)
    w2f = prep_w2(moe_w2)
    wb, mu, wo = w_branch.astype(BF16), merge_up.astype(BF16), w_out.astype(BF16)
    mb = merge_b.reshape(depth, 3, 1, d)
    wr_pad = jnp.zeros((d, LANES), F32).at[:, :N_EXPERTS].set(w_router)
    rb_col = router_bias.reshape(N_EXPERTS, 1)
    expand = expert_expand()
    wup, gbias = prep_gla_gate(gla_gate_up, gla_gate_b)
    lane0 = 2 * nh
    alog_rows = jnp.zeros((depth, 1, LANES), F32).at[:, 0, lane0:2 * lane0].set(gdn_a_log.reshape(depth, -1))
    dtb_rows = jnp.zeros((depth, 1, LANES), F32).at[:, 0, lane0:2 * lane0].set(gdn_dt_bias.reshape(depth, -1))
    hc = hyena_consts(cfg)
    gla_gate_blk = (2 * cfg.gla_qk + cfg.gla_v) // cfg.gla_v
    z_proj = functools.partial(matmul_plain, tm=tm, tn=tn, out_dtype=BF16)

    lat = jnp.concatenate([x[0], ctx[0]], axis=0)
    cvec = jnp.zeros((8, d), F32).at[0].set(c[0]).at[1].set(c_ctx)
    mods = modvec(cvec, w_mod, b_mod)

    for l in range(depth):
        with_ctx = l < depth - 1
        mod = mods[l]
        h = norm1(cfg, lat, norm1_g[l], mod)
        z_gdn = z_proj(h, w_gdn, l, name="w_in_gdn")
        z_gla = z_proj(h, w_gla, l, name="w_in_gla")
        z_hy = z_proj(h, w_hy, l, name="w_in_hy")
        z_small = matmul_plain(h, w_small, l, tm=tm, tn=w_small.shape[2], name="w_in_small")
        qn, kn, vs, bb, gcb, gct = gdn_prep(cfg, z_gdn, z_small, gdn_conv[l], alog_rows[l], dtb_rows[l])
        o_f, o_b = gdn_scan(cfg, *gdn_chunks(cfg, qn, kn, vs, bb, gcb, gct))
        a_all = head_norm(o_f, o_b, z_gdn, 3, gdn_norm[l], nh, 128, tr)
        (olf, olb), (ocf, ocb) = gla(cfg, z_gla, z_small, 0, wup[l], gbias[l])
        b_lat = head_norm(olf, olb, z_gla, gla_gate_blk, gla_norm[l], cfg.gla_h, 256, tr)
        b_ctx = head_norm(ocf, ocb, z_gla, gla_gate_blk, gla_norm[l], cfg.gla_h, 256, tr, gate_row0=cfg.seq)
        b_all = jnp.concatenate([b_lat, b_ctx], axis=0)
        vxx = hyena_prep(cfg, z_hy, hy_conv_w[l], hy_conv_b[l])
        filt_w = prep_hyena_filter(hy_w1[l], hy_b1[l], hy_w2[l], hy_b2[l], hy_w3[l], hy_freq[l])
        c_lat, c_ctx_out = hyena(cfg, hc, vxx, filt_w, hy_skip[l], with_ctx)
        c_all = jnp.concatenate([c_lat, c_ctx_out], axis=0).astype(BF16)
        s = merge_branches(cfg, (a_all, b_all, c_all), z_small, wb, mu, mb, l, tn=tn)
        lat = matmul_resid(cfg, s, wo, l, lat, mod, 2, tn=tn, name="w_out")
        h2, gate_rep = norm2_route(cfg, lat, norm2_g[l], mod, wr_pad, rb_col, expand)
        act = matmul_moe_act(cfg, h2, w1b, w3b, l, gate_rep)
        lat = matmul_resid(cfg, act, w2f, l, lat, mod, 5, tn=tn, name="moe_down")
    return final_norm(lat, final_g, cfg.seq, tr)[None]
```

```python
import functools
import math
from typing import NamedTuple

import numpy as np
import jax
import jax.numpy as jnp
from jax import lax
from jax.experimental import pallas as pl
from jax.experimental.pallas import tpu as pltpu

F32 = jnp.float32
BF16 = jnp.bfloat16

EPS = 1e-6
LANES = 128
V7X_VMEM_BYTES = 64 * 1024 * 1024
VMEM_LIMIT = (V7X_VMEM_BYTES * 13) // 16
SHORT_CONV = 3
GLA_CHUNK = 64
GLA_GATE_NORM = 16.0
N_EXPERTS = 16
N_GROUPS = 4
HY_EMB = 33
HY_FAST_DECAY = 0.3
HY_SLOW_DECAY = 1.5
HY_TARGET = 1e-2


class Cfg(NamedTuple):
    d: int
    seq: int
    ctx: int
    grid_w: int
    gdn_h: int
    gla_h: int
    hy_w: int
    merge_rank: int
    d_expert: int
    row_tile: int
    mm_tm: int

    @property
    def t(self):
        return self.seq + self.ctx

    @property
    def gdn_w(self):
        return self.gdn_h * 128

    @property
    def gla_qk(self):
        return self.gla_h * 128

    @property
    def gla_v(self):
        return self.gla_h * 256


def _cparams(sem):
    return pltpu.CompilerParams(dimension_semantics=sem, vmem_limit_bytes=VMEM_LIMIT)


def _split3(x):
    hi = x.astype(BF16)
    r1 = x - hi.astype(F32)
    mid = r1.astype(BF16)
    lo = (r1 - mid.astype(F32)).astype(BF16)
    return hi, mid, lo


def _dot(a, b):
    return jnp.dot(a, b, preferred_element_type=F32)


def _dot_sel(sel_bf16, x):
    hi, mid, lo = _split3(x)
    return _dot(sel_bf16, hi) + _dot(sel_bf16, mid) + _dot(sel_bf16, lo)


def _dot_x_sel(x, sel_bf16):
    hi, mid, lo = _split3(x)
    return _dot(hi, sel_bf16) + _dot(mid, sel_bf16) + _dot(lo, sel_bf16)


def _dot_hi(a, b):
    a1, a2, a3 = _split3(a)
    b1, b2, b3 = _split3(b)
    return (_dot(a1, b1) + (_dot(a1, b2) + _dot(a2, b1))
            + (_dot(a2, b2) + _dot(a1, b3) + _dot(a3, b1)))


def _dot_3(a, b):
    a1 = a.astype(BF16)
    a2 = (a - a1.astype(F32)).astype(BF16)
    b1 = b.astype(BF16)
    b2 = (b - b1.astype(F32)).astype(BF16)
    return _dot(a1, b1) + (_dot(a1, b2) + _dot(a2, b1))


def _silu(x):
    return x * jax.nn.sigmoid(x)


def _modvec_kernel(x_ref, w_ref, b_ref, o_ref):
    o_ref[...] = _dot(_silu(x_ref[...]), w_ref[...]) + b_ref[...]


def modvec(cvec, w_mod, b_mod, tn=512):
    depth, d, n = w_mod.shape
    return pl.pallas_call(
        _modvec_kernel,
        out_shape=jax.ShapeDtypeStruct((depth, 8, n), F32),
        grid=(depth, n // tn),
        in_specs=[pl.BlockSpec((8, d), lambda l, j: (0, 0)),
                  pl.BlockSpec((None, d, tn), lambda l, j: (l, 0, j)),
                  pl.BlockSpec((None, 1, tn), lambda l, j: (l, 0, j))],
        out_specs=pl.BlockSpec((None, 8, tn), lambda l, j: (l, 0, j)),
        compiler_params=_cparams(("parallel", "parallel")),
        name="modvec",
    )(cvec, w_mod, b_mod.reshape(depth, 1, n))


def _mod_row(mod_ref, is_ctx, idx, d):
    return mod_ref[pl.ds(is_ctx, 1), idx * d:(idx + 1) * d]


def _norm_mod(x, gain, shift, scale):
    y = x * lax.rsqrt(jnp.mean(x * x, axis=-1, keepdims=True) + EPS)
    return (y * gain) * (1.0 + scale) + shift


def _norm1_kernel(x_ref, g_ref, mod_ref, o_ref, *, d, n_lat_tiles):
    is_ctx = (pl.program_id(0) >= n_lat_tiles).astype(jnp.int32)
    h = _norm_mod(x_ref[...], g_ref[...], _mod_row(mod_ref, is_ctx, 0, d),
                  _mod_row(mod_ref, is_ctx, 1, d))
    o_ref[...] = h.astype(BF16)


def norm1(cfg, x, gain, mod):
    tr = cfg.row_tile
    t, d = x.shape
    return pl.pallas_call(
        functools.partial(_norm1_kernel, d=d, n_lat_tiles=cfg.seq // tr),
        out_shape=jax.ShapeDtypeStruct((t, d), BF16),
        grid=(t // tr,),
        in_specs=[pl.BlockSpec((tr, d), lambda i: (i, 0)),
                  pl.BlockSpec((1, d), lambda i: (0, 0)),
                  pl.BlockSpec((8, 6 * d), lambda i: (0, 0))],
        out_specs=pl.BlockSpec((tr, d), lambda i: (i, 0)),
        compiler_params=_cparams(("parallel",)),
        name="norm1",
    )(x, gain.reshape(1, d), mod)


def _route(sel_t, sc_t):
    per = N_EXPERTS // N_GROUPS
    grp_score = []
    for g in range(N_GROUPS):
        v = sel_t[g * per:(g + 1) * per]
        best = None
        for a in range(per):
            for b in range(a + 1, per):
                s = v[a] + v[b]
                best = s if best is None else jnp.maximum(best, s)
        grp_score.append(best)
    best_s, best_g = grp_score[0], jnp.zeros_like(grp_score[0])
    for g in range(1, N_GROUPS):
        better = grp_score[g] > best_s
        best_s = jnp.where(better, grp_score[g], best_s)
        best_g = jnp.where(better, float(g), best_g)
    picked = []
    for e in range(N_EXPERTS):
        g, i = divmod(e, per)
        rank = jnp.zeros_like(best_s)
        for j in range(per):
            if j == i:
                continue
            o = sel_t[g * per + j]
            ahead = (o >= sel_t[e]) if j < i else (o > sel_t[e])
            rank = rank + ahead.astype(F32)
        picked.append(jnp.where((best_g == float(g)) & (rank < 2.0), sc_t[e], 0.0))
    den = picked[0]
    for e in range(1, N_EXPERTS):
        den = den + picked[e]
    inv = 1.0 / den
    return [p * inv for p in picked]


def _norm2_kernel(x_ref, g_ref, mod_ref, wr_ref, rb_ref, ex_ref, o_ref, gate_ref, *, d,
                  n_lat_tiles):
    is_ctx = (pl.program_id(0) >= n_lat_tiles).astype(jnp.int32)
    h = _norm_mod(x_ref[...], g_ref[...], _mod_row(mod_ref, is_ctx, 3, d),
                  _mod_row(mod_ref, is_ctx, 4, d))
    o_ref[...] = h.astype(BF16)
    logits = _dot_3(h, wr_ref[...])
    lt = jnp.transpose(logits)
    sc = jax.nn.sigmoid(lt[0:N_EXPERTS, :])
    sel = sc + rb_ref[...]
    gate_rows = _route([sel[e:e + 1, :] for e in range(N_EXPERTS)],
                       [sc[e:e + 1, :] for e in range(N_EXPERTS)])
    rows = lax.broadcasted_iota(jnp.int32, lt.shape, 0)
    gt = jnp.zeros(lt.shape, F32)
    for e in range(N_EXPERTS):
        gt = jnp.where(rows == e, gate_rows[e], gt)
    gate = jnp.transpose(gt)
    gate_ref[...] = _dot_x_sel(gate, ex_ref[...])


def norm2_route(cfg, x, gain, mod, w_router_pad, rbias_col, expand):
    tr = cfg.row_tile
    t, d = x.shape
    return pl.pallas_call(
        functools.partial(_norm2_kernel, d=d, n_lat_tiles=cfg.seq // tr),
        out_shape=(jax.ShapeDtypeStruct((t, d), BF16),
                   jax.ShapeDtypeStruct((t, N_EXPERTS * LANES), F32)),
        grid=(t // tr,),
        in_specs=[pl.BlockSpec((tr, d), lambda i: (i, 0)),
                  pl.BlockSpec((1, d), lambda i: (0, 0)),
                  pl.BlockSpec((8, 6 * d), lambda i: (0, 0)),
                  pl.BlockSpec((d, LANES), lambda i: (0, 0)),
                  pl.BlockSpec((N_EXPERTS, 1), lambda i: (0, 0)),
                  pl.BlockSpec((LANES, N_EXPERTS * LANES), lambda i: (0, 0))],
        out_specs=(pl.BlockSpec((tr, d), lambda i: (i, 0)),
                   pl.BlockSpec((tr, N_EXPERTS * LANES), lambda i: (i, 0))),
        compiler_params=_cparams(("parallel",)),
        name="norm2_route",
    )(x, gain.reshape(1, d), mod, w_router_pad, rbias_col, expand)


def _final_norm_kernel(x_ref, g_ref, o_ref):
    x = x_ref[...]
    o_ref[...] = (x * lax.rsqrt(jnp.mean(x * x, axis=-1, keepdims=True) + EPS)) * g_ref[...]


def final_norm(x, gain, n_rows, tr):
    d = x.shape[1]
    return pl.pallas_call(
        _final_norm_kernel,
        out_shape=jax.ShapeDtypeStruct((n_rows, d), F32),
        grid=(n_rows // tr,),
        in_specs=[pl.BlockSpec((tr, d), lambda i: (i, 0)),
                  pl.BlockSpec((1, d), lambda i: (0, 0))],
        out_specs=pl.BlockSpec((tr, d), lambda i: (i, 0)),
        compiler_params=_cparams(("parallel",)),
        name="final_norm",
    )(x, gain.reshape(1, d))


def _mm_plain_kernel(a_ref, w_ref, o_ref):
    o_ref[...] = _dot(a_ref[...], w_ref[...]).astype(o_ref.dtype)


def matmul_plain(a, w, layer, *, tm, tn, n_cols=None, col0=0, out_dtype=F32, name="mm"):
    t, k = a.shape
    n_cols = w.shape[2] - col0 if n_cols is None else n_cols
    off = col0 // tn
    return pl.pallas_call(
        _mm_plain_kernel,
        out_shape=jax.ShapeDtypeStruct((t, n_cols), out_dtype),
        grid=(n_cols // tn, t // tm),
        in_specs=[pl.BlockSpec((tm, k), lambda j, i: (i, 0)),
                  pl.BlockSpec((None, k, tn), lambda j, i: (layer, 0, j + off))],
        out_specs=pl.BlockSpec((tm, tn), lambda j, i: (i, j)),
        compiler_params=_cparams(("parallel", "parallel")),
        name=name,
    )(a, w)


def _mm_resid_kernel(a_ref, w_ref, r_ref, mod_ref, o_ref, *, tm, n_lat):
    row = pl.program_id(1) * tm + lax.broadcasted_iota(jnp.int32, (tm, 1), 0)
    gate = jnp.where(row < n_lat, mod_ref[0:1, :], mod_ref[1:2, :])
    o_ref[...] = r_ref[...] + gate * _dot(a_ref[...], w_ref[...])


def matmul_resid(cfg, a, w, layer, resid, mod, idx, *, tn, name):
    t, k = a.shape
    d = w.shape[2]
    tm = cfg.mm_tm
    return pl.pallas_call(
        functools.partial(_mm_resid_kernel, tm=tm, n_lat=cfg.seq),
        out_shape=jax.ShapeDtypeStruct((t, d), F32),
        grid=(d // tn, t // tm),
        in_specs=[pl.BlockSpec((tm, k), lambda j, i: (i, 0)),
                  pl.BlockSpec((None, k, tn), lambda j, i: (layer, 0, j)),
                  pl.BlockSpec((tm, tn), lambda j, i: (i, j)),
                  pl.BlockSpec((8, tn), lambda j, i: (0, idx * (d // tn) + j))],
        out_specs=pl.BlockSpec((tm, tn), lambda j, i: (i, j)),
        input_output_aliases={2: 0},
        compiler_params=_cparams(("parallel", "parallel")),
        name=name,
    )(a, w, resid, mod)


def _mm_moe_act_kernel(a_ref, w1_ref, w3_ref, g_ref, o_ref, *, de, n_e):
    a = a_ref[...]
    for e in range(n_e):
        up = _dot(a, w1_ref[e])
        lin = _dot(a, w3_ref[e])
        g = g_ref[:, e * LANES:(e + 1) * LANES]
        g = jnp.concatenate([g] * (de // LANES), axis=1)
        o_ref[:, e * de:(e + 1) * de] = (_silu(up) * lin * g).astype(o_ref.dtype)


def matmul_moe_act(cfg, h, w1, w3, layer, gate_rep, *, n_e=2):
    t, k = h.shape
    de = cfg.d_expert
    tm = cfg.mm_tm
    wspec = pl.BlockSpec((None, n_e, k, de), lambda j, i: (layer, j, 0, 0))
    return pl.pallas_call(
        functools.partial(_mm_moe_act_kernel, de=de, n_e=n_e),
        out_shape=jax.ShapeDtypeStruct((t, N_EXPERTS * de), BF16),
        grid=(N_EXPERTS // n_e, t // tm),
        in_specs=[pl.BlockSpec((tm, k), lambda j, i: (i, 0)), wspec, wspec,
                  pl.BlockSpec((tm, n_e * LANES), lambda j, i: (i, j))],
        out_specs=pl.BlockSpec((tm, n_e * de), lambda j, i: (i, j)),
        compiler_params=_cparams(("parallel", "parallel")),
        name="moe_up",
    )(h, w1, w3, gate_rep)


def _merge_kernel(a0_ref, a1_ref, a2_ref, zg_ref, wb_ref, mu_ref, mb_ref, o_ref):
    zg = zg_ref[...].astype(BF16)
    acc = None
    for n, a_ref in enumerate((a0_ref, a1_ref, a2_ref)):
        y = _dot(a_ref[...], wb_ref[n])
        gate = jax.nn.sigmoid(_dot(zg, mu_ref[n]) + mb_ref[n])
        acc = gate * y if acc is None else acc + gate * y
    o_ref[...] = acc.astype(o_ref.dtype)


def merge_branches(cfg, outs, z_small, w_branch, merge_up, merge_b, layer, *, tn):
    t, bw = outs[0].shape
    d = w_branch.shape[3]
    r = cfg.merge_rank
    tm = cfg.mm_tm
    a_spec = pl.BlockSpec((tm, bw), lambda j, i: (i, 0))
    return pl.pallas_call(
        _merge_kernel,
        out_shape=jax.ShapeDtypeStruct((t, d), BF16),
        grid=(d // tn, t // tm),
        in_specs=[a_spec, a_spec, a_spec,
                  pl.BlockSpec((tm, r), lambda j, i: (i, 0)),
                  pl.BlockSpec((None, 3, bw, tn), lambda j, i: (layer, 0, 0, j)),
                  pl.BlockSpec((None, 3, r, tn), lambda j, i: (layer, 0, 0, j)),
                  pl.BlockSpec((None, 3, 1, tn), lambda j, i: (layer, 0, 0, j))],
        out_specs=pl.BlockSpec((tm, tn), lambda j, i: (i, j)),
        compiler_params=_cparams(("parallel", "parallel")),
        name="merge",
    )(outs[0], outs[1], outs[2], z_small, w_branch, merge_up, merge_b)


HALO_ROWS = 16


def _halo_specs(tr, width, col_block, n_rows):
    rb = tr // HALO_ROWS
    last = n_rows // HALO_ROWS - 1
    main = pl.BlockSpec((tr, width), lambda i: (i, col_block))
    prev = pl.BlockSpec((HALO_ROWS, width), lambda i: (jnp.maximum(i * rb - 1, 0), col_block))
    nxt = pl.BlockSpec((HALO_ROWS, width), lambda i: (jnp.minimum((i + 1) * rb, last), col_block))
    return main, prev, nxt


def _conv3(z_ref, prev_ref, next_ref, w_ref, n_lat_tiles):
    i = pl.program_id(0)
    x = z_ref[...].astype(F32)
    tr = x.shape[0]
    has_prev = jnp.logical_and(i != 0, i != n_lat_tiles).astype(F32)
    has_next = jnp.logical_and(i != n_lat_tiles - 1, i != n_lat_tiles).astype(F32)
    row = lax.broadcasted_iota(jnp.int32, (tr, 1), 0)
    halo_prev = prev_ref[HALO_ROWS - 1:HALO_ROWS, :].astype(F32) * has_prev
    halo_next = next_ref[0:1, :].astype(F32) * has_next
    x_prev = jnp.where(row == 0, halo_prev, pltpu.roll(x, 1, axis=0))
    x_next = jnp.where(row == tr - 1, halo_next, pltpu.roll(x, tr - 1, axis=0))
    return x_prev * w_ref[0:1, :] + x * w_ref[1:2, :] + x_next * w_ref[2:3, :]


def _tri_masks(n):
    r = lax.broadcasted_iota(jnp.int32, (n, n), 0)
    c = lax.broadcasted_iota(jnp.int32, (n, n), 1)
    return r, c


def _gdn_prep_kernel(z_ref, zp_ref, zn_ref, s_ref, cw_ref, alog_ref, dtb_ref,
                     q_ref, k_ref, v_ref, bb_ref, gcb_ref, gct_ref, *, n_heads, n_lat_tiles):
    tr = z_ref.shape[0]
    hw = n_heads * 128
    y = _silu(_conv3(z_ref, zp_ref, zn_ref, cw_ref, n_lat_tiles))
    for h in range(n_heads):
        q = y[:, h * 128:(h + 1) * 128]
        k = y[:, hw + h * 128:hw + (h + 1) * 128]
        q = q * (lax.rsqrt(jnp.sum(q * q, axis=-1, keepdims=True) + EPS) * (128.0 ** -0.5))
        k = k * lax.rsqrt(jnp.sum(k * k, axis=-1, keepdims=True) + EPS)
        q_ref[h] = q.astype(BF16)
        k_ref[h] = k.astype(BF16)
        v_ref[h] = y[:, 2 * hw + h * 128:2 * hw + (h + 1) * 128].astype(BF16)
    s = s_ref[...]
    nh2 = 2 * n_heads
    beta = jax.nn.sigmoid(s)
    g = -jnp.exp(alog_ref[...]) * jax.nn.softplus(s + dtb_ref[...])
    r, c = _tri_masks(tr)
    incl_lo = (c <= r).astype(BF16)
    incl_up = (c >= r).astype(BF16)
    lane = lax.broadcasted_iota(jnp.int32, (tr, LANES), 1)
    fwd_lane = lane < nh2 + n_heads
    gc = jnp.where(fwd_lane, _dot_sel(incl_lo, g), _dot_sel(incl_up, g))
    for ch in range(nh2):
        bb_ref[ch] = jnp.broadcast_to(beta[:, ch:ch + 1], (tr, LANES))
        gcb_ref[ch] = jnp.broadcast_to(gc[:, nh2 + ch:nh2 + ch + 1], (tr, LANES))
    gct_ref[...] = jnp.transpose(gc)[nh2:2 * nh2, :]


def gdn_prep(cfg, z_big, z_small, conv_w, alog_row, dtb_row):
    tr = cfg.row_tile
    t = z_big.shape[0]
    nh = cfg.gdn_h
    hw = nh * 128
    n_tiles = t // tr
    main, prev, nxt = _halo_specs(tr, 3 * hw, 0, t)
    head_out = jax.ShapeDtypeStruct((nh, t, 128), BF16)
    head_spec = pl.BlockSpec((nh, tr, 128), lambda i: (0, i, 0))
    col_out = jax.ShapeDtypeStruct((2 * nh, t, LANES), F32)
    col_spec = pl.BlockSpec((2 * nh, tr, LANES), lambda i: (0, i, 0))
    return pl.pallas_call(
        functools.partial(_gdn_prep_kernel, n_heads=nh, n_lat_tiles=cfg.seq // tr),
        out_shape=(head_out, head_out, head_out, col_out, col_out,
                   jax.ShapeDtypeStruct((n_tiles, 2 * nh, tr), F32)),
        grid=(n_tiles,),
        in_specs=[main, prev, nxt,
                  pl.BlockSpec((tr, LANES), lambda i: (i, cfg.merge_rank // LANES)),
                  pl.BlockSpec((SHORT_CONV, 3 * hw), lambda i: (0, 0)),
                  pl.BlockSpec((1, LANES), lambda i: (0, 0)),
                  pl.BlockSpec((1, LANES), lambda i: (0, 0))],
        out_specs=(head_spec, head_spec, head_spec, col_spec, col_spec,
                   pl.BlockSpec((None, 2 * nh, tr), lambda i: (i, 0, 0))),
        compiler_params=_cparams(("parallel",)),
        name="gdn_prep",
    )(z_big, z_big, z_big, z_small, conv_w, alog_row, dtb_row)


def _unit_tri_inverses(n_mats, r, c):
    n = n_mats[0].shape[0]
    eye = (r == c).astype(F32)

    def same_block(b):
        sh = int(math.log2(b))
        return (r >> sh) == (c >> sh)

    blk = same_block(8)
    pfs = [jnp.where(blk, -m, 0.0) for m in n_mats]
    ps = [pf.astype(BF16) for pf in pfs]
    p2 = [_dot(p, p).astype(BF16) for p in ps]
    p4 = [_dot(x, x).astype(BF16) for x in p2]
    ts = [eye + pf for pf in pfs]
    ts = [t + _dot(t.astype(BF16), x) for t, x in zip(ts, p2)]
    ts = [t + _dot(t.astype(BF16), x) for t, x in zip(ts, p4)]
    b = 8
    while b < n:
        sel = jnp.logical_and(same_block(2 * b), jnp.logical_not(same_block(b)))
        offs = [jnp.where(sel, m, 0.0).astype(BF16) for m in n_mats]
        tbs = [t.astype(BF16) for t in ts]
        xs = [_dot(tb, off).astype(BF16) for tb, off in zip(tbs, offs)]
        ts = [t - _dot(x, tb) for t, x, tb in zip(ts, xs, tbs)]
        b *= 2
    return ts, eye


GDN_HEADS_PER_STEP = 4


def _gdn_chunk_kernel(q_ref, k_ref, v_ref, bf_ref, bb_ref, gf_ref, gb_ref, gct_ref,
                      uw_ref, aq_ref, qk_ref, gl_ref, *, n_heads):
    hps = q_ref.shape[0]
    h0 = pl.program_id(1) * hps
    tr = q_ref.shape[1]
    r, c = _tri_masks(tr)
    nt = (((1,), (1,)), ((), ()))
    masks = (((c <= r), (c < r)), ((c >= r), (c > r)))
    chains = [(j, d) for j in range(hps) for d in range(2)]
    k_b = [k_ref[j] for j in range(hps)]
    k_f = [x.astype(F32) for x in k_b]
    a_qk = [lax.dot_general(q_ref[j], k_b[j], nt, preferred_element_type=F32) for j in range(hps)]
    beta, gc, gam, kb, n_mats = {}, {}, {}, {}, []
    for j, d in chains:
        beta[j, d] = (bf_ref, bb_ref)[d][j]
        gc[j, d] = (gf_ref, gb_ref)[d][j]
        gc_row = gct_ref[pl.ds(d * n_heads + h0 + j, 1), :]
        gc_col = jnp.concatenate([gc[j, d]] * (tr // LANES), axis=1)
        gam[j, d] = jnp.exp(jnp.where(masks[d][0], gc_col - gc_row, -jnp.inf))
        kb[j, d] = k_f[j] * beta[j, d]
        a_kk = lax.dot_general(kb[j, d].astype(BF16), k_b[j], nt, preferred_element_type=F32)
        n_mats.append(jnp.where(masks[d][1], a_kk * gam[j, d], 0.0))
    t_invs, eye = _unit_tri_inverses(n_mats, r, c)
    for (j, d), t_inv in zip(chains, t_invs):
        e = jnp.exp(gc[j, d])
        rhs = jnp.concatenate([v_ref[j].astype(F32) * beta[j, d], kb[j, d] * e], axis=1)
        sol = rhs + _dot((t_inv - eye).astype(BF16), rhs.astype(BF16))
        g_last = gc[j, d][tr - 1:tr, :] if d == 0 else gc[j, d][0:1, :]
        uw_ref[d, j] = sol.astype(BF16)
        aq_ref[d, j] = (a_qk[j] * gam[j, d]).astype(BF16)
        qk_ref[d, j] = jnp.concatenate([q_ref[j].astype(F32) * e,
                                        k_f[j] * jnp.exp(g_last - gc[j, d])], axis=1).astype(BF16)
        gl_ref[d, j] = jnp.broadcast_to(jnp.exp(g_last), (8, LANES))


def gdn_chunks(cfg, qn, kn, vs, bb, gcb, gct):
    tr = cfg.row_tile
    nh, t, _ = qn.shape
    hps = min(GDN_HEADS_PER_STEP, nh)
    assert nh % hps == 0
    n_tiles = t // tr
    head = pl.BlockSpec((hps, tr, 128), lambda i, h: (h, i, 0))
    head_b = pl.BlockSpec((hps, tr, 128), lambda i, h: (h + nh // hps, i, 0))
    big = jax.ShapeDtypeStruct((2, nh, t, 2 * 128), BF16)
    aq = jax.ShapeDtypeStruct((2, nh, t, tr), BF16)
    return pl.pallas_call(
        functools.partial(_gdn_chunk_kernel, n_heads=nh),
        out_shape=(big, aq, big, jax.ShapeDtypeStruct((2, nh, n_tiles * 8, LANES), F32)),
        grid=(n_tiles, nh // hps),
        in_specs=[head, head, head, head, head_b, head, head_b,
                  pl.BlockSpec((None, 2 * nh, tr), lambda i, h: (i, 0, 0))],
        out_specs=(pl.BlockSpec((2, hps, tr, 256), lambda i, h: (0, h, i, 0)),
                   pl.BlockSpec((2, hps, tr, tr), lambda i, h: (0, h, i, 0)),
                   pl.BlockSpec((2, hps, tr, 256), lambda i, h: (0, h, i, 0)),
                   pl.BlockSpec((2, hps, 8, LANES), lambda i, h: (0, h, i, 0))),
        compiler_params=_cparams(("parallel", "parallel")),
        name="gdn_chunks",
    )(qn, kn, vs, bb, bb, gcb, gcb, gct)


def _gdn_scan_kernel(uwf_ref, aqf_ref, qkf_ref, glf_ref, uwb_ref, aqb_ref, qkb_ref, glb_ref,
                     of_ref, ob_ref, s_ref, *, n_heads):
    @pl.when(pl.program_id(0) == 0)
    def _():
        s_ref[...] = jnp.zeros_like(s_ref)

    tn = (((0,), (0,)), ((), ()))
    for d, (uw_ref, aq_ref, qk_ref, gl_ref, o_ref) in enumerate(
            ((uwf_ref, aqf_ref, qkf_ref, glf_ref, of_ref), (uwb_ref, aqb_ref, qkb_ref, glb_ref, ob_ref))):
        for h in range(n_heads):
            st = s_ref[d, h]
            st_b = st.astype(BF16)
            uw = uw_ref[h]
            qk = qk_ref[h]
            v_new = uw[:, 0:128].astype(F32) - _dot(uw[:, 128:256], st_b)
            v_new_b = v_new.astype(BF16)
            o_ref[:, h * 128:(h + 1) * 128] = (
                _dot(qk[:, 0:128], st_b) + _dot(aq_ref[h], v_new_b)).astype(o_ref.dtype)
            s_ref[d, h] = gl_ref[h][0:1, :] * st + lax.dot_general(
                qk[:, 128:256], v_new_b, tn, preferred_element_type=F32)


def gdn_scan(cfg, uw, aq, qk, gl):
    tr = cfg.row_tile
    _, nh, t, _ = uw.shape
    n_tiles = t // tr
    last = n_tiles - 1

    def fwd(s):
        return jnp.where(s == 0, last, s - 1)

    def bwd(s):
        return jnp.where(s == 0, last, last - s)

    def specs(d, order):
        return [pl.BlockSpec((None, nh, tr, 256), lambda s: (d, 0, order(s), 0)),
                pl.BlockSpec((None, nh, tr, tr), lambda s: (d, 0, order(s), 0)),
                pl.BlockSpec((None, nh, tr, 256), lambda s: (d, 0, order(s), 0)),
                pl.BlockSpec((None, nh, 8, LANES), lambda s: (d, 0, order(s), 0))]

    out = jax.ShapeDtypeStruct((t, nh * 128), BF16)
    return pl.pallas_call(
        functools.partial(_gdn_scan_kernel, n_heads=nh),
        out_shape=(out, out),
        grid=(n_tiles,),
        in_specs=specs(0, fwd) + specs(1, bwd),
        out_specs=(pl.BlockSpec((tr, nh * 128), lambda s: (fwd(s), 0)),
                   pl.BlockSpec((tr, nh * 128), lambda s: (bwd(s), 0))),
        scratch_shapes=[pltpu.VMEM((2, nh, 128, 128), F32)],
        compiler_params=_cparams(("arbitrary",)),
        name="gdn_scan",
    )(uw, aq, qk, gl, uw, aq, qk, gl)


def _head_norm_kernel(of_ref, ob_ref, gate_ref, w_ref, o_ref, *, n_heads, dv):
    for h in range(n_heads):
        sl = slice(h * dv, (h + 1) * dv)
        o = of_ref[:, sl].astype(F32) + ob_ref[:, sl].astype(F32)
        o = o * lax.rsqrt(jnp.mean(o * o, axis=-1, keepdims=True) + EPS) * w_ref[...]
        o_ref[:, sl] = (o * _silu(gate_ref[:, sl].astype(F32))).astype(o_ref.dtype)


def head_norm(o_f, o_b, z_big, gate_col_block, norm_w, n_heads, dv, tr, gate_row0=0):
    t, w = o_f.shape
    spec = pl.BlockSpec((tr, w), lambda i: (i, 0))
    rb0 = gate_row0 // tr
    return pl.pallas_call(
        functools.partial(_head_norm_kernel, n_heads=n_heads, dv=dv),
        out_shape=jax.ShapeDtypeStruct((t, w), BF16),
        grid=(t // tr,),
        in_specs=[spec, spec, pl.BlockSpec((tr, w), lambda i: (i + rb0, gate_col_block)),
                  pl.BlockSpec((1, dv), lambda i: (0, 0))],
        out_specs=spec,
        compiler_params=_cparams(("parallel",)),
        name="head_norm",
    )(o_f, o_b, z_big, norm_w.reshape(1, dv))


GLA_TILE = 128
GLA_LR_LANE0 = 32


def _gla_direction(d, qk_ref, v_ref, sm_ref, wup_ref, gb_ref, o_ref, st_ref, n_heads):
    tr = GLA_TILE
    qkw = n_heads * 128
    r, c = _tri_masks(tr)
    sh = int(math.log2(GLA_CHUNK))
    same = (r >> sh) == (c >> sh)
    mask = jnp.logical_and(same, (c <= r) if d == 0 else (c >= r))
    logit = _dot_hi(sm_ref[...], wup_ref[:, d * qkw:(d + 1) * qkw]) + gb_ref[:, d * qkw:(d + 1) * qkw]
    g = jax.nn.log_sigmoid(logit) * (1.0 / GLA_GATE_NORM)
    b = _dot_sel(jnp.where(mask, 1.0, 0.0).astype(BF16), g)
    nt = (((1,), (1,)), ((), ()))
    tn = (((0,), (0,)), ((), ()))
    chunks = range(tr // GLA_CHUNK) if d == 0 else range(tr // GLA_CHUNK - 1, -1, -1)
    for h in range(n_heads):
        bq = b[:, h * 128:(h + 1) * 128]
        q = qk_ref[:, h * 128:(h + 1) * 128].astype(F32)
        k = qk_ref[:, qkw + h * 128:qkw + (h + 1) * 128].astype(F32)
        v_b = v_ref[:, h * 256:(h + 1) * 256].astype(BF16)
        qe = (q * jnp.exp(bq) * (128.0 ** -0.5)).astype(BF16)
        kinv = (k * jnp.exp(-bq)).astype(BF16)
        a = jnp.where(mask, lax.dot_general(qe, kinv, nt, preferred_element_type=F32), 0.0)
        o_intra = _dot(a.astype(BF16), v_b)
        for ci in chunks:
            lo = ci * GLA_CHUNK
            last = lo + GLA_CHUNK - 1 if d == 0 else lo
            b_last = bq[last:last + 1, :]
            kdec = (k[lo:lo + GLA_CHUNK] * jnp.exp(b_last - bq[lo:lo + GLA_CHUNK])).astype(BF16)
            ds = lax.dot_general(kdec, v_b[lo:lo + GLA_CHUNK], tn, preferred_element_type=F32)
            st = st_ref[d, h]
            o_ref[lo:lo + GLA_CHUNK, h * 256:(h + 1) * 256] = (
                o_intra[lo:lo + GLA_CHUNK] + _dot(qe[lo:lo + GLA_CHUNK], st.astype(BF16))).astype(o_ref.dtype)
            dec = jnp.transpose(jnp.broadcast_to(jnp.exp(b_last), (128, 128)))
            st_ref[d, h] = jnp.concatenate([dec, dec], axis=1) * st + ds


def _gla_kernel(qkf_ref, vf_ref, smf_ref, qkb_ref, vb_ref, smb_ref, wup_ref, gb_ref, s0_ref,
                of_ref, ob_ref, sfin_ref, st_ref, *, n_heads):
    @pl.when(pl.program_id(0) == 0)
    def _():
        st_ref[...] = s0_ref[...]

    _gla_direction(0, qkf_ref, vf_ref, smf_ref, wup_ref, gb_ref, of_ref, st_ref, n_heads)
    _gla_direction(1, qkb_ref, vb_ref, smb_ref, wup_ref, gb_ref, ob_ref, st_ref, n_heads)

    @pl.when(pl.program_id(0) == pl.num_programs(0) - 1)
    def _():
        sfin_ref[...] = st_ref[...]


def _gla_call(cfg, n_steps, arrays, spec_fn, out_rows_shape, out_spec_fn, wup, gbias, s0, name):
    nh = cfg.gla_h
    fwd = lambda s: s
    bwd = lambda s: n_steps - 1 - s
    full = lambda shape: pl.BlockSpec(shape, lambda s: (0,) * len(shape))
    z_view, zs_view = arrays
    st_shape = (2, nh, 128, 256)
    out = jax.ShapeDtypeStruct(out_rows_shape, BF16)
    return pl.pallas_call(
        functools.partial(_gla_kernel, n_heads=nh),
        out_shape=(out, out, jax.ShapeDtypeStruct(st_shape, F32)),
        grid=(n_steps,),
        in_specs=spec_fn(fwd) + spec_fn(bwd) + [full(wup.shape), full(gbias.shape), full(st_shape)],
        out_specs=(out_spec_fn(fwd), out_spec_fn(bwd), full(st_shape)),
        scratch_shapes=[pltpu.VMEM(st_shape, F32)],
        compiler_params=_cparams(("arbitrary",)),
        name=name,
    )(z_view, z_view, zs_view, z_view, z_view, zs_view, wup, gbias, s0)


def gla(cfg, z_big, z_small, col0, wup, gbias):
    nh = cfg.gla_h
    qk2, vw = 2 * nh * 128, nh * 256
    zw, sw = z_big.shape[1], z_small.shape[1]
    w = cfg.grid_w
    rows = cfg.seq // w
    assert rows == GLA_TILE and cfg.ctx % GLA_TILE == 0
    assert zw % qk2 == 0 and zw % vw == 0 and col0 % qk2 == 0 and (col0 + qk2) % vw == 0
    small_blk = cfg.merge_rank // LANES
    s0 = jnp.zeros((2, nh, 128, 256), F32)

    rb0 = cfg.seq // GLA_TILE
    ctx_specs = lambda order: [
        pl.BlockSpec((GLA_TILE, qk2), lambda s: (rb0 + order(s), col0 // qk2)),
        pl.BlockSpec((GLA_TILE, vw), lambda s: (rb0 + order(s), (col0 + qk2) // vw)),
        pl.BlockSpec((GLA_TILE, LANES), lambda s: (rb0 + order(s), small_blk))]
    ctx_out = lambda order: pl.BlockSpec((GLA_TILE, vw), lambda s: (order(s), 0))
    ocf, ocb, s_ctx = _gla_call(cfg, cfg.ctx // GLA_TILE, (z_big, z_small), ctx_specs,
                                (cfg.ctx, vw), ctx_out, wup, gbias, s0, "gla_ctx")

    t = z_big.shape[0]
    zv = z_big.reshape(t // w, w * zw)
    zsv = z_small.reshape(t // w, w * sw)
    lat_specs = lambda order: [
        pl.BlockSpec((GLA_TILE, qk2), lambda s: (0, (order(s) * zw + col0) // qk2)),
        pl.BlockSpec((GLA_TILE, vw), lambda s: (0, (order(s) * zw + col0 + qk2) // vw)),
        pl.BlockSpec((GLA_TILE, LANES), lambda s: (0, order(s) * (sw // LANES) + small_blk))]
    lat_out = lambda order: pl.BlockSpec((GLA_TILE, vw), lambda s: (0, order(s)))
    olf, olb, _ = _gla_call(cfg, w, (zv, zsv), lat_specs, (rows, w * vw), lat_out,
                            wup, gbias, s_ctx, "gla_lat")
    return (olf.reshape(cfg.seq, vw), olb.reshape(cfg.seq, vw)), (ocf, ocb)


FFT_G = 8


def _hy_prep_kernel(z_ref, zp_ref, zn_ref, w_ref, b_ref, v_ref, x1_ref, x2_ref, *, n_lat_tiles, hw):
    y = _conv3(z_ref, zp_ref, zn_ref, w_ref, n_lat_tiles) + b_ref[...]
    for p, o_ref in enumerate((v_ref, x1_ref, x2_ref)):
        o_ref[...] = y[:, p * hw:(p + 1) * hw]


def hyena_prep(cfg, z_hy, conv_w, conv_b):
    tr, hw = cfg.row_tile, cfg.hy_w
    t = z_hy.shape[0]
    main, prev, nxt = _halo_specs(tr, 3 * hw, 0, t)
    out = jax.ShapeDtypeStruct((t, hw), F32)
    ospec = pl.BlockSpec((tr, hw), lambda i: (i, 0))
    return pl.pallas_call(
        functools.partial(_hy_prep_kernel, n_lat_tiles=cfg.seq // tr, hw=hw),
        out_shape=(out, out, out),
        grid=(t // tr,),
        in_specs=[main, prev, nxt,
                  pl.BlockSpec((SHORT_CONV, 3 * hw), lambda i: (0, 0)),
                  pl.BlockSpec((1, 3 * hw), lambda i: (0, 0))],
        out_specs=(ospec, ospec, ospec),
        compiler_params=_cparams(("parallel",)),
        name="hyena_prep",
    )(z_hy, z_hy, z_hy, conv_w, conv_b.reshape(1, 3 * hw))


HY_HALF = LANES // 2


def _hy_taps_kernel(f_ref, w1_ref, b1_ref, w2_ref, b2_ref, fr_ref, w3a0_ref, w3b0_ref, w3a1_ref,
                    w3b1_ref, rate_ref, o0_ref, o1_ref, *, length, tt):
    th = tt // 2
    feats = f_ref[...]
    hid = jnp.sin(fr_ref[0:1, :] * (_dot_3(feats, w1_ref[...]) + b1_ref[...]))
    hid = jnp.sin(fr_ref[1:2, :] * (_dot_3(hid, w2_ref[...]) + b2_ref[...]))
    row = pl.program_id(0) * tt + lax.broadcasted_iota(jnp.int32, (th, 1), 0)
    for part, lane0 in ((0, 0), (1, HY_HALF)):
        n = row + part * th
        window = jnp.where(n == length, 0.0, jnp.exp(-feats[:, lane0:lane0 + 1] * rate_ref[...]))
        for w3_ref, o_ref in (((w3a0_ref, w3b0_ref)[part], o0_ref), ((w3a1_ref, w3b1_ref)[part], o1_ref)):
            o_ref[part * th:(part + 1) * th, :] = (_dot_3(hid, w3_ref[...]) * window).astype(o_ref.dtype)


def hyena_taps(feats, w1p, b1p, w2p, b2p, freqp, w3a, w3b, rates, length, hw):
    tt = min(512, length)
    n_half = length // tt
    out = jax.ShapeDtypeStruct((2 * length, hw), F32)
    ospec = pl.BlockSpec((tt, hw), lambda j: (j, 0))
    sq = pl.BlockSpec((LANES, LANES), lambda j: (0, 0))
    row = pl.BlockSpec((1, LANES), lambda j: (0, 0))
    w3spec = lambda order: pl.BlockSpec((LANES, hw), lambda j: (0, 2 * order + j // n_half))
    return pl.pallas_call(
        functools.partial(_hy_taps_kernel, length=length, tt=tt),
        out_shape=(out, out),
        grid=(2 * n_half,),
        in_specs=[pl.BlockSpec((tt // 2, LANES), lambda j: (j, 0)), sq, row, sq, row,
                  pl.BlockSpec((2, LANES), lambda j: (0, 0)),
                  w3spec(0), w3spec(0), w3spec(1), w3spec(1),
                  pl.BlockSpec((1, hw), lambda j: (0, 0))],
        out_specs=(ospec, ospec),
        compiler_params=_cparams(("parallel",)),
        name="hyena_taps",
    )(feats, w1p, b1p, w2p, b2p, freqp, w3a, w3b, w3a, w3b, rates)


FFT_BT = 8


def _fft_a_kernel(x_ref, l_ref, yr_ref, yi_ref):
    a, bt, c = x_ref.shape
    ph = yr_ref.shape[0]
    y = _dot(l_ref[...], x_ref[...].reshape(a * bt, c).astype(BF16))
    yr_ref[...] = y[0:ph * bt].reshape(ph, bt, c)
    yi_ref[...] = y[ph * bt:2 * ph * bt].reshape(ph, bt, c)


def fft_a(x, lhs, n_pages, ph):
    c = x.shape[1]
    x3 = x.reshape(x.shape[0] // LANES, LANES, c)
    out = jax.ShapeDtypeStruct((ph, LANES, c), F32)
    ospec = pl.BlockSpec((ph, FFT_BT, c), lambda j: (0, j, 0))
    return pl.pallas_call(
        _fft_a_kernel,
        out_shape=(out, out),
        grid=(LANES // FFT_BT,),
        in_specs=[pl.BlockSpec((n_pages, FFT_BT, c), lambda j: (0, j, 0)),
                  pl.BlockSpec(lhs.shape, lambda j: (0, 0))],
        out_specs=(ospec, ospec),
        compiler_params=_cparams(("parallel",)),
        name="fft_a",
    )(x3, lhs)


def _fft_b_kernel(yr_ref, yi_ref, m_ref, zr_ref, zi_ref):
    for g in range(FFT_G):
        s = jnp.concatenate([yr_ref[g], yi_ref[g]], axis=0).astype(BF16)
        z = _dot(m_ref[g], s)
        zr_ref[g] = z[0:LANES].astype(zr_ref.dtype)
        zi_ref[g] = z[LANES:2 * LANES].astype(zi_ref.dtype)


def fft_b(yr, yi, m2):
    ph, _, c = yr.shape
    out = jax.ShapeDtypeStruct((ph, LANES, c), BF16)
    spec = pl.BlockSpec((FFT_G, LANES, c), lambda j: (j, 0, 0))
    return pl.pallas_call(
        _fft_b_kernel,
        out_shape=(out, out),
        grid=(ph // FFT_G,),
        in_specs=[spec, spec, pl.BlockSpec((FFT_G, 2 * LANES, 2 * LANES), lambda j: (j, 0, 0))],
        out_specs=(spec, spec),
        compiler_params=_cparams(("parallel",)),
        name="fft_b",
    )(yr, yi, m2)


FFT_GM = 4


def _spec_mul_kernel(yr_ref, yi_ref, hr_ref, hi_ref, m2_ref, ma_ref, vr_ref, vi_ref):
    for g in range(FFT_GM):
        z = _dot(m2_ref[g], jnp.concatenate([yr_ref[g], yi_ref[g]], axis=0).astype(BF16))
        zr, zi = z[0:LANES], z[LANES:2 * LANES]
        hr, hi = hr_ref[g].astype(F32), hi_ref[g].astype(F32)
        s = jnp.concatenate([zr * hr - zi * hi, zr * hi + zi * hr], axis=0).astype(BF16)
        v = _dot(ma_ref[g], s)
        vr_ref[g] = v[0:LANES]
        vi_ref[g] = v[LANES:2 * LANES]


def spectrum_multiply(yr, yi, hr, hi, m2, ma):
    ph, _, c = yr.shape
    out = jax.ShapeDtypeStruct((ph, LANES, c), F32)
    spec = pl.BlockSpec((FFT_GM, LANES, c), lambda j: (j, 0, 0))
    mspec = pl.BlockSpec((FFT_GM, 2 * LANES, 2 * LANES), lambda j: (j, 0, 0))
    return pl.pallas_call(
        _spec_mul_kernel,
        out_shape=(out, out),
        grid=(ph // FFT_GM,),
        in_specs=[spec, spec, spec, spec, mspec, mspec],
        out_specs=(spec, spec),
        compiler_params=_cparams(("parallel",)),
        name="spectrum_multiply",
    )(yr, yi, hr, hi, m2, ma)


def _ifft_b_kernel(vr_ref, vi_ref, l_ref, x_ref, u_ref, sk_ref, o_ref):
    ph, bt, c = vr_ref.shape
    s = jnp.concatenate([vr_ref[...].reshape(ph * bt, c), vi_ref[...].reshape(ph * bt, c)],
                        axis=0).astype(BF16)
    y = _dot(l_ref[...], s).reshape(o_ref.shape)
    o_ref[...] = x_ref[...] * (y + sk_ref[...] * u_ref[...])


def ifft_b_gate(vr, vi, lhs, gate_x, u, skip_row):
    ph, _, c = vr.shape
    a_out = lhs.shape[0] // FFT_BT
    x3 = gate_x.reshape(gate_x.shape[0] // LANES, LANES, c)
    u3 = u.reshape(u.shape[0] // LANES, LANES, c)
    vspec = pl.BlockSpec((ph, FFT_BT, c), lambda j: (0, j, 0))
    tspec = pl.BlockSpec((a_out, FFT_BT, c), lambda j: (0, j, 0))
    out = pl.pallas_call(
        _ifft_b_kernel,
        out_shape=jax.ShapeDtypeStruct((a_out, LANES, c), F32),
        grid=(LANES // FFT_BT,),
        in_specs=[vspec, vspec, pl.BlockSpec(lhs.shape, lambda j: (0, 0)), tspec, tspec,
                  pl.BlockSpec((1, 1, c), lambda j: (0, 0, 0))],
        out_specs=tspec,
        compiler_params=_cparams(("parallel",)),
        name="ifft_b",
    )(vr, vi, lhs, x3, u3, skip_row.reshape(1, 1, c))
    return out.reshape(a_out * LANES, c)


def _hy_ctx_kernel(v_ref, x1_ref, x2_ref, t0_ref, t1_ref, sk_ref, f_ref, g_ref, o_ref, *, n):
    def conv(u, taps_ref):
        us = _dot(f_ref[:, 0:n], u.astype(BF16))
        hs = _dot(f_ref[...], taps_ref[...].astype(BF16))
        ur, ui, hr, hi = us[0:2 * n], us[2 * n:4 * n], hs[0:2 * n], hs[2 * n:4 * n]
        prod = jnp.concatenate([ur * hr - ui * hi, ur * hi + ui * hr], axis=0).astype(BF16)
        return _dot(g_ref[...], prod)

    v = v_ref[...]
    y = x1_ref[...] * (conv(v, t0_ref) + sk_ref[0:1, :] * v)
    o_ref[...] = x2_ref[...] * (conv(y, t1_ref) + sk_ref[1:2, :] * y)


def hyena_ctx(cfg, vxx, taps0, taps1, skip, fmat, gmat):
    n, hw = cfg.ctx, cfg.hy_w
    cb = 256
    rb = cfg.seq // n
    part = pl.BlockSpec((n, cb), lambda j: (rb, j))
    tspec = pl.BlockSpec((2 * n, cb), lambda j: (0, j))
    return pl.pallas_call(
        functools.partial(_hy_ctx_kernel, n=n),
        out_shape=jax.ShapeDtypeStruct((n, hw), F32),
        grid=(hw // cb,),
        in_specs=[part, part, part, tspec, tspec,
                  pl.BlockSpec((2, cb), lambda j: (0, j)),
                  pl.BlockSpec((4 * n, 2 * n), lambda j: (0, 0)),
                  pl.BlockSpec((n, 4 * n), lambda j: (0, 0))],
        out_specs=pl.BlockSpec((n, cb), lambda j: (0, j)),
        compiler_params=_cparams(("parallel",)),
        name="hyena_ctx",
    )(vxx[0], vxx[1], vxx[2], taps0, taps1, skip, fmat, gmat)


class HyenaConsts(NamedTuple):
    feats: jax.Array
    rates: jax.Array
    ph: int
    la_data: jax.Array
    la_taps: jax.Array
    m2: jax.Array
    ma: jax.Array
    lb: jax.Array
    feats_ctx: jax.Array
    f_ctx: jax.Array
    g_ctx: jax.Array


def _features(length):
    n = np.arange(2 * length, dtype=np.float64)
    pos = np.where(n < length, n, 2 * length - n)
    bands = (HY_EMB - 1) // 2
    f = np.linspace(1e-4, bands - 1, bands)
    omega = (2.0 * math.pi / length) * pos
    feats = np.zeros((2 * length, HY_HALF), np.float64)
    feats[:, 0] = pos / (length - 1)
    feats[:, 1:1 + bands] = np.cos(omega[:, None] * f[None, :])
    feats[:, 1 + bands:1 + 2 * bands] = -np.sin(omega[:, None] * f[None, :])
    tt = min(512, length)
    tiles = feats.reshape(2 * length // tt, 2, tt // 2, HY_HALF)
    packed = np.concatenate([tiles[:, 0], tiles[:, 1]], axis=-1).reshape(length, LANES)
    return jnp.asarray(packed, F32)


def hyena_consts(cfg):
    length, n, hw = cfg.seq, cfg.ctx, cfg.hy_w
    big_n = 2 * length
    p = big_n // LANES
    rates = np.abs(np.linspace(math.log(HY_TARGET) / HY_FAST_DECAY, math.log(HY_TARGET) / HY_SLOW_DECAY, hw))
    ph = p // 2 + 8
    k1 = np.arange(ph)
    kept = (k1 <= p // 2).astype(np.float64)
    a = np.arange(p)
    ang_a = 2.0 * math.pi * ((k1[:, None] * a[None, :]) % p) / p
    dft_a = np.concatenate([np.cos(ang_a), -np.sin(ang_a)], axis=0) * np.tile(kept, 2)[:, None]
    eye = np.eye(FFT_BT)
    la_taps = np.kron(dft_a, eye)
    la_data = np.kron(dft_a[:, :p // 2], eye)
    weight = kept * np.where((k1 == 0) | (k1 == p // 2), 1.0, 2.0) / big_n
    inv_a = (np.concatenate([np.cos(ang_a), -np.sin(ang_a)], axis=0) * np.tile(weight, 2)[:, None]).T
    lb = np.kron(inv_a[:p // 2], eye)
    b = np.arange(LANES)
    ang_b = 2.0 * math.pi * (((b[:, None] * b[None, :]) % LANES) / LANES)[None] \
        + 2.0 * math.pi * (k1[:, None, None] * b[None, None, :]) / big_n
    fr, fi = np.cos(ang_b), -np.sin(ang_b)
    m2 = np.concatenate([np.concatenate([fr, -fi], axis=2), np.concatenate([fi, fr], axis=2)], axis=1)
    gr, gi = fr.transpose(0, 2, 1), -fi.transpose(0, 2, 1)
    ma = np.concatenate([np.concatenate([gr, -gi], axis=2), np.concatenate([gi, gr], axis=2)], axis=1)
    kk = np.arange(2 * n)
    ac = 2.0 * math.pi * ((kk[:, None] * kk[None, :]) % (2 * n)) / (2 * n)
    f_ctx = np.concatenate([np.cos(ac), -np.sin(ac)], axis=0)
    g_ctx = np.concatenate([np.cos(ac), -np.sin(ac)], axis=1)[:n] / (2 * n)
    bf = lambda x: jnp.asarray(x, BF16)
    return HyenaConsts(_features(length), jnp.asarray(rates[None, :], F32), ph, bf(la_data), bf(la_taps),
                       bf(m2), bf(ma), bf(lb), _features(n), bf(f_ctx), bf(g_ctx))


def hyena(cfg, hc, vxx, filt_w, skip, with_ctx=True):
    length, n, hw = cfg.seq, cfg.ctx, cfg.hy_w
    half = length // LANES
    y_lat = vxx[0]
    taps = hyena_taps(hc.feats, *filt_w, hc.rates, length, hw)
    for order in range(2):
        hr, hi = fft_b(*fft_a(taps[order], hc.la_taps, 2 * half, hc.ph), hc.m2)
        yr, yi = fft_a(y_lat, hc.la_data, half, hc.ph)
        vr, vi = spectrum_multiply(yr, yi, hr, hi, hc.m2, hc.ma)
        y_lat = ifft_b_gate(vr, vi, hc.lb, vxx[1 + order], y_lat, skip[order:order + 1])
    if not with_ctx:
        return y_lat, jnp.zeros((n, hw), F32)
    taps_c = hyena_taps(hc.feats_ctx, *filt_w, hc.rates, n, hw)
    y_ctx = hyena_ctx(cfg, vxx, taps_c[0], taps_c[1], skip, hc.f_ctx, hc.g_ctx)
    return y_lat, y_ctx


def expert_expand():
    m = np.zeros((LANES, N_EXPERTS * LANES), np.float32)
    for e in range(N_EXPERTS):
        m[e, e * LANES:(e + 1) * LANES] = 1.0
    return jnp.asarray(m, BF16)


def prep_gla_gate(gate_up, gate_b):
    l, _, r, qk = gate_up.shape
    w = jnp.zeros((l, LANES, 2 * qk), F32)
    for z in range(2):
        w = w.at[:, GLA_LR_LANE0 + z * r:GLA_LR_LANE0 + (z + 1) * r, z * qk:(z + 1) * qk].set(gate_up[:, z])
    return w, gate_b.reshape(l, 1, 2 * qk)


def prep_hyena_filter(w1, b1, w2, b2, w3, freq):
    e, hdim = w1.shape
    assert hdim == HY_HALF
    h = HY_HALF
    w1p = jnp.zeros((LANES, LANES), F32).at[:e, :h].set(w1).at[h:h + e, h:].set(w1)
    w2p = jnp.zeros((LANES, LANES), F32).at[:h, :h].set(w2).at[h:, h:].set(w2)
    zero = jnp.zeros_like(w3)
    w3a = jnp.concatenate([w3, zero], axis=0)
    w3b = jnp.concatenate([zero, w3], axis=0)
    twice = lambda v: jnp.concatenate([v, v], axis=-1).reshape(-1, LANES)
    return w1p, twice(b1), w2p, twice(b2), twice(freq), w3a, w3b


def prep_w2(w2):
    l, e, f, d = w2.shape
    return w2.reshape(l, e * f, d).astype(BF16)


def kernel(x, c, ctx, c_ctx, norm1_g, norm2_g, w_mod, b_mod, w_in, gdn_conv, gdn_a_log, gdn_dt_bias, gdn_norm, gla_gate_up, gla_gate_b, gla_norm, hy_conv_w, hy_conv_b, hy_w1, hy_b1, hy_w2, hy_b2, hy_w3, hy_freq, hy_skip, merge_up, merge_b, w_branch, w_out, w_router, router_bias, moe_w1, moe_w3, moe_w2, final_g):
    cfg = Cfg(d=4096, seq=8192, ctx=256, grid_w=64, gdn_h=8, gla_h=4, hy_w=1024, merge_rank=256,
              d_expert=256, row_tile=256, mm_tm=768)
    return forward(cfg, x, c, ctx, c_ctx, norm1_g, norm2_g, w_mod, b_mod, w_in, gdn_conv, gdn_a_log,
                   gdn_dt_bias, gdn_norm, gla_gate_up, gla_gate_b, gla_norm, hy_conv_w, hy_conv_b, hy_w1,
                   hy_b1, hy_w2, hy_b2, hy_w3, hy_freq, hy_skip, merge_up, merge_b, w_branch, w_out,
                   w_router, router_bias, moe_w1, moe_w3, moe_w2, final_g)


def _split_w_in_kernel(w_ref, gdn_ref, gla_ref, hy_ref, *, gw4, gla0, gla_w, hy0, hy_w):
    def window(start, width):
        base = (start // LANES) * LANES
        span = ((start + width - base + LANES - 1) // LANES) * LANES
        x = w_ref[:, base:base + span]
        if start != base:
            x = pltpu.roll(x, span - (start - base), axis=1)
        return x[:, 0:width].astype(BF16)

    gdn_ref[...] = window(0, gw4)
    gla_ref[...] = window(gla0, gla_w)
    hy_ref[...] = window(hy0, hy_w)


def split_w_in(cfg, w_in, tr=256):
    l, d, n = w_in.shape
    gw4 = 4 * cfg.gdn_w
    gla0 = gw4 + 4 * cfg.gdn_h
    gla_w = 2 * cfg.gla_qk + 2 * cfg.gla_v
    hy0 = gla0 + gla_w + 32
    hy_w = 3 * cfg.hy_w
    n_read = ((hy0 + hy_w + LANES - 1) // LANES) * LANES
    assert n_read <= n
    out = lambda w: jax.ShapeDtypeStruct((l, d, w), BF16)
    ospec = lambda w: pl.BlockSpec((None, tr, w), lambda i, j: (i, j, 0))
    return pl.pallas_call(
        functools.partial(_split_w_in_kernel, gw4=gw4, gla0=gla0, gla_w=gla_w, hy0=hy0, hy_w=hy_w),
        out_shape=(out(gw4), out(gla_w), out(hy_w)),
        grid=(l, d // tr),
        in_specs=[pl.BlockSpec((None, tr, n_read), lambda i, j: (i, j, 0))],
        out_specs=(ospec(gw4), ospec(gla_w), ospec(hy_w)),
        compiler_params=_cparams(("parallel", "parallel")),
        name="split_w_in",
    )(w_in)


def prep_w_small(cfg, w_in):
    gw = cfg.gdn_w
    gdn_small = 4 * cfg.gdn_h
    gdn_in = 4 * gw + gdn_small
    gla_main = 2 * cfg.gla_qk + 2 * cfg.gla_v
    gla_in = gla_main + 32
    rec = gdn_in + gla_in
    hy_in = 3 * cfg.hy_w
    l, d, _ = w_in.shape
    zeros = lambda n: jnp.zeros((l, d, n), BF16)
    return jnp.concatenate([w_in[:, :, rec + hy_in:rec + hy_in + cfg.merge_rank].astype(BF16),
                            w_in[:, :, 4 * gw:gdn_in].astype(BF16), zeros(GLA_LR_LANE0 - gdn_small),
                            w_in[:, :, gdn_in + gla_main:gdn_in + gla_in].astype(BF16),
                            zeros(LANES - GLA_LR_LANE0 - 32)], axis=2)


def forward(cfg, x, c, ctx, c_ctx, norm1_g, norm2_g, w_mod, b_mod, w_in, gdn_conv, gdn_a_log, gdn_dt_bias,
            gdn_norm, gla_gate_up, gla_gate_b, gla_norm, hy_conv_w, hy_conv_b, hy_w1, hy_b1, hy_w2, hy_b2,
            hy_w3, hy_freq, hy_skip, merge_up, merge_b, w_branch, w_out, w_router, router_bias, moe_w1,
            moe_w3, moe_w2, final_g):
    d, tr, tm = cfg.d, cfg.row_tile, cfg.mm_tm
    depth = w_in.shape[0]
    nh = cfg.gdn_h
    tn = min(1024, d)
    w_gdn, w_gla, w_hy = split_w_in(cfg, w_in)
    w_small = prep_w_small(cfg, w_in)
    w1b, w3b = moe_w1.astype(BF16), moe_w3.astype(BF16)
    w2f = prep_w2(moe_w2)
    wb, mu, wo = w_branch.astype(BF16), merge_up.astype(BF16), w_out.astype(BF16)
    mb = merge_b.reshape(depth, 3, 1, d)
    wr_pad = jnp.zeros((d, LANES), F32).at[:, :N_EXPERTS].set(w_router)
    rb_col = router_bias.reshape(N_EXPERTS, 1)
    expand = expert_expand()
    wup, gbias = prep_gla_gate(gla_gate_up, gla_gate_b)
    lane0 = 2 * nh
    alog_rows = jnp.zeros((depth, 1, LANES), F32).at[:, 0, lane0:2 * lane0].set(gdn_a_log.reshape(depth, -1))
    dtb_rows = jnp.zeros((depth, 1, LANES), F32).at[:, 0, lane0:2 * lane0].set(gdn_dt_bias.reshape(depth, -1))
    hc = hyena_consts(cfg)
    gla_gate_blk = (2 * cfg.gla_qk + cfg.gla_v) // cfg.gla_v
    z_proj = functools.partial(matmul_plain, tm=tm, tn=tn, out_dtype=BF16)

    lat = jnp.concatenate([x[0], ctx[0]], axis=0)
    cvec = jnp.zeros((8, d), F32).at[0].set(c[0]).at[1].set(c_ctx)
    mods = modvec(cvec, w_mod, b_mod)

    for l in range(depth):
        with_ctx = l < depth - 1
        mod = mods[l]
        h = norm1(cfg, lat, norm1_g[l], mod)
        z_gdn = z_proj(h, w_gdn, l, name="w_in_gdn")
        z_gla = z_proj(h, w_gla, l, name="w_in_gla")
        z_hy = z_proj(h, w_hy, l, name="w_in_hy")
        z_small = matmul_plain(h, w_small, l, tm=tm, tn=w_small.shape[2], name="w_in_small")
        qn, kn, vs, bb, gcb, gct = gdn_prep(cfg, z_gdn, z_small, gdn_conv[l], alog_rows[l], dtb_rows[l])
        o_f, o_b = gdn_scan(cfg, *gdn_chunks(cfg, qn, kn, vs, bb, gcb, gct))
        a_all = head_norm(o_f, o_b, z_gdn, 3, gdn_norm[l], nh, 128, tr)
        (olf, olb), (ocf, ocb) = gla(cfg, z_gla, z_small, 0, wup[l], gbias[l])
        b_lat = head_norm(olf, olb, z_gla, gla_gate_blk, gla_norm[l], cfg.gla_h, 256, tr)
        b_ctx = head_norm(ocf, ocb, z_gla, gla_gate_blk, gla_norm[l], cfg.gla_h, 256, tr, gate_row0=cfg.seq)
        b_all = jnp.concatenate([b_lat, b_ctx], axis=0)
        vxx = hyena_prep(cfg, z_hy, hy_conv_w[l], hy_conv_b[l])
        filt_w = prep_hyena_filter(hy_w1[l], hy_b1[l], hy_w2[l], hy_b2[l], hy_w3[l], hy_freq[l])
        c_lat, c_ctx_out = hyena(cfg, hc, vxx, filt_w, hy_skip[l], with_ctx)
        c_all = jnp.concatenate([c_lat, c_ctx_out], axis=0).astype(BF16)
        s = merge_branches(cfg, (a_all, b_all, c_all), z_small, wb, mu, mb, l, tn=tn)
        lat = matmul_resid(cfg, s, wo, l, lat, mod, 2, tn=tn, name="w_out")
        h2, gate_rep = norm2_route(cfg, lat, norm2_g[l], mod, wr_pad, rb_col, expand)
        act = matmul_moe_act(cfg, h2, w1b, w3b, l, gate_rep)
        lat = matmul_resid(cfg, act, w2f, l, lat, mod, 5, tn=tn, name="moe_down")
    return final_norm(lat, final_g, cfg.seq, tr)[None]
```

```python
import functools
import math
from typing import NamedTuple

import numpy as np
import jax
import jax.numpy as jnp
from jax import lax
from jax.experimental import pallas as pl
from jax.experimental.pallas import tpu as pltpu

F32 = jnp.float32
BF16 = jnp.bfloat16

EPS = 1e-6
LANES = 128
V7X_VMEM_BYTES = 64 * 1024 * 1024
VMEM_LIMIT = (V7X_VMEM_BYTES * 13) // 16
SHORT_CONV = 3
GLA_CHUNK = 64
GLA_GATE_NORM = 16.0
N_EXPERTS = 16
N_GROUPS = 4
HY_EMB = 33
HY_FAST_DECAY = 0.3
HY_SLOW_DECAY = 1.5
HY_TARGET = 1e-2


class Cfg(NamedTuple):
    d: int
    seq: int
    ctx: int
    grid_w: int
    gdn_h: int
    gla_h: int
    hy_w: int
    merge_rank: int
    d_expert: int
    row_tile: int
    mm_tm: int

    @property
    def t(self):
        return self.seq + self.ctx

    @property
    def gdn_w(self):
        return self.gdn_h * 128

    @property
    def gla_qk(self):
        return self.gla_h * 128

    @property
    def gla_v(self):
        return self.gla_h * 256


def _cparams(sem):
    return pltpu.CompilerParams(dimension_semantics=sem, vmem_limit_bytes=VMEM_LIMIT)


def _split3(x):
    hi = x.astype(BF16)
    r1 = x - hi.astype(F32)
    mid = r1.astype(BF16)
    lo = (r1 - mid.astype(F32)).astype(BF16)
    return hi, mid, lo


def _dot(a, b):
    return jnp.dot(a, b, preferred_element_type=F32)


def _dot_sel(sel_bf16, x):
    hi, mid, lo = _split3(x)
    return _dot(sel_bf16, hi) + _dot(sel_bf16, mid) + _dot(sel_bf16, lo)


def _dot_x_sel(x, sel_bf16):
    hi, mid, lo = _split3(x)
    return _dot(hi, sel_bf16) + _dot(mid, sel_bf16) + _dot(lo, sel_bf16)


def _dot_hi(a, b):
    a1, a2, a3 = _split3(a)
    b1, b2, b3 = _split3(b)
    return (_dot(a1, b1) + (_dot(a1, b2) + _dot(a2, b1))
            + (_dot(a2, b2) + _dot(a1, b3) + _dot(a3, b1)))


def _dot_3(a, b):
    a1 = a.astype(BF16)
    a2 = (a - a1.astype(F32)).astype(BF16)
    b1 = b.astype(BF16)
    b2 = (b - b1.astype(F32)).astype(BF16)
    return _dot(a1, b1) + (_dot(a1, b2) + _dot(a2, b1))


def _silu(x):
    return x * jax.nn.sigmoid(x)


def _modvec_kernel(x_ref, w_ref, b_ref, o_ref):
    o_ref[...] = _dot(_silu(x_ref[...]), w_ref[...]) + b_ref[...]


def modvec(cvec, w_mod, b_mod, tn=512):
    depth, d, n = w_mod.shape
    return pl.pallas_call(
        _modvec_kernel,
        out_shape=jax.ShapeDtypeStruct((depth, 8, n), F32),
        grid=(depth, n // tn),
        in_specs=[pl.BlockSpec((8, d), lambda l, j: (0, 0)),
                  pl.BlockSpec((None, d, tn), lambda l, j: (l, 0, j)),
                  pl.BlockSpec((None, 1, tn), lambda l, j: (l, 0, j))],
        out_specs=pl.BlockSpec((None, 8, tn), lambda l, j: (l, 0, j)),
        compiler_params=_cparams(("parallel", "parallel")),
        name="modvec",
    )(cvec, w_mod, b_mod.reshape(depth, 1, n))


def _mod_row(mod_ref, is_ctx, idx, d):
    return mod_ref[pl.ds(is_ctx, 1), idx * d:(idx + 1) * d]


def _norm_mod(x, gain, shift, scale):
    y = x * lax.rsqrt(jnp.mean(x * x, axis=-1, keepdims=True) + EPS)
    return (y * gain) * (1.0 + scale) + shift


def _norm1_kernel(x_ref, g_ref, mod_ref, o_ref, *, d, n_lat_tiles):
    is_ctx = (pl.program_id(0) >= n_lat_tiles).astype(jnp.int32)
    h = _norm_mod(x_ref[...], g_ref[...], _mod_row(mod_ref, is_ctx, 0, d),
                  _mod_row(mod_ref, is_ctx, 1, d))
    o_ref[...] = h.astype(BF16)


def norm1(cfg, x, gain, mod):
    tr = cfg.row_tile
    t, d = x.shape
    return pl.pallas_call(
        functools.partial(_norm1_kernel, d=d, n_lat_tiles=cfg.seq // tr),
        out_shape=jax.ShapeDtypeStruct((t, d), BF16),
        grid=(t // tr,),
        in_specs=[pl.BlockSpec((tr, d), lambda i: (i, 0)),
                  pl.BlockSpec((1, d), lambda i: (0, 0)),
                  pl.BlockSpec((8, 6 * d), lambda i: (0, 0))],
        out_specs=pl.BlockSpec((tr, d), lambda i: (i, 0)),
        compiler_params=_cparams(("parallel",)),
        name="norm1",
    )(x, gain.reshape(1, d), mod)


def _route(sel_t, sc_t):
    per = N_EXPERTS // N_GROUPS
    grp_score = []
    for g in range(N_GROUPS):
        v = sel_t[g * per:(g + 1) * per]
        best = None
        for a in range(per):
            for b in range(a + 1, per):
                s = v[a] + v[b]
                best = s if best is None else jnp.maximum(best, s)
        grp_score.append(best)
    best_s, best_g = grp_score[0], jnp.zeros_like(grp_score[0])
    for g in range(1, N_GROUPS):
        better = grp_score[g] > best_s
        best_s = jnp.where(better, grp_score[g], best_s)
        best_g = jnp.where(better, float(g), best_g)
    picked = []
    for e in range(N_EXPERTS):
        g, i = divmod(e, per)
        rank = jnp.zeros_like(best_s)
        for j in range(per):
            if j == i:
                continue
            o = sel_t[g * per + j]
            ahead = (o >= sel_t[e]) if j < i else (o > sel_t[e])
            rank = rank + ahead.astype(F32)
        picked.append(jnp.where((best_g == float(g)) & (rank < 2.0), sc_t[e], 0.0))
    den = picked[0]
    for e in range(1, N_EXPERTS):
        den = den + picked[e]
    inv = 1.0 / den
    return [p * inv for p in picked]


def _norm2_kernel(x_ref, g_ref, mod_ref, wr_ref, rb_ref, ex_ref, o_ref, gate_ref, *, d,
                  n_lat_tiles):
    is_ctx = (pl.program_id(0) >= n_lat_tiles).astype(jnp.int32)
    h = _norm_mod(x_ref[...], g_ref[...], _mod_row(mod_ref, is_ctx, 3, d),
                  _mod_row(mod_ref, is_ctx, 4, d))
    o_ref[...] = h.astype(BF16)
    logits = _dot_3(h, wr_ref[...])
    lt = jnp.transpose(logits)
    sc = jax.nn.sigmoid(lt[0:N_EXPERTS, :])
    sel = sc + rb_ref[...]
    gate_rows = _route([sel[e:e + 1, :] for e in range(N_EXPERTS)],
                       [sc[e:e + 1, :] for e in range(N_EXPERTS)])
    rows = lax.broadcasted_iota(jnp.int32, lt.shape, 0)
    gt = jnp.zeros(lt.shape, F32)
    for e in range(N_EXPERTS):
        gt = jnp.where(rows == e, gate_rows[e], gt)
    gate = jnp.transpose(gt)
    gate_ref[...] = _dot_x_sel(gate, ex_ref[...])


def norm2_route(cfg, x, gain, mod, w_router_pad, rbias_col, expand):
    tr = cfg.row_tile
    t, d = x.shape
    return pl.pallas_call(
        functools.partial(_norm2_kernel, d=d, n_lat_tiles=cfg.seq // tr),
        out_shape=(jax.ShapeDtypeStruct((t, d), BF16),
                   jax.ShapeDtypeStruct((t, N_EXPERTS * LANES), F32)),
        grid=(t // tr,),
        in_specs=[pl.BlockSpec((tr, d), lambda i: (i, 0)),
                  pl.BlockSpec((1, d), lambda i: (0, 0)),
                  pl.BlockSpec((8, 6 * d), lambda i: (0, 0)),
                  pl.BlockSpec((d, LANES), lambda i: (0, 0)),
                  pl.BlockSpec((N_EXPERTS, 1), lambda i: (0, 0)),
                  pl.BlockSpec((LANES, N_EXPERTS * LANES), lambda i: (0, 0))],
        out_specs=(pl.BlockSpec((tr, d), lambda i: (i, 0)),
                   pl.BlockSpec((tr, N_EXPERTS * LANES), lambda i: (i, 0))),
        compiler_params=_cparams(("parallel",)),
        name="norm2_route",
    )(x, gain.reshape(1, d), mod, w_router_pad, rbias_col, expand)


def _final_norm_kernel(x_ref, g_ref, o_ref):
    x = x_ref[...]
    o_ref[...] = (x * lax.rsqrt(jnp.mean(x * x, axis=-1, keepdims=True) + EPS)) * g_ref[...]


def final_norm(x, gain, n_rows, tr):
    d = x.shape[1]
    return pl.pallas_call(
        _final_norm_kernel,
        out_shape=jax.ShapeDtypeStruct((n_rows, d), F32),
        grid=(n_rows // tr,),
        in_specs=[pl.BlockSpec((tr, d), lambda i: (i, 0)),
                  pl.BlockSpec((1, d), lambda i: (0, 0))],
        out_specs=pl.BlockSpec((tr, d), lambda i: (i, 0)),
        compiler_params=_cparams(("parallel",)),
        name="final_norm",
    )(x, gain.reshape(1, d))


def _mm_plain_kernel(a_ref, w_ref, o_ref):
    o_ref[...] = _dot(a_ref[...], w_ref[...].astype(BF16)).astype(o_ref.dtype)


def matmul_plain(a, w, layer, *, tm, tn, n_cols=None, col0=0, out_dtype=F32, name="mm"):
    t, k = a.shape
    n_cols = w.shape[2] - col0 if n_cols is None else n_cols
    off = col0 // tn
    return pl.pallas_call(
        _mm_plain_kernel,
        out_shape=jax.ShapeDtypeStruct((t, n_cols), out_dtype),
        grid=(n_cols // tn, t // tm),
        in_specs=[pl.BlockSpec((tm, k), lambda j, i: (i, 0)),
                  pl.BlockSpec((None, k, tn), lambda j, i: (layer, 0, j + off))],
        out_specs=pl.BlockSpec((tm, tn), lambda j, i: (i, j)),
        compiler_params=_cparams(("parallel", "parallel")),
        name=name,
    )(a, w)


def _mm_nt_kernel(a_ref, w_ref, o_ref):
    nt = (((1,), (1,)), ((), ()))
    o_ref[...] = lax.dot_general(a_ref[...], w_ref[...].astype(BF16), nt,
                                 preferred_element_type=F32).astype(o_ref.dtype)


def matmul_nt(a, w_t, layer, *, tm, tn, out_dtype=F32, name="mm_nt"):
    t, k = a.shape
    n = w_t.shape[1]
    return pl.pallas_call(
        _mm_nt_kernel,
        out_shape=jax.ShapeDtypeStruct((t, n), out_dtype),
        grid=(n // tn, t // tm),
        in_specs=[pl.BlockSpec((tm, k), lambda j, i: (i, 0)),
                  pl.BlockSpec((None, tn, k), lambda j, i: (layer, j, 0))],
        out_specs=pl.BlockSpec((tm, tn), lambda j, i: (i, j)),
        compiler_params=_cparams(("parallel", "parallel")),
        name=name,
    )(a, w_t)


def _cast_rows_kernel(w_ref, o_ref):
    o_ref[...] = w_ref[...].astype(o_ref.dtype)


def cast_rows(w_t, row0, n_rows):
    l, _, k = w_t.shape
    tr = math.gcd(row0, n_rows) if row0 else n_rows
    while tr > 512 and tr % 2 == 0:
        tr //= 2
    assert tr % 16 == 0
    off = row0 // tr
    return pl.pallas_call(
        _cast_rows_kernel,
        out_shape=jax.ShapeDtypeStruct((l, n_rows, k), BF16),
        grid=(l, n_rows // tr),
        in_specs=[pl.BlockSpec((None, tr, k), lambda i, j: (i, j + off, 0))],
        out_specs=pl.BlockSpec((None, tr, k), lambda i, j: (i, j, 0)),
        compiler_params=_cparams(("parallel", "parallel")),
        name="cast_rows",
    )(w_t)


def _mm_resid_kernel(a_ref, w_ref, r_ref, mod_ref, o_ref, *, tm, n_lat):
    row = pl.program_id(1) * tm + lax.broadcasted_iota(jnp.int32, (tm, 1), 0)
    gate = jnp.where(row < n_lat, mod_ref[0:1, :], mod_ref[1:2, :])
    o_ref[...] = r_ref[...] + gate * _dot(a_ref[...], w_ref[...])


def matmul_resid(cfg, a, w, layer, resid, mod, idx, *, tn, name):
    t, k = a.shape
    d = w.shape[2]
    tm = cfg.mm_tm
    return pl.pallas_call(
        functools.partial(_mm_resid_kernel, tm=tm, n_lat=cfg.seq),
        out_shape=jax.ShapeDtypeStruct((t, d), F32),
        grid=(d // tn, t // tm),
        in_specs=[pl.BlockSpec((tm, k), lambda j, i: (i, 0)),
                  pl.BlockSpec((None, k, tn), lambda j, i: (layer, 0, j)),
                  pl.BlockSpec((tm, tn), lambda j, i: (i, j)),
                  pl.BlockSpec((8, tn), lambda j, i: (0, idx * (d // tn) + j))],
        out_specs=pl.BlockSpec((tm, tn), lambda j, i: (i, j)),
        input_output_aliases={2: 0},
        compiler_params=_cparams(("parallel", "parallel")),
        name=name,
    )(a, w, resid, mod)


def _mm_moe_act_kernel(a_ref, w1_ref, w3_ref, g_ref, o_ref, *, de, n_e):
    a = a_ref[...]
    for e in range(n_e):
        up = _dot(a, w1_ref[e])
        lin = _dot(a, w3_ref[e])
        g = g_ref[:, e * LANES:(e + 1) * LANES]
        g = jnp.concatenate([g] * (de // LANES), axis=1)
        o_ref[:, e * de:(e + 1) * de] = (_silu(up) * lin * g).astype(o_ref.dtype)


def matmul_moe_act(cfg, h, w1, w3, layer, gate_rep, *, n_e=2):
    t, k = h.shape
    de = cfg.d_expert
    tm = cfg.mm_tm
    wspec = pl.BlockSpec((None, n_e, k, de), lambda j, i: (layer, j, 0, 0))
    return pl.pallas_call(
        functools.partial(_mm_moe_act_kernel, de=de, n_e=n_e),
        out_shape=jax.ShapeDtypeStruct((t, N_EXPERTS * de), BF16),
        grid=(N_EXPERTS // n_e, t // tm),
        in_specs=[pl.BlockSpec((tm, k), lambda j, i: (i, 0)), wspec, wspec,
                  pl.BlockSpec((tm, n_e * LANES), lambda j, i: (i, j))],
        out_specs=pl.BlockSpec((tm, n_e * de), lambda j, i: (i, j)),
        compiler_params=_cparams(("parallel", "parallel")),
        name="moe_up",
    )(h, w1, w3, gate_rep)


def _merge_kernel(a0_ref, a1_ref, a2_ref, zg_ref, wb_ref, mu_ref, mb_ref, o_ref):
    zg = zg_ref[...].astype(BF16)
    acc = None
    for n, a_ref in enumerate((a0_ref, a1_ref, a2_ref)):
        y = _dot(a_ref[...], wb_ref[n])
        gate = jax.nn.sigmoid(_dot(zg, mu_ref[n]) + mb_ref[n])
        acc = gate * y if acc is None else acc + gate * y
    o_ref[...] = acc.astype(o_ref.dtype)


def merge_branches(cfg, outs, z_small, w_branch, merge_up, merge_b, layer, *, tn):
    t, bw = outs[0].shape
    d = w_branch.shape[3]
    r = cfg.merge_rank
    tm = cfg.mm_tm
    a_spec = pl.BlockSpec((tm, bw), lambda j, i: (i, 0))
    return pl.pallas_call(
        _merge_kernel,
        out_shape=jax.ShapeDtypeStruct((t, d), BF16),
        grid=(d // tn, t // tm),
        in_specs=[a_spec, a_spec, a_spec,
                  pl.BlockSpec((tm, r), lambda j, i: (i, 0)),
                  pl.BlockSpec((None, 3, bw, tn), lambda j, i: (layer, 0, 0, j)),
                  pl.BlockSpec((None, 3, r, tn), lambda j, i: (layer, 0, 0, j)),
                  pl.BlockSpec((None, 3, 1, tn), lambda j, i: (layer, 0, 0, j))],
        out_specs=pl.BlockSpec((tm, tn), lambda j, i: (i, j)),
        compiler_params=_cparams(("parallel", "parallel")),
        name="merge",
    )(outs[0], outs[1], outs[2], z_small, w_branch, merge_up, merge_b)


HALO_ROWS = 16


def _halo_specs(tr, width, col_block, n_rows):
    rb = tr // HALO_ROWS
    last = n_rows // HALO_ROWS - 1
    main = pl.BlockSpec((tr, width), lambda i: (i, col_block))
    prev = pl.BlockSpec((HALO_ROWS, width), lambda i: (jnp.maximum(i * rb - 1, 0), col_block))
    nxt = pl.BlockSpec((HALO_ROWS, width), lambda i: (jnp.minimum((i + 1) * rb, last), col_block))
    return main, prev, nxt


def _conv3(z_ref, prev_ref, next_ref, w_ref, n_lat_tiles):
    i = pl.program_id(0)
    x = z_ref[...].astype(F32)
    tr = x.shape[0]
    has_prev = jnp.logical_and(i != 0, i != n_lat_tiles).astype(F32)
    has_next = jnp.logical_and(i != n_lat_tiles - 1, i != n_lat_tiles).astype(F32)
    row = lax.broadcasted_iota(jnp.int32, (tr, 1), 0)
    halo_prev = prev_ref[HALO_ROWS - 1:HALO_ROWS, :].astype(F32) * has_prev
    halo_next = next_ref[0:1, :].astype(F32) * has_next
    x_prev = jnp.where(row == 0, halo_prev, pltpu.roll(x, 1, axis=0))
    x_next = jnp.where(row == tr - 1, halo_next, pltpu.roll(x, tr - 1, axis=0))
    return x_prev * w_ref[0:1, :] + x * w_ref[1:2, :] + x_next * w_ref[2:3, :]


def _tri_masks(n):
    r = lax.broadcasted_iota(jnp.int32, (n, n), 0)
    c = lax.broadcasted_iota(jnp.int32, (n, n), 1)
    return r, c


def _gdn_prep_kernel(z_ref, zp_ref, zn_ref, s_ref, cw_ref, alog_ref, dtb_ref,
                     q_ref, k_ref, v_ref, bb_ref, gcb_ref, gct_ref, *, n_heads, n_lat_tiles):
    tr = z_ref.shape[0]
    hw = n_heads * 128
    y = _silu(_conv3(z_ref, zp_ref, zn_ref, cw_ref, n_lat_tiles))
    for h in range(n_heads):
        q = y[:, h * 128:(h + 1) * 128]
        k = y[:, hw + h * 128:hw + (h + 1) * 128]
        q = q * (lax.rsqrt(jnp.sum(q * q, axis=-1, keepdims=True) + EPS) * (128.0 ** -0.5))
        k = k * lax.rsqrt(jnp.sum(k * k, axis=-1, keepdims=True) + EPS)
        q_ref[h] = q.astype(BF16)
        k_ref[h] = k.astype(BF16)
        v_ref[h] = y[:, 2 * hw + h * 128:2 * hw + (h + 1) * 128].astype(BF16)
    s = s_ref[...]
    nh2 = 2 * n_heads
    beta = jax.nn.sigmoid(s)
    g = -jnp.exp(alog_ref[...]) * jax.nn.softplus(s + dtb_ref[...])
    r, c = _tri_masks(tr)
    incl_lo = (c <= r).astype(BF16)
    incl_up = (c >= r).astype(BF16)
    lane = lax.broadcasted_iota(jnp.int32, (tr, LANES), 1)
    fwd_lane = lane < nh2 + n_heads
    gc = jnp.where(fwd_lane, _dot_sel(incl_lo, g), _dot_sel(incl_up, g))
    for ch in range(nh2):
        bb_ref[ch] = jnp.broadcast_to(beta[:, ch:ch + 1], (tr, LANES))
        gcb_ref[ch] = jnp.broadcast_to(gc[:, nh2 + ch:nh2 + ch + 1], (tr, LANES))
    gct_ref[...] = jnp.transpose(gc)[nh2:2 * nh2, :]


def gdn_prep(cfg, z_big, z_small, conv_w, alog_row, dtb_row):
    tr = cfg.row_tile
    t = z_big.shape[0]
    nh = cfg.gdn_h
    hw = nh * 128
    n_tiles = t // tr
    main, prev, nxt = _halo_specs(tr, 3 * hw, 0, t)
    head_out = jax.ShapeDtypeStruct((nh, t, 128), BF16)
    head_spec = pl.BlockSpec((nh, tr, 128), lambda i: (0, i, 0))
    col_out = jax.ShapeDtypeStruct((2 * nh, t, LANES), F32)
    col_spec = pl.BlockSpec((2 * nh, tr, LANES), lambda i: (0, i, 0))
    return pl.pallas_call(
        functools.partial(_gdn_prep_kernel, n_heads=nh, n_lat_tiles=cfg.seq // tr),
        out_shape=(head_out, head_out, head_out, col_out, col_out,
                   jax.ShapeDtypeStruct((n_tiles, 2 * nh, tr), F32)),
        grid=(n_tiles,),
        in_specs=[main, prev, nxt,
                  pl.BlockSpec((tr, LANES), lambda i: (i, cfg.merge_rank // LANES)),
                  pl.BlockSpec((SHORT_CONV, 3 * hw), lambda i: (0, 0)),
                  pl.BlockSpec((1, LANES), lambda i: (0, 0)),
                  pl.BlockSpec((1, LANES), lambda i: (0, 0))],
        out_specs=(head_spec, head_spec, head_spec, col_spec, col_spec,
                   pl.BlockSpec((None, 2 * nh, tr), lambda i: (i, 0, 0))),
        compiler_params=_cparams(("parallel",)),
        name="gdn_prep",
    )(z_big, z_big, z_big, z_small, conv_w, alog_row, dtb_row)


def _unit_tri_inverses(n_mats, r, c):
    n = n_mats[0].shape[0]
    eye = (r == c).astype(F32)

    def same_block(b):
        sh = int(math.log2(b))
        return (r >> sh) == (c >> sh)

    blk = same_block(8)
    pfs = [jnp.where(blk, -m, 0.0) for m in n_mats]
    ps = [pf.astype(BF16) for pf in pfs]
    p2 = [_dot(p, p).astype(BF16) for p in ps]
    p4 = [_dot(x, x).astype(BF16) for x in p2]
    ts = [eye + pf for pf in pfs]
    ts = [t + _dot(t.astype(BF16), x) for t, x in zip(ts, p2)]
    ts = [t + _dot(t.astype(BF16), x) for t, x in zip(ts, p4)]
    b = 8
    while b < n:
        sel = jnp.logical_and(same_block(2 * b), jnp.logical_not(same_block(b)))
        offs = [jnp.where(sel, m, 0.0).astype(BF16) for m in n_mats]
        tbs = [t.astype(BF16) for t in ts]
        xs = [_dot(tb, off).astype(BF16) for tb, off in zip(tbs, offs)]
        ts = [t - _dot(x, tb) for t, x, tb in zip(ts, xs, tbs)]
        b *= 2
    return ts, eye


GDN_HEADS_PER_STEP = 4


def _gdn_chunk_kernel(q_ref, k_ref, v_ref, bf_ref, bb_ref, gf_ref, gb_ref, gct_ref,
                      uw_ref, aq_ref, qk_ref, gl_ref, *, n_heads):
    hps = q_ref.shape[0]
    h0 = pl.program_id(1) * hps
    tr = q_ref.shape[1]
    r, c = _tri_masks(tr)
    nt = (((1,), (1,)), ((), ()))
    masks = (((c <= r), (c < r)), ((c >= r), (c > r)))
    chains = [(j, d) for j in range(hps) for d in range(2)]
    k_b = [k_ref[j] for j in range(hps)]
    k_f = [x.astype(F32) for x in k_b]
    a_qk = [lax.dot_general(q_ref[j], k_b[j], nt, preferred_element_type=F32) for j in range(hps)]
    beta, gc, gam, kb, n_mats = {}, {}, {}, {}, []
    for j, d in chains:
        beta[j, d] = (bf_ref, bb_ref)[d][j]
        gc[j, d] = (gf_ref, gb_ref)[d][j]
        gc_row = gct_ref[pl.ds(d * n_heads + h0 + j, 1), :]
        gc_col = jnp.concatenate([gc[j, d]] * (tr // LANES), axis=1)
        gam[j, d] = jnp.exp(jnp.where(masks[d][0], gc_col - gc_row, -jnp.inf))
        kb[j, d] = k_f[j] * beta[j, d]
        a_kk = lax.dot_general(kb[j, d].astype(BF16), k_b[j], nt, preferred_element_type=F32)
        n_mats.append(jnp.where(masks[d][1], a_kk * gam[j, d], 0.0))
    t_invs, eye = _unit_tri_inverses(n_mats, r, c)
    for (j, d), t_inv in zip(chains, t_invs):
        e = jnp.exp(gc[j, d])
        rhs = jnp.concatenate([v_ref[j].astype(F32) * beta[j, d], kb[j, d] * e], axis=1)
        sol = rhs + _dot((t_inv - eye).astype(BF16), rhs.astype(BF16))
        g_last = gc[j, d][tr - 1:tr, :] if d == 0 else gc[j, d][0:1, :]
        uw_ref[d, j] = sol.astype(BF16)
        aq_ref[d, j] = (a_qk[j] * gam[j, d]).astype(BF16)
        qk_ref[d, j] = jnp.concatenate([q_ref[j].astype(F32) * e,
                                        k_f[j] * jnp.exp(g_last - gc[j, d])], axis=1).astype(BF16)
        gl_ref[d, j] = jnp.broadcast_to(jnp.exp(g_last), (8, LANES))


def gdn_chunks(cfg, qn, kn, vs, bb, gcb, gct):
    tr = cfg.row_tile
    nh, t, _ = qn.shape
    hps = min(GDN_HEADS_PER_STEP, nh)
    assert nh % hps == 0
    n_tiles = t // tr
    head = pl.BlockSpec((hps, tr, 128), lambda i, h: (h, i, 0))
    head_b = pl.BlockSpec((hps, tr, 128), lambda i, h: (h + nh // hps, i, 0))
    big = jax.ShapeDtypeStruct((2, nh, t, 2 * 128), BF16)
    aq = jax.ShapeDtypeStruct((2, nh, t, tr), BF16)
    return pl.pallas_call(
        functools.partial(_gdn_chunk_kernel, n_heads=nh),
        out_shape=(big, aq, big, jax.ShapeDtypeStruct((2, nh, n_tiles * 8, LANES), F32)),
        grid=(n_tiles, nh // hps),
        in_specs=[head, head, head, head, head_b, head, head_b,
                  pl.BlockSpec((None, 2 * nh, tr), lambda i, h: (i, 0, 0))],
        out_specs=(pl.BlockSpec((2, hps, tr, 256), lambda i, h: (0, h, i, 0)),
                   pl.BlockSpec((2, hps, tr, tr), lambda i, h: (0, h, i, 0)),
                   pl.BlockSpec((2, hps, tr, 256), lambda i, h: (0, h, i, 0)),
                   pl.BlockSpec((2, hps, 8, LANES), lambda i, h: (0, h, i, 0))),
        compiler_params=_cparams(("parallel", "parallel")),
        name="gdn_chunks",
    )(qn, kn, vs, bb, bb, gcb, gcb, gct)


def _gdn_scan_kernel(uwf_ref, aqf_ref, qkf_ref, glf_ref, uwb_ref, aqb_ref, qkb_ref, glb_ref,
                     of_ref, ob_ref, s_ref, *, n_heads):
    @pl.when(pl.program_id(0) == 0)
    def _():
        s_ref[...] = jnp.zeros_like(s_ref)

    tn = (((0,), (0,)), ((), ()))
    for d, (uw_ref, aq_ref, qk_ref, gl_ref, o_ref) in enumerate(
            ((uwf_ref, aqf_ref, qkf_ref, glf_ref, of_ref), (uwb_ref, aqb_ref, qkb_ref, glb_ref, ob_ref))):
        for h in range(n_heads):
            st = s_ref[d, h]
            st_b = st.astype(BF16)
            uw = uw_ref[h]
            qk = qk_ref[h]
            v_new = uw[:, 0:128].astype(F32) - _dot(uw[:, 128:256], st_b)
            v_new_b = v_new.astype(BF16)
            o_ref[:, h * 128:(h + 1) * 128] = (
                _dot(qk[:, 0:128], st_b) + _dot(aq_ref[h], v_new_b)).astype(o_ref.dtype)
            s_ref[d, h] = gl_ref[h][0:1, :] * st + lax.dot_general(
                qk[:, 128:256], v_new_b, tn, preferred_element_type=F32)


def gdn_scan(cfg, uw, aq, qk, gl):
    tr = cfg.row_tile
    _, nh, t, _ = uw.shape
    n_tiles = t // tr
    last = n_tiles - 1

    def fwd(s):
        return jnp.where(s == 0, last, s - 1)

    def bwd(s):
        return jnp.where(s == 0, last, last - s)

    def specs(d, order):
        return [pl.BlockSpec((None, nh, tr, 256), lambda s: (d, 0, order(s), 0)),
                pl.BlockSpec((None, nh, tr, tr), lambda s: (d, 0, order(s), 0)),
                pl.BlockSpec((None, nh, tr, 256), lambda s: (d, 0, order(s), 0)),
                pl.BlockSpec((None, nh, 8, LANES), lambda s: (d, 0, order(s), 0))]

    out = jax.ShapeDtypeStruct((t, nh * 128), BF16)
    return pl.pallas_call(
        functools.partial(_gdn_scan_kernel, n_heads=nh),
        out_shape=(out, out),
        grid=(n_tiles,),
        in_specs=specs(0, fwd) + specs(1, bwd),
        out_specs=(pl.BlockSpec((tr, nh * 128), lambda s: (fwd(s), 0)),
                   pl.BlockSpec((tr, nh * 128), lambda s: (bwd(s), 0))),
        scratch_shapes=[pltpu.VMEM((2, nh, 128, 128), F32)],
        compiler_params=_cparams(("arbitrary",)),
        name="gdn_scan",
    )(uw, aq, qk, gl, uw, aq, qk, gl)


def _head_norm_kernel(of_ref, ob_ref, gate_ref, w_ref, o_ref, *, n_heads, dv):
    for h in range(n_heads):
        sl = slice(h * dv, (h + 1) * dv)
        o = of_ref[:, sl].astype(F32) + ob_ref[:, sl].astype(F32)
        o = o * lax.rsqrt(jnp.mean(o * o, axis=-1, keepdims=True) + EPS) * w_ref[...]
        o_ref[:, sl] = (o * _silu(gate_ref[:, sl].astype(F32))).astype(o_ref.dtype)


def head_norm(o_f, o_b, z_big, gate_col_block, norm_w, n_heads, dv, tr, gate_row0=0):
    t, w = o_f.shape
    spec = pl.BlockSpec((tr, w), lambda i: (i, 0))
    rb0 = gate_row0 // tr
    return pl.pallas_call(
        functools.partial(_head_norm_kernel, n_heads=n_heads, dv=dv),
        out_shape=jax.ShapeDtypeStruct((t, w), BF16),
        grid=(t // tr,),
        in_specs=[spec, spec, pl.BlockSpec((tr, w), lambda i: (i + rb0, gate_col_block)),
                  pl.BlockSpec((1, dv), lambda i: (0, 0))],
        out_specs=spec,
        compiler_params=_cparams(("parallel",)),
        name="head_norm",
    )(o_f, o_b, z_big, norm_w.reshape(1, dv))


GLA_TILE = 128
GLA_LR_LANE0 = 32


def _gla_direction(d, qk_ref, v_ref, sm_ref, wup_ref, gb_ref, o_ref, st_ref, n_heads):
    tr = GLA_TILE
    qkw = n_heads * 128
    r, c = _tri_masks(tr)
    sh = int(math.log2(GLA_CHUNK))
    same = (r >> sh) == (c >> sh)
    mask = jnp.logical_and(same, (c <= r) if d == 0 else (c >= r))
    logit = _dot_hi(sm_ref[...], wup_ref[:, d * qkw:(d + 1) * qkw]) + gb_ref[:, d * qkw:(d + 1) * qkw]
    g = jax.nn.log_sigmoid(logit) * (1.0 / GLA_GATE_NORM)
    b = _dot_sel(jnp.where(mask, 1.0, 0.0).astype(BF16), g)
    nt = (((1,), (1,)), ((), ()))
    tn = (((0,), (0,)), ((), ()))
    chunks = range(tr // GLA_CHUNK) if d == 0 else range(tr // GLA_CHUNK - 1, -1, -1)
    for h in range(n_heads):
        bq = b[:, h * 128:(h + 1) * 128]
        q = qk_ref[:, h * 128:(h + 1) * 128].astype(F32)
        k = qk_ref[:, qkw + h * 128:qkw + (h + 1) * 128].astype(F32)
        v_b = v_ref[:, h * 256:(h + 1) * 256].astype(BF16)
        qe = (q * jnp.exp(bq) * (128.0 ** -0.5)).astype(BF16)
        kinv = (k * jnp.exp(-bq)).astype(BF16)
        a = jnp.where(mask, lax.dot_general(qe, kinv, nt, preferred_element_type=F32), 0.0)
        o_intra = _dot(a.astype(BF16), v_b)
        for ci in chunks:
            lo = ci * GLA_CHUNK
            last = lo + GLA_CHUNK - 1 if d == 0 else lo
            b_last = bq[last:last + 1, :]
            kdec = (k[lo:lo + GLA_CHUNK] * jnp.exp(b_last - bq[lo:lo + GLA_CHUNK])).astype(BF16)
            ds = lax.dot_general(kdec, v_b[lo:lo + GLA_CHUNK], tn, preferred_element_type=F32)
            st = st_ref[d, h]
            o_ref[lo:lo + GLA_CHUNK, h * 256:(h + 1) * 256] = (
                o_intra[lo:lo + GLA_CHUNK] + _dot(qe[lo:lo + GLA_CHUNK], st.astype(BF16))).astype(o_ref.dtype)
            dec = jnp.transpose(jnp.broadcast_to(jnp.exp(b_last), (128, 128)))
            st_ref[d, h] = jnp.concatenate([dec, dec], axis=1) * st + ds


def _gla_kernel(qkf_ref, vf_ref, smf_ref, qkb_ref, vb_ref, smb_ref, wup_ref, gb_ref, s0_ref,
                of_ref, ob_ref, sfin_ref, st_ref, *, n_heads):
    @pl.when(pl.program_id(0) == 0)
    def _():
        st_ref[...] = s0_ref[...]

    _gla_direction(0, qkf_ref, vf_ref, smf_ref, wup_ref, gb_ref, of_ref, st_ref, n_heads)
    _gla_direction(1, qkb_ref, vb_ref, smb_ref, wup_ref, gb_ref, ob_ref, st_ref, n_heads)

    @pl.when(pl.program_id(0) == pl.num_programs(0) - 1)
    def _():
        sfin_ref[...] = st_ref[...]


def _gla_call(cfg, n_steps, arrays, spec_fn, out_rows_shape, out_spec_fn, wup, gbias, s0, name):
    nh = cfg.gla_h
    fwd = lambda s: s
    bwd = lambda s: n_steps - 1 - s
    full = lambda shape: pl.BlockSpec(shape, lambda s: (0,) * len(shape))
    z_view, zs_view = arrays
    st_shape = (2, nh, 128, 256)
    out = jax.ShapeDtypeStruct(out_rows_shape, BF16)
    return pl.pallas_call(
        functools.partial(_gla_kernel, n_heads=nh),
        out_shape=(out, out, jax.ShapeDtypeStruct(st_shape, F32)),
        grid=(n_steps,),
        in_specs=spec_fn(fwd) + spec_fn(bwd) + [full(wup.shape), full(gbias.shape), full(st_shape)],
        out_specs=(out_spec_fn(fwd), out_spec_fn(bwd), full(st_shape)),
        scratch_shapes=[pltpu.VMEM(st_shape, F32)],
        compiler_params=_cparams(("arbitrary",)),
        name=name,
    )(z_view, z_view, zs_view, z_view, z_view, zs_view, wup, gbias, s0)


def gla(cfg, z_big, z_small, col0, wup, gbias):
    nh = cfg.gla_h
    qk2, vw = 2 * nh * 128, nh * 256
    zw, sw = z_big.shape[1], z_small.shape[1]
    w = cfg.grid_w
    rows = cfg.seq // w
    assert rows == GLA_TILE and cfg.ctx % GLA_TILE == 0
    assert zw % qk2 == 0 and zw % vw == 0 and col0 % qk2 == 0 and (col0 + qk2) % vw == 0
    small_blk = cfg.merge_rank // LANES
    s0 = jnp.zeros((2, nh, 128, 256), F32)

    rb0 = cfg.seq // GLA_TILE
    ctx_specs = lambda order: [
        pl.BlockSpec((GLA_TILE, qk2), lambda s: (rb0 + order(s), col0 // qk2)),
        pl.BlockSpec((GLA_TILE, vw), lambda s: (rb0 + order(s), (col0 + qk2) // vw)),
        pl.BlockSpec((GLA_TILE, LANES), lambda s: (rb0 + order(s), small_blk))]
    ctx_out = lambda order: pl.BlockSpec((GLA_TILE, vw), lambda s: (order(s), 0))
    ocf, ocb, s_ctx = _gla_call(cfg, cfg.ctx // GLA_TILE, (z_big, z_small), ctx_specs,
                                (cfg.ctx, vw), ctx_out, wup, gbias, s0, "gla_ctx")

    t = z_big.shape[0]
    zv = z_big.reshape(t // w, w * zw)
    zsv = z_small.reshape(t // w, w * sw)
    lat_specs = lambda order: [
        pl.BlockSpec((GLA_TILE, qk2), lambda s: (0, (order(s) * zw + col0) // qk2)),
        pl.BlockSpec((GLA_TILE, vw), lambda s: (0, (order(s) * zw + col0 + qk2) // vw)),
        pl.BlockSpec((GLA_TILE, LANES), lambda s: (0, order(s) * (sw // LANES) + small_blk))]
    lat_out = lambda order: pl.BlockSpec((GLA_TILE, vw), lambda s: (0, order(s)))
    olf, olb, _ = _gla_call(cfg, w, (zv, zsv), lat_specs, (rows, w * vw), lat_out,
                            wup, gbias, s_ctx, "gla_lat")
    return (olf.reshape(cfg.seq, vw), olb.reshape(cfg.seq, vw)), (ocf, ocb)


FFT_G = 8


def _hy_prep_kernel(z_ref, zp_ref, zn_ref, w_ref, b_ref, v_ref, x1_ref, x2_ref, *, n_lat_tiles, hw):
    y = _conv3(z_ref, zp_ref, zn_ref, w_ref, n_lat_tiles) + b_ref[...]
    for p, o_ref in enumerate((v_ref, x1_ref, x2_ref)):
        o_ref[...] = y[:, p * hw:(p + 1) * hw]


def hyena_prep(cfg, z_hy, conv_w, conv_b):
    tr, hw = cfg.row_tile, cfg.hy_w
    t = z_hy.shape[0]
    main, prev, nxt = _halo_specs(tr, 3 * hw, 0, t)
    out = jax.ShapeDtypeStruct((t, hw), F32)
    ospec = pl.BlockSpec((tr, hw), lambda i: (i, 0))
    return pl.pallas_call(
        functools.partial(_hy_prep_kernel, n_lat_tiles=cfg.seq // tr, hw=hw),
        out_shape=(out, out, out),
        grid=(t // tr,),
        in_specs=[main, prev, nxt,
                  pl.BlockSpec((SHORT_CONV, 3 * hw), lambda i: (0, 0)),
                  pl.BlockSpec((1, 3 * hw), lambda i: (0, 0))],
        out_specs=(ospec, ospec, ospec),
        compiler_params=_cparams(("parallel",)),
        name="hyena_prep",
    )(z_hy, z_hy, z_hy, conv_w, conv_b.reshape(1, 3 * hw))


HY_HALF = LANES // 2


def _hy_taps_kernel(f_ref, w1_ref, b1_ref, w2_ref, b2_ref, fr_ref, w3a0_ref, w3b0_ref, w3a1_ref,
                    w3b1_ref, rate_ref, o0_ref, o1_ref, *, length, tt):
    th = tt // 2
    feats = f_ref[...]
    hid = jnp.sin(fr_ref[0:1, :] * (_dot_3(feats, w1_ref[...]) + b1_ref[...]))
    hid = jnp.sin(fr_ref[1:2, :] * (_dot_3(hid, w2_ref[...]) + b2_ref[...]))
    row = pl.program_id(0) * tt + lax.broadcasted_iota(jnp.int32, (th, 1), 0)
    for part, lane0 in ((0, 0), (1, HY_HALF)):
        n = row + part * th
        window = jnp.where(n == length, 0.0, jnp.exp(-feats[:, lane0:lane0 + 1] * rate_ref[...]))
        for w3_ref, o_ref in (((w3a0_ref, w3b0_ref)[part], o0_ref), ((w3a1_ref, w3b1_ref)[part], o1_ref)):
            o_ref[part * th:(part + 1) * th, :] = (_dot_3(hid, w3_ref[...]) * window).astype(o_ref.dtype)


def hyena_taps(feats, w1p, b1p, w2p, b2p, freqp, w3a, w3b, rates, length, hw):
    tt = min(512, length)
    n_half = length // tt
    out = jax.ShapeDtypeStruct((2 * length, hw), F32)
    ospec = pl.BlockSpec((tt, hw), lambda j: (j, 0))
    sq = pl.BlockSpec((LANES, LANES), lambda j: (0, 0))
    row = pl.BlockSpec((1, LANES), lambda j: (0, 0))
    w3spec = lambda order: pl.BlockSpec((LANES, hw), lambda j: (0, 2 * order + j // n_half))
    return pl.pallas_call(
        functools.partial(_hy_taps_kernel, length=length, tt=tt),
        out_shape=(out, out),
        grid=(2 * n_half,),
        in_specs=[pl.BlockSpec((tt // 2, LANES), lambda j: (j, 0)), sq, row, sq, row,
                  pl.BlockSpec((2, LANES), lambda j: (0, 0)),
                  w3spec(0), w3spec(0), w3spec(1), w3spec(1),
                  pl.BlockSpec((1, hw), lambda j: (0, 0))],
        out_specs=(ospec, ospec),
        compiler_params=_cparams(("parallel",)),
        name="hyena_taps",
    )(feats, w1p, b1p, w2p, b2p, freqp, w3a, w3b, w3a, w3b, rates)


FFT_BT = 8


def _fft_a_kernel(x_ref, l_ref, yr_ref, yi_ref):
    a, bt, c = x_ref.shape
    ph = yr_ref.shape[0]
    y = _dot(l_ref[...], x_ref[...].reshape(a * bt, c).astype(BF16))
    yr_ref[...] = y[0:ph * bt].reshape(ph, bt, c)
    yi_ref[...] = y[ph * bt:2 * ph * bt].reshape(ph, bt, c)


def fft_a(x, lhs, n_pages, ph):
    c = x.shape[1]
    x3 = x.reshape(x.shape[0] // LANES, LANES, c)
    out = jax.ShapeDtypeStruct((ph, LANES, c), F32)
    ospec = pl.BlockSpec((ph, FFT_BT, c), lambda j: (0, j, 0))
    return pl.pallas_call(
        _fft_a_kernel,
        out_shape=(out, out),
        grid=(LANES // FFT_BT,),
        in_specs=[pl.BlockSpec((n_pages, FFT_BT, c), lambda j: (0, j, 0)),
                  pl.BlockSpec(lhs.shape, lambda j: (0, 0))],
        out_specs=(ospec, ospec),
        compiler_params=_cparams(("parallel",)),
        name="fft_a",
    )(x3, lhs)


def _fft_b_kernel(yr_ref, yi_ref, m_ref, zr_ref, zi_ref):
    for g in range(FFT_G):
        s = jnp.concatenate([yr_ref[g], yi_ref[g]], axis=0).astype(BF16)
        z = _dot(m_ref[g], s)
        zr_ref[g] = z[0:LANES].astype(zr_ref.dtype)
        zi_ref[g] = z[LANES:2 * LANES].astype(zi_ref.dtype)


def fft_b(yr, yi, m2):
    ph, _, c = yr.shape
    out = jax.ShapeDtypeStruct((ph, LANES, c), BF16)
    spec = pl.BlockSpec((FFT_G, LANES, c), lambda j: (j, 0, 0))
    return pl.pallas_call(
        _fft_b_kernel,
        out_shape=(out, out),
        grid=(ph // FFT_G,),
        in_specs=[spec, spec, pl.BlockSpec((FFT_G, 2 * LANES, 2 * LANES), lambda j: (j, 0, 0))],
        out_specs=(spec, spec),
        compiler_params=_cparams(("parallel",)),
        name="fft_b",
    )(yr, yi, m2)


FFT_GM = 4


def _spec_mul_kernel(yr_ref, yi_ref, hr_ref, hi_ref, m2_ref, ma_ref, vr_ref, vi_ref):
    for g in range(FFT_GM):
        z = _dot(m2_ref[g], jnp.concatenate([yr_ref[g], yi_ref[g]], axis=0).astype(BF16))
        zr, zi = z[0:LANES], z[LANES:2 * LANES]
        hr, hi = hr_ref[g].astype(F32), hi_ref[g].astype(F32)
        s = jnp.concatenate([zr * hr - zi * hi, zr * hi + zi * hr], axis=0).astype(BF16)
        v = _dot(ma_ref[g], s)
        vr_ref[g] = v[0:LANES]
        vi_ref[g] = v[LANES:2 * LANES]


def spectrum_multiply(yr, yi, hr, hi, m2, ma):
    ph, _, c = yr.shape
    out = jax.ShapeDtypeStruct((ph, LANES, c), F32)
    spec = pl.BlockSpec((FFT_GM, LANES, c), lambda j: (j, 0, 0))
    mspec = pl.BlockSpec((FFT_GM, 2 * LANES, 2 * LANES), lambda j: (j, 0, 0))
    return pl.pallas_call(
        _spec_mul_kernel,
        out_shape=(out, out),
        grid=(ph // FFT_GM,),
        in_specs=[spec, spec, spec, spec, mspec, mspec],
        out_specs=(spec, spec),
        compiler_params=_cparams(("parallel",)),
        name="spectrum_multiply",
    )(yr, yi, hr, hi, m2, ma)


def _ifft_b_kernel(vr_ref, vi_ref, l_ref, x_ref, u_ref, sk_ref, o_ref):
    ph, bt, c = vr_ref.shape
    s = jnp.concatenate([vr_ref[...].reshape(ph * bt, c), vi_ref[...].reshape(ph * bt, c)],
                        axis=0).astype(BF16)
    y = _dot(l_ref[...], s).reshape(o_ref.shape)
    o_ref[...] = x_ref[...] * (y + sk_ref[...] * u_ref[...])


def ifft_b_gate(vr, vi, lhs, gate_x, u, skip_row):
    ph, _, c = vr.shape
    a_out = lhs.shape[0] // FFT_BT
    x3 = gate_x.reshape(gate_x.shape[0] // LANES, LANES, c)
    u3 = u.reshape(u.shape[0] // LANES, LANES, c)
    vspec = pl.BlockSpec((ph, FFT_BT, c), lambda j: (0, j, 0))
    tspec = pl.BlockSpec((a_out, FFT_BT, c), lambda j: (0, j, 0))
    out = pl.pallas_call(
        _ifft_b_kernel,
        out_shape=jax.ShapeDtypeStruct((a_out, LANES, c), F32),
        grid=(LANES // FFT_BT,),
        in_specs=[vspec, vspec, pl.BlockSpec(lhs.shape, lambda j: (0, 0)), tspec, tspec,
                  pl.BlockSpec((1, 1, c), lambda j: (0, 0, 0))],
        out_specs=tspec,
        compiler_params=_cparams(("parallel",)),
        name="ifft_b",
    )(vr, vi, lhs, x3, u3, skip_row.reshape(1, 1, c))
    return out.reshape(a_out * LANES, c)


def _hy_ctx_kernel(v_ref, x1_ref, x2_ref, t0_ref, t1_ref, sk_ref, f_ref, g_ref, o_ref, *, n):
    def conv(u, taps_ref):
        us = _dot(f_ref[:, 0:n], u.astype(BF16))
        hs = _dot(f_ref[...], taps_ref[...].astype(BF16))
        ur, ui, hr, hi = us[0:2 * n], us[2 * n:4 * n], hs[0:2 * n], hs[2 * n:4 * n]
        prod = jnp.concatenate([ur * hr - ui * hi, ur * hi + ui * hr], axis=0).astype(BF16)
        return _dot(g_ref[...], prod)

    v = v_ref[...]
    y = x1_ref[...] * (conv(v, t0_ref) + sk_ref[0:1, :] * v)
    o_ref[...] = x2_ref[...] * (conv(y, t1_ref) + sk_ref[1:2, :] * y)


def hyena_ctx(cfg, vxx, taps0, taps1, skip, fmat, gmat):
    n, hw = cfg.ctx, cfg.hy_w
    cb = 256
    rb = cfg.seq // n
    part = pl.BlockSpec((n, cb), lambda j: (rb, j))
    tspec = pl.BlockSpec((2 * n, cb), lambda j: (0, j))
    return pl.pallas_call(
        functools.partial(_hy_ctx_kernel, n=n),
        out_shape=jax.ShapeDtypeStruct((n, hw), F32),
        grid=(hw // cb,),
        in_specs=[part, part, part, tspec, tspec,
                  pl.BlockSpec((2, cb), lambda j: (0, j)),
                  pl.BlockSpec((4 * n, 2 * n), lambda j: (0, 0)),
                  pl.BlockSpec((n, 4 * n), lambda j: (0, 0))],
        out_specs=pl.BlockSpec((n, cb), lambda j: (0, j)),
        compiler_params=_cparams(("parallel",)),
        name="hyena_ctx",
    )(vxx[0], vxx[1], vxx[2], taps0, taps1, skip, fmat, gmat)


class HyenaConsts(NamedTuple):
    feats: jax.Array
    rates: jax.Array
    ph: int
    la_data: jax.Array
    la_taps: jax.Array
    m2: jax.Array
    ma: jax.Array
    lb: jax.Array
    feats_ctx: jax.Array
    f_ctx: jax.Array
    g_ctx: jax.Array


def _features(length):
    n = np.arange(2 * length, dtype=np.float64)
    pos = np.where(n < length, n, 2 * length - n)
    bands = (HY_EMB - 1) // 2
    f = np.linspace(1e-4, bands - 1, bands)
    omega = (2.0 * math.pi / length) * pos
    feats = np.zeros((2 * length, HY_HALF), np.float64)
    feats[:, 0] = pos / (length - 1)
    feats[:, 1:1 + bands] = np.cos(omega[:, None] * f[None, :])
    feats[:, 1 + bands:1 + 2 * bands] = -np.sin(omega[:, None] * f[None, :])
    tt = min(512, length)
    tiles = feats.reshape(2 * length // tt, 2, tt // 2, HY_HALF)
    packed = np.concatenate([tiles[:, 0], tiles[:, 1]], axis=-1).reshape(length, LANES)
    return jnp.asarray(packed, F32)


def hyena_consts(cfg):
    length, n, hw = cfg.seq, cfg.ctx, cfg.hy_w
    big_n = 2 * length
    p = big_n // LANES
    rates = np.abs(np.linspace(math.log(HY_TARGET) / HY_FAST_DECAY, math.log(HY_TARGET) / HY_SLOW_DECAY, hw))
    ph = p // 2 + 8
    k1 = np.arange(ph)
    kept = (k1 <= p // 2).astype(np.float64)
    a = np.arange(p)
    ang_a = 2.0 * math.pi * ((k1[:, None] * a[None, :]) % p) / p
    dft_a = np.concatenate([np.cos(ang_a), -np.sin(ang_a)], axis=0) * np.tile(kept, 2)[:, None]
    eye = np.eye(FFT_BT)
    la_taps = np.kron(dft_a, eye)
    la_data = np.kron(dft_a[:, :p // 2], eye)
    weight = kept * np.where((k1 == 0) | (k1 == p // 2), 1.0, 2.0) / big_n
    inv_a = (np.concatenate([np.cos(ang_a), -np.sin(ang_a)], axis=0) * np.tile(weight, 2)[:, None]).T
    lb = np.kron(inv_a[:p // 2], eye)
    b = np.arange(LANES)
    ang_b = 2.0 * math.pi * (((b[:, None] * b[None, :]) % LANES) / LANES)[None] \
        + 2.0 * math.pi * (k1[:, None, None] * b[None, None, :]) / big_n
    fr, fi = np.cos(ang_b), -np.sin(ang_b)
    m2 = np.concatenate([np.concatenate([fr, -fi], axis=2), np.concatenate([fi, fr], axis=2)], axis=1)
    gr, gi = fr.transpose(0, 2, 1), -fi.transpose(0, 2, 1)
    ma = np.concatenate([np.concatenate([gr, -gi], axis=2), np.concatenate([gi, gr], axis=2)], axis=1)
    kk = np.arange(2 * n)
    ac = 2.0 * math.pi * ((kk[:, None] * kk[None, :]) % (2 * n)) / (2 * n)
    f_ctx = np.concatenate([np.cos(ac), -np.sin(ac)], axis=0)
    g_ctx = np.concatenate([np.cos(ac), -np.sin(ac)], axis=1)[:n] / (2 * n)
    bf = lambda x: jnp.asarray(x, BF16)
    return HyenaConsts(_features(length), jnp.asarray(rates[None, :], F32), ph, bf(la_data), bf(la_taps),
                       bf(m2), bf(ma), bf(lb), _features(n), bf(f_ctx), bf(g_ctx))


def hyena(cfg, hc, vxx, filt_w, skip, with_ctx=True):
    length, n, hw = cfg.seq, cfg.ctx, cfg.hy_w
    half = length // LANES
    y_lat = vxx[0]
    taps = hyena_taps(hc.feats, *filt_w, hc.rates, length, hw)
    for order in range(2):
        hr, hi = fft_b(*fft_a(taps[order], hc.la_taps, 2 * half, hc.ph), hc.m2)
        yr, yi = fft_a(y_lat, hc.la_data, half, hc.ph)
        vr, vi = spectrum_multiply(yr, yi, hr, hi, hc.m2, hc.ma)
        y_lat = ifft_b_gate(vr, vi, hc.lb, vxx[1 + order], y_lat, skip[order:order + 1])
    if not with_ctx:
        return y_lat, jnp.zeros((n, hw), F32)
    taps_c = hyena_taps(hc.feats_ctx, *filt_w, hc.rates, n, hw)
    y_ctx = hyena_ctx(cfg, vxx, taps_c[0], taps_c[1], skip, hc.f_ctx, hc.g_ctx)
    return y_lat, y_ctx


def expert_expand():
    m = np.zeros((LANES, N_EXPERTS * LANES), np.float32)
    for e in range(N_EXPERTS):
        m[e, e * LANES:(e + 1) * LANES] = 1.0
    return jnp.asarray(m, BF16)


def prep_gla_gate(gate_up, gate_b):
    l, _, r, qk = gate_up.shape
    w = jnp.zeros((l, LANES, 2 * qk), F32)
    for z in range(2):
        w = w.at[:, GLA_LR_LANE0 + z * r:GLA_LR_LANE0 + (z + 1) * r, z * qk:(z + 1) * qk].set(gate_up[:, z])
    return w, gate_b.reshape(l, 1, 2 * qk)


def prep_hyena_filter(w1, b1, w2, b2, w3, freq):
    e, hdim = w1.shape
    assert hdim == HY_HALF
    h = HY_HALF
    w1p = jnp.zeros((LANES, LANES), F32).at[:e, :h].set(w1).at[h:h + e, h:].set(w1)
    w2p = jnp.zeros((LANES, LANES), F32).at[:h, :h].set(w2).at[h:, h:].set(w2)
    zero = jnp.zeros_like(w3)
    w3a = jnp.concatenate([w3, zero], axis=0)
    w3b = jnp.concatenate([zero, w3], axis=0)
    twice = lambda v: jnp.concatenate([v, v], axis=-1).reshape(-1, LANES)
    return w1p, twice(b1), w2p, twice(b2), twice(freq), w3a, w3b


def prep_w2(w2):
    l, e, f, d = w2.shape
    return w2.reshape(l, e * f, d).astype(BF16)


def kernel(x, c, ctx, c_ctx, norm1_g, norm2_g, w_mod, b_mod, w_in, gdn_conv, gdn_a_log, gdn_dt_bias, gdn_norm, gla_gate_up, gla_gate_b, gla_norm, hy_conv_w, hy_conv_b, hy_w1, hy_b1, hy_w2, hy_b2, hy_w3, hy_freq, hy_skip, merge_up, merge_b, w_branch, w_out, w_router, router_bias, moe_w1, moe_w3, moe_w2, final_g):
    cfg = Cfg(d=4096, seq=8192, ctx=256, grid_w=64, gdn_h=8, gla_h=4, hy_w=1024, merge_rank=256,
              d_expert=256, row_tile=256, mm_tm=768)
    return forward(cfg, x, c, ctx, c_ctx, norm1_g, norm2_g, w_mod, b_mod, w_in, gdn_conv, gdn_a_log,
                   gdn_dt_bias, gdn_norm, gla_gate_up, gla_gate_b, gla_norm, hy_conv_w, hy_conv_b, hy_w1,
                   hy_b1, hy_w2, hy_b2, hy_w3, hy_freq, hy_skip, merge_up, merge_b, w_branch, w_out,
                   w_router, router_bias, moe_w1, moe_w3, moe_w2, final_g)


def split_w_in(cfg, w_in):
    gw4 = 4 * cfg.gdn_w
    gdn_small = 4 * cfg.gdn_h
    gla0 = gw4 + gdn_small
    gla_w = 2 * cfg.gla_qk + 2 * cfg.gla_v
    hy0 = gla0 + gla_w + 32
    hy_w = 3 * cfg.hy_w
    w_t = jnp.swapaxes(w_in, 1, 2)
    l, _, k = w_t.shape
    zeros = lambda n: jnp.zeros((l, n, k), w_t.dtype)
    small = jnp.concatenate([w_t[:, hy0 + hy_w:hy0 + hy_w + cfg.merge_rank],
                             w_t[:, gw4:gla0], zeros(GLA_LR_LANE0 - gdn_small),
                             w_t[:, gla0 + gla_w:hy0], zeros(LANES - GLA_LR_LANE0 - 32)], axis=1)
    return cast_rows(w_t, 0, gw4), cast_rows(w_t, gla0, gla_w), cast_rows(w_t, hy0, hy_w), small


def forward(cfg, x, c, ctx, c_ctx, norm1_g, norm2_g, w_mod, b_mod, w_in, gdn_conv, gdn_a_log, gdn_dt_bias,
            gdn_norm, gla_gate_up, gla_gate_b, gla_norm, hy_conv_w, hy_conv_b, hy_w1, hy_b1, hy_w2, hy_b2,
            hy_w3, hy_freq, hy_skip, merge_up, merge_b, w_branch, w_out, w_router, router_bias, moe_w1,
            moe_w3, moe_w2, final_g):
    d, tr, tm = cfg.d, cfg.row_tile, cfg.mm_tm
    depth = w_in.shape[0]
    nh = cfg.gdn_h
    tn = min(1024, d)
    w_gdn, w_gla, w_hy, w_small = split_w_in(cfg, w_in)
    w1b, w3b = moe_w1.astype(BF16), moe_w3.astype(BF16)
    w2f = prep_w2(moe_w2)
    wb, mu, wo = w_branch.astype(BF16), merge_up.astype(BF16), w_out.astype(BF16)
    mb = merge_b.reshape(depth, 3, 1, d)
    wr_pad = jnp.zeros((d, LANES), F32).at[:, :N_EXPERTS].set(w_router)
    rb_col = router_bias.reshape(N_EXPERTS, 1)
    expand = expert_expand()
    wup, gbias = prep_gla_gate(gla_gate_up, gla_gate_b)
    lane0 = 2 * nh
    alog_rows = jnp.zeros((depth, 1, LANES), F32).at[:, 0, lane0:2 * lane0].set(gdn_a_log.reshape(depth, -1))
    dtb_rows = jnp.zeros((depth, 1, LANES), F32).at[:, 0, lane0:2 * lane0].set(gdn_dt_bias.reshape(depth, -1))
    hc = hyena_consts(cfg)
    gla_gate_blk = (2 * cfg.gla_qk + cfg.gla_v) // cfg.gla_v
    z_proj = functools.partial(matmul_nt, tm=tm, tn=tn, out_dtype=BF16)

    lat = jnp.concatenate([x[0], ctx[0]], axis=0)
    cvec = jnp.zeros((8, d), F32).at[0].set(c[0]).at[1].set(c_ctx)
    mods = modvec(cvec, w_mod, b_mod)

    for l in range(depth):
        with_ctx = l < depth - 1
        mod = mods[l]
        h = norm1(cfg, lat, norm1_g[l], mod)
        z_gdn = z_proj(h, w_gdn, l, name="w_in_gdn")
        z_gla = z_proj(h, w_gla, l, name="w_in_gla")
        z_hy = z_proj(h, w_hy, l, name="w_in_hy")
        z_small = matmul_nt(h, w_small, l, tm=tm, tn=w_small.shape[1], name="w_in_small")
        qn, kn, vs, bb, gcb, gct = gdn_prep(cfg, z_gdn, z_small, gdn_conv[l], alog_rows[l], dtb_rows[l])
        o_f, o_b = gdn_scan(cfg, *gdn_chunks(cfg, qn, kn, vs, bb, gcb, gct))
        a_all = head_norm(o_f, o_b, z_gdn, 3, gdn_norm[l], nh, 128, tr)
        (olf, olb), (ocf, ocb) = gla(cfg, z_gla, z_small, 0, wup[l], gbias[l])
        b_lat = head_norm(olf, olb, z_gla, gla_gate_blk, gla_norm[l], cfg.gla_h, 256, tr)
        b_ctx = head_norm(ocf, ocb, z_gla, gla_gate_blk, gla_norm[l], cfg.gla_h, 256, tr, gate_row0=cfg.seq)
        b_all = jnp.concatenate([b_lat, b_ctx], axis=0)
        vxx = hyena_prep(cfg, z_hy, hy_conv_w[l], hy_conv_b[l])
        filt_w = prep_hyena_filter(hy_w1[l], hy_b1[l], hy_w2[l], hy_b2[l], hy_w3[l], hy_freq[l])
        c_lat, c_ctx_out = hyena(cfg, hc, vxx, filt_w, hy_skip[l], with_ctx)
        c_all = jnp.concatenate([c_lat, c_ctx_out], axis=0).astype(BF16)
        s = merge_branches(cfg, (a_all, b_all, c_all), z_small, wb, mu, mb, l, tn=tn)
        lat = matmul_resid(cfg, s, wo, l, lat, mod, 2, tn=tn, name="w_out")
        h2, gate_rep = norm2_route(cfg, lat, norm2_g[l], mod, wr_pad, rb_col, expand)
        act = matmul_moe_act(cfg, h2, w1b, w3b, l, gate_rep)
        lat = matmul_resid(cfg, act, w2f, l, lat, mod, 5, tn=tn, name="moe_down")
    return final_norm(lat, final_g, cfg.seq, tr)[None]
```

```python
import functools
import math
from typing import NamedTuple

import numpy as np
import jax
import jax.numpy as jnp
from jax import lax
from jax.experimental import pallas as pl
from jax.experimental.pallas import tpu as pltpu

F32 = jnp.float32
BF16 = jnp.bfloat16

EPS = 1e-6
LANES = 128
V7X_VMEM_BYTES = 64 * 1024 * 1024
VMEM_LIMIT = (V7X_VMEM_BYTES * 13) // 16
SHORT_CONV = 3
GLA_CHUNK = 64
GLA_GATE_NORM = 16.0
N_EXPERTS = 16
N_GROUPS = 4
HY_EMB = 33
HY_FAST_DECAY = 0.3
HY_SLOW_DECAY = 1.5
HY_TARGET = 1e-2


class Cfg(NamedTuple):
    d: int
    seq: int
    ctx: int
    grid_w: int
    gdn_h: int
    gla_h: int
    hy_w: int
    merge_rank: int
    d_expert: int
    row_tile: int
    mm_tm: int
    moe_tile: int = 512

    @property
    def t(self):
        return self.seq + self.ctx

    @property
    def gdn_w(self):
        return self.gdn_h * 128

    @property
    def gla_qk(self):
        return self.gla_h * 128

    @property
    def gla_v(self):
        return self.gla_h * 256


def _cparams(sem):
    return pltpu.CompilerParams(dimension_semantics=sem, vmem_limit_bytes=VMEM_LIMIT)


def _split3(x):
    hi = x.astype(BF16)
    r1 = x - hi.astype(F32)
    mid = r1.astype(BF16)
    lo = (r1 - mid.astype(F32)).astype(BF16)
    return hi, mid, lo


def _dot(a, b):
    return jnp.dot(a, b, preferred_element_type=F32)


def _dot_sel(sel_bf16, x):
    hi, mid, lo = _split3(x)
    return _dot(sel_bf16, hi) + _dot(sel_bf16, mid) + _dot(sel_bf16, lo)


def _dot_x_sel(x, sel_bf16):
    hi, mid, lo = _split3(x)
    return _dot(hi, sel_bf16) + _dot(mid, sel_bf16) + _dot(lo, sel_bf16)


def _dot_hi(a, b):
    a1, a2, a3 = _split3(a)
    b1, b2, b3 = _split3(b)
    return (_dot(a1, b1) + (_dot(a1, b2) + _dot(a2, b1))
            + (_dot(a2, b2) + _dot(a1, b3) + _dot(a3, b1)))


def _dot_3(a, b):
    a1 = a.astype(BF16)
    a2 = (a - a1.astype(F32)).astype(BF16)
    b1 = b.astype(BF16)
    b2 = (b - b1.astype(F32)).astype(BF16)
    return _dot(a1, b1) + (_dot(a1, b2) + _dot(a2, b1))


def _silu(x):
    return x * jax.nn.sigmoid(x)


def _modvec_kernel(x_ref, w_ref, b_ref, o_ref):
    o_ref[...] = _dot(_silu(x_ref[...]), w_ref[...]) + b_ref[...]


def modvec(cvec, w_mod, b_mod, tn=512):
    depth, d, n = w_mod.shape
    return pl.pallas_call(
        _modvec_kernel,
        out_shape=jax.ShapeDtypeStruct((depth, 8, n), F32),
        grid=(depth, n // tn),
        in_specs=[pl.BlockSpec((8, d), lambda l, j: (0, 0)),
                  pl.BlockSpec((None, d, tn), lambda l, j: (l, 0, j)),
                  pl.BlockSpec((None, 1, tn), lambda l, j: (l, 0, j))],
        out_specs=pl.BlockSpec((None, 8, tn), lambda l, j: (l, 0, j)),
        compiler_params=_cparams(("parallel", "parallel")),
        name="modvec",
    )(cvec, w_mod, b_mod.reshape(depth, 1, n))


def _mod_row(mod_ref, is_ctx, idx, d):
    return mod_ref[pl.ds(is_ctx, 1), idx * d:(idx + 1) * d]


def _norm_mod(x, gain, shift, scale):
    y = x * lax.rsqrt(jnp.mean(x * x, axis=-1, keepdims=True) + EPS)
    return (y * gain) * (1.0 + scale) + shift


def _norm1_kernel(x_ref, g_ref, mod_ref, o_ref, *, d, n_lat_tiles):
    is_ctx = (pl.program_id(0) >= n_lat_tiles).astype(jnp.int32)
    h = _norm_mod(x_ref[...], g_ref[...], _mod_row(mod_ref, is_ctx, 0, d),
                  _mod_row(mod_ref, is_ctx, 1, d))
    o_ref[...] = h.astype(BF16)


def norm1(cfg, x, gain, mod):
    tr = cfg.row_tile
    t, d = x.shape
    return pl.pallas_call(
        functools.partial(_norm1_kernel, d=d, n_lat_tiles=cfg.seq // tr),
        out_shape=jax.ShapeDtypeStruct((t, d), BF16),
        grid=(t // tr,),
        in_specs=[pl.BlockSpec((tr, d), lambda i: (i, 0)),
                  pl.BlockSpec((1, d), lambda i: (0, 0)),
                  pl.BlockSpec((8, 6 * d), lambda i: (0, 0))],
        out_specs=pl.BlockSpec((tr, d), lambda i: (i, 0)),
        compiler_params=_cparams(("parallel",)),
        name="norm1",
    )(x, gain.reshape(1, d), mod)


def _route(sel_t, sc_t):
    per = N_EXPERTS // N_GROUPS
    grp_score = []
    for g in range(N_GROUPS):
        v = sel_t[g * per:(g + 1) * per]
        best = None
        for a in range(per):
            for b in range(a + 1, per):
                s = v[a] + v[b]
                best = s if best is None else jnp.maximum(best, s)
        grp_score.append(best)
    best_s, best_g = grp_score[0], jnp.zeros_like(grp_score[0])
    for g in range(1, N_GROUPS):
        better = grp_score[g] > best_s
        best_s = jnp.where(better, grp_score[g], best_s)
        best_g = jnp.where(better, float(g), best_g)
    picked = []
    for e in range(N_EXPERTS):
        g, i = divmod(e, per)
        rank = jnp.zeros_like(best_s)
        for j in range(per):
            if j == i:
                continue
            o = sel_t[g * per + j]
            ahead = (o >= sel_t[e]) if j < i else (o > sel_t[e])
            rank = rank + ahead.astype(F32)
        picked.append(jnp.where((best_g == float(g)) & (rank < 2.0), sc_t[e], 0.0))
    den = picked[0]
    for e in range(1, N_EXPERTS):
        den = den + picked[e]
    inv = 1.0 / den
    return [p * inv for p in picked], best_g


def _norm2_kernel(x_ref, g_ref, mod_ref, wr_ref, rb_ref, ex_ref, o_ref, gate_ref, *, d,
                  n_lat_tiles):
    is_ctx = (pl.program_id(0) >= n_lat_tiles).astype(jnp.int32)
    h = _norm_mod(x_ref[...], g_ref[...], _mod_row(mod_ref, is_ctx, 3, d),
                  _mod_row(mod_ref, is_ctx, 4, d))
    o_ref[...] = h.astype(BF16)
    logits = _dot_3(h, wr_ref[...])
    lt = jnp.transpose(logits)
    sc = jax.nn.sigmoid(lt[0:N_EXPERTS, :])
    sel = sc + rb_ref[...]
    gate_rows, _ = _route([sel[e:e + 1, :] for e in range(N_EXPERTS)],
                          [sc[e:e + 1, :] for e in range(N_EXPERTS)])
    rows = lax.broadcasted_iota(jnp.int32, lt.shape, 0)
    gt = jnp.zeros(lt.shape, F32)
    for e in range(N_EXPERTS):
        gt = jnp.where(rows == e, gate_rows[e], gt)
    gate = jnp.transpose(gt)
    gate_ref[...] = _dot_x_sel(gate, ex_ref[...])


def norm2_route(cfg, x, gain, mod, w_router_pad, rbias_col, expand):
    tr = cfg.row_tile
    t, d = x.shape
    return pl.pallas_call(
        functools.partial(_norm2_kernel, d=d, n_lat_tiles=cfg.seq // tr),
        out_shape=(jax.ShapeDtypeStruct((t, d), BF16),
                   jax.ShapeDtypeStruct((t, N_EXPERTS * LANES), F32)),
        grid=(t // tr,),
        in_specs=[pl.BlockSpec((tr, d), lambda i: (i, 0)),
                  pl.BlockSpec((1, d), lambda i: (0, 0)),
                  pl.BlockSpec((8, 6 * d), lambda i: (0, 0)),
                  pl.BlockSpec((d, LANES), lambda i: (0, 0)),
                  pl.BlockSpec((N_EXPERTS, 1), lambda i: (0, 0)),
                  pl.BlockSpec((LANES, N_EXPERTS * LANES), lambda i: (0, 0))],
        out_specs=(pl.BlockSpec((tr, d), lambda i: (i, 0)),
                   pl.BlockSpec((tr, N_EXPERTS * LANES), lambda i: (i, 0))),
        compiler_params=_cparams(("parallel",)),
        name="norm2_route",
    )(x, gain.reshape(1, d), mod, w_router_pad, rbias_col, expand)


PER_GROUP = N_EXPERTS // N_GROUPS


def _norm2_sparse_kernel(x_ref, g_ref, mod_ref, wr_ref, rb_ref, ex_ref, hp_ref, gate_ref, gid_ref, *,
                         d, n_lat_tiles):
    is_ctx = (pl.program_id(0) >= n_lat_tiles).astype(jnp.int32)
    h = _norm_mod(x_ref[...], g_ref[...], _mod_row(mod_ref, is_ctx, 3, d),
                  _mod_row(mod_ref, is_ctx, 4, d))
    bits = lambda v: lax.bitcast_convert_type(v.astype(BF16).astype(F32), jnp.uint32)
    hp_ref[...] = (bits(h[:, 0:d // 2]) >> 16) | (bits(h[:, d // 2:d]) & jnp.uint32(0xFFFF0000))
    logits = _dot_3(h, wr_ref[...])
    lt = jnp.transpose(logits)
    sc = jax.nn.sigmoid(lt[0:N_EXPERTS, :])
    sel = sc + rb_ref[...]
    gate_rows, best_g = _route([sel[e:e + 1, :] for e in range(N_EXPERTS)],
                               [sc[e:e + 1, :] for e in range(N_EXPERTS)])
    gid_ref[...] = best_g
    rows = lax.broadcasted_iota(jnp.int32, lt.shape, 0)
    gt = jnp.zeros(lt.shape, F32)
    for i in range(PER_GROUP):
        gi = gate_rows[i]
        for g in range(1, N_GROUPS):
            gi = gi + gate_rows[g * PER_GROUP + i]
        gt = jnp.where(rows == i, gi, gt)
    gate_ref[...] = _dot_x_sel(jnp.transpose(gt), ex_ref[...])


def norm2_route_sparse(cfg, x, gain, mod, w_router_pad, rbias_col, expand4):
    tr = cfg.row_tile
    t, d = x.shape
    return pl.pallas_call(
        functools.partial(_norm2_sparse_kernel, d=d, n_lat_tiles=cfg.seq // tr),
        out_shape=(jax.ShapeDtypeStruct((t, d // 2), jnp.uint32),
                   jax.ShapeDtypeStruct((t, PER_GROUP * LANES), F32),
                   jax.ShapeDtypeStruct((t // tr, 1, tr), F32)),
        grid=(t // tr,),
        in_specs=[pl.BlockSpec((tr, d), lambda i: (i, 0)),
                  pl.BlockSpec((1, d), lambda i: (0, 0)),
                  pl.BlockSpec((8, 6 * d), lambda i: (0, 0)),
                  pl.BlockSpec((d, LANES), lambda i: (0, 0)),
                  pl.BlockSpec((N_EXPERTS, 1), lambda i: (0, 0)),
                  pl.BlockSpec((LANES, PER_GROUP * LANES), lambda i: (0, 0))],
        out_specs=(pl.BlockSpec((tr, d // 2), lambda i: (i, 0)),
                   pl.BlockSpec((tr, PER_GROUP * LANES), lambda i: (i, 0)),
                   pl.BlockSpec((None, 1, tr), lambda i: (i, 0, 0))),
        compiler_params=_cparams(("parallel",)),
        name="norm2_route",
    )(x, gain.reshape(1, d), mod, w_router_pad, rbias_col, expand4)


def _moe_pos_kernel(gid_ref, pos_ref, tg_ref, *, tile):
    gid = gid_ref[...]
    r_n = gid.shape[0]
    r, c = _tri_masks(LANES)
    upper = jnp.where(r <= c, 1.0, 0.0).astype(BF16)
    rr = lax.broadcasted_iota(jnp.int32, (r_n, r_n), 0)
    cc = lax.broadcasted_iota(jnp.int32, (r_n, r_n), 1)
    lower = jnp.where(cc < rr, 1.0, 0.0).astype(BF16)
    tile_idx = lax.broadcasted_iota(jnp.int32, (1, LANES), 1).astype(F32) * float(tile)
    pos = jnp.zeros(gid.shape, F32)
    tg = jnp.zeros((1, LANES), F32)
    base = jnp.zeros((1, LANES), F32)
    for g in range(N_GROUPS):
        m = jnp.where(gid == float(g), 1.0, 0.0)
        incl = _dot(m.astype(BF16), upper)
        row_tot = jnp.broadcast_to(incl[:, LANES - 1:LANES], incl.shape)
        row_off = _dot_sel(lower, row_tot)
        rank = row_off + incl - m
        count = row_off[r_n - 1:r_n, :] + row_tot[r_n - 1:r_n, :]
        padded = jnp.floor((count + float(tile - 1)) * (1.0 / tile)) * float(tile)
        pos = pos + m * (base + rank)
        base = base + padded
        if g < N_GROUPS - 1:
            tg = tg + jnp.where(tile_idx >= base, 1.0, 0.0)
    pos_ref[...] = pos.astype(jnp.int32)
    tg_ref[...] = jnp.concatenate([tg, jnp.where(tile_idx < base, 1.0, 0.0)], axis=0).astype(jnp.int32)


def moe_positions(gid, tile):
    t = gid.shape[0] * gid.shape[2]
    assert t % LANES == 0 and tile & (tile - 1) == 0
    rows = t // LANES
    rows_p = ((rows + 7) // 8) * 8
    g2 = jnp.full((rows_p, LANES), -1.0, F32).at[:rows].set(gid.reshape(rows, LANES))
    pos, tg = pl.pallas_call(
        functools.partial(_moe_pos_kernel, tile=tile),
        out_shape=(jax.ShapeDtypeStruct((rows_p, LANES), jnp.int32),
                   jax.ShapeDtypeStruct((2, LANES), jnp.int32)),
        name="moe_positions",
    )(g2)
    return pos[:rows].reshape(t), tg


MOE_DMA_ROWS = 256


def _row_copies(src_a, dst_a, src_b, dst_b, sems, t, p):
    return (pltpu.make_async_copy(src_a.at[pl.ds(t, 1)], dst_a.at[pl.ds(p, 1)], sems.at[0]),
            pltpu.make_async_copy(src_b.at[pl.ds(t, 1)], dst_b.at[pl.ds(p, 1)], sems.at[1]))


def _moe_permute_kernel(pos_ref, hp_ref, gate_ref, xs0_ref, gs0_ref, xs_ref, gs_ref, sems):
    del xs0_ref, gs0_ref
    base = pl.program_id(0) * MOE_DMA_ROWS

    def start(r, carry):
        for cp in _row_copies(hp_ref, xs_ref, gate_ref, gs_ref, sems, base + r, pos_ref[base + r]):
            cp.start()
        return carry

    def wait(r, carry):
        for cp in _row_copies(hp_ref, xs_ref, gate_ref, gs_ref, sems, base + r, pos_ref[base + r]):
            cp.wait()
        return carry

    lax.fori_loop(0, MOE_DMA_ROWS, start, 0)
    lax.fori_loop(0, MOE_DMA_ROWS, wait, 0)


def moe_permute(pos, hp, gate4, n_sorted):
    t, wd = hp.shape
    gw = gate4.shape[1]
    any_spec = pl.BlockSpec(memory_space=pl.ANY)
    return pl.pallas_call(
        _moe_permute_kernel,
        out_shape=(jax.ShapeDtypeStruct((n_sorted, wd), hp.dtype),
                   jax.ShapeDtypeStruct((n_sorted, gw), gate4.dtype)),
        grid_spec=pltpu.PrefetchScalarGridSpec(
            num_scalar_prefetch=1, grid=(t // MOE_DMA_ROWS,),
            in_specs=[any_spec, any_spec, any_spec, any_spec],
            out_specs=(any_spec, any_spec),
            scratch_shapes=[pltpu.SemaphoreType.DMA((2,))]),
        input_output_aliases={3: 0, 4: 1},
        compiler_params=_cparams(("arbitrary",)),
        name="moe_permute",
    )(pos, hp, gate4, jnp.zeros((n_sorted, wd), hp.dtype), jnp.zeros((n_sorted, gw), gate4.dtype))


def _moe_grouped_kernel(tg_ref, xs_ref, w1_ref, w3_ref, w2_ref, g_ref, o_ref, a_ref, *, de):
    i, e = pl.program_id(0), pl.program_id(1)

    @pl.when(e == 0)
    def _():
        x = xs_ref[...]
        lo = lax.bitcast_convert_type(x << 16, F32)
        hi = lax.bitcast_convert_type(x & jnp.uint32(0xFFFF0000), F32)
        a_ref[...] = jnp.concatenate([lo, hi], axis=1).astype(BF16)
        o_ref[...] = jnp.zeros_like(o_ref)

    @pl.when(tg_ref[1, i] > 0)
    def _():
        a = a_ref[...]
        g = jnp.concatenate([g_ref[...]] * (de // LANES), axis=1)
        act = (_silu(_dot(a, w1_ref[...])) * _dot(a, w3_ref[...]) * g).astype(BF16)
        o_ref[...] += _dot(act, w2_ref[...])


def moe_grouped(cfg, tg, xs, gs, w1, w3, w2, layer, tile):
    n_sorted, wd = xs.shape
    d, de = 2 * wd, cfg.d_expert
    expert = lambda i, e, tg_ref: tg_ref[0, i] * PER_GROUP + e
    return pl.pallas_call(
        functools.partial(_moe_grouped_kernel, de=de),
        out_shape=jax.ShapeDtypeStruct((n_sorted, d), F32),
        grid_spec=pltpu.PrefetchScalarGridSpec(
            num_scalar_prefetch=1, grid=(n_sorted // tile, PER_GROUP),
            in_specs=[pl.BlockSpec((tile, wd), lambda i, e, tg_ref: (i, 0)),
                      pl.BlockSpec((None, None, d, de), lambda i, e, tg_ref: (layer, expert(i, e, tg_ref), 0, 0)),
                      pl.BlockSpec((None, None, d, de), lambda i, e, tg_ref: (layer, expert(i, e, tg_ref), 0, 0)),
                      pl.BlockSpec((None, None, de, d), lambda i, e, tg_ref: (layer, expert(i, e, tg_ref), 0, 0)),
                      pl.BlockSpec((tile, LANES), lambda i, e, tg_ref: (i, e))],
            out_specs=pl.BlockSpec((tile, d), lambda i, e, tg_ref: (i, 0)),
            scratch_shapes=[pltpu.VMEM((tile, d), BF16)]),
        compiler_params=_cparams(("parallel", "arbitrary")),
        name="moe_grouped",
    )(tg, xs, w1, w3, w2, gs)


def _moe_unpermute_kernel(pos_ref, ys_ref, lat_ref, mod_ref, o_ref, buf_ref, sem, *, tr, n_lat):
    base = pl.program_id(0) * tr

    def copy(r):
        return pltpu.make_async_copy(ys_ref.at[pl.ds(pos_ref[base + r], 1)], buf_ref.at[pl.ds(r, 1)], sem.at[0])

    def start(r, carry):
        copy(r).start()
        return carry

    def wait(r, carry):
        copy(r).wait()
        return carry

    lax.fori_loop(0, tr, start, 0)
    lax.fori_loop(0, tr, wait, 0)
    row = base + lax.broadcasted_iota(jnp.int32, (tr, 1), 0)
    gate = jnp.where(row < n_lat, mod_ref[0:1, :], mod_ref[1:2, :])
    o_ref[...] = lat_ref[...] + gate * buf_ref[...]


def moe_unpermute_resid(cfg, pos, ys, lat, mod, idx):
    t, d = lat.shape
    tr = MOE_DMA_ROWS
    return pl.pallas_call(
        functools.partial(_moe_unpermute_kernel, tr=tr, n_lat=cfg.seq),
        out_shape=jax.ShapeDtypeStruct((t, d), F32),
        grid_spec=pltpu.PrefetchScalarGridSpec(
            num_scalar_prefetch=1, grid=(t // tr,),
            in_specs=[pl.BlockSpec(memory_space=pl.ANY),
                      pl.BlockSpec((tr, d), lambda i, pos_ref: (i, 0)),
                      pl.BlockSpec((8, d), lambda i, pos_ref: (0, idx))],
            out_specs=pl.BlockSpec((tr, d), lambda i, pos_ref: (i, 0)),
            scratch_shapes=[pltpu.VMEM((tr, d), F32), pltpu.SemaphoreType.DMA((1,))]),
        input_output_aliases={2: 0},
        compiler_params=_cparams(("arbitrary",)),
        name="moe_unpermute",
    )(pos, ys, lat, mod)


def _final_norm_kernel(x_ref, g_ref, o_ref):
    x = x_ref[...]
    o_ref[...] = (x * lax.rsqrt(jnp.mean(x * x, axis=-1, keepdims=True) + EPS)) * g_ref[...]


def final_norm(x, gain, n_rows, tr):
    d = x.shape[1]
    return pl.pallas_call(
        _final_norm_kernel,
        out_shape=jax.ShapeDtypeStruct((n_rows, d), F32),
        grid=(n_rows // tr,),
        in_specs=[pl.BlockSpec((tr, d), lambda i: (i, 0)),
                  pl.BlockSpec((1, d), lambda i: (0, 0))],
        out_specs=pl.BlockSpec((tr, d), lambda i: (i, 0)),
        compiler_params=_cparams(("parallel",)),
        name="final_norm",
    )(x, gain.reshape(1, d))


def _mm_plain_kernel(a_ref, w_ref, o_ref):
    o_ref[...] = _dot(a_ref[...], w_ref[...].astype(BF16)).astype(o_ref.dtype)


def matmul_plain(a, w, layer, *, tm, tn, n_cols=None, col0=0, out_dtype=F32, name="mm"):
    t, k = a.shape
    n_cols = w.shape[2] - col0 if n_cols is None else n_cols
    off = col0 // tn
    return pl.pallas_call(
        _mm_plain_kernel,
        out_shape=jax.ShapeDtypeStruct((t, n_cols), out_dtype),
        grid=(n_cols // tn, t // tm),
        in_specs=[pl.BlockSpec((tm, k), lambda j, i: (i, 0)),
                  pl.BlockSpec((None, k, tn), lambda j, i: (layer, 0, j + off))],
        out_specs=pl.BlockSpec((tm, tn), lambda j, i: (i, j)),
        compiler_params=_cparams(("parallel", "parallel")),
        name=name,
    )(a, w)


def _mm_nt_kernel(a_ref, w_ref, o_ref):
    nt = (((1,), (1,)), ((), ()))
    o_ref[...] = lax.dot_general(a_ref[...], w_ref[...].astype(BF16), nt,
                                 preferred_element_type=F32).astype(o_ref.dtype)


def matmul_nt(a, w_t, layer, *, tm, tn, out_dtype=F32, name="mm_nt"):
    t, k = a.shape
    n = w_t.shape[1]
    return pl.pallas_call(
        _mm_nt_kernel,
        out_shape=jax.ShapeDtypeStruct((t, n), out_dtype),
        grid=(n // tn, t // tm),
        in_specs=[pl.BlockSpec((tm, k), lambda j, i: (i, 0)),
                  pl.BlockSpec((None, tn, k), lambda j, i: (layer, j, 0))],
        out_specs=pl.BlockSpec((tm, tn), lambda j, i: (i, j)),
        compiler_params=_cparams(("parallel", "parallel")),
        name=name,
    )(a, w_t)


def _cast_rows_kernel(w_ref, o_ref):
    o_ref[...] = w_ref[...].astype(o_ref.dtype)


def cast_rows(w_t, row0, n_rows):
    l, _, k = w_t.shape
    tr = math.gcd(row0, n_rows) if row0 else n_rows
    while tr > 512 and tr % 2 == 0:
        tr //= 2
    assert tr % 16 == 0
    off = row0 // tr
    return pl.pallas_call(
        _cast_rows_kernel,
        out_shape=jax.ShapeDtypeStruct((l, n_rows, k), BF16),
        grid=(l, n_rows // tr),
        in_specs=[pl.BlockSpec((None, tr, k), lambda i, j: (i, j + off, 0))],
        out_specs=pl.BlockSpec((None, tr, k), lambda i, j: (i, j, 0)),
        compiler_params=_cparams(("parallel", "parallel")),
        name="cast_rows",
    )(w_t)


def _mm_resid_kernel(a_ref, w_ref, r_ref, mod_ref, o_ref, *, tm, n_lat):
    row = pl.program_id(1) * tm + lax.broadcasted_iota(jnp.int32, (tm, 1), 0)
    gate = jnp.where(row < n_lat, mod_ref[0:1, :], mod_ref[1:2, :])
    o_ref[...] = r_ref[...] + gate * _dot(a_ref[...], w_ref[...])


def matmul_resid(cfg, a, w, layer, resid, mod, idx, *, tn, name):
    t, k = a.shape
    d = w.shape[2]
    tm = cfg.mm_tm
    return pl.pallas_call(
        functools.partial(_mm_resid_kernel, tm=tm, n_lat=cfg.seq),
        out_shape=jax.ShapeDtypeStruct((t, d), F32),
        grid=(d // tn, t // tm),
        in_specs=[pl.BlockSpec((tm, k), lambda j, i: (i, 0)),
                  pl.BlockSpec((None, k, tn), lambda j, i: (layer, 0, j)),
                  pl.BlockSpec((tm, tn), lambda j, i: (i, j)),
                  pl.BlockSpec((8, tn), lambda j, i: (0, idx * (d // tn) + j))],
        out_specs=pl.BlockSpec((tm, tn), lambda j, i: (i, j)),
        input_output_aliases={2: 0},
        compiler_params=_cparams(("parallel", "parallel")),
        name=name,
    )(a, w, resid, mod)


def _mm_moe_act_kernel(a_ref, w1_ref, w3_ref, g_ref, o_ref, *, de, n_e):
    a = a_ref[...]
    for e in range(n_e):
        up = _dot(a, w1_ref[e])
        lin = _dot(a, w3_ref[e])
        g = g_ref[:, e * LANES:(e + 1) * LANES]
        g = jnp.concatenate([g] * (de // LANES), axis=1)
        o_ref[:, e * de:(e + 1) * de] = (_silu(up) * lin * g).astype(o_ref.dtype)


def matmul_moe_act(cfg, h, w1, w3, layer, gate_rep, *, n_e=2):
    t, k = h.shape
    de = cfg.d_expert
    tm = cfg.mm_tm
    wspec = pl.BlockSpec((None, n_e, k, de), lambda j, i: (layer, j, 0, 0))
    return pl.pallas_call(
        functools.partial(_mm_moe_act_kernel, de=de, n_e=n_e),
        out_shape=jax.ShapeDtypeStruct((t, N_EXPERTS * de), BF16),
        grid=(N_EXPERTS // n_e, t // tm),
        in_specs=[pl.BlockSpec((tm, k), lambda j, i: (i, 0)), wspec, wspec,
                  pl.BlockSpec((tm, n_e * LANES), lambda j, i: (i, j))],
        out_specs=pl.BlockSpec((tm, n_e * de), lambda j, i: (i, j)),
        compiler_params=_cparams(("parallel", "parallel")),
        name="moe_up",
    )(h, w1, w3, gate_rep)


def _merge_kernel(a0_ref, a1_ref, a2_ref, zg_ref, wb_ref, mu_ref, mb_ref, o_ref):
    zg = zg_ref[...].astype(BF16)
    acc = None
    for n, a_ref in enumerate((a0_ref, a1_ref, a2_ref)):
        y = _dot(a_ref[...], wb_ref[n])
        gate = jax.nn.sigmoid(_dot(zg, mu_ref[n]) + mb_ref[n])
        acc = gate * y if acc is None else acc + gate * y
    o_ref[...] = acc.astype(o_ref.dtype)


def merge_branches(cfg, outs, z_small, w_branch, merge_up, merge_b, layer, *, tn):
    t, bw = outs[0].shape
    d = w_branch.shape[3]
    r = cfg.merge_rank
    tm = cfg.mm_tm
    a_spec = pl.BlockSpec((tm, bw), lambda j, i: (i, 0))
    return pl.pallas_call(
        _merge_kernel,
        out_shape=jax.ShapeDtypeStruct((t, d), BF16),
        grid=(d // tn, t // tm),
        in_specs=[a_spec, a_spec, a_spec,
                  pl.BlockSpec((tm, r), lambda j, i: (i, 0)),
                  pl.BlockSpec((None, 3, bw, tn), lambda j, i: (layer, 0, 0, j)),
                  pl.BlockSpec((None, 3, r, tn), lambda j, i: (layer, 0, 0, j)),
                  pl.BlockSpec((None, 3, 1, tn), lambda j, i: (layer, 0, 0, j))],
        out_specs=pl.BlockSpec((tm, tn), lambda j, i: (i, j)),
        compiler_params=_cparams(("parallel", "parallel")),
        name="merge",
    )(outs[0], outs[1], outs[2], z_small, w_branch, merge_up, merge_b)


HALO_ROWS = 16


def _halo_specs(tr, width, col_block, n_rows):
    rb = tr // HALO_ROWS
    last = n_rows // HALO_ROWS - 1
    main = pl.BlockSpec((tr, width), lambda i: (i, col_block))
    prev = pl.BlockSpec((HALO_ROWS, width), lambda i: (jnp.maximum(i * rb - 1, 0), col_block))
    nxt = pl.BlockSpec((HALO_ROWS, width), lambda i: (jnp.minimum((i + 1) * rb, last), col_block))
    return main, prev, nxt


def _conv3(z_ref, prev_ref, next_ref, w_ref, n_lat_tiles):
    i = pl.program_id(0)
    x = z_ref[...].astype(F32)
    tr = x.shape[0]
    has_prev = jnp.logical_and(i != 0, i != n_lat_tiles).astype(F32)
    has_next = jnp.logical_and(i != n_lat_tiles - 1, i != n_lat_tiles).astype(F32)
    row = lax.broadcasted_iota(jnp.int32, (tr, 1), 0)
    halo_prev = prev_ref[HALO_ROWS - 1:HALO_ROWS, :].astype(F32) * has_prev
    halo_next = next_ref[0:1, :].astype(F32) * has_next
    x_prev = jnp.where(row == 0, halo_prev, pltpu.roll(x, 1, axis=0))
    x_next = jnp.where(row == tr - 1, halo_next, pltpu.roll(x, tr - 1, axis=0))
    return x_prev * w_ref[0:1, :] + x * w_ref[1:2, :] + x_next * w_ref[2:3, :]


def _tri_masks(n):
    r = lax.broadcasted_iota(jnp.int32, (n, n), 0)
    c = lax.broadcasted_iota(jnp.int32, (n, n), 1)
    return r, c


def _gdn_prep_kernel(z_ref, zp_ref, zn_ref, s_ref, cw_ref, alog_ref, dtb_ref,
                     q_ref, k_ref, v_ref, bb_ref, gcb_ref, gct_ref, *, n_heads, n_lat_tiles):
    tr = z_ref.shape[0]
    hw = n_heads * 128
    y = _silu(_conv3(z_ref, zp_ref, zn_ref, cw_ref, n_lat_tiles))
    for h in range(n_heads):
        q = y[:, h * 128:(h + 1) * 128]
        k = y[:, hw + h * 128:hw + (h + 1) * 128]
        q = q * (lax.rsqrt(jnp.sum(q * q, axis=-1, keepdims=True) + EPS) * (128.0 ** -0.5))
        k = k * lax.rsqrt(jnp.sum(k * k, axis=-1, keepdims=True) + EPS)
        q_ref[h] = q.astype(BF16)
        k_ref[h] = k.astype(BF16)
        v_ref[h] = y[:, 2 * hw + h * 128:2 * hw + (h + 1) * 128].astype(BF16)
    s = s_ref[...]
    nh2 = 2 * n_heads
    beta = jax.nn.sigmoid(s)
    g = -jnp.exp(alog_ref[...]) * jax.nn.softplus(s + dtb_ref[...])
    r, c = _tri_masks(tr)
    incl_lo = (c <= r).astype(BF16)
    incl_up = (c >= r).astype(BF16)
    lane = lax.broadcasted_iota(jnp.int32, (tr, LANES), 1)
    fwd_lane = lane < nh2 + n_heads
    gc = jnp.where(fwd_lane, _dot_sel(incl_lo, g), _dot_sel(incl_up, g))
    for ch in range(nh2):
        bb_ref[ch] = jnp.broadcast_to(beta[:, ch:ch + 1], (tr, LANES))
        gcb_ref[ch] = jnp.broadcast_to(gc[:, nh2 + ch:nh2 + ch + 1], (tr, LANES))
    gct_ref[...] = jnp.transpose(gc)[nh2:2 * nh2, :]


def gdn_prep(cfg, z_big, z_small, conv_w, alog_row, dtb_row):
    tr = cfg.row_tile
    t = z_big.shape[0]
    nh = cfg.gdn_h
    hw = nh * 128
    n_tiles = t // tr
    main, prev, nxt = _halo_specs(tr, 3 * hw, 0, t)
    head_out = jax.ShapeDtypeStruct((nh, t, 128), BF16)
    head_spec = pl.BlockSpec((nh, tr, 128), lambda i: (0, i, 0))
    col_out = jax.ShapeDtypeStruct((2 * nh, t, LANES), F32)
    col_spec = pl.BlockSpec((2 * nh, tr, LANES), lambda i: (0, i, 0))
    return pl.pallas_call(
        functools.partial(_gdn_prep_kernel, n_heads=nh, n_lat_tiles=cfg.seq // tr),
        out_shape=(head_out, head_out, head_out, col_out, col_out,
                   jax.ShapeDtypeStruct((n_tiles, 2 * nh, tr), F32)),
        grid=(n_tiles,),
        in_specs=[main, prev, nxt,
                  pl.BlockSpec((tr, LANES), lambda i: (i, cfg.merge_rank // LANES)),
                  pl.BlockSpec((SHORT_CONV, 3 * hw), lambda i: (0, 0)),
                  pl.BlockSpec((1, LANES), lambda i: (0, 0)),
                  pl.BlockSpec((1, LANES), lambda i: (0, 0))],
        out_specs=(head_spec, head_spec, head_spec, col_spec, col_spec,
                   pl.BlockSpec((None, 2 * nh, tr), lambda i: (i, 0, 0))),
        compiler_params=_cparams(("parallel",)),
        name="gdn_prep",
    )(z_big, z_big, z_big, z_small, conv_w, alog_row, dtb_row)


def _unit_tri_inverses(n_mats, r, c):
    n = n_mats[0].shape[0]
    eye = (r == c).astype(F32)

    def same_block(b):
        sh = int(math.log2(b))
        return (r >> sh) == (c >> sh)

    blk = same_block(8)
    pfs = [jnp.where(blk, -m, 0.0) for m in n_mats]
    ps = [pf.astype(BF16) for pf in pfs]
    p2 = [_dot(p, p).astype(BF16) for p in ps]
    p4 = [_dot(x, x).astype(BF16) for x in p2]
    ts = [eye + pf for pf in pfs]
    ts = [t + _dot(t.astype(BF16), x) for t, x in zip(ts, p2)]
    ts = [t + _dot(t.astype(BF16), x) for t, x in zip(ts, p4)]
    b = 8
    while b < n:
        sel = jnp.logical_and(same_block(2 * b), jnp.logical_not(same_block(b)))
        offs = [jnp.where(sel, m, 0.0).astype(BF16) for m in n_mats]
        tbs = [t.astype(BF16) for t in ts]
        xs = [_dot(tb, off).astype(BF16) for tb, off in zip(tbs, offs)]
        ts = [t - _dot(x, tb) for t, x, tb in zip(ts, xs, tbs)]
        b *= 2
    return ts, eye


GDN_HEADS_PER_STEP = 4


def _gdn_chunk_kernel(q_ref, k_ref, v_ref, bf_ref, bb_ref, gf_ref, gb_ref, gct_ref,
                      uw_ref, aq_ref, qk_ref, gl_ref, *, n_heads):
    hps = q_ref.shape[0]
    h0 = pl.program_id(1) * hps
    tr = q_ref.shape[1]
    r, c = _tri_masks(tr)
    nt = (((1,), (1,)), ((), ()))
    masks = (((c <= r), (c < r)), ((c >= r), (c > r)))
    chains = [(j, d) for j in range(hps) for d in range(2)]
    k_b = [k_ref[j] for j in range(hps)]
    k_f = [x.astype(F32) for x in k_b]
    a_qk = [lax.dot_general(q_ref[j], k_b[j], nt, preferred_element_type=F32) for j in range(hps)]
    beta, gc, gam, kb, n_mats = {}, {}, {}, {}, []
    for j, d in chains:
        beta[j, d] = (bf_ref, bb_ref)[d][j]
        gc[j, d] = (gf_ref, gb_ref)[d][j]
        gc_row = gct_ref[pl.ds(d * n_heads + h0 + j, 1), :]
        gc_col = jnp.concatenate([gc[j, d]] * (tr // LANES), axis=1)
        gam[j, d] = jnp.exp(jnp.where(masks[d][0], gc_col - gc_row, -jnp.inf))
        kb[j, d] = k_f[j] * beta[j, d]
        a_kk = lax.dot_general(kb[j, d].astype(BF16), k_b[j], nt, preferred_element_type=F32)
        n_mats.append(jnp.where(masks[d][1], a_kk * gam[j, d], 0.0))
    t_invs, eye = _unit_tri_inverses(n_mats, r, c)
    for (j, d), t_inv in zip(chains, t_invs):
        e = jnp.exp(gc[j, d])
        rhs = jnp.concatenate([v_ref[j].astype(F32) * beta[j, d], kb[j, d] * e], axis=1)
        sol = rhs + _dot((t_inv - eye).astype(BF16), rhs.astype(BF16))
        g_last = gc[j, d][tr - 1:tr, :] if d == 0 else gc[j, d][0:1, :]
        uw_ref[d, j] = sol.astype(BF16)
        aq_ref[d, j] = (a_qk[j] * gam[j, d]).astype(BF16)
        qk_ref[d, j] = jnp.concatenate([q_ref[j].astype(F32) * e,
                                        k_f[j] * jnp.exp(g_last - gc[j, d])], axis=1).astype(BF16)
        gl_ref[d, j] = jnp.broadcast_to(jnp.exp(g_last), (8, LANES))


def gdn_chunks(cfg, qn, kn, vs, bb, gcb, gct):
    tr = cfg.row_tile
    nh, t, _ = qn.shape
    hps = min(GDN_HEADS_PER_STEP, nh)
    assert nh % hps == 0
    n_tiles = t // tr
    head = pl.BlockSpec((hps, tr, 128), lambda i, h: (h, i, 0))
    head_b = pl.BlockSpec((hps, tr, 128), lambda i, h: (h + nh // hps, i, 0))
    big = jax.ShapeDtypeStruct((2, nh, t, 2 * 128), BF16)
    aq = jax.ShapeDtypeStruct((2, nh, t, tr), BF16)
    return pl.pallas_call(
        functools.partial(_gdn_chunk_kernel, n_heads=nh),
        out_shape=(big, aq, big, jax.ShapeDtypeStruct((2, nh, n_tiles * 8, LANES), F32)),
        grid=(n_tiles, nh // hps),
        in_specs=[head, head, head, head, head_b, head, head_b,
                  pl.BlockSpec((None, 2 * nh, tr), lambda i, h: (i, 0, 0))],
        out_specs=(pl.BlockSpec((2, hps, tr, 256), lambda i, h: (0, h, i, 0)),
                   pl.BlockSpec((2, hps, tr, tr), lambda i, h: (0, h, i, 0)),
                   pl.BlockSpec((2, hps, tr, 256), lambda i, h: (0, h, i, 0)),
                   pl.BlockSpec((2, hps, 8, LANES), lambda i, h: (0, h, i, 0))),
        compiler_params=_cparams(("parallel", "parallel")),
        name="gdn_chunks",
    )(qn, kn, vs, bb, bb, gcb, gcb, gct)


def _gdn_scan_kernel(uwf_ref, aqf_ref, qkf_ref, glf_ref, uwb_ref, aqb_ref, qkb_ref, glb_ref,
                     of_ref, ob_ref, s_ref, *, n_heads):
    @pl.when(pl.program_id(0) == 0)
    def _():
        s_ref[...] = jnp.zeros_like(s_ref)

    tn = (((0,), (0,)), ((), ()))
    for d, (uw_ref, aq_ref, qk_ref, gl_ref, o_ref) in enumerate(
            ((uwf_ref, aqf_ref, qkf_ref, glf_ref, of_ref), (uwb_ref, aqb_ref, qkb_ref, glb_ref, ob_ref))):
        for h in range(n_heads):
            st = s_ref[d, h]
            st_b = st.astype(BF16)
            uw = uw_ref[h]
            qk = qk_ref[h]
            v_new = uw[:, 0:128].astype(F32) - _dot(uw[:, 128:256], st_b)
            v_new_b = v_new.astype(BF16)
            o_ref[:, h * 128:(h + 1) * 128] = (
                _dot(qk[:, 0:128], st_b) + _dot(aq_ref[h], v_new_b)).astype(o_ref.dtype)
            s_ref[d, h] = gl_ref[h][0:1, :] * st + lax.dot_general(
                qk[:, 128:256], v_new_b, tn, preferred_element_type=F32)


def gdn_scan(cfg, uw, aq, qk, gl):
    tr = cfg.row_tile
    _, nh, t, _ = uw.shape
    n_tiles = t // tr
    last = n_tiles - 1

    def fwd(s):
        return jnp.where(s == 0, last, s - 1)

    def bwd(s):
        return jnp.where(s == 0, last, last - s)

    def specs(d, order):
        return [pl.BlockSpec((None, nh, tr, 256), lambda s: (d, 0, order(s), 0)),
                pl.BlockSpec((None, nh, tr, tr), lambda s: (d, 0, order(s), 0)),
                pl.BlockSpec((None, nh, tr, 256), lambda s: (d, 0, order(s), 0)),
                pl.BlockSpec((None, nh, 8, LANES), lambda s: (d, 0, order(s), 0))]

    out = jax.ShapeDtypeStruct((t, nh * 128), BF16)
    return pl.pallas_call(
        functools.partial(_gdn_scan_kernel, n_heads=nh),
        out_shape=(out, out),
        grid=(n_tiles,),
        in_specs=specs(0, fwd) + specs(1, bwd),
        out_specs=(pl.BlockSpec((tr, nh * 128), lambda s: (fwd(s), 0)),
                   pl.BlockSpec((tr, nh * 128), lambda s: (bwd(s), 0))),
        scratch_shapes=[pltpu.VMEM((2, nh, 128, 128), F32)],
        compiler_params=_cparams(("arbitrary",)),
        name="gdn_scan",
    )(uw, aq, qk, gl, uw, aq, qk, gl)


def _head_norm_kernel(of_ref, ob_ref, gate_ref, w_ref, o_ref, *, n_heads, dv):
    for h in range(n_heads):
        sl = slice(h * dv, (h + 1) * dv)
        o = of_ref[:, sl].astype(F32) + ob_ref[:, sl].astype(F32)
        o = o * lax.rsqrt(jnp.mean(o * o, axis=-1, keepdims=True) + EPS) * w_ref[...]
        o_ref[:, sl] = (o * _silu(gate_ref[:, sl].astype(F32))).astype(o_ref.dtype)


def head_norm(o_f, o_b, z_big, gate_col_block, norm_w, n_heads, dv, tr, gate_row0=0):
    t, w = o_f.shape
    spec = pl.BlockSpec((tr, w), lambda i: (i, 0))
    rb0 = gate_row0 // tr
    return pl.pallas_call(
        functools.partial(_head_norm_kernel, n_heads=n_heads, dv=dv),
        out_shape=jax.ShapeDtypeStruct((t, w), BF16),
        grid=(t // tr,),
        in_specs=[spec, spec, pl.BlockSpec((tr, w), lambda i: (i + rb0, gate_col_block)),
                  pl.BlockSpec((1, dv), lambda i: (0, 0))],
        out_specs=spec,
        compiler_params=_cparams(("parallel",)),
        name="head_norm",
    )(o_f, o_b, z_big, norm_w.reshape(1, dv))


GLA_TILE = 128
GLA_LR_LANE0 = 32


def _gla_direction(d, qk_ref, v_ref, sm_ref, wup_ref, gb_ref, o_ref, st_ref, n_heads):
    tr = GLA_TILE
    qkw = n_heads * 128
    r, c = _tri_masks(tr)
    sh = int(math.log2(GLA_CHUNK))
    same = (r >> sh) == (c >> sh)
    mask = jnp.logical_and(same, (c <= r) if d == 0 else (c >= r))
    logit = _dot_hi(sm_ref[...], wup_ref[:, d * qkw:(d + 1) * qkw]) + gb_ref[:, d * qkw:(d + 1) * qkw]
    g = jax.nn.log_sigmoid(logit) * (1.0 / GLA_GATE_NORM)
    b = _dot_sel(jnp.where(mask, 1.0, 0.0).astype(BF16), g)
    nt = (((1,), (1,)), ((), ()))
    tn = (((0,), (0,)), ((), ()))
    chunks = range(tr // GLA_CHUNK) if d == 0 else range(tr // GLA_CHUNK - 1, -1, -1)
    for h in range(n_heads):
        bq = b[:, h * 128:(h + 1) * 128]
        q = qk_ref[:, h * 128:(h + 1) * 128].astype(F32)
        k = qk_ref[:, qkw + h * 128:qkw + (h + 1) * 128].astype(F32)
        v_b = v_ref[:, h * 256:(h + 1) * 256].astype(BF16)
        qe = (q * jnp.exp(bq) * (128.0 ** -0.5)).astype(BF16)
        kinv = (k * jnp.exp(-bq)).astype(BF16)
        a = jnp.where(mask, lax.dot_general(qe, kinv, nt, preferred_element_type=F32), 0.0)
        o_intra = _dot(a.astype(BF16), v_b)
        for ci in chunks:
            lo = ci * GLA_CHUNK
            last = lo + GLA_CHUNK - 1 if d == 0 else lo
            b_last = bq[last:last + 1, :]
            kdec = (k[lo:lo + GLA_CHUNK] * jnp.exp(b_last - bq[lo:lo + GLA_CHUNK])).astype(BF16)
            ds = lax.dot_general(kdec, v_b[lo:lo + GLA_CHUNK], tn, preferred_element_type=F32)
            st = st_ref[d, h]
            o_ref[lo:lo + GLA_CHUNK, h * 256:(h + 1) * 256] = (
                o_intra[lo:lo + GLA_CHUNK] + _dot(qe[lo:lo + GLA_CHUNK], st.astype(BF16))).astype(o_ref.dtype)
            dec = jnp.transpose(jnp.broadcast_to(jnp.exp(b_last), (128, 128)))
            st_ref[d, h] = jnp.concatenate([dec, dec], axis=1) * st + ds


def _gla_kernel(qkf_ref, vf_ref, smf_ref, qkb_ref, vb_ref, smb_ref, wup_ref, gb_ref, s0_ref,
                of_ref, ob_ref, sfin_ref, st_ref, *, n_heads):
    @pl.when(pl.program_id(0) == 0)
    def _():
        st_ref[...] = s0_ref[...]

    _gla_direction(0, qkf_ref, vf_ref, smf_ref, wup_ref, gb_ref, of_ref, st_ref, n_heads)
    _gla_direction(1, qkb_ref, vb_ref, smb_ref, wup_ref, gb_ref, ob_ref, st_ref, n_heads)

    @pl.when(pl.program_id(0) == pl.num_programs(0) - 1)
    def _():
        sfin_ref[...] = st_ref[...]


def _gla_call(cfg, n_steps, arrays, spec_fn, out_rows_shape, out_spec_fn, wup, gbias, s0, name):
    nh = cfg.gla_h
    fwd = lambda s: s
    bwd = lambda s: n_steps - 1 - s
    full = lambda shape: pl.BlockSpec(shape, lambda s: (0,) * len(shape))
    z_view, zs_view = arrays
    st_shape = (2, nh, 128, 256)
    out = jax.ShapeDtypeStruct(out_rows_shape, BF16)
    return pl.pallas_call(
        functools.partial(_gla_kernel, n_heads=nh),
        out_shape=(out, out, jax.ShapeDtypeStruct(st_shape, F32)),
        grid=(n_steps,),
        in_specs=spec_fn(fwd) + spec_fn(bwd) + [full(wup.shape), full(gbias.shape), full(st_shape)],
        out_specs=(out_spec_fn(fwd), out_spec_fn(bwd), full(st_shape)),
        scratch_shapes=[pltpu.VMEM(st_shape, F32)],
        compiler_params=_cparams(("arbitrary",)),
        name=name,
    )(z_view, z_view, zs_view, z_view, z_view, zs_view, wup, gbias, s0)


def gla(cfg, z_big, z_small, col0, wup, gbias):
    nh = cfg.gla_h
    qk2, vw = 2 * nh * 128, nh * 256
    zw, sw = z_big.shape[1], z_small.shape[1]
    w = cfg.grid_w
    rows = cfg.seq // w
    assert rows == GLA_TILE and cfg.ctx % GLA_TILE == 0
    assert zw % qk2 == 0 and zw % vw == 0 and col0 % qk2 == 0 and (col0 + qk2) % vw == 0
    small_blk = cfg.merge_rank // LANES
    s0 = jnp.zeros((2, nh, 128, 256), F32)

    rb0 = cfg.seq // GLA_TILE
    ctx_specs = lambda order: [
        pl.BlockSpec((GLA_TILE, qk2), lambda s: (rb0 + order(s), col0 // qk2)),
        pl.BlockSpec((GLA_TILE, vw), lambda s: (rb0 + order(s), (col0 + qk2) // vw)),
        pl.BlockSpec((GLA_TILE, LANES), lambda s: (rb0 + order(s), small_blk))]
    ctx_out = lambda order: pl.BlockSpec((GLA_TILE, vw), lambda s: (order(s), 0))
    ocf, ocb, s_ctx = _gla_call(cfg, cfg.ctx // GLA_TILE, (z_big, z_small), ctx_specs,
                                (cfg.ctx, vw), ctx_out, wup, gbias, s0, "gla_ctx")

    t = z_big.shape[0]
    zv = z_big.reshape(t // w, w * zw)
    zsv = z_small.reshape(t // w, w * sw)
    lat_specs = lambda order: [
        pl.BlockSpec((GLA_TILE, qk2), lambda s: (0, (order(s) * zw + col0) // qk2)),
        pl.BlockSpec((GLA_TILE, vw), lambda s: (0, (order(s) * zw + col0 + qk2) // vw)),
        pl.BlockSpec((GLA_TILE, LANES), lambda s: (0, order(s) * (sw // LANES) + small_blk))]
    lat_out = lambda order: pl.BlockSpec((GLA_TILE, vw), lambda s: (0, order(s)))
    olf, olb, _ = _gla_call(cfg, w, (zv, zsv), lat_specs, (rows, w * vw), lat_out,
                            wup, gbias, s_ctx, "gla_lat")
    return (olf.reshape(cfg.seq, vw), olb.reshape(cfg.seq, vw)), (ocf, ocb)


FFT_G = 8


def _hy_prep_kernel(z_ref, zp_ref, zn_ref, w_ref, b_ref, v_ref, x1_ref, x2_ref, *, n_lat_tiles, hw):
    y = _conv3(z_ref, zp_ref, zn_ref, w_ref, n_lat_tiles) + b_ref[...]
    for p, o_ref in enumerate((v_ref, x1_ref, x2_ref)):
        o_ref[...] = y[:, p * hw:(p + 1) * hw]


def hyena_prep(cfg, z_hy, conv_w, conv_b):
    tr, hw = cfg.row_tile, cfg.hy_w
    t = z_hy.shape[0]
    main, prev, nxt = _halo_specs(tr, 3 * hw, 0, t)
    out = jax.ShapeDtypeStruct((t, hw), F32)
    ospec = pl.BlockSpec((tr, hw), lambda i: (i, 0))
    return pl.pallas_call(
        functools.partial(_hy_prep_kernel, n_lat_tiles=cfg.seq // tr, hw=hw),
        out_shape=(out, out, out),
        grid=(t // tr,),
        in_specs=[main, prev, nxt,
                  pl.BlockSpec((SHORT_CONV, 3 * hw), lambda i: (0, 0)),
                  pl.BlockSpec((1, 3 * hw), lambda i: (0, 0))],
        out_specs=(ospec, ospec, ospec),
        compiler_params=_cparams(("parallel",)),
        name="hyena_prep",
    )(z_hy, z_hy, z_hy, conv_w, conv_b.reshape(1, 3 * hw))


HY_HALF = LANES // 2


def _hy_taps_kernel(f_ref, w1_ref, b1_ref, w2_ref, b2_ref, fr_ref, w3a0_ref, w3b0_ref, w3a1_ref,
                    w3b1_ref, rate_ref, o0_ref, o1_ref, *, length, tt):
    th = tt // 2
    feats = f_ref[...]
    hid = jnp.sin(fr_ref[0:1, :] * (_dot_3(feats, w1_ref[...]) + b1_ref[...]))
    hid = jnp.sin(fr_ref[1:2, :] * (_dot_3(hid, w2_ref[...]) + b2_ref[...]))
    row = pl.program_id(0) * tt + lax.broadcasted_iota(jnp.int32, (th, 1), 0)
    for part, lane0 in ((0, 0), (1, HY_HALF)):
        n = row + part * th
        window = jnp.where(n == length, 0.0, jnp.exp(-feats[:, lane0:lane0 + 1] * rate_ref[...]))
        for w3_ref, o_ref in (((w3a0_ref, w3b0_ref)[part], o0_ref), ((w3a1_ref, w3b1_ref)[part], o1_ref)):
            o_ref[part * th:(part + 1) * th, :] = (_dot_3(hid, w3_ref[...]) * window).astype(o_ref.dtype)


def hyena_taps(feats, w1p, b1p, w2p, b2p, freqp, w3a, w3b, rates, length, hw):
    tt = min(512, length)
    n_half = length // tt
    out = jax.ShapeDtypeStruct((2 * length, hw), F32)
    ospec = pl.BlockSpec((tt, hw), lambda j: (j, 0))
    sq = pl.BlockSpec((LANES, LANES), lambda j: (0, 0))
    row = pl.BlockSpec((1, LANES), lambda j: (0, 0))
    w3spec = lambda order: pl.BlockSpec((LANES, hw), lambda j: (0, 2 * order + j // n_half))
    return pl.pallas_call(
        functools.partial(_hy_taps_kernel, length=length, tt=tt),
        out_shape=(out, out),
        grid=(2 * n_half,),
        in_specs=[pl.BlockSpec((tt // 2, LANES), lambda j: (j, 0)), sq, row, sq, row,
                  pl.BlockSpec((2, LANES), lambda j: (0, 0)),
                  w3spec(0), w3spec(0), w3spec(1), w3spec(1),
                  pl.BlockSpec((1, hw), lambda j: (0, 0))],
        out_specs=(ospec, ospec),
        compiler_params=_cparams(("parallel",)),
        name="hyena_taps",
    )(feats, w1p, b1p, w2p, b2p, freqp, w3a, w3b, w3a, w3b, rates)


FFT_BT = 8


def _fft_a_kernel(x_ref, l_ref, yr_ref, yi_ref):
    a, bt, c = x_ref.shape
    ph = yr_ref.shape[0]
    y = _dot(l_ref[...], x_ref[...].reshape(a * bt, c).astype(BF16))
    yr_ref[...] = y[0:ph * bt].reshape(ph, bt, c)
    yi_ref[...] = y[ph * bt:2 * ph * bt].reshape(ph, bt, c)


def fft_a(x, lhs, n_pages, ph):
    c = x.shape[1]
    x3 = x.reshape(x.shape[0] // LANES, LANES, c)
    out = jax.ShapeDtypeStruct((ph, LANES, c), F32)
    ospec = pl.BlockSpec((ph, FFT_BT, c), lambda j: (0, j, 0))
    return pl.pallas_call(
        _fft_a_kernel,
        out_shape=(out, out),
        grid=(LANES // FFT_BT,),
        in_specs=[pl.BlockSpec((n_pages, FFT_BT, c), lambda j: (0, j, 0)),
                  pl.BlockSpec(lhs.shape, lambda j: (0, 0))],
        out_specs=(ospec, ospec),
        compiler_params=_cparams(("parallel",)),
        name="fft_a",
    )(x3, lhs)


def _fft_b_kernel(yr_ref, yi_ref, m_ref, zr_ref, zi_ref):
    for g in range(FFT_G):
        s = jnp.concatenate([yr_ref[g], yi_ref[g]], axis=0).astype(BF16)
        z = _dot(m_ref[g], s)
        zr_ref[g] = z[0:LANES].astype(zr_ref.dtype)
        zi_ref[g] = z[LANES:2 * LANES].astype(zi_ref.dtype)


def fft_b(yr, yi, m2):
    ph, _, c = yr.shape
    out = jax.ShapeDtypeStruct((ph, LANES, c), BF16)
    spec = pl.BlockSpec((FFT_G, LANES, c), lambda j: (j, 0, 0))
    return pl.pallas_call(
        _fft_b_kernel,
        out_shape=(out, out),
        grid=(ph // FFT_G,),
        in_specs=[spec, spec, pl.BlockSpec((FFT_G, 2 * LANES, 2 * LANES), lambda j: (j, 0, 0))],
        out_specs=(spec, spec),
        compiler_params=_cparams(("parallel",)),
        name="fft_b",
    )(yr, yi, m2)


FFT_GM = 4


def _spec_mul_kernel(yr_ref, yi_ref, hr_ref, hi_ref, m2_ref, ma_ref, vr_ref, vi_ref):
    for g in range(FFT_GM):
        z = _dot(m2_ref[g], jnp.concatenate([yr_ref[g], yi_ref[g]], axis=0).astype(BF16))
        zr, zi = z[0:LANES], z[LANES:2 * LANES]
        hr, hi = hr_ref[g].astype(F32), hi_ref[g].astype(F32)
        s = jnp.concatenate([zr * hr - zi * hi, zr * hi + zi * hr], axis=0).astype(BF16)
        v = _dot(ma_ref[g], s)
        vr_ref[g] = v[0:LANES]
        vi_ref[g] = v[LANES:2 * LANES]


def spectrum_multiply(yr, yi, hr, hi, m2, ma):
    ph, _, c = yr.shape
    out = jax.ShapeDtypeStruct((ph, LANES, c), F32)
    spec = pl.BlockSpec((FFT_GM, LANES, c), lambda j: (j, 0, 0))
    mspec = pl.BlockSpec((FFT_GM, 2 * LANES, 2 * LANES), lambda j: (j, 0, 0))
    return pl.pallas_call(
        _spec_mul_kernel,
        out_shape=(out, out),
        grid=(ph // FFT_GM,),
        in_specs=[spec, spec, spec, spec, mspec, mspec],
        out_specs=(spec, spec),
        compiler_params=_cparams(("parallel",)),
        name="spectrum_multiply",
    )(yr, yi, hr, hi, m2, ma)


def _ifft_b_kernel(vr_ref, vi_ref, l_ref, x_ref, u_ref, sk_ref, o_ref):
    ph, bt, c = vr_ref.shape
    s = jnp.concatenate([vr_ref[...].reshape(ph * bt, c), vi_ref[...].reshape(ph * bt, c)],
                        axis=0).astype(BF16)
    y = _dot(l_ref[...], s).reshape(o_ref.shape)
    o_ref[...] = x_ref[...] * (y + sk_ref[...] * u_ref[...])


def ifft_b_gate(vr, vi, lhs, gate_x, u, skip_row):
    ph, _, c = vr.shape
    a_out = lhs.shape[0] // FFT_BT
    x3 = gate_x.reshape(gate_x.shape[0] // LANES, LANES, c)
    u3 = u.reshape(u.shape[0] // LANES, LANES, c)
    vspec = pl.BlockSpec((ph, FFT_BT, c), lambda j: (0, j, 0))
    tspec = pl.BlockSpec((a_out, FFT_BT, c), lambda j: (0, j, 0))
    out = pl.pallas_call(
        _ifft_b_kernel,
        out_shape=jax.ShapeDtypeStruct((a_out, LANES, c), F32),
        grid=(LANES // FFT_BT,),
        in_specs=[vspec, vspec, pl.BlockSpec(lhs.shape, lambda j: (0, 0)), tspec, tspec,
                  pl.BlockSpec((1, 1, c), lambda j: (0, 0, 0))],
        out_specs=tspec,
        compiler_params=_cparams(("parallel",)),
        name="ifft_b",
    )(vr, vi, lhs, x3, u3, skip_row.reshape(1, 1, c))
    return out.reshape(a_out * LANES, c)


def _hy_ctx_kernel(v_ref, x1_ref, x2_ref, t0_ref, t1_ref, sk_ref, f_ref, g_ref, o_ref, *, n):
    def conv(u, taps_ref):
        us = _dot(f_ref[:, 0:n], u.astype(BF16))
        hs = _dot(f_ref[...], taps_ref[...].astype(BF16))
        ur, ui, hr, hi = us[0:2 * n], us[2 * n:4 * n], hs[0:2 * n], hs[2 * n:4 * n]
        prod = jnp.concatenate([ur * hr - ui * hi, ur * hi + ui * hr], axis=0).astype(BF16)
        return _dot(g_ref[...], prod)

    v = v_ref[...]
    y = x1_ref[...] * (conv(v, t0_ref) + sk_ref[0:1, :] * v)
    o_ref[...] = x2_ref[...] * (conv(y, t1_ref) + sk_ref[1:2, :] * y)


def hyena_ctx(cfg, vxx, taps0, taps1, skip, fmat, gmat):
    n, hw = cfg.ctx, cfg.hy_w
    cb = 256
    rb = cfg.seq // n
    part = pl.BlockSpec((n, cb), lambda j: (rb, j))
    tspec = pl.BlockSpec((2 * n, cb), lambda j: (0, j))
    return pl.pallas_call(
        functools.partial(_hy_ctx_kernel, n=n),
        out_shape=jax.ShapeDtypeStruct((n, hw), F32),
        grid=(hw // cb,),
        in_specs=[part, part, part, tspec, tspec,
                  pl.BlockSpec((2, cb), lambda j: (0, j)),
                  pl.BlockSpec((4 * n, 2 * n), lambda j: (0, 0)),
                  pl.BlockSpec((n, 4 * n), lambda j: (0, 0))],
        out_specs=pl.BlockSpec((n, cb), lambda j: (0, j)),
        compiler_params=_cparams(("parallel",)),
        name="hyena_ctx",
    )(vxx[0], vxx[1], vxx[2], taps0, taps1, skip, fmat, gmat)


class HyenaConsts(NamedTuple):
    feats: jax.Array
    rates: jax.Array
    ph: int
    la_data: jax.Array
    la_taps: jax.Array
    m2: jax.Array
    ma: jax.Array
    lb: jax.Array
    feats_ctx: jax.Array
    f_ctx: jax.Array
    g_ctx: jax.Array


def _features(length):
    n = np.arange(2 * length, dtype=np.float64)
    pos = np.where(n < length, n, 2 * length - n)
    bands = (HY_EMB - 1) // 2
    f = np.linspace(1e-4, bands - 1, bands)
    omega = (2.0 * math.pi / length) * pos
    feats = np.zeros((2 * length, HY_HALF), np.float64)
    feats[:, 0] = pos / (length - 1)
    feats[:, 1:1 + bands] = np.cos(omega[:, None] * f[None, :])
    feats[:, 1 + bands:1 + 2 * bands] = -np.sin(omega[:, None] * f[None, :])
    tt = min(512, length)
    tiles = feats.reshape(2 * length // tt, 2, tt // 2, HY_HALF)
    packed = np.concatenate([tiles[:, 0], tiles[:, 1]], axis=-1).reshape(length, LANES)
    return jnp.asarray(packed, F32)


def hyena_consts(cfg):
    length, n, hw = cfg.seq, cfg.ctx, cfg.hy_w
    big_n = 2 * length
    p = big_n // LANES
    rates = np.abs(np.linspace(math.log(HY_TARGET) / HY_FAST_DECAY, math.log(HY_TARGET) / HY_SLOW_DECAY, hw))
    ph = p // 2 + 8
    k1 = np.arange(ph)
    kept = (k1 <= p // 2).astype(np.float64)
    a = np.arange(p)
    ang_a = 2.0 * math.pi * ((k1[:, None] * a[None, :]) % p) / p
    dft_a = np.concatenate([np.cos(ang_a), -np.sin(ang_a)], axis=0) * np.tile(kept, 2)[:, None]
    eye = np.eye(FFT_BT)
    la_taps = np.kron(dft_a, eye)
    la_data = np.kron(dft_a[:, :p // 2], eye)
    weight = kept * np.where((k1 == 0) | (k1 == p // 2), 1.0, 2.0) / big_n
    inv_a = (np.concatenate([np.cos(ang_a), -np.sin(ang_a)], axis=0) * np.tile(weight, 2)[:, None]).T
    lb = np.kron(inv_a[:p // 2], eye)
    b = np.arange(LANES)
    ang_b = 2.0 * math.pi * (((b[:, None] * b[None, :]) % LANES) / LANES)[None] \
        + 2.0 * math.pi * (k1[:, None, None] * b[None, None, :]) / big_n
    fr, fi = np.cos(ang_b), -np.sin(ang_b)
    m2 = np.concatenate([np.concatenate([fr, -fi], axis=2), np.concatenate([fi, fr], axis=2)], axis=1)
    gr, gi = fr.transpose(0, 2, 1), -fi.transpose(0, 2, 1)
    ma = np.concatenate([np.concatenate([gr, -gi], axis=2), np.concatenate([gi, gr], axis=2)], axis=1)
    kk = np.arange(2 * n)
    ac = 2.0 * math.pi * ((kk[:, None] * kk[None, :]) % (2 * n)) / (2 * n)
    f_ctx = np.concatenate([np.cos(ac), -np.sin(ac)], axis=0)
    g_ctx = np.concatenate([np.cos(ac), -np.sin(ac)], axis=1)[:n] / (2 * n)
    bf = lambda x: jnp.asarray(x, BF16)
    return HyenaConsts(_features(length), jnp.asarray(rates[None, :], F32), ph, bf(la_data), bf(la_taps),
                       bf(m2), bf(ma), bf(lb), _features(n), bf(f_ctx), bf(g_ctx))


def hyena(cfg, hc, vxx, filt_w, skip, with_ctx=True):
    length, n, hw = cfg.seq, cfg.ctx, cfg.hy_w
    half = length // LANES
    y_lat = vxx[0]
    taps = hyena_taps(hc.feats, *filt_w, hc.rates, length, hw)
    for order in range(2):
        hr, hi = fft_b(*fft_a(taps[order], hc.la_taps, 2 * half, hc.ph), hc.m2)
        yr, yi = fft_a(y_lat, hc.la_data, half, hc.ph)
        vr, vi = spectrum_multiply(yr, yi, hr, hi, hc.m2, hc.ma)
        y_lat = ifft_b_gate(vr, vi, hc.lb, vxx[1 + order], y_lat, skip[order:order + 1])
    if not with_ctx:
        return y_lat, jnp.zeros((n, hw), F32)
    taps_c = hyena_taps(hc.feats_ctx, *filt_w, hc.rates, n, hw)
    y_ctx = hyena_ctx(cfg, vxx, taps_c[0], taps_c[1], skip, hc.f_ctx, hc.g_ctx)
    return y_lat, y_ctx


def expert_expand(n=N_EXPERTS):
    m = np.zeros((LANES, n * LANES), np.float32)
    for e in range(n):
        m[e, e * LANES:(e + 1) * LANES] = 1.0
    return jnp.asarray(m, BF16)


def prep_gla_gate(gate_up, gate_b):
    l, _, r, qk = gate_up.shape
    w = jnp.zeros((l, LANES, 2 * qk), F32)
    for z in range(2):
        w = w.at[:, GLA_LR_LANE0 + z * r:GLA_LR_LANE0 + (z + 1) * r, z * qk:(z + 1) * qk].set(gate_up[:, z])
    return w, gate_b.reshape(l, 1, 2 * qk)


def prep_hyena_filter(w1, b1, w2, b2, w3, freq):
    e, hdim = w1.shape
    assert hdim == HY_HALF
    h = HY_HALF
    w1p = jnp.zeros((LANES, LANES), F32).at[:e, :h].set(w1).at[h:h + e, h:].set(w1)
    w2p = jnp.zeros((LANES, LANES), F32).at[:h, :h].set(w2).at[h:, h:].set(w2)
    zero = jnp.zeros_like(w3)
    w3a = jnp.concatenate([w3, zero], axis=0)
    w3b = jnp.concatenate([zero, w3], axis=0)
    twice = lambda v: jnp.concatenate([v, v], axis=-1).reshape(-1, LANES)
    return w1p, twice(b1), w2p, twice(b2), twice(freq), w3a, w3b


def prep_w2(w2):
    l, e, f, d = w2.shape
    return w2.reshape(l, e * f, d).astype(BF16)


def kernel(x, c, ctx, c_ctx, norm1_g, norm2_g, w_mod, b_mod, w_in, gdn_conv, gdn_a_log, gdn_dt_bias, gdn_norm, gla_gate_up, gla_gate_b, gla_norm, hy_conv_w, hy_conv_b, hy_w1, hy_b1, hy_w2, hy_b2, hy_w3, hy_freq, hy_skip, merge_up, merge_b, w_branch, w_out, w_router, router_bias, moe_w1, moe_w3, moe_w2, final_g):
    cfg = Cfg(d=4096, seq=8192, ctx=256, grid_w=64, gdn_h=8, gla_h=4, hy_w=1024, merge_rank=256,
              d_expert=256, row_tile=256, mm_tm=768)
    return forward(cfg, x, c, ctx, c_ctx, norm1_g, norm2_g, w_mod, b_mod, w_in, gdn_conv, gdn_a_log,
                   gdn_dt_bias, gdn_norm, gla_gate_up, gla_gate_b, gla_norm, hy_conv_w, hy_conv_b, hy_w1,
                   hy_b1, hy_w2, hy_b2, hy_w3, hy_freq, hy_skip, merge_up, merge_b, w_branch, w_out,
                   w_router, router_bias, moe_w1, moe_w3, moe_w2, final_g)


def split_w_in(cfg, w_in):
    gw4 = 4 * cfg.gdn_w
    gdn_small = 4 * cfg.gdn_h
    gla0 = gw4 + gdn_small
    gla_w = 2 * cfg.gla_qk + 2 * cfg.gla_v
    hy0 = gla0 + gla_w + 32
    hy_w = 3 * cfg.hy_w
    w_t = jnp.swapaxes(w_in, 1, 2)
    l, _, k = w_t.shape
    zeros = lambda n: jnp.zeros((l, n, k), w_t.dtype)
    small = jnp.concatenate([w_t[:, hy0 + hy_w:hy0 + hy_w + cfg.merge_rank],
                             w_t[:, gw4:gla0], zeros(GLA_LR_LANE0 - gdn_small),
                             w_t[:, gla0 + gla_w:hy0], zeros(LANES - GLA_LR_LANE0 - 32)], axis=1)
    return cast_rows(w_t, 0, gw4), cast_rows(w_t, gla0, gla_w), cast_rows(w_t, hy0, hy_w), small


def forward(cfg, x, c, ctx, c_ctx, norm1_g, norm2_g, w_mod, b_mod, w_in, gdn_conv, gdn_a_log, gdn_dt_bias,
            gdn_norm, gla_gate_up, gla_gate_b, gla_norm, hy_conv_w, hy_conv_b, hy_w1, hy_b1, hy_w2, hy_b2,
            hy_w3, hy_freq, hy_skip, merge_up, merge_b, w_branch, w_out, w_router, router_bias, moe_w1,
            moe_w3, moe_w2, final_g):
    d, tr, tm = cfg.d, cfg.row_tile, cfg.mm_tm
    depth = w_in.shape[0]
    nh = cfg.gdn_h
    tn = min(1024, d)
    w_gdn, w_gla, w_hy, w_small = split_w_in(cfg, w_in)
    w1b, w3b = moe_w1.astype(BF16), moe_w3.astype(BF16)
    w2b = moe_w2.astype(BF16)
    n_sorted = -(-(cfg.t + N_GROUPS * (cfg.moe_tile - 1)) // cfg.moe_tile) * cfg.moe_tile
    wb, mu, wo = w_branch.astype(BF16), merge_up.astype(BF16), w_out.astype(BF16)
    mb = merge_b.reshape(depth, 3, 1, d)
    wr_pad = jnp.zeros((d, LANES), F32).at[:, :N_EXPERTS].set(w_router)
    rb_col = router_bias.reshape(N_EXPERTS, 1)
    expand = expert_expand(PER_GROUP)
    wup, gbias = prep_gla_gate(gla_gate_up, gla_gate_b)
    lane0 = 2 * nh
    alog_rows = jnp.zeros((depth, 1, LANES), F32).at[:, 0, lane0:2 * lane0].set(gdn_a_log.reshape(depth, -1))
    dtb_rows = jnp.zeros((depth, 1, LANES), F32).at[:, 0, lane0:2 * lane0].set(gdn_dt_bias.reshape(depth, -1))
    hc = hyena_consts(cfg)
    gla_gate_blk = (2 * cfg.gla_qk + cfg.gla_v) // cfg.gla_v
    z_proj = functools.partial(matmul_nt, tm=tm, tn=tn, out_dtype=BF16)

    lat = jnp.concatenate([x[0], ctx[0]], axis=0)
    cvec = jnp.zeros((8, d), F32).at[0].set(c[0]).at[1].set(c_ctx)
    mods = modvec(cvec, w_mod, b_mod)

    for l in range(depth):
        with_ctx = l < depth - 1
        mod = mods[l]
        h = norm1(cfg, lat, norm1_g[l], mod)
        z_gdn = z_proj(h, w_gdn, l, name="w_in_gdn")
        z_gla = z_proj(h, w_gla, l, name="w_in_gla")
        z_hy = z_proj(h, w_hy, l, name="w_in_hy")
        z_small = matmul_nt(h, w_small, l, tm=tm, tn=w_small.shape[1], name="w_in_small")
        qn, kn, vs, bb, gcb, gct = gdn_prep(cfg, z_gdn, z_small, gdn_conv[l], alog_rows[l], dtb_rows[l])
        o_f, o_b = gdn_scan(cfg, *gdn_chunks(cfg, qn, kn, vs, bb, gcb, gct))
        a_all = head_norm(o_f, o_b, z_gdn, 3, gdn_norm[l], nh, 128, tr)
        (olf, olb), (ocf, ocb) = gla(cfg, z_gla, z_small, 0, wup[l], gbias[l])
        b_lat = head_norm(olf, olb, z_gla, gla_gate_blk, gla_norm[l], cfg.gla_h, 256, tr)
        b_ctx = head_norm(ocf, ocb, z_gla, gla_gate_blk, gla_norm[l], cfg.gla_h, 256, tr, gate_row0=cfg.seq)
        b_all = jnp.concatenate([b_lat, b_ctx], axis=0)
        vxx = hyena_prep(cfg, z_hy, hy_conv_w[l], hy_conv_b[l])
        filt_w = prep_hyena_filter(hy_w1[l], hy_b1[l], hy_w2[l], hy_b2[l], hy_w3[l], hy_freq[l])
        c_lat, c_ctx_out = hyena(cfg, hc, vxx, filt_w, hy_skip[l], with_ctx)
        c_all = jnp.concatenate([c_lat, c_ctx_out], axis=0).astype(BF16)
        s = merge_branches(cfg, (a_all, b_all, c_all), z_small, wb, mu, mb, l, tn=tn)
        lat = matmul_resid(cfg, s, wo, l, lat, mod, 2, tn=tn, name="w_out")
        hp, gate4, gid = norm2_route_sparse(cfg, lat, norm2_g[l], mod, wr_pad, rb_col, expand)
        pos, tile_groups = moe_positions(gid, cfg.moe_tile)
        xs, gs = moe_permute(pos, hp, gate4, n_sorted)
        ys = moe_grouped(cfg, tile_groups, xs, gs, w1b, w3b, w2b, l, cfg.moe_tile)
        lat = moe_unpermute_resid(cfg, pos, ys, lat, mod, 5)
    return final_norm(lat, final_g, cfg.seq, tr)[None]
```

```python
import functools
import math
from typing import NamedTuple

import numpy as np
import jax
import jax.numpy as jnp
from jax import lax
from jax.experimental import pallas as pl
from jax.experimental.pallas import tpu as pltpu

F32 = jnp.float32
BF16 = jnp.bfloat16

EPS = 1e-6
LANES = 128
V7X_VMEM_BYTES = 64 * 1024 * 1024
VMEM_LIMIT = (V7X_VMEM_BYTES * 13) // 16
SHORT_CONV = 3
GLA_CHUNK = 64
GLA_GATE_NORM = 16.0
N_EXPERTS = 16
N_GROUPS = 4
HY_EMB = 33
HY_FAST_DECAY = 0.3
HY_SLOW_DECAY = 1.5
HY_TARGET = 1e-2


class Cfg(NamedTuple):
    d: int
    seq: int
    ctx: int
    grid_w: int
    gdn_h: int
    gla_h: int
    hy_w: int
    merge_rank: int
    d_expert: int
    row_tile: int
    mm_tm: int

    @property
    def t(self):
        return self.seq + self.ctx

    @property
    def gdn_w(self):
        return self.gdn_h * 128

    @property
    def gla_qk(self):
        return self.gla_h * 128

    @property
    def gla_v(self):
        return self.gla_h * 256


def _cparams(sem):
    return pltpu.CompilerParams(dimension_semantics=sem, vmem_limit_bytes=VMEM_LIMIT)


def _split3(x):
    hi = x.astype(BF16)
    r1 = x - hi.astype(F32)
    mid = r1.astype(BF16)
    lo = (r1 - mid.astype(F32)).astype(BF16)
    return hi, mid, lo


def _dot(a, b):
    return jnp.dot(a, b, preferred_element_type=F32)


def _dot_sel(sel_bf16, x):
    hi, mid, lo = _split3(x)
    return _dot(sel_bf16, hi) + _dot(sel_bf16, mid) + _dot(sel_bf16, lo)


def _dot_x_sel(x, sel_bf16):
    hi, mid, lo = _split3(x)
    return _dot(hi, sel_bf16) + _dot(mid, sel_bf16) + _dot(lo, sel_bf16)


def _dot_hi(a, b):
    a1, a2, a3 = _split3(a)
    b1, b2, b3 = _split3(b)
    return (_dot(a1, b1) + (_dot(a1, b2) + _dot(a2, b1))
            + (_dot(a2, b2) + _dot(a1, b3) + _dot(a3, b1)))


def _dot_3(a, b):
    a1 = a.astype(BF16)
    a2 = (a - a1.astype(F32)).astype(BF16)
    b1 = b.astype(BF16)
    b2 = (b - b1.astype(F32)).astype(BF16)
    return _dot(a1, b1) + (_dot(a1, b2) + _dot(a2, b1))


def _silu(x):
    return x * jax.nn.sigmoid(x)


def _modvec_kernel(x_ref, w_ref, b_ref, o_ref):
    o_ref[...] = _dot(_silu(x_ref[...]), w_ref[...]) + b_ref[...]


def modvec(cvec, w_mod, b_mod, tn=512):
    depth, d, n = w_mod.shape
    return pl.pallas_call(
        _modvec_kernel,
        out_shape=jax.ShapeDtypeStruct((depth, 8, n), F32),
        grid=(depth, n // tn),
        in_specs=[pl.BlockSpec((8, d), lambda l, j: (0, 0)),
                  pl.BlockSpec((None, d, tn), lambda l, j: (l, 0, j)),
                  pl.BlockSpec((None, 1, tn), lambda l, j: (l, 0, j))],
        out_specs=pl.BlockSpec((None, 8, tn), lambda l, j: (l, 0, j)),
        compiler_params=_cparams(("parallel", "parallel")),
        name="modvec",
    )(cvec, w_mod, b_mod.reshape(depth, 1, n))


def _mod_row(mod_ref, is_ctx, idx, d):
    return mod_ref[pl.ds(is_ctx, 1), idx * d:(idx + 1) * d]


def _norm_mod(x, gain, shift, scale):
    y = x * lax.rsqrt(jnp.mean(x * x, axis=-1, keepdims=True) + EPS)
    return (y * gain) * (1.0 + scale) + shift


def _norm1_kernel(x_ref, g_ref, mod_ref, o_ref, *, d, n_lat_tiles):
    is_ctx = (pl.program_id(0) >= n_lat_tiles).astype(jnp.int32)
    h = _norm_mod(x_ref[...], g_ref[...], _mod_row(mod_ref, is_ctx, 0, d),
                  _mod_row(mod_ref, is_ctx, 1, d))
    o_ref[...] = h.astype(BF16)


def norm1(cfg, x, gain, mod):
    tr = cfg.row_tile
    t, d = x.shape
    return pl.pallas_call(
        functools.partial(_norm1_kernel, d=d, n_lat_tiles=cfg.seq // tr),
        out_shape=jax.ShapeDtypeStruct((t, d), BF16),
        grid=(t // tr,),
        in_specs=[pl.BlockSpec((tr, d), lambda i: (i, 0)),
                  pl.BlockSpec((1, d), lambda i: (0, 0)),
                  pl.BlockSpec((8, 6 * d), lambda i: (0, 0))],
        out_specs=pl.BlockSpec((tr, d), lambda i: (i, 0)),
        compiler_params=_cparams(("parallel",)),
        name="norm1",
    )(x, gain.reshape(1, d), mod)


def _route(sel_t, sc_t):
    per = N_EXPERTS // N_GROUPS
    grp_score = []
    for g in range(N_GROUPS):
        v = sel_t[g * per:(g + 1) * per]
        best = None
        for a in range(per):
            for b in range(a + 1, per):
                s = v[a] + v[b]
                best = s if best is None else jnp.maximum(best, s)
        grp_score.append(best)
    best_s, best_g = grp_score[0], jnp.zeros_like(grp_score[0])
    for g in range(1, N_GROUPS):
        better = grp_score[g] > best_s
        best_s = jnp.where(better, grp_score[g], best_s)
        best_g = jnp.where(better, float(g), best_g)
    picked = []
    for e in range(N_EXPERTS):
        g, i = divmod(e, per)
        rank = jnp.zeros_like(best_s)
        for j in range(per):
            if j == i:
                continue
            o = sel_t[g * per + j]
            ahead = (o >= sel_t[e]) if j < i else (o > sel_t[e])
            rank = rank + ahead.astype(F32)
        picked.append(jnp.where((best_g == float(g)) & (rank < 2.0), sc_t[e], 0.0))
    den = picked[0]
    for e in range(1, N_EXPERTS):
        den = den + picked[e]
    inv = 1.0 / den
    return [p * inv for p in picked], best_g


def _norm2_kernel(x_ref, g_ref, mod_ref, wr_ref, rb_ref, ex_ref, o_ref, gate_ref, *, d,
                  n_lat_tiles):
    is_ctx = (pl.program_id(0) >= n_lat_tiles).astype(jnp.int32)
    h = _norm_mod(x_ref[...], g_ref[...], _mod_row(mod_ref, is_ctx, 3, d),
                  _mod_row(mod_ref, is_ctx, 4, d))
    o_ref[...] = h.astype(BF16)
    logits = _dot_3(h, wr_ref[...])
    lt = jnp.transpose(logits)
    sc = jax.nn.sigmoid(lt[0:N_EXPERTS, :])
    sel = sc + rb_ref[...]
    gate_rows, _ = _route([sel[e:e + 1, :] for e in range(N_EXPERTS)],
                          [sc[e:e + 1, :] for e in range(N_EXPERTS)])
    rows = lax.broadcasted_iota(jnp.int32, lt.shape, 0)
    gt = jnp.zeros(lt.shape, F32)
    for e in range(N_EXPERTS):
        gt = jnp.where(rows == e, gate_rows[e], gt)
    gate = jnp.transpose(gt)
    gate_ref[...] = _dot_x_sel(gate, ex_ref[...])


def norm2_route(cfg, x, gain, mod, w_router_pad, rbias_col, expand):
    tr = cfg.row_tile
    t, d = x.shape
    return pl.pallas_call(
        functools.partial(_norm2_kernel, d=d, n_lat_tiles=cfg.seq // tr),
        out_shape=(jax.ShapeDtypeStruct((t, d), BF16),
                   jax.ShapeDtypeStruct((t, N_EXPERTS * LANES), F32)),
        grid=(t // tr,),
        in_specs=[pl.BlockSpec((tr, d), lambda i: (i, 0)),
                  pl.BlockSpec((1, d), lambda i: (0, 0)),
                  pl.BlockSpec((8, 6 * d), lambda i: (0, 0)),
                  pl.BlockSpec((d, LANES), lambda i: (0, 0)),
                  pl.BlockSpec((N_EXPERTS, 1), lambda i: (0, 0)),
                  pl.BlockSpec((LANES, N_EXPERTS * LANES), lambda i: (0, 0))],
        out_specs=(pl.BlockSpec((tr, d), lambda i: (i, 0)),
                   pl.BlockSpec((tr, N_EXPERTS * LANES), lambda i: (i, 0))),
        compiler_params=_cparams(("parallel",)),
        name="norm2_route",
    )(x, gain.reshape(1, d), mod, w_router_pad, rbias_col, expand)


def _final_norm_kernel(x_ref, g_ref, o_ref):
    x = x_ref[...]
    o_ref[...] = (x * lax.rsqrt(jnp.mean(x * x, axis=-1, keepdims=True) + EPS)) * g_ref[...]


def final_norm(x, gain, n_rows, tr):
    d = x.shape[1]
    return pl.pallas_call(
        _final_norm_kernel,
        out_shape=jax.ShapeDtypeStruct((n_rows, d), F32),
        grid=(n_rows // tr,),
        in_specs=[pl.BlockSpec((tr, d), lambda i: (i, 0)),
                  pl.BlockSpec((1, d), lambda i: (0, 0))],
        out_specs=pl.BlockSpec((tr, d), lambda i: (i, 0)),
        compiler_params=_cparams(("parallel",)),
        name="final_norm",
    )(x, gain.reshape(1, d))


def _mm_plain_kernel(a_ref, w_ref, o_ref):
    o_ref[...] = _dot(a_ref[...], w_ref[...].astype(BF16)).astype(o_ref.dtype)


def matmul_plain(a, w, layer, *, tm, tn, n_cols=None, col0=0, out_dtype=F32, name="mm"):
    t, k = a.shape
    n_cols = w.shape[2] - col0 if n_cols is None else n_cols
    off = col0 // tn
    return pl.pallas_call(
        _mm_plain_kernel,
        out_shape=jax.ShapeDtypeStruct((t, n_cols), out_dtype),
        grid=(n_cols // tn, t // tm),
        in_specs=[pl.BlockSpec((tm, k), lambda j, i: (i, 0)),
                  pl.BlockSpec((None, k, tn), lambda j, i: (layer, 0, j + off))],
        out_specs=pl.BlockSpec((tm, tn), lambda j, i: (i, j)),
        compiler_params=_cparams(("parallel", "parallel")),
        name=name,
    )(a, w)


def _mm_nt_kernel(a_ref, w_ref, o_ref):
    nt = (((1,), (1,)), ((), ()))
    o_ref[...] = lax.dot_general(a_ref[...], w_ref[...].astype(BF16), nt,
                                 preferred_element_type=F32).astype(o_ref.dtype)


def matmul_nt(a, w_t, layer, *, tm, tn, out_dtype=F32, name="mm_nt", row0=0, n_rows=None):
    k = a.shape[1]
    t = a.shape[0] if n_rows is None else n_rows
    n = w_t.shape[1]
    rb0 = row0 // tm
    return pl.pallas_call(
        _mm_nt_kernel,
        out_shape=jax.ShapeDtypeStruct((t, n), out_dtype),
        grid=(n // tn, t // tm),
        in_specs=[pl.BlockSpec((tm, k), lambda j, i: (i + rb0, 0)),
                  pl.BlockSpec((None, tn, k), lambda j, i: (layer, j, 0))],
        out_specs=pl.BlockSpec((tm, tn), lambda j, i: (i, j)),
        compiler_params=_cparams(("parallel", "parallel")),
        name=name,
    )(a, w_t)


def _cast_rows_kernel(w_ref, o_ref):
    o_ref[...] = w_ref[...].astype(o_ref.dtype)


def cast_rows(w_t, row0, n_rows):
    l, _, k = w_t.shape
    tr = math.gcd(row0, n_rows) if row0 else n_rows
    while tr > 512 and tr % 2 == 0:
        tr //= 2
    assert tr % 16 == 0
    off = row0 // tr
    return pl.pallas_call(
        _cast_rows_kernel,
        out_shape=jax.ShapeDtypeStruct((l, n_rows, k), BF16),
        grid=(l, n_rows // tr),
        in_specs=[pl.BlockSpec((None, tr, k), lambda i, j: (i, j + off, 0))],
        out_specs=pl.BlockSpec((None, tr, k), lambda i, j: (i, j, 0)),
        compiler_params=_cparams(("parallel", "parallel")),
        name="cast_rows",
    )(w_t)


def _mm_resid_kernel(a_ref, w_ref, r_ref, mod_ref, o_ref, *, tm, n_lat):
    row = pl.program_id(1) * tm + lax.broadcasted_iota(jnp.int32, (tm, 1), 0)
    gate = jnp.where(row < n_lat, mod_ref[0:1, :], mod_ref[1:2, :])
    o_ref[...] = r_ref[...] + gate * _dot(a_ref[...], w_ref[...])


def matmul_resid(cfg, a, w, layer, resid, mod, idx, *, tn, name):
    t, k = a.shape
    d = w.shape[2]
    tm = cfg.mm_tm
    return pl.pallas_call(
        functools.partial(_mm_resid_kernel, tm=tm, n_lat=cfg.seq),
        out_shape=jax.ShapeDtypeStruct((t, d), F32),
        grid=(d // tn, t // tm),
        in_specs=[pl.BlockSpec((tm, k), lambda j, i: (i, 0)),
                  pl.BlockSpec((None, k, tn), lambda j, i: (layer, 0, j)),
                  pl.BlockSpec((tm, tn), lambda j, i: (i, j)),
                  pl.BlockSpec((8, tn), lambda j, i: (0, idx * (d // tn) + j))],
        out_specs=pl.BlockSpec((tm, tn), lambda j, i: (i, j)),
        input_output_aliases={2: 0},
        compiler_params=_cparams(("parallel", "parallel")),
        name=name,
    )(a, w, resid, mod)


def _mm_moe_act_kernel(a_ref, w1_ref, w3_ref, g_ref, o_ref, *, de, n_e):
    a = a_ref[...]
    for e in range(n_e):
        up = _dot(a, w1_ref[e])
        lin = _dot(a, w3_ref[e])
        g = g_ref[:, e * LANES:(e + 1) * LANES]
        g = jnp.concatenate([g] * (de // LANES), axis=1)
        o_ref[:, e * de:(e + 1) * de] = (_silu(up) * lin * g).astype(o_ref.dtype)


def matmul_moe_act(cfg, h, w1, w3, layer, gate_rep, *, n_e=2):
    t, k = h.shape
    de = cfg.d_expert
    tm = cfg.mm_tm
    wspec = pl.BlockSpec((None, n_e, k, de), lambda j, i: (layer, j, 0, 0))
    return pl.pallas_call(
        functools.partial(_mm_moe_act_kernel, de=de, n_e=n_e),
        out_shape=jax.ShapeDtypeStruct((t, N_EXPERTS * de), BF16),
        grid=(N_EXPERTS // n_e, t // tm),
        in_specs=[pl.BlockSpec((tm, k), lambda j, i: (i, 0)), wspec, wspec,
                  pl.BlockSpec((tm, n_e * LANES), lambda j, i: (i, j))],
        out_specs=pl.BlockSpec((tm, n_e * de), lambda j, i: (i, j)),
        compiler_params=_cparams(("parallel", "parallel")),
        name="moe_up",
    )(h, w1, w3, gate_rep)


def _merge_kernel(a0_ref, a1_ref, a2_ref, zg_ref, wb_ref, mu_ref, mb_ref, o_ref):
    zg = zg_ref[...].astype(BF16)
    acc = None
    for n, a_ref in enumerate((a0_ref, a1_ref, a2_ref)):
        y = _dot(a_ref[...], wb_ref[n])
        gate = jax.nn.sigmoid(_dot(zg, mu_ref[n]) + mb_ref[n])
        acc = gate * y if acc is None else acc + gate * y
    o_ref[...] = acc.astype(o_ref.dtype)


def merge_branches(cfg, outs, z_small, w_branch, merge_up, merge_b, layer, *, tn):
    t, bw = outs[0].shape
    d = w_branch.shape[3]
    r = cfg.merge_rank
    tm = cfg.mm_tm
    a_spec = pl.BlockSpec((tm, bw), lambda j, i: (i, 0))
    return pl.pallas_call(
        _merge_kernel,
        out_shape=jax.ShapeDtypeStruct((t, d), BF16),
        grid=(d // tn, t // tm),
        in_specs=[a_spec, a_spec, a_spec,
                  pl.BlockSpec((tm, r), lambda j, i: (i, 0)),
                  pl.BlockSpec((None, 3, bw, tn), lambda j, i: (layer, 0, 0, j)),
                  pl.BlockSpec((None, 3, r, tn), lambda j, i: (layer, 0, 0, j)),
                  pl.BlockSpec((None, 3, 1, tn), lambda j, i: (layer, 0, 0, j))],
        out_specs=pl.BlockSpec((tm, tn), lambda j, i: (i, j)),
        compiler_params=_cparams(("parallel", "parallel")),
        name="merge",
    )(outs[0], outs[1], outs[2], z_small, w_branch, merge_up, merge_b)


HALO_ROWS = 16


def _halo_specs(tr, width, col_block, n_rows):
    rb = tr // HALO_ROWS
    last = n_rows // HALO_ROWS - 1
    main = pl.BlockSpec((tr, width), lambda i: (i, col_block))
    prev = pl.BlockSpec((HALO_ROWS, width), lambda i: (jnp.maximum(i * rb - 1, 0), col_block))
    nxt = pl.BlockSpec((HALO_ROWS, width), lambda i: (jnp.minimum((i + 1) * rb, last), col_block))
    return main, prev, nxt


def _conv3(z_ref, prev_ref, next_ref, w_ref, n_lat_tiles):
    i = pl.program_id(0)
    x = z_ref[...].astype(F32)
    tr = x.shape[0]
    has_prev = jnp.logical_and(i != 0, i != n_lat_tiles).astype(F32)
    has_next = jnp.logical_and(i != n_lat_tiles - 1, i != n_lat_tiles).astype(F32)
    row = lax.broadcasted_iota(jnp.int32, (tr, 1), 0)
    halo_prev = prev_ref[HALO_ROWS - 1:HALO_ROWS, :].astype(F32) * has_prev
    halo_next = next_ref[0:1, :].astype(F32) * has_next
    x_prev = jnp.where(row == 0, halo_prev, pltpu.roll(x, 1, axis=0))
    x_next = jnp.where(row == tr - 1, halo_next, pltpu.roll(x, tr - 1, axis=0))
    return x_prev * w_ref[0:1, :] + x * w_ref[1:2, :] + x_next * w_ref[2:3, :]


def _tri_masks(n):
    r = lax.broadcasted_iota(jnp.int32, (n, n), 0)
    c = lax.broadcasted_iota(jnp.int32, (n, n), 1)
    return r, c


def _gdn_prep_kernel(z_ref, zp_ref, zn_ref, s_ref, cw_ref, alog_ref, dtb_ref,
                     q_ref, k_ref, v_ref, bb_ref, gcb_ref, gct_ref, *, n_heads, n_lat_tiles):
    tr = z_ref.shape[0]
    hw = n_heads * 128
    y = _silu(_conv3(z_ref, zp_ref, zn_ref, cw_ref, n_lat_tiles))
    for h in range(n_heads):
        q = y[:, h * 128:(h + 1) * 128]
        k = y[:, hw + h * 128:hw + (h + 1) * 128]
        q = q * (lax.rsqrt(jnp.sum(q * q, axis=-1, keepdims=True) + EPS) * (128.0 ** -0.5))
        k = k * lax.rsqrt(jnp.sum(k * k, axis=-1, keepdims=True) + EPS)
        q_ref[h] = q.astype(BF16)
        k_ref[h] = k.astype(BF16)
        v_ref[h] = y[:, 2 * hw + h * 128:2 * hw + (h + 1) * 128].astype(BF16)
    s = s_ref[...]
    nh2 = 2 * n_heads
    beta = jax.nn.sigmoid(s)
    g = -jnp.exp(alog_ref[...]) * jax.nn.softplus(s + dtb_ref[...])
    r, c = _tri_masks(tr)
    incl_lo = (c <= r).astype(BF16)
    incl_up = (c >= r).astype(BF16)
    lane = lax.broadcasted_iota(jnp.int32, (tr, LANES), 1)
    fwd_lane = lane < nh2 + n_heads
    gc = jnp.where(fwd_lane, _dot_sel(incl_lo, g), _dot_sel(incl_up, g))
    for ch in range(nh2):
        bb_ref[ch] = jnp.broadcast_to(beta[:, ch:ch + 1], (tr, LANES))
        gcb_ref[ch] = jnp.broadcast_to(gc[:, nh2 + ch:nh2 + ch + 1], (tr, LANES))
    gct_ref[...] = jnp.transpose(gc)[nh2:2 * nh2, :]


def gdn_prep(cfg, z_big, z_small, conv_w, alog_row, dtb_row):
    tr = cfg.row_tile
    t = z_big.shape[0]
    nh = cfg.gdn_h
    hw = nh * 128
    n_tiles = t // tr
    main, prev, nxt = _halo_specs(tr, 3 * hw, 0, t)
    head_out = jax.ShapeDtypeStruct((nh, t, 128), BF16)
    head_spec = pl.BlockSpec((nh, tr, 128), lambda i: (0, i, 0))
    col_out = jax.ShapeDtypeStruct((2 * nh, t, LANES), F32)
    col_spec = pl.BlockSpec((2 * nh, tr, LANES), lambda i: (0, i, 0))
    return pl.pallas_call(
        functools.partial(_gdn_prep_kernel, n_heads=nh, n_lat_tiles=cfg.seq // tr),
        out_shape=(head_out, head_out, head_out, col_out, col_out,
                   jax.ShapeDtypeStruct((n_tiles, 2 * nh, tr), F32)),
        grid=(n_tiles,),
        in_specs=[main, prev, nxt,
                  pl.BlockSpec((tr, LANES), lambda i: (i, cfg.merge_rank // LANES)),
                  pl.BlockSpec((SHORT_CONV, 3 * hw), lambda i: (0, 0)),
                  pl.BlockSpec((1, LANES), lambda i: (0, 0)),
                  pl.BlockSpec((1, LANES), lambda i: (0, 0))],
        out_specs=(head_spec, head_spec, head_spec, col_spec, col_spec,
                   pl.BlockSpec((None, 2 * nh, tr), lambda i: (i, 0, 0))),
        compiler_params=_cparams(("parallel",)),
        name="gdn_prep",
    )(z_big, z_big, z_big, z_small, conv_w, alog_row, dtb_row)


def _unit_tri_inverses(n_mats, r, c):
    n = n_mats[0].shape[0]
    eye = (r == c).astype(F32)

    def same_block(b):
        sh = int(math.log2(b))
        return (r >> sh) == (c >> sh)

    blk = same_block(8)
    pfs = [jnp.where(blk, -m, 0.0) for m in n_mats]
    ps = [pf.astype(BF16) for pf in pfs]
    p2 = [_dot(p, p).astype(BF16) for p in ps]
    p4 = [_dot(x, x).astype(BF16) for x in p2]
    ts = [eye + pf for pf in pfs]
    ts = [t + _dot(t.astype(BF16), x) for t, x in zip(ts, p2)]
    ts = [t + _dot(t.astype(BF16), x) for t, x in zip(ts, p4)]
    b = 8
    while b < n:
        sel = jnp.logical_and(same_block(2 * b), jnp.logical_not(same_block(b)))
        offs = [jnp.where(sel, m, 0.0).astype(BF16) for m in n_mats]
        tbs = [t.astype(BF16) for t in ts]
        xs = [_dot(tb, off).astype(BF16) for tb, off in zip(tbs, offs)]
        ts = [t - _dot(x, tb) for t, x, tb in zip(ts, xs, tbs)]
        b *= 2
    return ts, eye


GDN_HEADS_PER_STEP = 4


def _gdn_chunk_kernel(q_ref, k_ref, v_ref, bf_ref, bb_ref, gf_ref, gb_ref, gct_ref,
                      uw_ref, aq_ref, qk_ref, gl_ref, *, n_heads):
    hps = q_ref.shape[0]
    h0 = pl.program_id(1) * hps
    tr = q_ref.shape[1]
    r, c = _tri_masks(tr)
    nt = (((1,), (1,)), ((), ()))
    masks = (((c <= r), (c < r)), ((c >= r), (c > r)))
    chains = [(j, d) for j in range(hps) for d in range(2)]
    k_b = [k_ref[j] for j in range(hps)]
    k_f = [x.astype(F32) for x in k_b]
    a_qk = [lax.dot_general(q_ref[j], k_b[j], nt, preferred_element_type=F32) for j in range(hps)]
    beta, gc, gam, kb, n_mats = {}, {}, {}, {}, []
    for j, d in chains:
        beta[j, d] = (bf_ref, bb_ref)[d][j]
        gc[j, d] = (gf_ref, gb_ref)[d][j]
        gc_row = gct_ref[pl.ds(d * n_heads + h0 + j, 1), :]
        gc_col = jnp.concatenate([gc[j, d]] * (tr // LANES), axis=1)
        gam[j, d] = jnp.exp(jnp.where(masks[d][0], gc_col - gc_row, -jnp.inf))
        kb[j, d] = k_f[j] * beta[j, d]
        a_kk = lax.dot_general(kb[j, d].astype(BF16), k_b[j], nt, preferred_element_type=F32)
        n_mats.append(jnp.where(masks[d][1], a_kk * gam[j, d], 0.0))
    t_invs, eye = _unit_tri_inverses(n_mats, r, c)
    for (j, d), t_inv in zip(chains, t_invs):
        e = jnp.exp(gc[j, d])
        rhs = jnp.concatenate([v_ref[j].astype(F32) * beta[j, d], kb[j, d] * e], axis=1)
        sol = rhs + _dot((t_inv - eye).astype(BF16), rhs.astype(BF16))
        g_last = gc[j, d][tr - 1:tr, :] if d == 0 else gc[j, d][0:1, :]
        uw_ref[d, j] = sol.astype(BF16)
        aq_ref[d, j] = (a_qk[j] * gam[j, d]).astype(BF16)
        qk_ref[d, j] = jnp.concatenate([q_ref[j].astype(F32) * e,
                                        k_f[j] * jnp.exp(g_last - gc[j, d])], axis=1).astype(BF16)
        gl_ref[d, j] = jnp.broadcast_to(jnp.exp(g_last), (8, LANES))


def gdn_chunks(cfg, qn, kn, vs, bb, gcb, gct):
    tr = cfg.row_tile
    nh, t, _ = qn.shape
    hps = min(GDN_HEADS_PER_STEP, nh)
    assert nh % hps == 0
    n_tiles = t // tr
    head = pl.BlockSpec((hps, tr, 128), lambda i, h: (h, i, 0))
    head_b = pl.BlockSpec((hps, tr, 128), lambda i, h: (h + nh // hps, i, 0))
    big = jax.ShapeDtypeStruct((2, nh, t, 2 * 128), BF16)
    aq = jax.ShapeDtypeStruct((2, nh, t, tr), BF16)
    return pl.pallas_call(
        functools.partial(_gdn_chunk_kernel, n_heads=nh),
        out_shape=(big, aq, big, jax.ShapeDtypeStruct((2, nh, n_tiles * 8, LANES), F32)),
        grid=(n_tiles, nh // hps),
        in_specs=[head, head, head, head, head_b, head, head_b,
                  pl.BlockSpec((None, 2 * nh, tr), lambda i, h: (i, 0, 0))],
        out_specs=(pl.BlockSpec((2, hps, tr, 256), lambda i, h: (0, h, i, 0)),
                   pl.BlockSpec((2, hps, tr, tr), lambda i, h: (0, h, i, 0)),
                   pl.BlockSpec((2, hps, tr, 256), lambda i, h: (0, h, i, 0)),
                   pl.BlockSpec((2, hps, 8, LANES), lambda i, h: (0, h, i, 0))),
        compiler_params=_cparams(("parallel", "parallel")),
        name="gdn_chunks",
    )(qn, kn, vs, bb, bb, gcb, gcb, gct)


def _gdn_scan_kernel(uwf_ref, aqf_ref, qkf_ref, glf_ref, uwb_ref, aqb_ref, qkb_ref, glb_ref,
                     of_ref, ob_ref, s_ref, *, n_heads):
    @pl.when(pl.program_id(0) == 0)
    def _():
        s_ref[...] = jnp.zeros_like(s_ref)

    tn = (((0,), (0,)), ((), ()))
    for d, (uw_ref, aq_ref, qk_ref, gl_ref, o_ref) in enumerate(
            ((uwf_ref, aqf_ref, qkf_ref, glf_ref, of_ref), (uwb_ref, aqb_ref, qkb_ref, glb_ref, ob_ref))):
        for h in range(n_heads):
            st = s_ref[d, h]
            st_b = st.astype(BF16)
            uw = uw_ref[h]
            qk = qk_ref[h]
            v_new = uw[:, 0:128].astype(F32) - _dot(uw[:, 128:256], st_b)
            v_new_b = v_new.astype(BF16)
            o_ref[:, h * 128:(h + 1) * 128] = (
                _dot(qk[:, 0:128], st_b) + _dot(aq_ref[h], v_new_b)).astype(o_ref.dtype)
            s_ref[d, h] = gl_ref[h][0:1, :] * st + lax.dot_general(
                qk[:, 128:256], v_new_b, tn, preferred_element_type=F32)


def gdn_scan(cfg, uw, aq, qk, gl):
    tr = cfg.row_tile
    _, nh, t, _ = uw.shape
    n_tiles = t // tr
    last = n_tiles - 1

    def fwd(s):
        return jnp.where(s == 0, last, s - 1)

    def bwd(s):
        return jnp.where(s == 0, last, last - s)

    def specs(d, order):
        return [pl.BlockSpec((None, nh, tr, 256), lambda s: (d, 0, order(s), 0)),
                pl.BlockSpec((None, nh, tr, tr), lambda s: (d, 0, order(s), 0)),
                pl.BlockSpec((None, nh, tr, 256), lambda s: (d, 0, order(s), 0)),
                pl.BlockSpec((None, nh, 8, LANES), lambda s: (d, 0, order(s), 0))]

    out = jax.ShapeDtypeStruct((t, nh * 128), BF16)
    return pl.pallas_call(
        functools.partial(_gdn_scan_kernel, n_heads=nh),
        out_shape=(out, out),
        grid=(n_tiles,),
        in_specs=specs(0, fwd) + specs(1, bwd),
        out_specs=(pl.BlockSpec((tr, nh * 128), lambda s: (fwd(s), 0)),
                   pl.BlockSpec((tr, nh * 128), lambda s: (bwd(s), 0))),
        scratch_shapes=[pltpu.VMEM((2, nh, 128, 128), F32)],
        compiler_params=_cparams(("arbitrary",)),
        name="gdn_scan",
    )(uw, aq, qk, gl, uw, aq, qk, gl)


def _head_norm_kernel(of_ref, ob_ref, gate_ref, w_ref, o_ref, *, n_heads, dv):
    for h in range(n_heads):
        sl = slice(h * dv, (h + 1) * dv)
        o = of_ref[:, sl].astype(F32) + ob_ref[:, sl].astype(F32)
        o = o * lax.rsqrt(jnp.mean(o * o, axis=-1, keepdims=True) + EPS) * w_ref[...]
        o_ref[:, sl] = (o * _silu(gate_ref[:, sl].astype(F32))).astype(o_ref.dtype)


def head_norm(o_f, o_b, z_big, gate_col_block, norm_w, n_heads, dv, tr, gate_row0=0):
    t, w = o_f.shape
    spec = pl.BlockSpec((tr, w), lambda i: (i, 0))
    rb0 = gate_row0 // tr
    return pl.pallas_call(
        functools.partial(_head_norm_kernel, n_heads=n_heads, dv=dv),
        out_shape=jax.ShapeDtypeStruct((t, w), BF16),
        grid=(t // tr,),
        in_specs=[spec, spec, pl.BlockSpec((tr, w), lambda i: (i + rb0, gate_col_block)),
                  pl.BlockSpec((1, dv), lambda i: (0, 0))],
        out_specs=spec,
        compiler_params=_cparams(("parallel",)),
        name="head_norm",
    )(o_f, o_b, z_big, norm_w.reshape(1, dv))


GLA_TILE = 128
GLA_LR_LANE0 = 32


def _gla_direction(d, qk_ref, v_ref, sm_ref, wup_ref, gb_ref, o_ref, st_ref, n_heads):
    tr = GLA_TILE
    qkw = n_heads * 128
    r, c = _tri_masks(tr)
    sh = int(math.log2(GLA_CHUNK))
    same = (r >> sh) == (c >> sh)
    mask = jnp.logical_and(same, (c <= r) if d == 0 else (c >= r))
    logit = _dot_hi(sm_ref[...], wup_ref[:, d * qkw:(d + 1) * qkw]) + gb_ref[:, d * qkw:(d + 1) * qkw]
    g = jax.nn.log_sigmoid(logit) * (1.0 / GLA_GATE_NORM)
    b = _dot_sel(jnp.where(mask, 1.0, 0.0).astype(BF16), g)
    nt = (((1,), (1,)), ((), ()))
    tn = (((0,), (0,)), ((), ()))
    chunks = range(tr // GLA_CHUNK) if d == 0 else range(tr // GLA_CHUNK - 1, -1, -1)
    for h in range(n_heads):
        bq = b[:, h * 128:(h + 1) * 128]
        q = qk_ref[:, h * 128:(h + 1) * 128].astype(F32)
        k = qk_ref[:, qkw + h * 128:qkw + (h + 1) * 128].astype(F32)
        v_b = v_ref[:, h * 256:(h + 1) * 256].astype(BF16)
        qe = (q * jnp.exp(bq) * (128.0 ** -0.5)).astype(BF16)
        kinv = (k * jnp.exp(-bq)).astype(BF16)
        a = jnp.where(mask, lax.dot_general(qe, kinv, nt, preferred_element_type=F32), 0.0)
        o_intra = _dot(a.astype(BF16), v_b)
        for ci in chunks:
            lo = ci * GLA_CHUNK
            last = lo + GLA_CHUNK - 1 if d == 0 else lo
            b_last = bq[last:last + 1, :]
            kdec = (k[lo:lo + GLA_CHUNK] * jnp.exp(b_last - bq[lo:lo + GLA_CHUNK])).astype(BF16)
            ds = lax.dot_general(kdec, v_b[lo:lo + GLA_CHUNK], tn, preferred_element_type=F32)
            st = st_ref[d, h]
            o_ref[lo:lo + GLA_CHUNK, h * 256:(h + 1) * 256] = (
                o_intra[lo:lo + GLA_CHUNK] + _dot(qe[lo:lo + GLA_CHUNK], st.astype(BF16))).astype(o_ref.dtype)
            dec = jnp.transpose(jnp.broadcast_to(jnp.exp(b_last), (128, 128)))
            st_ref[d, h] = jnp.concatenate([dec, dec], axis=1) * st + ds


def _gla_kernel(qkf_ref, vf_ref, smf_ref, qkb_ref, vb_ref, smb_ref, wup_ref, gb_ref, s0_ref,
                of_ref, ob_ref, sfin_ref, st_ref, *, n_heads):
    @pl.when(pl.program_id(0) == 0)
    def _():
        st_ref[...] = s0_ref[...]

    _gla_direction(0, qkf_ref, vf_ref, smf_ref, wup_ref, gb_ref, of_ref, st_ref, n_heads)
    _gla_direction(1, qkb_ref, vb_ref, smb_ref, wup_ref, gb_ref, ob_ref, st_ref, n_heads)

    @pl.when(pl.program_id(0) == pl.num_programs(0) - 1)
    def _():
        sfin_ref[...] = st_ref[...]


def _gla_call(cfg, n_steps, arrays, spec_fn, out_rows_shape, out_spec_fn, wup, gbias, s0, name):
    nh = cfg.gla_h
    fwd = lambda s: s
    bwd = lambda s: n_steps - 1 - s
    full = lambda shape: pl.BlockSpec(shape, lambda s: (0,) * len(shape))
    z_view, zs_view = arrays
    st_shape = (2, nh, 128, 256)
    out = jax.ShapeDtypeStruct(out_rows_shape, BF16)
    return pl.pallas_call(
        functools.partial(_gla_kernel, n_heads=nh),
        out_shape=(out, out, jax.ShapeDtypeStruct(st_shape, F32)),
        grid=(n_steps,),
        in_specs=spec_fn(fwd) + spec_fn(bwd) + [full(wup.shape), full(gbias.shape), full(st_shape)],
        out_specs=(out_spec_fn(fwd), out_spec_fn(bwd), full(st_shape)),
        scratch_shapes=[pltpu.VMEM(st_shape, F32)],
        compiler_params=_cparams(("arbitrary",)),
        name=name,
    )(z_view, z_view, zs_view, z_view, z_view, zs_view, wup, gbias, s0)


def gla(cfg, z_cm, z_ctx, zs_cm, zs_ctx, wup, gbias):
    nh = cfg.gla_h
    qk2, vw = 2 * nh * 128, nh * 256
    w, rows, _ = z_cm.shape
    assert rows == GLA_TILE and cfg.ctx % GLA_TILE == 0 and qk2 == vw
    s0 = jnp.zeros((2, nh, 128, 256), F32)

    ctx_specs = lambda order: [
        pl.BlockSpec((GLA_TILE, qk2), lambda s: (order(s), 0)),
        pl.BlockSpec((GLA_TILE, vw), lambda s: (order(s), 1)),
        pl.BlockSpec((GLA_TILE, LANES), lambda s: (order(s), 0))]
    ctx_out = lambda order: pl.BlockSpec((GLA_TILE, vw), lambda s: (order(s), 0))
    ocf, ocb, s_ctx = _gla_call(cfg, cfg.ctx // GLA_TILE, (z_ctx, zs_ctx), ctx_specs,
                                (cfg.ctx, vw), ctx_out, wup, gbias, s0, "gla_ctx")

    lat_specs = lambda order: [
        pl.BlockSpec((None, GLA_TILE, qk2), lambda s: (order(s), 0, 0)),
        pl.BlockSpec((None, GLA_TILE, vw), lambda s: (order(s), 0, 1)),
        pl.BlockSpec((None, GLA_TILE, LANES), lambda s: (order(s), 0, 0))]
    lat_out = lambda order: pl.BlockSpec((None, GLA_TILE, vw), lambda s: (order(s), 0, 0))
    olf, olb, _ = _gla_call(cfg, w, (z_cm, zs_cm), lat_specs, (w, rows, vw), lat_out,
                            wup, gbias, s_ctx, "gla_lat")
    return (olf, olb), (ocf, ocb)


def column_major_perm(cfg, n_rows=16):
    w = cfg.grid_w
    p = np.zeros((n_rows * w, n_rows * w), np.float32)
    r, c = np.meshgrid(np.arange(n_rows), np.arange(w), indexing="ij")
    p[(c * n_rows + r).ravel(), (r * w + c).ravel()] = 1.0
    return jnp.asarray(p, BF16)


def _mm_nt_cm_kernel(a_ref, w_ref, p_ref, o_ref):
    nt = (((1,), (1,)), ((), ()))
    z = lax.dot_general(a_ref[...], w_ref[...], nt, preferred_element_type=F32).astype(BF16)
    o_ref[...] = _dot(p_ref[...], z).astype(o_ref.dtype).reshape(o_ref.shape)


def matmul_nt_cm(cfg, a, w_t, layer, perm, *, tn, name):
    k = a.shape[1]
    n = w_t.shape[1]
    tm = perm.shape[0]
    gw = cfg.grid_w
    rows = cfg.seq // gw
    return pl.pallas_call(
        _mm_nt_cm_kernel,
        out_shape=jax.ShapeDtypeStruct((gw, rows, n), BF16),
        grid=(n // tn, cfg.seq // tm),
        in_specs=[pl.BlockSpec((tm, k), lambda j, i: (i, 0)),
                  pl.BlockSpec((None, tn, k), lambda j, i: (layer, j, 0)),
                  pl.BlockSpec((tm, tm), lambda j, i: (0, 0))],
        out_specs=pl.BlockSpec((gw, tm // gw, tn), lambda j, i: (0, i, j)),
        compiler_params=_cparams(("parallel", "parallel")),
        name=name,
    )(a, w_t, perm)


def _head_norm_cm_kernel(of_ref, ob_ref, gate_ref, w_ref, p_ref, o_ref, *, n_heads, dv):
    tm = o_ref.shape[0]
    parts = []
    for h in range(n_heads):
        sl = slice(h * dv, (h + 1) * dv)
        o = (of_ref[:, :, sl].astype(F32) + ob_ref[:, :, sl].astype(F32)).reshape(tm, dv)
        o = o * lax.rsqrt(jnp.mean(o * o, axis=-1, keepdims=True) + EPS) * w_ref[...]
        parts.append((o * _silu(gate_ref[:, :, sl].astype(F32).reshape(tm, dv))).astype(BF16))
    nt = (((0,), (0,)), ((), ()))
    y = jnp.concatenate(parts, axis=1)
    o_ref[...] = lax.dot_general(p_ref[...], y, nt, preferred_element_type=F32).astype(o_ref.dtype)


def head_norm_cm(cfg, o_f, o_b, z_cm, gate_col_block, norm_w, n_heads, dv, perm):
    gw, rows, vw = o_f.shape
    tm = perm.shape[0]
    nr = tm // gw
    spec = pl.BlockSpec((gw, nr, vw), lambda i: (0, i, 0))
    return pl.pallas_call(
        functools.partial(_head_norm_cm_kernel, n_heads=n_heads, dv=dv),
        out_shape=jax.ShapeDtypeStruct((gw * rows, vw), BF16),
        grid=(rows // nr,),
        in_specs=[spec, spec, pl.BlockSpec((gw, nr, vw), lambda i: (0, i, gate_col_block)),
                  pl.BlockSpec((1, dv), lambda i: (0, 0)),
                  pl.BlockSpec((tm, tm), lambda i: (0, 0))],
        out_specs=pl.BlockSpec((tm, vw), lambda i: (i, 0)),
        compiler_params=_cparams(("parallel",)),
        name="head_norm_cm",
    )(o_f, o_b, z_cm, norm_w.reshape(1, dv), perm)


FFT_G = 8


def _hy_prep_kernel(z_ref, zp_ref, zn_ref, w_ref, b_ref, v_ref, x1_ref, x2_ref, *, n_lat_tiles, hw):
    y = _conv3(z_ref, zp_ref, zn_ref, w_ref, n_lat_tiles) + b_ref[...]
    for p, o_ref in enumerate((v_ref, x1_ref, x2_ref)):
        o_ref[...] = y[:, p * hw:(p + 1) * hw]


def hyena_prep(cfg, z_hy, conv_w, conv_b):
    tr, hw = cfg.row_tile, cfg.hy_w
    t = z_hy.shape[0]
    main, prev, nxt = _halo_specs(tr, 3 * hw, 0, t)
    out = jax.ShapeDtypeStruct((t, hw), F32)
    ospec = pl.BlockSpec((tr, hw), lambda i: (i, 0))
    return pl.pallas_call(
        functools.partial(_hy_prep_kernel, n_lat_tiles=cfg.seq // tr, hw=hw),
        out_shape=(out, out, out),
        grid=(t // tr,),
        in_specs=[main, prev, nxt,
                  pl.BlockSpec((SHORT_CONV, 3 * hw), lambda i: (0, 0)),
                  pl.BlockSpec((1, 3 * hw), lambda i: (0, 0))],
        out_specs=(ospec, ospec, ospec),
        compiler_params=_cparams(("parallel",)),
        name="hyena_prep",
    )(z_hy, z_hy, z_hy, conv_w, conv_b.reshape(1, 3 * hw))


HY_HALF = LANES // 2


def _hy_taps_kernel(f_ref, w1_ref, b1_ref, w2_ref, b2_ref, fr_ref, w3a0_ref, w3b0_ref, w3a1_ref,
                    w3b1_ref, rate_ref, o0_ref, o1_ref, *, length, tt):
    th = tt // 2
    feats = f_ref[...]
    hid = jnp.sin(fr_ref[0:1, :] * (_dot_3(feats, w1_ref[...]) + b1_ref[...]))
    hid = jnp.sin(fr_ref[1:2, :] * (_dot_3(hid, w2_ref[...]) + b2_ref[...]))
    row = pl.program_id(0) * tt + lax.broadcasted_iota(jnp.int32, (th, 1), 0)
    for part, lane0 in ((0, 0), (1, HY_HALF)):
        n = row + part * th
        window = jnp.where(n == length, 0.0, jnp.exp(-feats[:, lane0:lane0 + 1] * rate_ref[...]))
        for w3_ref, o_ref in (((w3a0_ref, w3b0_ref)[part], o0_ref), ((w3a1_ref, w3b1_ref)[part], o1_ref)):
            o_ref[part * th:(part + 1) * th, :] = (_dot_3(hid, w3_ref[...]) * window).astype(o_ref.dtype)


def hyena_taps(feats, w1p, b1p, w2p, b2p, freqp, w3a, w3b, rates, length, hw):
    tt = min(512, length)
    n_half = length // tt
    out = jax.ShapeDtypeStruct((2 * length, hw), F32)
    ospec = pl.BlockSpec((tt, hw), lambda j: (j, 0))
    sq = pl.BlockSpec((LANES, LANES), lambda j: (0, 0))
    row = pl.BlockSpec((1, LANES), lambda j: (0, 0))
    w3spec = lambda order: pl.BlockSpec((LANES, hw), lambda j: (0, 2 * order + j // n_half))
    return pl.pallas_call(
        functools.partial(_hy_taps_kernel, length=length, tt=tt),
        out_shape=(out, out),
        grid=(2 * n_half,),
        in_specs=[pl.BlockSpec((tt // 2, LANES), lambda j: (j, 0)), sq, row, sq, row,
                  pl.BlockSpec((2, LANES), lambda j: (0, 0)),
                  w3spec(0), w3spec(0), w3spec(1), w3spec(1),
                  pl.BlockSpec((1, hw), lambda j: (0, 0))],
        out_specs=(ospec, ospec),
        compiler_params=_cparams(("parallel",)),
        name="hyena_taps",
    )(feats, w1p, b1p, w2p, b2p, freqp, w3a, w3b, w3a, w3b, rates)


FFT_BT = 8


def _fft_a_kernel(x_ref, l_ref, yr_ref, yi_ref):
    a, bt, c = x_ref.shape
    ph = yr_ref.shape[0]
    y = _dot(l_ref[...], x_ref[...].reshape(a * bt, c).astype(BF16))
    yr_ref[...] = y[0:ph * bt].reshape(ph, bt, c)
    yi_ref[...] = y[ph * bt:2 * ph * bt].reshape(ph, bt, c)


def fft_a(x, lhs, n_pages, ph):
    c = x.shape[1]
    x3 = x.reshape(x.shape[0] // LANES, LANES, c)
    out = jax.ShapeDtypeStruct((ph, LANES, c), F32)
    ospec = pl.BlockSpec((ph, FFT_BT, c), lambda j: (0, j, 0))
    return pl.pallas_call(
        _fft_a_kernel,
        out_shape=(out, out),
        grid=(LANES // FFT_BT,),
        in_specs=[pl.BlockSpec((n_pages, FFT_BT, c), lambda j: (0, j, 0)),
                  pl.BlockSpec(lhs.shape, lambda j: (0, 0))],
        out_specs=(ospec, ospec),
        compiler_params=_cparams(("parallel",)),
        name="fft_a",
    )(x3, lhs)


def _fft_b_kernel(yr_ref, yi_ref, m_ref, zr_ref, zi_ref):
    for g in range(FFT_G):
        s = jnp.concatenate([yr_ref[g], yi_ref[g]], axis=0).astype(BF16)
        z = _dot(m_ref[g], s)
        zr_ref[g] = z[0:LANES].astype(zr_ref.dtype)
        zi_ref[g] = z[LANES:2 * LANES].astype(zi_ref.dtype)


def fft_b(yr, yi, m2):
    ph, _, c = yr.shape
    out = jax.ShapeDtypeStruct((ph, LANES, c), BF16)
    spec = pl.BlockSpec((FFT_G, LANES, c), lambda j: (j, 0, 0))
    return pl.pallas_call(
        _fft_b_kernel,
        out_shape=(out, out),
        grid=(ph // FFT_G,),
        in_specs=[spec, spec, pl.BlockSpec((FFT_G, 2 * LANES, 2 * LANES), lambda j: (j, 0, 0))],
        out_specs=(spec, spec),
        compiler_params=_cparams(("parallel",)),
        name="fft_b",
    )(yr, yi, m2)


FFT_GM = 4


def _spec_mul_kernel(yr_ref, yi_ref, hr_ref, hi_ref, m2_ref, ma_ref, vr_ref, vi_ref):
    for g in range(FFT_GM):
        z = _dot(m2_ref[g], jnp.concatenate([yr_ref[g], yi_ref[g]], axis=0).astype(BF16))
        zr, zi = z[0:LANES], z[LANES:2 * LANES]
        hr, hi = hr_ref[g].astype(F32), hi_ref[g].astype(F32)
        s = jnp.concatenate([zr * hr - zi * hi, zr * hi + zi * hr], axis=0).astype(BF16)
        v = _dot(ma_ref[g], s)
        vr_ref[g] = v[0:LANES]
        vi_ref[g] = v[LANES:2 * LANES]


def spectrum_multiply(yr, yi, hr, hi, m2, ma):
    ph, _, c = yr.shape
    out = jax.ShapeDtypeStruct((ph, LANES, c), F32)
    spec = pl.BlockSpec((FFT_GM, LANES, c), lambda j: (j, 0, 0))
    mspec = pl.BlockSpec((FFT_GM, 2 * LANES, 2 * LANES), lambda j: (j, 0, 0))
    return pl.pallas_call(
        _spec_mul_kernel,
        out_shape=(out, out),
        grid=(ph // FFT_GM,),
        in_specs=[spec, spec, spec, spec, mspec, mspec],
        out_specs=(spec, spec),
        compiler_params=_cparams(("parallel",)),
        name="spectrum_multiply",
    )(yr, yi, hr, hi, m2, ma)


def _ifft_b_kernel(vr_ref, vi_ref, l_ref, x_ref, u_ref, sk_ref, o_ref):
    ph, bt, c = vr_ref.shape
    s = jnp.concatenate([vr_ref[...].reshape(ph * bt, c), vi_ref[...].reshape(ph * bt, c)],
                        axis=0).astype(BF16)
    y = _dot(l_ref[...], s).reshape(o_ref.shape)
    o_ref[...] = x_ref[...] * (y + sk_ref[...] * u_ref[...])


def ifft_b_gate(vr, vi, lhs, gate_x, u, skip_row):
    ph, _, c = vr.shape
    a_out = lhs.shape[0] // FFT_BT
    x3 = gate_x.reshape(gate_x.shape[0] // LANES, LANES, c)
    u3 = u.reshape(u.shape[0] // LANES, LANES, c)
    vspec = pl.BlockSpec((ph, FFT_BT, c), lambda j: (0, j, 0))
    tspec = pl.BlockSpec((a_out, FFT_BT, c), lambda j: (0, j, 0))
    out = pl.pallas_call(
        _ifft_b_kernel,
        out_shape=jax.ShapeDtypeStruct((a_out, LANES, c), F32),
        grid=(LANES // FFT_BT,),
        in_specs=[vspec, vspec, pl.BlockSpec(lhs.shape, lambda j: (0, 0)), tspec, tspec,
                  pl.BlockSpec((1, 1, c), lambda j: (0, 0, 0))],
        out_specs=tspec,
        compiler_params=_cparams(("parallel",)),
        name="ifft_b",
    )(vr, vi, lhs, x3, u3, skip_row.reshape(1, 1, c))
    return out.reshape(a_out * LANES, c)


def _hy_ctx_kernel(v_ref, x1_ref, x2_ref, t0_ref, t1_ref, sk_ref, f_ref, g_ref, o_ref, *, n):
    def conv(u, taps_ref):
        us = _dot(f_ref[:, 0:n], u.astype(BF16))
        hs = _dot(f_ref[...], taps_ref[...].astype(BF16))
        ur, ui, hr, hi = us[0:2 * n], us[2 * n:4 * n], hs[0:2 * n], hs[2 * n:4 * n]
        prod = jnp.concatenate([ur * hr - ui * hi, ur * hi + ui * hr], axis=0).astype(BF16)
        return _dot(g_ref[...], prod)

    v = v_ref[...]
    y = x1_ref[...] * (conv(v, t0_ref) + sk_ref[0:1, :] * v)
    o_ref[...] = x2_ref[...] * (conv(y, t1_ref) + sk_ref[1:2, :] * y)


def hyena_ctx(cfg, vxx, taps0, taps1, skip, fmat, gmat):
    n, hw = cfg.ctx, cfg.hy_w
    cb = 256
    rb = cfg.seq // n
    part = pl.BlockSpec((n, cb), lambda j: (rb, j))
    tspec = pl.BlockSpec((2 * n, cb), lambda j: (0, j))
    return pl.pallas_call(
        functools.partial(_hy_ctx_kernel, n=n),
        out_shape=jax.ShapeDtypeStruct((n, hw), F32),
        grid=(hw // cb,),
        in_specs=[part, part, part, tspec, tspec,
                  pl.BlockSpec((2, cb), lambda j: (0, j)),
                  pl.BlockSpec((4 * n, 2 * n), lambda j: (0, 0)),
                  pl.BlockSpec((n, 4 * n), lambda j: (0, 0))],
        out_specs=pl.BlockSpec((n, cb), lambda j: (0, j)),
        compiler_params=_cparams(("parallel",)),
        name="hyena_ctx",
    )(vxx[0], vxx[1], vxx[2], taps0, taps1, skip, fmat, gmat)


class HyenaConsts(NamedTuple):
    feats: jax.Array
    rates: jax.Array
    ph: int
    la_data: jax.Array
    la_taps: jax.Array
    m2: jax.Array
    ma: jax.Array
    lb: jax.Array
    feats_ctx: jax.Array
    f_ctx: jax.Array
    g_ctx: jax.Array


def _features(length):
    n = np.arange(2 * length, dtype=np.float64)
    pos = np.where(n < length, n, 2 * length - n)
    bands = (HY_EMB - 1) // 2
    f = np.linspace(1e-4, bands - 1, bands)
    omega = (2.0 * math.pi / length) * pos
    feats = np.zeros((2 * length, HY_HALF), np.float64)
    feats[:, 0] = pos / (length - 1)
    feats[:, 1:1 + bands] = np.cos(omega[:, None] * f[None, :])
    feats[:, 1 + bands:1 + 2 * bands] = -np.sin(omega[:, None] * f[None, :])
    tt = min(512, length)
    tiles = feats.reshape(2 * length // tt, 2, tt // 2, HY_HALF)
    packed = np.concatenate([tiles[:, 0], tiles[:, 1]], axis=-1).reshape(length, LANES)
    return jnp.asarray(packed, F32)


def hyena_consts(cfg):
    length, n, hw = cfg.seq, cfg.ctx, cfg.hy_w
    big_n = 2 * length
    p = big_n // LANES
    rates = np.abs(np.linspace(math.log(HY_TARGET) / HY_FAST_DECAY, math.log(HY_TARGET) / HY_SLOW_DECAY, hw))
    ph = p // 2 + 8
    k1 = np.arange(ph)
    kept = (k1 <= p // 2).astype(np.float64)
    a = np.arange(p)
    ang_a = 2.0 * math.pi * ((k1[:, None] * a[None, :]) % p) / p
    dft_a = np.concatenate([np.cos(ang_a), -np.sin(ang_a)], axis=0) * np.tile(kept, 2)[:, None]
    eye = np.eye(FFT_BT)
    la_taps = np.kron(dft_a, eye)
    la_data = np.kron(dft_a[:, :p // 2], eye)
    weight = kept * np.where((k1 == 0) | (k1 == p // 2), 1.0, 2.0) / big_n
    inv_a = (np.concatenate([np.cos(ang_a), -np.sin(ang_a)], axis=0) * np.tile(weight, 2)[:, None]).T
    lb = np.kron(inv_a[:p // 2], eye)
    b = np.arange(LANES)
    ang_b = 2.0 * math.pi * (((b[:, None] * b[None, :]) % LANES) / LANES)[None] \
        + 2.0 * math.pi * (k1[:, None, None] * b[None, None, :]) / big_n
    fr, fi = np.cos(ang_b), -np.sin(ang_b)
    m2 = np.concatenate([np.concatenate([fr, -fi], axis=2), np.concatenate([fi, fr], axis=2)], axis=1)
    gr, gi = fr.transpose(0, 2, 1), -fi.transpose(0, 2, 1)
    ma = np.concatenate([np.concatenate([gr, -gi], axis=2), np.concatenate([gi, gr], axis=2)], axis=1)
    kk = np.arange(2 * n)
    ac = 2.0 * math.pi * ((kk[:, None] * kk[None, :]) % (2 * n)) / (2 * n)
    f_ctx = np.concatenate([np.cos(ac), -np.sin(ac)], axis=0)
    g_ctx = np.concatenate([np.cos(ac), -np.sin(ac)], axis=1)[:n] / (2 * n)
    bf = lambda x: jnp.asarray(x, BF16)
    return HyenaConsts(_features(length), jnp.asarray(rates[None, :], F32), ph, bf(la_data), bf(la_taps),
                       bf(m2), bf(ma), bf(lb), _features(n), bf(f_ctx), bf(g_ctx))


def hyena(cfg, hc, vxx, filt_w, skip, with_ctx=True):
    length, n, hw = cfg.seq, cfg.ctx, cfg.hy_w
    half = length // LANES
    y_lat = vxx[0]
    taps = hyena_taps(hc.feats, *filt_w, hc.rates, length, hw)
    for order in range(2):
        hr, hi = fft_b(*fft_a(taps[order], hc.la_taps, 2 * half, hc.ph), hc.m2)
        yr, yi = fft_a(y_lat, hc.la_data, half, hc.ph)
        vr, vi = spectrum_multiply(yr, yi, hr, hi, hc.m2, hc.ma)
        y_lat = ifft_b_gate(vr, vi, hc.lb, vxx[1 + order], y_lat, skip[order:order + 1])
    if not with_ctx:
        return y_lat, jnp.zeros((n, hw), F32)
    taps_c = hyena_taps(hc.feats_ctx, *filt_w, hc.rates, n, hw)
    y_ctx = hyena_ctx(cfg, vxx, taps_c[0], taps_c[1], skip, hc.f_ctx, hc.g_ctx)
    return y_lat, y_ctx


def expert_expand(n=N_EXPERTS):
    m = np.zeros((LANES, n * LANES), np.float32)
    for e in range(n):
        m[e, e * LANES:(e + 1) * LANES] = 1.0
    return jnp.asarray(m, BF16)


def prep_gla_gate(gate_up, gate_b):
    l, _, r, qk = gate_up.shape
    w = jnp.zeros((l, LANES, 2 * qk), F32)
    for z in range(2):
        w = w.at[:, GLA_LR_LANE0 + z * r:GLA_LR_LANE0 + (z + 1) * r, z * qk:(z + 1) * qk].set(gate_up[:, z])
    return w, gate_b.reshape(l, 1, 2 * qk)


def prep_hyena_filter(w1, b1, w2, b2, w3, freq):
    e, hdim = w1.shape
    assert hdim == HY_HALF
    h = HY_HALF
    w1p = jnp.zeros((LANES, LANES), F32).at[:e, :h].set(w1).at[h:h + e, h:].set(w1)
    w2p = jnp.zeros((LANES, LANES), F32).at[:h, :h].set(w2).at[h:, h:].set(w2)
    zero = jnp.zeros_like(w3)
    w3a = jnp.concatenate([w3, zero], axis=0)
    w3b = jnp.concatenate([zero, w3], axis=0)
    twice = lambda v: jnp.concatenate([v, v], axis=-1).reshape(-1, LANES)
    return w1p, twice(b1), w2p, twice(b2), twice(freq), w3a, w3b


def prep_w2(w2):
    l, e, f, d = w2.shape
    return w2.reshape(l, e * f, d).astype(BF16)


def kernel(x, c, ctx, c_ctx, norm1_g, norm2_g, w_mod, b_mod, w_in, gdn_conv, gdn_a_log, gdn_dt_bias, gdn_norm, gla_gate_up, gla_gate_b, gla_norm, hy_conv_w, hy_conv_b, hy_w1, hy_b1, hy_w2, hy_b2, hy_w3, hy_freq, hy_skip, merge_up, merge_b, w_branch, w_out, w_router, router_bias, moe_w1, moe_w3, moe_w2, final_g):
    cfg = Cfg(d=4096, seq=8192, ctx=256, grid_w=64, gdn_h=8, gla_h=4, hy_w=1024, merge_rank=256,
              d_expert=256, row_tile=256, mm_tm=768)
    return forward(cfg, x, c, ctx, c_ctx, norm1_g, norm2_g, w_mod, b_mod, w_in, gdn_conv, gdn_a_log,
                   gdn_dt_bias, gdn_norm, gla_gate_up, gla_gate_b, gla_norm, hy_conv_w, hy_conv_b, hy_w1,
                   hy_b1, hy_w2, hy_b2, hy_w3, hy_freq, hy_skip, merge_up, merge_b, w_branch, w_out,
                   w_router, router_bias, moe_w1, moe_w3, moe_w2, final_g)


def split_w_in(cfg, w_in):
    gw4 = 4 * cfg.gdn_w
    gdn_small = 4 * cfg.gdn_h
    gla0 = gw4 + gdn_small
    gla_w = 2 * cfg.gla_qk + 2 * cfg.gla_v
    hy0 = gla0 + gla_w + 32
    hy_w = 3 * cfg.hy_w
    w_t = jnp.swapaxes(w_in, 1, 2)
    l, _, k = w_t.shape
    zeros = lambda n: jnp.zeros((l, n, k), w_t.dtype)
    small = jnp.concatenate([w_t[:, hy0 + hy_w:hy0 + hy_w + cfg.merge_rank],
                             w_t[:, gw4:gla0], zeros(GLA_LR_LANE0 - gdn_small),
                             w_t[:, gla0 + gla_w:hy0], zeros(LANES - GLA_LR_LANE0 - 32)], axis=1)
    return cast_rows(w_t, 0, gw4), cast_rows(w_t, gla0, gla_w), cast_rows(w_t, hy0, hy_w), small


def forward(cfg, x, c, ctx, c_ctx, norm1_g, norm2_g, w_mod, b_mod, w_in, gdn_conv, gdn_a_log, gdn_dt_bias,
            gdn_norm, gla_gate_up, gla_gate_b, gla_norm, hy_conv_w, hy_conv_b, hy_w1, hy_b1, hy_w2, hy_b2,
            hy_w3, hy_freq, hy_skip, merge_up, merge_b, w_branch, w_out, w_router, router_bias, moe_w1,
            moe_w3, moe_w2, final_g):
    d, tr, tm = cfg.d, cfg.row_tile, cfg.mm_tm
    depth = w_in.shape[0]
    nh = cfg.gdn_h
    tn = min(1024, d)
    w_gdn, w_gla, w_hy, w_small = split_w_in(cfg, w_in)
    w1b, w3b = moe_w1.astype(BF16), moe_w3.astype(BF16)
    w2f = prep_w2(moe_w2)
    wb, mu, wo = w_branch.astype(BF16), merge_up.astype(BF16), w_out.astype(BF16)
    mb = merge_b.reshape(depth, 3, 1, d)
    wr_pad = jnp.zeros((d, LANES), F32).at[:, :N_EXPERTS].set(w_router)
    rb_col = router_bias.reshape(N_EXPERTS, 1)
    expand = expert_expand()
    wup, gbias = prep_gla_gate(gla_gate_up, gla_gate_b)
    lane0 = 2 * nh
    alog_rows = jnp.zeros((depth, 1, LANES), F32).at[:, 0, lane0:2 * lane0].set(gdn_a_log.reshape(depth, -1))
    dtb_rows = jnp.zeros((depth, 1, LANES), F32).at[:, 0, lane0:2 * lane0].set(gdn_dt_bias.reshape(depth, -1))
    hc = hyena_consts(cfg)
    perm = column_major_perm(cfg)
    gla_gate_blk = (2 * cfg.gla_qk + cfg.gla_v) // cfg.gla_v
    z_proj = functools.partial(matmul_nt, tm=tm, tn=tn, out_dtype=BF16)

    lat = jnp.concatenate([x[0], ctx[0]], axis=0)
    cvec = jnp.zeros((8, d), F32).at[0].set(c[0]).at[1].set(c_ctx)
    mods = modvec(cvec, w_mod, b_mod)

    for l in range(depth):
        with_ctx = l < depth - 1
        mod = mods[l]
        h = norm1(cfg, lat, norm1_g[l], mod)
        z_gdn = z_proj(h, w_gdn, l, name="w_in_gdn")
        z_gla = matmul_nt_cm(cfg, h, w_gla, l, perm, tn=tn, name="w_in_gla")
        z_gla_ctx = matmul_nt(h, w_gla, l, tm=cfg.ctx, tn=tn, out_dtype=BF16, name="w_in_gla_ctx",
                              row0=cfg.seq, n_rows=cfg.ctx)
        z_hy = z_proj(h, w_hy, l, name="w_in_hy")
        z_small = matmul_nt(h, w_small, l, tm=tm, tn=w_small.shape[1], name="w_in_small")
        qn, kn, vs, bb, gcb, gct = gdn_prep(cfg, z_gdn, z_small, gdn_conv[l], alog_rows[l], dtb_rows[l])
        o_f, o_b = gdn_scan(cfg, *gdn_chunks(cfg, qn, kn, vs, bb, gcb, gct))
        a_all = head_norm(o_f, o_b, z_gdn, 3, gdn_norm[l], nh, 128, tr)
        lr = z_small[:, cfg.merge_rank:cfg.merge_rank + LANES]
        lr_cm = jnp.swapaxes(lr[:cfg.seq].reshape(cfg.seq // cfg.grid_w, cfg.grid_w, LANES), 0, 1)
        (olf, olb), (ocf, ocb) = gla(cfg, z_gla, z_gla_ctx, lr_cm, lr[cfg.seq:], wup[l], gbias[l])
        b_lat = head_norm_cm(cfg, olf, olb, z_gla, gla_gate_blk, gla_norm[l], cfg.gla_h, 256, perm)
        b_ctx = head_norm(ocf, ocb, z_gla_ctx, gla_gate_blk, gla_norm[l], cfg.gla_h, 256, tr)
        b_all = jnp.concatenate([b_lat, b_ctx], axis=0)
        vxx = hyena_prep(cfg, z_hy, hy_conv_w[l], hy_conv_b[l])
        filt_w = prep_hyena_filter(hy_w1[l], hy_b1[l], hy_w2[l], hy_b2[l], hy_w3[l], hy_freq[l])
        c_lat, c_ctx_out = hyena(cfg, hc, vxx, filt_w, hy_skip[l], with_ctx)
        c_all = jnp.concatenate([c_lat, c_ctx_out], axis=0).astype(BF16)
        s = merge_branches(cfg, (a_all, b_all, c_all), z_small, wb, mu, mb, l, tn=tn)
        lat = matmul_resid(cfg, s, wo, l, lat, mod, 2, tn=tn, name="w_out")
        h2, gate_rep = norm2_route(cfg, lat, norm2_g[l], mod, wr_pad, rb_col, expand)
        act = matmul_moe_act(cfg, h2, w1b, w3b, l, gate_rep)
        lat = matmul_resid(cfg, act, w2f, l, lat, mod, 5, tn=tn, name="moe_down")
    return final_norm(lat, final_g, cfg.seq, tr)[None]
```

```python
import functools
import math
from typing import NamedTuple

import numpy as np
import jax
import jax.numpy as jnp
from jax import lax
from jax.experimental import pallas as pl
from jax.experimental.pallas import tpu as pltpu

F32 = jnp.float32
BF16 = jnp.bfloat16

EPS = 1e-6
LANES = 128
V7X_VMEM_BYTES = 64 * 1024 * 1024
VMEM_LIMIT = (V7X_VMEM_BYTES * 13) // 16
SHORT_CONV = 3
GLA_CHUNK = 64
GLA_GATE_NORM = 16.0
N_EXPERTS = 16
N_GROUPS = 4
HY_EMB = 33
HY_FAST_DECAY = 0.3
HY_SLOW_DECAY = 1.5
HY_TARGET = 1e-2


class Cfg(NamedTuple):
    d: int
    seq: int
    ctx: int
    grid_w: int
    gdn_h: int
    gla_h: int
    hy_w: int
    merge_rank: int
    d_expert: int
    row_tile: int
    mm_tm: int

    @property
    def t(self):
        return self.seq + self.ctx

    @property
    def gdn_w(self):
        return self.gdn_h * 128

    @property
    def gla_qk(self):
        return self.gla_h * 128

    @property
    def gla_v(self):
        return self.gla_h * 256


def _cparams(sem):
    return pltpu.CompilerParams(dimension_semantics=sem, vmem_limit_bytes=VMEM_LIMIT)


def _split3(x):
    hi = x.astype(BF16)
    r1 = x - hi.astype(F32)
    mid = r1.astype(BF16)
    lo = (r1 - mid.astype(F32)).astype(BF16)
    return hi, mid, lo


def _dot(a, b):
    return jnp.dot(a, b, preferred_element_type=F32)


def _dot_sel(sel_bf16, x):
    hi, mid, lo = _split3(x)
    return _dot(sel_bf16, hi) + _dot(sel_bf16, mid) + _dot(sel_bf16, lo)


def _dot_x_sel(x, sel_bf16):
    hi, mid, lo = _split3(x)
    return _dot(hi, sel_bf16) + _dot(mid, sel_bf16) + _dot(lo, sel_bf16)


def _dot_hi(a, b):
    a1, a2, a3 = _split3(a)
    b1, b2, b3 = _split3(b)
    return (_dot(a1, b1) + (_dot(a1, b2) + _dot(a2, b1))
            + (_dot(a2, b2) + _dot(a1, b3) + _dot(a3, b1)))


def _dot_3(a, b):
    a1 = a.astype(BF16)
    a2 = (a - a1.astype(F32)).astype(BF16)
    b1 = b.astype(BF16)
    b2 = (b - b1.astype(F32)).astype(BF16)
    return _dot(a1, b1) + (_dot(a1, b2) + _dot(a2, b1))


def _silu(x):
    return x * jax.nn.sigmoid(x)


def _modvec_kernel(x_ref, w_ref, b_ref, o_ref):
    o_ref[...] = _dot(_silu(x_ref[...]), w_ref[...]) + b_ref[...]


def modvec(cvec, w_mod, b_mod, tn=512):
    depth, d, n = w_mod.shape
    return pl.pallas_call(
        _modvec_kernel,
        out_shape=jax.ShapeDtypeStruct((depth, 8, n), F32),
        grid=(depth, n // tn),
        in_specs=[pl.BlockSpec((8, d), lambda l, j: (0, 0)),
                  pl.BlockSpec((None, d, tn), lambda l, j: (l, 0, j)),
                  pl.BlockSpec((None, 1, tn), lambda l, j: (l, 0, j))],
        out_specs=pl.BlockSpec((None, 8, tn), lambda l, j: (l, 0, j)),
        compiler_params=_cparams(("parallel", "parallel")),
        name="modvec",
    )(cvec, w_mod, b_mod.reshape(depth, 1, n))


def _mod_row(mod_ref, is_ctx, idx, d):
    return mod_ref[pl.ds(is_ctx, 1), idx * d:(idx + 1) * d]


def _norm_mod(x, gain, shift, scale):
    y = x * lax.rsqrt(jnp.mean(x * x, axis=-1, keepdims=True) + EPS)
    return (y * gain) * (1.0 + scale) + shift


def _norm1_kernel(x_ref, g_ref, mod_ref, o_ref, *, d, n_lat_tiles):
    is_ctx = (pl.program_id(0) >= n_lat_tiles).astype(jnp.int32)
    h = _norm_mod(x_ref[...], g_ref[...], _mod_row(mod_ref, is_ctx, 0, d),
                  _mod_row(mod_ref, is_ctx, 1, d))
    o_ref[...] = h.astype(BF16)


def norm1(cfg, x, gain, mod):
    tr = cfg.row_tile
    t, d = x.shape
    return pl.pallas_call(
        functools.partial(_norm1_kernel, d=d, n_lat_tiles=cfg.seq // tr),
        out_shape=jax.ShapeDtypeStruct((t, d), BF16),
        grid=(t // tr,),
        in_specs=[pl.BlockSpec((tr, d), lambda i: (i, 0)),
                  pl.BlockSpec((1, d), lambda i: (0, 0)),
                  pl.BlockSpec((8, 6 * d), lambda i: (0, 0))],
        out_specs=pl.BlockSpec((tr, d), lambda i: (i, 0)),
        compiler_params=_cparams(("parallel",)),
        name="norm1",
    )(x, gain.reshape(1, d), mod)


def _route(sel_t, sc_t):
    per = N_EXPERTS // N_GROUPS
    grp_score = []
    for g in range(N_GROUPS):
        v = sel_t[g * per:(g + 1) * per]
        best = None
        for a in range(per):
            for b in range(a + 1, per):
                s = v[a] + v[b]
                best = s if best is None else jnp.maximum(best, s)
        grp_score.append(best)
    best_s, best_g = grp_score[0], jnp.zeros_like(grp_score[0])
    for g in range(1, N_GROUPS):
        better = grp_score[g] > best_s
        best_s = jnp.where(better, grp_score[g], best_s)
        best_g = jnp.where(better, float(g), best_g)
    picked = []
    for e in range(N_EXPERTS):
        g, i = divmod(e, per)
        rank = jnp.zeros_like(best_s)
        for j in range(per):
            if j == i:
                continue
            o = sel_t[g * per + j]
            ahead = (o >= sel_t[e]) if j < i else (o > sel_t[e])
            rank = rank + ahead.astype(F32)
        picked.append(jnp.where((best_g == float(g)) & (rank < 2.0), sc_t[e], 0.0))
    den = picked[0]
    for e in range(1, N_EXPERTS):
        den = den + picked[e]
    inv = 1.0 / den
    return [p * inv for p in picked], best_g


def _norm2_kernel(x_ref, g_ref, mod_ref, wr_ref, rb_ref, ex_ref, o_ref, gate_ref, *, d,
                  n_lat_tiles):
    is_ctx = (pl.program_id(0) >= n_lat_tiles).astype(jnp.int32)
    h = _norm_mod(x_ref[...], g_ref[...], _mod_row(mod_ref, is_ctx, 3, d),
                  _mod_row(mod_ref, is_ctx, 4, d))
    o_ref[...] = h.astype(BF16)
    logits = _dot_3(h, wr_ref[...])
    lt = jnp.transpose(logits)
    sc = jax.nn.sigmoid(lt[0:N_EXPERTS, :])
    sel = sc + rb_ref[...]
    gate_rows, _ = _route([sel[e:e + 1, :] for e in range(N_EXPERTS)],
                          [sc[e:e + 1, :] for e in range(N_EXPERTS)])
    rows = lax.broadcasted_iota(jnp.int32, lt.shape, 0)
    gt = jnp.zeros(lt.shape, F32)
    for e in range(N_EXPERTS):
        gt = jnp.where(rows == e, gate_rows[e], gt)
    gate = jnp.transpose(gt)
    gate_ref[...] = _dot_x_sel(gate, ex_ref[...])


def norm2_route(cfg, x, gain, mod, w_router_pad, rbias_col, expand):
    tr = cfg.row_tile
    t, d = x.shape
    return pl.pallas_call(
        functools.partial(_norm2_kernel, d=d, n_lat_tiles=cfg.seq // tr),
        out_shape=(jax.ShapeDtypeStruct((t, d), BF16),
                   jax.ShapeDtypeStruct((t, N_EXPERTS * LANES), F32)),
        grid=(t // tr,),
        in_specs=[pl.BlockSpec((tr, d), lambda i: (i, 0)),
                  pl.BlockSpec((1, d), lambda i: (0, 0)),
                  pl.BlockSpec((8, 6 * d), lambda i: (0, 0)),
                  pl.BlockSpec((d, LANES), lambda i: (0, 0)),
                  pl.BlockSpec((N_EXPERTS, 1), lambda i: (0, 0)),
                  pl.BlockSpec((LANES, N_EXPERTS * LANES), lambda i: (0, 0))],
        out_specs=(pl.BlockSpec((tr, d), lambda i: (i, 0)),
                   pl.BlockSpec((tr, N_EXPERTS * LANES), lambda i: (i, 0))),
        compiler_params=_cparams(("parallel",)),
        name="norm2_route",
    )(x, gain.reshape(1, d), mod, w_router_pad, rbias_col, expand)


def _final_norm_kernel(x_ref, g_ref, o_ref):
    x = x_ref[...]
    o_ref[...] = (x * lax.rsqrt(jnp.mean(x * x, axis=-1, keepdims=True) + EPS)) * g_ref[...]


def final_norm(x, gain, n_rows, tr):
    d = x.shape[1]
    return pl.pallas_call(
        _final_norm_kernel,
        out_shape=jax.ShapeDtypeStruct((n_rows, d), F32),
        grid=(n_rows // tr,),
        in_specs=[pl.BlockSpec((tr, d), lambda i: (i, 0)),
                  pl.BlockSpec((1, d), lambda i: (0, 0))],
        out_specs=pl.BlockSpec((tr, d), lambda i: (i, 0)),
        compiler_params=_cparams(("parallel",)),
        name="final_norm",
    )(x, gain.reshape(1, d))


def _mm_plain_kernel(a_ref, w_ref, o_ref):
    o_ref[...] = _dot(a_ref[...], w_ref[...].astype(BF16)).astype(o_ref.dtype)


def matmul_plain(a, w, layer, *, tm, tn, n_cols=None, col0=0, out_dtype=F32, name="mm"):
    t, k = a.shape
    n_cols = w.shape[2] - col0 if n_cols is None else n_cols
    off = col0 // tn
    return pl.pallas_call(
        _mm_plain_kernel,
        out_shape=jax.ShapeDtypeStruct((t, n_cols), out_dtype),
        grid=(n_cols // tn, t // tm),
        in_specs=[pl.BlockSpec((tm, k), lambda j, i: (i, 0)),
                  pl.BlockSpec((None, k, tn), lambda j, i: (layer, 0, j + off))],
        out_specs=pl.BlockSpec((tm, tn), lambda j, i: (i, j)),
        compiler_params=_cparams(("parallel", "parallel")),
        name=name,
    )(a, w)


def _mm_nt_kernel(a_ref, w_ref, o_ref):
    nt = (((1,), (1,)), ((), ()))
    o_ref[...] = lax.dot_general(a_ref[...], w_ref[...].astype(BF16), nt,
                                 preferred_element_type=F32).astype(o_ref.dtype)


def matmul_nt(a, w_t, layer, *, tm, tn, out_dtype=F32, name="mm_nt", row0=0, n_rows=None):
    k = a.shape[1]
    t = a.shape[0] if n_rows is None else n_rows
    n = w_t.shape[1]
    rb0 = row0 // tm
    return pl.pallas_call(
        _mm_nt_kernel,
        out_shape=jax.ShapeDtypeStruct((t, n), out_dtype),
        grid=(n // tn, t // tm),
        in_specs=[pl.BlockSpec((tm, k), lambda j, i: (i + rb0, 0)),
                  pl.BlockSpec((None, tn, k), lambda j, i: (layer, j, 0))],
        out_specs=pl.BlockSpec((tm, tn), lambda j, i: (i, j)),
        compiler_params=_cparams(("parallel", "parallel")),
        name=name,
    )(a, w_t)


def _cast_rows_kernel(w_ref, o_ref):
    o_ref[...] = w_ref[...].astype(o_ref.dtype)


def cast_rows(w_t, row0, n_rows):
    l, _, k = w_t.shape
    tr = math.gcd(row0, n_rows) if row0 else n_rows
    while tr > 512 and tr % 2 == 0:
        tr //= 2
    assert tr % 16 == 0
    off = row0 // tr
    return pl.pallas_call(
        _cast_rows_kernel,
        out_shape=jax.ShapeDtypeStruct((l, n_rows, k), BF16),
        grid=(l, n_rows // tr),
        in_specs=[pl.BlockSpec((None, tr, k), lambda i, j: (i, j + off, 0))],
        out_specs=pl.BlockSpec((None, tr, k), lambda i, j: (i, j, 0)),
        compiler_params=_cparams(("parallel", "parallel")),
        name="cast_rows",
    )(w_t)


def _mm_resid_kernel(a_ref, w_ref, r_ref, mod_ref, o_ref, *, tm, n_lat):
    row = pl.program_id(1) * tm + lax.broadcasted_iota(jnp.int32, (tm, 1), 0)
    gate = jnp.where(row < n_lat, mod_ref[0:1, :], mod_ref[1:2, :])
    o_ref[...] = r_ref[...] + gate * _dot(a_ref[...], w_ref[...])


def matmul_resid(cfg, a, w, layer, resid, mod, idx, *, tn, name):
    t, k = a.shape
    d = w.shape[2]
    tm = cfg.mm_tm
    return pl.pallas_call(
        functools.partial(_mm_resid_kernel, tm=tm, n_lat=cfg.seq),
        out_shape=jax.ShapeDtypeStruct((t, d), F32),
        grid=(d // tn, t // tm),
        in_specs=[pl.BlockSpec((tm, k), lambda j, i: (i, 0)),
                  pl.BlockSpec((None, k, tn), lambda j, i: (layer, 0, j)),
                  pl.BlockSpec((tm, tn), lambda j, i: (i, j)),
                  pl.BlockSpec((8, tn), lambda j, i: (0, idx * (d // tn) + j))],
        out_specs=pl.BlockSpec((tm, tn), lambda j, i: (i, j)),
        input_output_aliases={2: 0},
        compiler_params=_cparams(("parallel", "parallel")),
        name=name,
    )(a, w, resid, mod)


def _mm_moe_act_kernel(a_ref, w1_ref, w3_ref, g_ref, o_ref, *, de, n_e):
    a = a_ref[...]
    for e in range(n_e):
        up = _dot(a, w1_ref[e])
        lin = _dot(a, w3_ref[e])
        g = g_ref[:, e * LANES:(e + 1) * LANES]
        g = jnp.concatenate([g] * (de // LANES), axis=1)
        o_ref[:, e * de:(e + 1) * de] = (_silu(up) * lin * g).astype(o_ref.dtype)


def matmul_moe_act(cfg, h, w1, w3, layer, gate_rep, *, n_e=2):
    t, k = h.shape
    de = cfg.d_expert
    tm = cfg.mm_tm
    wspec = pl.BlockSpec((None, n_e, k, de), lambda j, i: (layer, j, 0, 0))
    return pl.pallas_call(
        functools.partial(_mm_moe_act_kernel, de=de, n_e=n_e),
        out_shape=jax.ShapeDtypeStruct((t, N_EXPERTS * de), BF16),
        grid=(N_EXPERTS // n_e, t // tm),
        in_specs=[pl.BlockSpec((tm, k), lambda j, i: (i, 0)), wspec, wspec,
                  pl.BlockSpec((tm, n_e * LANES), lambda j, i: (i, j))],
        out_specs=pl.BlockSpec((tm, n_e * de), lambda j, i: (i, j)),
        compiler_params=_cparams(("parallel", "parallel")),
        name="moe_up",
    )(h, w1, w3, gate_rep)


def _merge_kernel(a0_ref, a1_ref, a2_ref, zg_ref, wb_ref, mu_ref, mb_ref, o_ref):
    zg = zg_ref[...].astype(BF16)
    acc = None
    for n, a_ref in enumerate((a0_ref, a1_ref, a2_ref)):
        y = _dot(a_ref[...], wb_ref[n])
        gate = jax.nn.sigmoid(_dot(zg, mu_ref[n]) + mb_ref[n])
        acc = gate * y if acc is None else acc + gate * y
    o_ref[...] = acc.astype(o_ref.dtype)


def merge_branches(cfg, outs, z_small, w_branch, merge_up, merge_b, layer, *, tn):
    t, bw = outs[0].shape
    d = w_branch.shape[3]
    r = cfg.merge_rank
    tm = cfg.mm_tm
    a_spec = pl.BlockSpec((tm, bw), lambda j, i: (i, 0))
    return pl.pallas_call(
        _merge_kernel,
        out_shape=jax.ShapeDtypeStruct((t, d), BF16),
        grid=(d // tn, t // tm),
        in_specs=[a_spec, a_spec, a_spec,
                  pl.BlockSpec((tm, r), lambda j, i: (i, 0)),
                  pl.BlockSpec((None, 3, bw, tn), lambda j, i: (layer, 0, 0, j)),
                  pl.BlockSpec((None, 3, r, tn), lambda j, i: (layer, 0, 0, j)),
                  pl.BlockSpec((None, 3, 1, tn), lambda j, i: (layer, 0, 0, j))],
        out_specs=pl.BlockSpec((tm, tn), lambda j, i: (i, j)),
        compiler_params=_cparams(("parallel", "parallel")),
        name="merge",
    )(outs[0], outs[1], outs[2], z_small, w_branch, merge_up, merge_b)


HALO_ROWS = 16


def _halo_specs(tr, width, col_block, n_rows):
    rb = tr // HALO_ROWS
    last = n_rows // HALO_ROWS - 1
    main = pl.BlockSpec((tr, width), lambda i: (i, col_block))
    prev = pl.BlockSpec((HALO_ROWS, width), lambda i: (jnp.maximum(i * rb - 1, 0), col_block))
    nxt = pl.BlockSpec((HALO_ROWS, width), lambda i: (jnp.minimum((i + 1) * rb, last), col_block))
    return main, prev, nxt


def _conv3(z_ref, prev_ref, next_ref, w_ref, n_lat_tiles):
    i = pl.program_id(0)
    x = z_ref[...].astype(F32)
    tr = x.shape[0]
    has_prev = jnp.logical_and(i != 0, i != n_lat_tiles).astype(F32)
    has_next = jnp.logical_and(i != n_lat_tiles - 1, i != n_lat_tiles).astype(F32)
    row = lax.broadcasted_iota(jnp.int32, (tr, 1), 0)
    halo_prev = prev_ref[HALO_ROWS - 1:HALO_ROWS, :].astype(F32) * has_prev
    halo_next = next_ref[0:1, :].astype(F32) * has_next
    x_prev = jnp.where(row == 0, halo_prev, pltpu.roll(x, 1, axis=0))
    x_next = jnp.where(row == tr - 1, halo_next, pltpu.roll(x, tr - 1, axis=0))
    return x_prev * w_ref[0:1, :] + x * w_ref[1:2, :] + x_next * w_ref[2:3, :]


def _tri_masks(n):
    r = lax.broadcasted_iota(jnp.int32, (n, n), 0)
    c = lax.broadcasted_iota(jnp.int32, (n, n), 1)
    return r, c


def _gdn_prep_kernel(z_ref, zp_ref, zn_ref, s_ref, cw_ref, alog_ref, dtb_ref,
                     q_ref, k_ref, v_ref, bb_ref, gcb_ref, gct_ref, *, n_heads, n_lat_tiles):
    tr = z_ref.shape[0]
    hw = n_heads * 128
    y = _silu(_conv3(z_ref, zp_ref, zn_ref, cw_ref, n_lat_tiles))
    for h in range(n_heads):
        q = y[:, h * 128:(h + 1) * 128]
        k = y[:, hw + h * 128:hw + (h + 1) * 128]
        q = q * (lax.rsqrt(jnp.sum(q * q, axis=-1, keepdims=True) + EPS) * (128.0 ** -0.5))
        k = k * lax.rsqrt(jnp.sum(k * k, axis=-1, keepdims=True) + EPS)
        q_ref[h] = q.astype(BF16)
        k_ref[h] = k.astype(BF16)
        v_ref[h] = y[:, 2 * hw + h * 128:2 * hw + (h + 1) * 128].astype(BF16)
    s = s_ref[...]
    nh2 = 2 * n_heads
    beta = jax.nn.sigmoid(s)
    g = -jnp.exp(alog_ref[...]) * jax.nn.softplus(s + dtb_ref[...])
    r, c = _tri_masks(tr)
    incl_lo = (c <= r).astype(BF16)
    incl_up = (c >= r).astype(BF16)
    lane = lax.broadcasted_iota(jnp.int32, (tr, LANES), 1)
    fwd_lane = lane < nh2 + n_heads
    gc = jnp.where(fwd_lane, _dot_sel(incl_lo, g), _dot_sel(incl_up, g))
    for ch in range(nh2):
        bb_ref[ch] = jnp.broadcast_to(beta[:, ch:ch + 1], (tr, LANES))
        gcb_ref[ch] = jnp.broadcast_to(gc[:, nh2 + ch:nh2 + ch + 1], (tr, LANES))
    gct_ref[...] = jnp.transpose(gc)[nh2:2 * nh2, :]


def gdn_prep(cfg, z_big, z_small, conv_w, alog_row, dtb_row):
    tr = cfg.row_tile
    t = z_big.shape[0]
    nh = cfg.gdn_h
    hw = nh * 128
    n_tiles = t // tr
    main, prev, nxt = _halo_specs(tr, 3 * hw, 0, t)
    head_out = jax.ShapeDtypeStruct((nh, t, 128), BF16)
    head_spec = pl.BlockSpec((nh, tr, 128), lambda i: (0, i, 0))
    col_out = jax.ShapeDtypeStruct((2 * nh, t, LANES), F32)
    col_spec = pl.BlockSpec((2 * nh, tr, LANES), lambda i: (0, i, 0))
    return pl.pallas_call(
        functools.partial(_gdn_prep_kernel, n_heads=nh, n_lat_tiles=cfg.seq // tr),
        out_shape=(head_out, head_out, head_out, col_out, col_out,
                   jax.ShapeDtypeStruct((n_tiles, 2 * nh, tr), F32)),
        grid=(n_tiles,),
        in_specs=[main, prev, nxt,
                  pl.BlockSpec((tr, LANES), lambda i: (i, cfg.merge_rank // LANES)),
                  pl.BlockSpec((SHORT_CONV, 3 * hw), lambda i: (0, 0)),
                  pl.BlockSpec((1, LANES), lambda i: (0, 0)),
                  pl.BlockSpec((1, LANES), lambda i: (0, 0))],
        out_specs=(head_spec, head_spec, head_spec, col_spec, col_spec,
                   pl.BlockSpec((None, 2 * nh, tr), lambda i: (i, 0, 0))),
        compiler_params=_cparams(("parallel",)),
        name="gdn_prep",
    )(z_big, z_big, z_big, z_small, conv_w, alog_row, dtb_row)


def _unit_tri_inverses(n_mats, r, c):
    n = n_mats[0].shape[0]
    eye = (r == c).astype(F32)

    def same_block(b):
        sh = int(math.log2(b))
        return (r >> sh) == (c >> sh)

    blk = same_block(8)
    pfs = [jnp.where(blk, -m, 0.0) for m in n_mats]
    ps = [pf.astype(BF16) for pf in pfs]
    p2 = [_dot(p, p).astype(BF16) for p in ps]
    p4 = [_dot(x, x).astype(BF16) for x in p2]
    ts = [eye + pf for pf in pfs]
    ts = [t + _dot(t.astype(BF16), x) for t, x in zip(ts, p2)]
    ts = [t + _dot(t.astype(BF16), x) for t, x in zip(ts, p4)]
    b = 8
    while b < n:
        sel = jnp.logical_and(same_block(2 * b), jnp.logical_not(same_block(b)))
        offs = [jnp.where(sel, m, 0.0).astype(BF16) for m in n_mats]
        tbs = [t.astype(BF16) for t in ts]
        xs = [_dot(tb, off).astype(BF16) for tb, off in zip(tbs, offs)]
        ts = [t - _dot(x, tb) for t, x, tb in zip(ts, xs, tbs)]
        b *= 2
    return ts, eye


GDN_HEADS_PER_STEP = 4


def _gdn_chunk_kernel(q_ref, k_ref, v_ref, bf_ref, bb_ref, gf_ref, gb_ref, gct_ref,
                      uw_ref, aq_ref, qk_ref, gl_ref, *, n_heads):
    hps = q_ref.shape[0]
    h0 = pl.program_id(1) * hps
    tr = q_ref.shape[1]
    r, c = _tri_masks(tr)
    nt = (((1,), (1,)), ((), ()))
    masks = (((c <= r), (c < r)), ((c >= r), (c > r)))
    chains = [(j, d) for j in range(hps) for d in range(2)]
    k_b = [k_ref[j] for j in range(hps)]
    k_f = [x.astype(F32) for x in k_b]
    a_qk = [lax.dot_general(q_ref[j], k_b[j], nt, preferred_element_type=F32) for j in range(hps)]
    beta, gc, gam, kb, n_mats = {}, {}, {}, {}, []
    for j, d in chains:
        beta[j, d] = (bf_ref, bb_ref)[d][j]
        gc[j, d] = (gf_ref, gb_ref)[d][j]
        gc_row = gct_ref[pl.ds(d * n_heads + h0 + j, 1), :]
        gc_col = jnp.concatenate([gc[j, d]] * (tr // LANES), axis=1)
        gam[j, d] = jnp.exp(jnp.where(masks[d][0], gc_col - gc_row, -jnp.inf))
        kb[j, d] = k_f[j] * beta[j, d]
        a_kk = lax.dot_general(kb[j, d].astype(BF16), k_b[j], nt, preferred_element_type=F32)
        n_mats.append(jnp.where(masks[d][1], a_kk * gam[j, d], 0.0))
    t_invs, eye = _unit_tri_inverses(n_mats, r, c)
    for (j, d), t_inv in zip(chains, t_invs):
        e = jnp.exp(gc[j, d])
        rhs = jnp.concatenate([v_ref[j].astype(F32) * beta[j, d], kb[j, d] * e], axis=1)
        sol = rhs + _dot((t_inv - eye).astype(BF16), rhs.astype(BF16))
        g_last = gc[j, d][tr - 1:tr, :] if d == 0 else gc[j, d][0:1, :]
        uw_ref[d, j] = sol.astype(BF16)
        aq_ref[d, j] = (a_qk[j] * gam[j, d]).astype(BF16)
        qk_ref[d, j] = jnp.concatenate([q_ref[j].astype(F32) * e,
                                        k_f[j] * jnp.exp(g_last - gc[j, d])], axis=1).astype(BF16)
        gl_ref[d, j] = jnp.broadcast_to(jnp.exp(g_last), (8, LANES))


def gdn_chunks(cfg, qn, kn, vs, bb, gcb, gct):
    tr = cfg.row_tile
    nh, t, _ = qn.shape
    hps = min(GDN_HEADS_PER_STEP, nh)
    assert nh % hps == 0
    n_tiles = t // tr
    head = pl.BlockSpec((hps, tr, 128), lambda i, h: (h, i, 0))
    head_b = pl.BlockSpec((hps, tr, 128), lambda i, h: (h + nh // hps, i, 0))
    big = jax.ShapeDtypeStruct((2, nh, t, 2 * 128), BF16)
    aq = jax.ShapeDtypeStruct((2, nh, t, tr), BF16)
    return pl.pallas_call(
        functools.partial(_gdn_chunk_kernel, n_heads=nh),
        out_shape=(big, aq, big, jax.ShapeDtypeStruct((2, nh, n_tiles * 8, LANES), F32)),
        grid=(n_tiles, nh // hps),
        in_specs=[head, head, head, head, head_b, head, head_b,
                  pl.BlockSpec((None, 2 * nh, tr), lambda i, h: (i, 0, 0))],
        out_specs=(pl.BlockSpec((2, hps, tr, 256), lambda i, h: (0, h, i, 0)),
                   pl.BlockSpec((2, hps, tr, tr), lambda i, h: (0, h, i, 0)),
                   pl.BlockSpec((2, hps, tr, 256), lambda i, h: (0, h, i, 0)),
                   pl.BlockSpec((2, hps, 8, LANES), lambda i, h: (0, h, i, 0))),
        compiler_params=_cparams(("parallel", "parallel")),
        name="gdn_chunks",
    )(qn, kn, vs, bb, bb, gcb, gcb, gct)


def _gdn_scan_kernel(uwf_ref, aqf_ref, qkf_ref, glf_ref, uwb_ref, aqb_ref, qkb_ref, glb_ref,
                     of_ref, ob_ref, s_ref, *, n_heads):
    @pl.when(pl.program_id(0) == 0)
    def _():
        s_ref[...] = jnp.zeros_like(s_ref)

    tn = (((0,), (0,)), ((), ()))
    for d, (uw_ref, aq_ref, qk_ref, gl_ref, o_ref) in enumerate(
            ((uwf_ref, aqf_ref, qkf_ref, glf_ref, of_ref), (uwb_ref, aqb_ref, qkb_ref, glb_ref, ob_ref))):
        for h in range(n_heads):
            st = s_ref[d, h]
            st_b = st.astype(BF16)
            uw = uw_ref[h]
            qk = qk_ref[h]
            v_new = uw[:, 0:128].astype(F32) - _dot(uw[:, 128:256], st_b)
            v_new_b = v_new.astype(BF16)
            o_ref[:, h * 128:(h + 1) * 128] = (
                _dot(qk[:, 0:128], st_b) + _dot(aq_ref[h], v_new_b)).astype(o_ref.dtype)
            s_ref[d, h] = gl_ref[h][0:1, :] * st + lax.dot_general(
                qk[:, 128:256], v_new_b, tn, preferred_element_type=F32)


def gdn_scan(cfg, uw, aq, qk, gl):
    tr = cfg.row_tile
    _, nh, t, _ = uw.shape
    n_tiles = t // tr
    last = n_tiles - 1

    def fwd(s):
        return jnp.where(s == 0, last, s - 1)

    def bwd(s):
        return jnp.where(s == 0, last, last - s)

    def specs(d, order):
        return [pl.BlockSpec((None, nh, tr, 256), lambda s: (d, 0, order(s), 0)),
                pl.BlockSpec((None, nh, tr, tr), lambda s: (d, 0, order(s), 0)),
                pl.BlockSpec((None, nh, tr, 256), lambda s: (d, 0, order(s), 0)),
                pl.BlockSpec((None, nh, 8, LANES), lambda s: (d, 0, order(s), 0))]

    out = jax.ShapeDtypeStruct((t, nh * 128), BF16)
    return pl.pallas_call(
        functools.partial(_gdn_scan_kernel, n_heads=nh),
        out_shape=(out, out),
        grid=(n_tiles,),
        in_specs=specs(0, fwd) + specs(1, bwd),
        out_specs=(pl.BlockSpec((tr, nh * 128), lambda s: (fwd(s), 0)),
                   pl.BlockSpec((tr, nh * 128), lambda s: (bwd(s), 0))),
        scratch_shapes=[pltpu.VMEM((2, nh, 128, 128), F32)],
        compiler_params=_cparams(("arbitrary",)),
        name="gdn_scan",
    )(uw, aq, qk, gl, uw, aq, qk, gl)


def _head_norm_kernel(of_ref, ob_ref, gate_ref, w_ref, o_ref, *, n_heads, dv):
    for h in range(n_heads):
        sl = slice(h * dv, (h + 1) * dv)
        o = of_ref[:, sl].astype(F32) + ob_ref[:, sl].astype(F32)
        o = o * lax.rsqrt(jnp.mean(o * o, axis=-1, keepdims=True) + EPS) * w_ref[...]
        o_ref[:, sl] = (o * _silu(gate_ref[:, sl].astype(F32))).astype(o_ref.dtype)


def head_norm(o_f, o_b, z_big, gate_col_block, norm_w, n_heads, dv, tr, gate_row0=0):
    t, w = o_f.shape
    spec = pl.BlockSpec((tr, w), lambda i: (i, 0))
    rb0 = gate_row0 // tr
    return pl.pallas_call(
        functools.partial(_head_norm_kernel, n_heads=n_heads, dv=dv),
        out_shape=jax.ShapeDtypeStruct((t, w), BF16),
        grid=(t // tr,),
        in_specs=[spec, spec, pl.BlockSpec((tr, w), lambda i: (i + rb0, gate_col_block)),
                  pl.BlockSpec((1, dv), lambda i: (0, 0))],
        out_specs=spec,
        compiler_params=_cparams(("parallel",)),
        name="head_norm",
    )(o_f, o_b, z_big, norm_w.reshape(1, dv))


GLA_TILE = 128
GLA_LR_LANE0 = 32


def _gla_direction(d, qk_ref, v_ref, sm_ref, wup_ref, gb_ref, o_ref, st_ref, n_heads):
    tr = GLA_TILE
    qkw = n_heads * 128
    r, c = _tri_masks(tr)
    sh = int(math.log2(GLA_CHUNK))
    same = (r >> sh) == (c >> sh)
    mask = jnp.logical_and(same, (c <= r) if d == 0 else (c >= r))
    logit = _dot_3(sm_ref[...], wup_ref[:, d * qkw:(d + 1) * qkw]) + gb_ref[:, d * qkw:(d + 1) * qkw]
    g = jax.nn.log_sigmoid(logit) * (1.0 / GLA_GATE_NORM)
    b = _dot_sel(jnp.where(mask, 1.0, 0.0).astype(BF16), g)
    nt = (((1,), (1,)), ((), ()))
    tn = (((0,), (0,)), ((), ()))
    chunks = range(tr // GLA_CHUNK) if d == 0 else range(tr // GLA_CHUNK - 1, -1, -1)
    for h in range(n_heads):
        bq = b[:, h * 128:(h + 1) * 128]
        q = qk_ref[:, h * 128:(h + 1) * 128].astype(F32)
        k = qk_ref[:, qkw + h * 128:qkw + (h + 1) * 128].astype(F32)
        v_b = v_ref[:, h * 256:(h + 1) * 256].astype(BF16)
        qe = (q * jnp.exp(bq) * (128.0 ** -0.5)).astype(BF16)
        kinv = (k * jnp.exp(-bq)).astype(BF16)
        a = jnp.where(mask, lax.dot_general(qe, kinv, nt, preferred_element_type=F32), 0.0)
        o_intra = _dot(a.astype(BF16), v_b)
        for ci in chunks:
            lo = ci * GLA_CHUNK
            last = lo + GLA_CHUNK - 1 if d == 0 else lo
            b_last = bq[last:last + 1, :]
            kdec = (k[lo:lo + GLA_CHUNK] * jnp.exp(b_last - bq[lo:lo + GLA_CHUNK])).astype(BF16)
            ds = lax.dot_general(kdec, v_b[lo:lo + GLA_CHUNK], tn, preferred_element_type=F32)
            st = st_ref[d, h]
            o_ref[lo:lo + GLA_CHUNK, h * 256:(h + 1) * 256] = (
                o_intra[lo:lo + GLA_CHUNK] + _dot(qe[lo:lo + GLA_CHUNK], st.astype(BF16))).astype(o_ref.dtype)
            dec = jnp.transpose(jnp.broadcast_to(jnp.exp(b_last), (128, 128)))
            st_ref[d, h] = jnp.concatenate([dec, dec], axis=1) * st + ds


def _gla_kernel(qkf_ref, vf_ref, smf_ref, qkb_ref, vb_ref, smb_ref, wup_ref, gb_ref, s0_ref,
                of_ref, ob_ref, sfin_ref, st_ref, *, n_heads):
    @pl.when(pl.program_id(0) == 0)
    def _():
        st_ref[...] = s0_ref[...]

    _gla_direction(0, qkf_ref, vf_ref, smf_ref, wup_ref, gb_ref, of_ref, st_ref, n_heads)
    _gla_direction(1, qkb_ref, vb_ref, smb_ref, wup_ref, gb_ref, ob_ref, st_ref, n_heads)

    @pl.when(pl.program_id(0) == pl.num_programs(0) - 1)
    def _():
        sfin_ref[...] = st_ref[...]


def _gla_call(cfg, n_steps, arrays, spec_fn, out_rows_shape, out_spec_fn, wup, gbias, s0, name):
    nh = cfg.gla_h
    fwd = lambda s: s
    bwd = lambda s: n_steps - 1 - s
    full = lambda shape: pl.BlockSpec(shape, lambda s: (0,) * len(shape))
    z_view, zs_view = arrays
    st_shape = (2, nh, 128, 256)
    out = jax.ShapeDtypeStruct(out_rows_shape, BF16)
    return pl.pallas_call(
        functools.partial(_gla_kernel, n_heads=nh),
        out_shape=(out, out, jax.ShapeDtypeStruct(st_shape, F32)),
        grid=(n_steps,),
        in_specs=spec_fn(fwd) + spec_fn(bwd) + [full(wup.shape), full(gbias.shape), full(st_shape)],
        out_specs=(out_spec_fn(fwd), out_spec_fn(bwd), full(st_shape)),
        scratch_shapes=[pltpu.VMEM(st_shape, F32)],
        compiler_params=_cparams(("arbitrary",)),
        name=name,
    )(z_view, z_view, zs_view, z_view, z_view, zs_view, wup, gbias, s0)


def gla(cfg, z_cm, z_ctx, zs_cm, zs_ctx, wup, gbias):
    nh = cfg.gla_h
    qk2, vw = 2 * nh * 128, nh * 256
    w, rows, _ = z_cm.shape
    assert rows == GLA_TILE and cfg.ctx % GLA_TILE == 0 and qk2 == vw
    s0 = jnp.zeros((2, nh, 128, 256), F32)

    ctx_specs = lambda order: [
        pl.BlockSpec((GLA_TILE, qk2), lambda s: (order(s), 0)),
        pl.BlockSpec((GLA_TILE, vw), lambda s: (order(s), 1)),
        pl.BlockSpec((GLA_TILE, LANES), lambda s: (order(s), 0))]
    ctx_out = lambda order: pl.BlockSpec((GLA_TILE, vw), lambda s: (order(s), 0))
    ocf, ocb, s_ctx = _gla_call(cfg, cfg.ctx // GLA_TILE, (z_ctx, zs_ctx), ctx_specs,
                                (cfg.ctx, vw), ctx_out, wup, gbias, s0, "gla_ctx")

    lat_specs = lambda order: [
        pl.BlockSpec((None, GLA_TILE, qk2), lambda s: (order(s), 0, 0)),
        pl.BlockSpec((None, GLA_TILE, vw), lambda s: (order(s), 0, 1)),
        pl.BlockSpec((None, GLA_TILE, LANES), lambda s: (order(s), 0, 0))]
    lat_out = lambda order: pl.BlockSpec((None, GLA_TILE, vw), lambda s: (order(s), 0, 0))
    olf, olb, _ = _gla_call(cfg, w, (z_cm, zs_cm), lat_specs, (w, rows, vw), lat_out,
                            wup, gbias, s_ctx, "gla_lat")
    return (olf, olb), (ocf, ocb)


def column_major_perm(cfg, n_rows=16):
    w = cfg.grid_w
    p = np.zeros((n_rows * w, n_rows * w), np.float32)
    r, c = np.meshgrid(np.arange(n_rows), np.arange(w), indexing="ij")
    p[(c * n_rows + r).ravel(), (r * w + c).ravel()] = 1.0
    return jnp.asarray(p, BF16)


def _mm_nt_cm_kernel(a_ref, w_ref, p_ref, o_ref):
    nt = (((1,), (1,)), ((), ()))
    z = lax.dot_general(a_ref[...], w_ref[...], nt, preferred_element_type=F32).astype(BF16)
    o_ref[...] = _dot(p_ref[...], z).astype(o_ref.dtype).reshape(o_ref.shape)


def matmul_nt_cm(cfg, a, w_t, layer, perm, *, tn, name):
    k = a.shape[1]
    n = w_t.shape[1]
    tm = perm.shape[0]
    gw = cfg.grid_w
    rows = cfg.seq // gw
    return pl.pallas_call(
        _mm_nt_cm_kernel,
        out_shape=jax.ShapeDtypeStruct((gw, rows, n), BF16),
        grid=(n // tn, cfg.seq // tm),
        in_specs=[pl.BlockSpec((tm, k), lambda j, i: (i, 0)),
                  pl.BlockSpec((None, tn, k), lambda j, i: (layer, j, 0)),
                  pl.BlockSpec((tm, tm), lambda j, i: (0, 0))],
        out_specs=pl.BlockSpec((gw, tm // gw, tn), lambda j, i: (0, i, j)),
        compiler_params=_cparams(("parallel", "parallel")),
        name=name,
    )(a, w_t, perm)


def _head_norm_cm_kernel(of_ref, ob_ref, gate_ref, w_ref, p_ref, o_ref, *, n_heads, dv):
    tm = o_ref.shape[0]
    parts = []
    for h in range(n_heads):
        sl = slice(h * dv, (h + 1) * dv)
        o = (of_ref[:, :, sl].astype(F32) + ob_ref[:, :, sl].astype(F32)).reshape(tm, dv)
        o = o * lax.rsqrt(jnp.mean(o * o, axis=-1, keepdims=True) + EPS) * w_ref[...]
        parts.append((o * _silu(gate_ref[:, :, sl].astype(F32).reshape(tm, dv))).astype(BF16))
    nt = (((0,), (0,)), ((), ()))
    y = jnp.concatenate(parts, axis=1)
    o_ref[...] = lax.dot_general(p_ref[...], y, nt, preferred_element_type=F32).astype(o_ref.dtype)


def head_norm_cm(cfg, o_f, o_b, z_cm, gate_col_block, norm_w, n_heads, dv, perm):
    gw, rows, vw = o_f.shape
    tm = perm.shape[0]
    nr = tm // gw
    spec = pl.BlockSpec((gw, nr, vw), lambda i: (0, i, 0))
    return pl.pallas_call(
        functools.partial(_head_norm_cm_kernel, n_heads=n_heads, dv=dv),
        out_shape=jax.ShapeDtypeStruct((gw * rows, vw), BF16),
        grid=(rows // nr,),
        in_specs=[spec, spec, pl.BlockSpec((gw, nr, vw), lambda i: (0, i, gate_col_block)),
                  pl.BlockSpec((1, dv), lambda i: (0, 0)),
                  pl.BlockSpec((tm, tm), lambda i: (0, 0))],
        out_specs=pl.BlockSpec((tm, vw), lambda i: (i, 0)),
        compiler_params=_cparams(("parallel",)),
        name="head_norm_cm",
    )(o_f, o_b, z_cm, norm_w.reshape(1, dv), perm)


FFT_G = 8


def _hy_prep_kernel(z_ref, zp_ref, zn_ref, w_ref, b_ref, v_ref, x1_ref, x2_ref, *, n_lat_tiles, hw):
    y = _conv3(z_ref, zp_ref, zn_ref, w_ref, n_lat_tiles) + b_ref[...]
    for p, o_ref in enumerate((v_ref, x1_ref, x2_ref)):
        o_ref[...] = y[:, p * hw:(p + 1) * hw]


def hyena_prep(cfg, z_hy, conv_w, conv_b):
    tr, hw = cfg.row_tile, cfg.hy_w
    t = z_hy.shape[0]
    main, prev, nxt = _halo_specs(tr, 3 * hw, 0, t)
    out = jax.ShapeDtypeStruct((t, hw), F32)
    ospec = pl.BlockSpec((tr, hw), lambda i: (i, 0))
    return pl.pallas_call(
        functools.partial(_hy_prep_kernel, n_lat_tiles=cfg.seq // tr, hw=hw),
        out_shape=(out, out, out),
        grid=(t // tr,),
        in_specs=[main, prev, nxt,
                  pl.BlockSpec((SHORT_CONV, 3 * hw), lambda i: (0, 0)),
                  pl.BlockSpec((1, 3 * hw), lambda i: (0, 0))],
        out_specs=(ospec, ospec, ospec),
        compiler_params=_cparams(("parallel",)),
        name="hyena_prep",
    )(z_hy, z_hy, z_hy, conv_w, conv_b.reshape(1, 3 * hw))


HY_HALF = LANES // 2


def _hy_taps_kernel(f_ref, w1_ref, b1_ref, w2_ref, b2_ref, fr_ref, w3a0_ref, w3b0_ref, w3a1_ref,
                    w3b1_ref, rate_ref, o0_ref, o1_ref, *, length, tt):
    th = tt // 2
    feats = f_ref[...]
    hid = jnp.sin(fr_ref[0:1, :] * (_dot_3(feats, w1_ref[...]) + b1_ref[...]))
    hid = jnp.sin(fr_ref[1:2, :] * (_dot_3(hid, w2_ref[...]) + b2_ref[...]))
    row = pl.program_id(0) * tt + lax.broadcasted_iota(jnp.int32, (th, 1), 0)
    for part, lane0 in ((0, 0), (1, HY_HALF)):
        n = row + part * th
        window = jnp.where(n == length, 0.0, jnp.exp(-feats[:, lane0:lane0 + 1] * rate_ref[...]))
        for w3_ref, o_ref in (((w3a0_ref, w3b0_ref)[part], o0_ref), ((w3a1_ref, w3b1_ref)[part], o1_ref)):
            o_ref[part * th:(part + 1) * th, :] = (_dot_3(hid, w3_ref[...]) * window).astype(o_ref.dtype)


def hyena_taps(feats, w1p, b1p, w2p, b2p, freqp, w3a, w3b, rates, length, hw):
    tt = min(512, length)
    n_half = length // tt
    out = jax.ShapeDtypeStruct((2 * length, hw), F32)
    ospec = pl.BlockSpec((tt, hw), lambda j: (j, 0))
    sq = pl.BlockSpec((LANES, LANES), lambda j: (0, 0))
    row = pl.BlockSpec((1, LANES), lambda j: (0, 0))
    w3spec = lambda order: pl.BlockSpec((LANES, hw), lambda j: (0, 2 * order + j // n_half))
    return pl.pallas_call(
        functools.partial(_hy_taps_kernel, length=length, tt=tt),
        out_shape=(out, out),
        grid=(2 * n_half,),
        in_specs=[pl.BlockSpec((tt // 2, LANES), lambda j: (j, 0)), sq, row, sq, row,
                  pl.BlockSpec((2, LANES), lambda j: (0, 0)),
                  w3spec(0), w3spec(0), w3spec(1), w3spec(1),
                  pl.BlockSpec((1, hw), lambda j: (0, 0))],
        out_specs=(ospec, ospec),
        compiler_params=_cparams(("parallel",)),
        name="hyena_taps",
    )(feats, w1p, b1p, w2p, b2p, freqp, w3a, w3b, w3a, w3b, rates)


FFT_BT = 8


def _taps_fft_a_kernel(f_ref, w1_ref, b1_ref, w2_ref, b2_ref, fr_ref, w3a0_ref, w3b0_ref, w3a1_ref,
                       w3b1_ref, rate_ref, l_ref, yr0_ref, yi0_ref, yr1_ref, yi1_ref):
    ph, bt, c = yr0_ref.shape
    feats = f_ref[...]
    hid = jnp.sin(fr_ref[0:1, :] * (_dot_3(feats, w1_ref[...]) + b1_ref[...]))
    hid = jnp.sin(fr_ref[1:2, :] * (_dot_3(hid, w2_ref[...]) + b2_ref[...]))
    row = lax.broadcasted_iota(jnp.int32, (feats.shape[0], 1), 0)
    is_row_l = jnp.logical_and(row == 0, pl.program_id(0) == 0)
    win_a = jnp.exp(-feats[:, 0:1] * rate_ref[...])
    win_b = jnp.where(is_row_l, 0.0, jnp.exp(-feats[:, HY_HALF:HY_HALF + 1] * rate_ref[...]))
    for w3a_ref, w3b_ref, yr_ref, yi_ref in ((w3a0_ref, w3b0_ref, yr0_ref, yi0_ref),
                                             (w3a1_ref, w3b1_ref, yr1_ref, yi1_ref)):
        taps = jnp.concatenate([_dot_3(hid, w3a_ref[...]) * win_a, _dot_3(hid, w3b_ref[...]) * win_b],
                               axis=0).astype(BF16)
        y = _dot(l_ref[...], taps)
        yr_ref[...] = y[0:ph * bt].reshape(ph, bt, c)
        yi_ref[...] = y[ph * bt:2 * ph * bt].reshape(ph, bt, c)


def taps_fft_a(feats_band, w1p, b1p, w2p, b2p, freqp, w3a, w3b, rates, lhs, ph, hw):
    nb, rows, _ = feats_band.shape
    out = jax.ShapeDtypeStruct((ph, LANES, hw), F32)
    ospec = pl.BlockSpec((ph, FFT_BT, hw), lambda j: (0, j, 0))
    sq = pl.BlockSpec((LANES, LANES), lambda j: (0, 0))
    row = pl.BlockSpec((1, LANES), lambda j: (0, 0))
    w3spec = lambda blk: pl.BlockSpec((LANES, hw), lambda j: (0, blk))
    return pl.pallas_call(
        _taps_fft_a_kernel,
        out_shape=(out, out, out, out),
        grid=(nb,),
        in_specs=[pl.BlockSpec((None, rows, LANES), lambda j: (j, 0, 0)), sq, row, sq, row,
                  pl.BlockSpec((2, LANES), lambda j: (0, 0)),
                  w3spec(0), w3spec(1), w3spec(2), w3spec(3),
                  pl.BlockSpec((1, hw), lambda j: (0, 0)),
                  pl.BlockSpec(lhs.shape, lambda j: (0, 0))],
        out_specs=(ospec, ospec, ospec, ospec),
        compiler_params=_cparams(("parallel",)),
        name="taps_fft_a",
    )(feats_band, w1p, b1p, w2p, b2p, freqp, w3a, w3b, w3a, w3b, rates, lhs)


def _fft_a_kernel(x_ref, l_ref, yr_ref, yi_ref):
    a, bt, c = x_ref.shape
    ph = yr_ref.shape[0]
    y = _dot(l_ref[...], x_ref[...].reshape(a * bt, c).astype(BF16))
    yr_ref[...] = y[0:ph * bt].reshape(ph, bt, c)
    yi_ref[...] = y[ph * bt:2 * ph * bt].reshape(ph, bt, c)


def fft_a(x, lhs, n_pages, ph):
    c = x.shape[1]
    x3 = x.reshape(x.shape[0] // LANES, LANES, c)
    out = jax.ShapeDtypeStruct((ph, LANES, c), F32)
    ospec = pl.BlockSpec((ph, FFT_BT, c), lambda j: (0, j, 0))
    return pl.pallas_call(
        _fft_a_kernel,
        out_shape=(out, out),
        grid=(LANES // FFT_BT,),
        in_specs=[pl.BlockSpec((n_pages, FFT_BT, c), lambda j: (0, j, 0)),
                  pl.BlockSpec(lhs.shape, lambda j: (0, 0))],
        out_specs=(ospec, ospec),
        compiler_params=_cparams(("parallel",)),
        name="fft_a",
    )(x3, lhs)


def _fft_b_kernel(yr_ref, yi_ref, m_ref, zr_ref, zi_ref):
    for g in range(FFT_G):
        s = jnp.concatenate([yr_ref[g], yi_ref[g]], axis=0).astype(BF16)
        z = _dot(m_ref[g], s)
        zr_ref[g] = z[0:LANES].astype(zr_ref.dtype)
        zi_ref[g] = z[LANES:2 * LANES].astype(zi_ref.dtype)


def fft_b(yr, yi, m2):
    ph, _, c = yr.shape
    out = jax.ShapeDtypeStruct((ph, LANES, c), BF16)
    spec = pl.BlockSpec((FFT_G, LANES, c), lambda j: (j, 0, 0))
    return pl.pallas_call(
        _fft_b_kernel,
        out_shape=(out, out),
        grid=(ph // FFT_G,),
        in_specs=[spec, spec, pl.BlockSpec((FFT_G, 2 * LANES, 2 * LANES), lambda j: (j, 0, 0))],
        out_specs=(spec, spec),
        compiler_params=_cparams(("parallel",)),
        name="fft_b",
    )(yr, yi, m2)


FFT_GM = 4


def _spec_mul_kernel(yr_ref, yi_ref, hr_ref, hi_ref, m2_ref, ma_ref, vr_ref, vi_ref):
    for g in range(FFT_GM):
        z = _dot(m2_ref[g], jnp.concatenate([yr_ref[g], yi_ref[g]], axis=0).astype(BF16))
        zr, zi = z[0:LANES], z[LANES:2 * LANES]
        hr, hi = hr_ref[g].astype(F32), hi_ref[g].astype(F32)
        s = jnp.concatenate([zr * hr - zi * hi, zr * hi + zi * hr], axis=0).astype(BF16)
        v = _dot(ma_ref[g], s)
        vr_ref[g] = v[0:LANES]
        vi_ref[g] = v[LANES:2 * LANES]


def spectrum_multiply(yr, yi, hr, hi, m2, ma):
    ph, _, c = yr.shape
    out = jax.ShapeDtypeStruct((ph, LANES, c), F32)
    spec = pl.BlockSpec((FFT_GM, LANES, c), lambda j: (j, 0, 0))
    mspec = pl.BlockSpec((FFT_GM, 2 * LANES, 2 * LANES), lambda j: (j, 0, 0))
    return pl.pallas_call(
        _spec_mul_kernel,
        out_shape=(out, out),
        grid=(ph // FFT_GM,),
        in_specs=[spec, spec, spec, spec, mspec, mspec],
        out_specs=(spec, spec),
        compiler_params=_cparams(("parallel",)),
        name="spectrum_multiply",
    )(yr, yi, hr, hi, m2, ma)


def _ifft_b_kernel(vr_ref, vi_ref, l_ref, x_ref, u_ref, sk_ref, o_ref):
    ph, bt, c = vr_ref.shape
    s = jnp.concatenate([vr_ref[...].reshape(ph * bt, c), vi_ref[...].reshape(ph * bt, c)],
                        axis=0).astype(BF16)
    y = _dot(l_ref[...], s).reshape(o_ref.shape)
    o_ref[...] = x_ref[...] * (y + sk_ref[...] * u_ref[...])


def ifft_b_gate(vr, vi, lhs, gate_x, u, skip_row):
    ph, _, c = vr.shape
    a_out = lhs.shape[0] // FFT_BT
    x3 = gate_x.reshape(gate_x.shape[0] // LANES, LANES, c)
    u3 = u.reshape(u.shape[0] // LANES, LANES, c)
    vspec = pl.BlockSpec((ph, FFT_BT, c), lambda j: (0, j, 0))
    tspec = pl.BlockSpec((a_out, FFT_BT, c), lambda j: (0, j, 0))
    out = pl.pallas_call(
        _ifft_b_kernel,
        out_shape=jax.ShapeDtypeStruct((a_out, LANES, c), F32),
        grid=(LANES // FFT_BT,),
        in_specs=[vspec, vspec, pl.BlockSpec(lhs.shape, lambda j: (0, 0)), tspec, tspec,
                  pl.BlockSpec((1, 1, c), lambda j: (0, 0, 0))],
        out_specs=tspec,
        compiler_params=_cparams(("parallel",)),
        name="ifft_b",
    )(vr, vi, lhs, x3, u3, skip_row.reshape(1, 1, c))
    return out.reshape(a_out * LANES, c)


def _hy_ctx_kernel(v_ref, x1_ref, x2_ref, t0_ref, t1_ref, sk_ref, f_ref, g_ref, o_ref, *, n):
    def conv(u, taps_ref):
        us = _dot(f_ref[:, 0:n], u.astype(BF16))
        hs = _dot(f_ref[...], taps_ref[...].astype(BF16))
        ur, ui, hr, hi = us[0:2 * n], us[2 * n:4 * n], hs[0:2 * n], hs[2 * n:4 * n]
        prod = jnp.concatenate([ur * hr - ui * hi, ur * hi + ui * hr], axis=0).astype(BF16)
        return _dot(g_ref[...], prod)

    v = v_ref[...]
    y = x1_ref[...] * (conv(v, t0_ref) + sk_ref[0:1, :] * v)
    o_ref[...] = x2_ref[...] * (conv(y, t1_ref) + sk_ref[1:2, :] * y)


def hyena_ctx(cfg, vxx, taps0, taps1, skip, fmat, gmat):
    n, hw = cfg.ctx, cfg.hy_w
    cb = 256
    rb = cfg.seq // n
    part = pl.BlockSpec((n, cb), lambda j: (rb, j))
    tspec = pl.BlockSpec((2 * n, cb), lambda j: (0, j))
    return pl.pallas_call(
        functools.partial(_hy_ctx_kernel, n=n),
        out_shape=jax.ShapeDtypeStruct((n, hw), F32),
        grid=(hw // cb,),
        in_specs=[part, part, part, tspec, tspec,
                  pl.BlockSpec((2, cb), lambda j: (0, j)),
                  pl.BlockSpec((4 * n, 2 * n), lambda j: (0, 0)),
                  pl.BlockSpec((n, 4 * n), lambda j: (0, 0))],
        out_specs=pl.BlockSpec((n, cb), lambda j: (0, j)),
        compiler_params=_cparams(("parallel",)),
        name="hyena_ctx",
    )(vxx[0], vxx[1], vxx[2], taps0, taps1, skip, fmat, gmat)


class HyenaConsts(NamedTuple):
    feats: jax.Array
    rates: jax.Array
    ph: int
    la_data: jax.Array
    la_taps: jax.Array
    m2: jax.Array
    ma: jax.Array
    lb: jax.Array
    feats_ctx: jax.Array
    f_ctx: jax.Array
    g_ctx: jax.Array


def _feature_rows(length):
    n = np.arange(2 * length, dtype=np.float64)
    pos = np.where(n < length, n, 2 * length - n)
    bands = (HY_EMB - 1) // 2
    f = np.linspace(1e-4, bands - 1, bands)
    omega = (2.0 * math.pi / length) * pos
    feats = np.zeros((2 * length, HY_HALF), np.float64)
    feats[:, 0] = pos / (length - 1)
    feats[:, 1:1 + bands] = np.cos(omega[:, None] * f[None, :])
    feats[:, 1 + bands:1 + 2 * bands] = -np.sin(omega[:, None] * f[None, :])
    return feats


def _features(length):
    feats = _feature_rows(length)
    tt = min(512, length)
    tiles = feats.reshape(2 * length // tt, 2, tt // 2, HY_HALF)
    packed = np.concatenate([tiles[:, 0], tiles[:, 1]], axis=-1).reshape(length, LANES)
    return jnp.asarray(packed, F32)


def _features_bands(length):
    feats = _feature_rows(length)
    half_pages = length // LANES
    by_time = feats.reshape(2, half_pages, LANES // FFT_BT, FFT_BT, HY_HALF)
    packed = np.concatenate([by_time[0], by_time[1]], axis=-1)
    return jnp.asarray(packed.transpose(1, 0, 2, 3).reshape(LANES // FFT_BT, half_pages * FFT_BT, LANES), F32)


def hyena_consts(cfg):
    length, n, hw = cfg.seq, cfg.ctx, cfg.hy_w
    big_n = 2 * length
    p = big_n // LANES
    rates = np.abs(np.linspace(math.log(HY_TARGET) / HY_FAST_DECAY, math.log(HY_TARGET) / HY_SLOW_DECAY, hw))
    ph = p // 2 + 8
    k1 = np.arange(ph)
    kept = (k1 <= p // 2).astype(np.float64)
    a = np.arange(p)
    ang_a = 2.0 * math.pi * ((k1[:, None] * a[None, :]) % p) / p
    dft_a = np.concatenate([np.cos(ang_a), -np.sin(ang_a)], axis=0) * np.tile(kept, 2)[:, None]
    eye = np.eye(FFT_BT)
    la_taps = np.kron(dft_a, eye)
    la_data = np.kron(dft_a[:, :p // 2], eye)
    weight = kept * np.where((k1 == 0) | (k1 == p // 2), 1.0, 2.0) / big_n
    inv_a = (np.concatenate([np.cos(ang_a), -np.sin(ang_a)], axis=0) * np.tile(weight, 2)[:, None]).T
    lb = np.kron(inv_a[:p // 2], eye)
    b = np.arange(LANES)
    ang_b = 2.0 * math.pi * (((b[:, None] * b[None, :]) % LANES) / LANES)[None] \
        + 2.0 * math.pi * (k1[:, None, None] * b[None, None, :]) / big_n
    fr, fi = np.cos(ang_b), -np.sin(ang_b)
    m2 = np.concatenate([np.concatenate([fr, -fi], axis=2), np.concatenate([fi, fr], axis=2)], axis=1)
    gr, gi = fr.transpose(0, 2, 1), -fi.transpose(0, 2, 1)
    ma = np.concatenate([np.concatenate([gr, -gi], axis=2), np.concatenate([gi, gr], axis=2)], axis=1)
    kk = np.arange(2 * n)
    ac = 2.0 * math.pi * ((kk[:, None] * kk[None, :]) % (2 * n)) / (2 * n)
    f_ctx = np.concatenate([np.cos(ac), -np.sin(ac)], axis=0)
    g_ctx = np.concatenate([np.cos(ac), -np.sin(ac)], axis=1)[:n] / (2 * n)
    bf = lambda x: jnp.asarray(x, BF16)
    return HyenaConsts(_features_bands(length), jnp.asarray(rates[None, :], F32), ph, bf(la_data), bf(la_taps),
                       bf(m2), bf(ma), bf(lb), _features(n), bf(f_ctx), bf(g_ctx))


def hyena(cfg, hc, vxx, filt_w, skip, with_ctx=True):
    length, n, hw = cfg.seq, cfg.ctx, cfg.hy_w
    half = length // LANES
    y_lat = vxx[0]
    taps_y = taps_fft_a(hc.feats, *filt_w, hc.rates, hc.la_taps, hc.ph, hw)
    for order in range(2):
        hr, hi = fft_b(taps_y[2 * order], taps_y[2 * order + 1], hc.m2)
        yr, yi = fft_a(y_lat, hc.la_data, half, hc.ph)
        vr, vi = spectrum_multiply(yr, yi, hr, hi, hc.m2, hc.ma)
        y_lat = ifft_b_gate(vr, vi, hc.lb, vxx[1 + order], y_lat, skip[order:order + 1])
    if not with_ctx:
        return y_lat, jnp.zeros((n, hw), F32)
    taps_c = hyena_taps(hc.feats_ctx, *filt_w, hc.rates, n, hw)
    y_ctx = hyena_ctx(cfg, vxx, taps_c[0], taps_c[1], skip, hc.f_ctx, hc.g_ctx)
    return y_lat, y_ctx


def expert_expand(n=N_EXPERTS):
    m = np.zeros((LANES, n * LANES), np.float32)
    for e in range(n):
        m[e, e * LANES:(e + 1) * LANES] = 1.0
    return jnp.asarray(m, BF16)


def prep_gla_gate(gate_up, gate_b):
    l, _, r, qk = gate_up.shape
    w = jnp.zeros((l, LANES, 2 * qk), F32)
    for z in range(2):
        w = w.at[:, GLA_LR_LANE0 + z * r:GLA_LR_LANE0 + (z + 1) * r, z * qk:(z + 1) * qk].set(gate_up[:, z])
    return w, gate_b.reshape(l, 1, 2 * qk)


def prep_hyena_filter(w1, b1, w2, b2, w3, freq):
    e, hdim = w1.shape
    assert hdim == HY_HALF
    h = HY_HALF
    w1p = jnp.zeros((LANES, LANES), F32).at[:e, :h].set(w1).at[h:h + e, h:].set(w1)
    w2p = jnp.zeros((LANES, LANES), F32).at[:h, :h].set(w2).at[h:, h:].set(w2)
    zero = jnp.zeros_like(w3)
    w3a = jnp.concatenate([w3, zero], axis=0)
    w3b = jnp.concatenate([zero, w3], axis=0)
    twice = lambda v: jnp.concatenate([v, v], axis=-1).reshape(-1, LANES)
    return w1p, twice(b1), w2p, twice(b2), twice(freq), w3a, w3b


def prep_w2(w2):
    l, e, f, d = w2.shape
    return w2.reshape(l, e * f, d).astype(BF16)


def kernel(x, c, ctx, c_ctx, norm1_g, norm2_g, w_mod, b_mod, w_in, gdn_conv, gdn_a_log, gdn_dt_bias, gdn_norm, gla_gate_up, gla_gate_b, gla_norm, hy_conv_w, hy_conv_b, hy_w1, hy_b1, hy_w2, hy_b2, hy_w3, hy_freq, hy_skip, merge_up, merge_b, w_branch, w_out, w_router, router_bias, moe_w1, moe_w3, moe_w2, final_g):
    cfg = Cfg(d=4096, seq=8192, ctx=256, grid_w=64, gdn_h=8, gla_h=4, hy_w=1024, merge_rank=256,
              d_expert=256, row_tile=256, mm_tm=768)
    return forward(cfg, x, c, ctx, c_ctx, norm1_g, norm2_g, w_mod, b_mod, w_in, gdn_conv, gdn_a_log,
                   gdn_dt_bias, gdn_norm, gla_gate_up, gla_gate_b, gla_norm, hy_conv_w, hy_conv_b, hy_w1,
                   hy_b1, hy_w2, hy_b2, hy_w3, hy_freq, hy_skip, merge_up, merge_b, w_branch, w_out,
                   w_router, router_bias, moe_w1, moe_w3, moe_w2, final_g)


def split_w_in(cfg, w_in):
    gw4 = 4 * cfg.gdn_w
    gdn_small = 4 * cfg.gdn_h
    gla0 = gw4 + gdn_small
    gla_w = 2 * cfg.gla_qk + 2 * cfg.gla_v
    hy0 = gla0 + gla_w + 32
    hy_w = 3 * cfg.hy_w
    w_t = jnp.swapaxes(w_in, 1, 2)
    l, _, k = w_t.shape
    zeros = lambda n: jnp.zeros((l, n, k), w_t.dtype)
    small = jnp.concatenate([w_t[:, hy0 + hy_w:hy0 + hy_w + cfg.merge_rank],
                             w_t[:, gw4:gla0], zeros(GLA_LR_LANE0 - gdn_small),
                             w_t[:, gla0 + gla_w:hy0], zeros(LANES - GLA_LR_LANE0 - 32)], axis=1)
    return cast_rows(w_t, 0, gw4), cast_rows(w_t, gla0, gla_w), cast_rows(w_t, hy0, hy_w), small


def forward(cfg, x, c, ctx, c_ctx, norm1_g, norm2_g, w_mod, b_mod, w_in, gdn_conv, gdn_a_log, gdn_dt_bias,
            gdn_norm, gla_gate_up, gla_gate_b, gla_norm, hy_conv_w, hy_conv_b, hy_w1, hy_b1, hy_w2, hy_b2,
            hy_w3, hy_freq, hy_skip, merge_up, merge_b, w_branch, w_out, w_router, router_bias, moe_w1,
            moe_w3, moe_w2, final_g):
    d, tr, tm = cfg.d, cfg.row_tile, cfg.mm_tm
    depth = w_in.shape[0]
    nh = cfg.gdn_h
    tn = min(1024, d)
    w_gdn, w_gla, w_hy, w_small = split_w_in(cfg, w_in)
    w1b, w3b = moe_w1.astype(BF16), moe_w3.astype(BF16)
    w2f = prep_w2(moe_w2)
    wb, mu, wo = w_branch.astype(BF16), merge_up.astype(BF16), w_out.astype(BF16)
    mb = merge_b.reshape(depth, 3, 1, d)
    wr_pad = jnp.zeros((d, LANES), F32).at[:, :N_EXPERTS].set(w_router)
    rb_col = router_bias.reshape(N_EXPERTS, 1)
    expand = expert_expand()
    wup, gbias = prep_gla_gate(gla_gate_up, gla_gate_b)
    lane0 = 2 * nh
    alog_rows = jnp.zeros((depth, 1, LANES), F32).at[:, 0, lane0:2 * lane0].set(gdn_a_log.reshape(depth, -1))
    dtb_rows = jnp.zeros((depth, 1, LANES), F32).at[:, 0, lane0:2 * lane0].set(gdn_dt_bias.reshape(depth, -1))
    hc = hyena_consts(cfg)
    perm = column_major_perm(cfg)
    gla_gate_blk = (2 * cfg.gla_qk + cfg.gla_v) // cfg.gla_v
    z_proj = functools.partial(matmul_nt, tm=tm, tn=tn, out_dtype=BF16)

    lat = jnp.concatenate([x[0], ctx[0]], axis=0)
    cvec = jnp.zeros((8, d), F32).at[0].set(c[0]).at[1].set(c_ctx)
    mods = modvec(cvec, w_mod, b_mod)

    for l in range(depth):
        with_ctx = l < depth - 1
        mod = mods[l]
        h = norm1(cfg, lat, norm1_g[l], mod)
        z_gdn = z_proj(h, w_gdn, l, name="w_in_gdn")
        z_gla = matmul_nt_cm(cfg, h, w_gla, l, perm, tn=tn, name="w_in_gla")
        z_gla_ctx = matmul_nt(h, w_gla, l, tm=cfg.ctx, tn=tn, out_dtype=BF16, name="w_in_gla_ctx",
                              row0=cfg.seq, n_rows=cfg.ctx)
        z_hy = z_proj(h, w_hy, l, name="w_in_hy")
        z_small = matmul_nt(h, w_small, l, tm=tm, tn=w_small.shape[1], name="w_in_small")
        qn, kn, vs, bb, gcb, gct = gdn_prep(cfg, z_gdn, z_small, gdn_conv[l], alog_rows[l], dtb_rows[l])
        o_f, o_b = gdn_scan(cfg, *gdn_chunks(cfg, qn, kn, vs, bb, gcb, gct))
        a_all = head_norm(o_f, o_b, z_gdn, 3, gdn_norm[l], nh, 128, tr)
        lr = z_small[:, cfg.merge_rank:cfg.merge_rank + LANES]
        lr_cm = jnp.swapaxes(lr[:cfg.seq].reshape(cfg.seq // cfg.grid_w, cfg.grid_w, LANES), 0, 1)
        (olf, olb), (ocf, ocb) = gla(cfg, z_gla, z_gla_ctx, lr_cm, lr[cfg.seq:], wup[l], gbias[l])
        b_lat = head_norm_cm(cfg, olf, olb, z_gla, gla_gate_blk, gla_norm[l], cfg.gla_h, 256, perm)
        b_ctx = head_norm(ocf, ocb, z_gla_ctx, gla_gate_blk, gla_norm[l], cfg.gla_h, 256, tr)
        b_all = jnp.concatenate([b_lat, b_ctx], axis=0)
        vxx = hyena_prep(cfg, z_hy, hy_conv_w[l], hy_conv_b[l])
        filt_w = prep_hyena_filter(hy_w1[l], hy_b1[l], hy_w2[l], hy_b2[l], hy_w3[l], hy_freq[l])
        c_lat, c_ctx_out = hyena(cfg, hc, vxx, filt_w, hy_skip[l], with_ctx)
        c_all = jnp.concatenate([c_lat, c_ctx_out], axis=0).astype(BF16)
        s = merge_branches(cfg, (a_all, b_all, c_all), z_small, wb, mu, mb, l, tn=tn)
        lat = matmul_resid(cfg, s, wo, l, lat, mod, 2, tn=tn, name="w_out")
        h2, gate_rep = norm2_route(cfg, lat, norm2_g[l], mod, wr_pad, rb_col, expand)
        act = matmul_moe_act(cfg, h2, w1b, w3b, l, gate_rep)
        lat = matmul_resid(cfg, act, w2f, l, lat, mod, 5, tn=tn, name="moe_down")
    return final_norm(lat, final_g, cfg.seq, tr)[None]
```

```python
import functools
import math
from typing import NamedTuple

import numpy as np
import jax
import jax.numpy as jnp
from jax import lax
from jax.experimental import pallas as pl
from jax.experimental.pallas import tpu as pltpu

F32 = jnp.float32
BF16 = jnp.bfloat16

EPS = 1e-6
LANES = 128
V7X_VMEM_BYTES = 64 * 1024 * 1024
VMEM_LIMIT = (V7X_VMEM_BYTES * 13) // 16
SHORT_CONV = 3
GLA_CHUNK = 64
GLA_GATE_NORM = 16.0
N_EXPERTS = 16
N_GROUPS = 4
HY_EMB = 33
HY_FAST_DECAY = 0.3
HY_SLOW_DECAY = 1.5
HY_TARGET = 1e-2


class Cfg(NamedTuple):
    d: int
    seq: int
    ctx: int
    grid_w: int
    gdn_h: int
    gla_h: int
    hy_w: int
    merge_rank: int
    d_expert: int
    row_tile: int
    mm_tm: int

    @property
    def t(self):
        return self.seq + self.ctx

    @property
    def gdn_w(self):
        return self.gdn_h * 128

    @property
    def gla_qk(self):
        return self.gla_h * 128

    @property
    def gla_v(self):
        return self.gla_h * 256


def _cparams(sem):
    return pltpu.CompilerParams(dimension_semantics=sem, vmem_limit_bytes=VMEM_LIMIT)


def _split3(x):
    hi = x.astype(BF16)
    r1 = x - hi.astype(F32)
    mid = r1.astype(BF16)
    lo = (r1 - mid.astype(F32)).astype(BF16)
    return hi, mid, lo


def _dot(a, b):
    return jnp.dot(a, b, preferred_element_type=F32)


def _dot_sel(sel_bf16, x):
    hi, mid, lo = _split3(x)
    return _dot(sel_bf16, hi) + _dot(sel_bf16, mid) + _dot(sel_bf16, lo)


def _dot_x_sel(x, sel_bf16):
    hi, mid, lo = _split3(x)
    return _dot(hi, sel_bf16) + _dot(mid, sel_bf16) + _dot(lo, sel_bf16)


def _dot_hi(a, b):
    a1, a2, a3 = _split3(a)
    b1, b2, b3 = _split3(b)
    return (_dot(a1, b1) + (_dot(a1, b2) + _dot(a2, b1))
            + (_dot(a2, b2) + _dot(a1, b3) + _dot(a3, b1)))


def _dot_3(a, b):
    a1 = a.astype(BF16)
    a2 = (a - a1.astype(F32)).astype(BF16)
    b1 = b.astype(BF16)
    b2 = (b - b1.astype(F32)).astype(BF16)
    return _dot(a1, b1) + (_dot(a1, b2) + _dot(a2, b1))


def _silu(x):
    return x * jax.nn.sigmoid(x)


def _modvec_kernel(x_ref, w_ref, b_ref, o_ref):
    o_ref[...] = _dot(_silu(x_ref[...]), w_ref[...]) + b_ref[...]


def modvec(cvec, w_mod, b_mod, tn=512):
    depth, d, n = w_mod.shape
    return pl.pallas_call(
        _modvec_kernel,
        out_shape=jax.ShapeDtypeStruct((depth, 8, n), F32),
        grid=(depth, n // tn),
        in_specs=[pl.BlockSpec((8, d), lambda l, j: (0, 0)),
                  pl.BlockSpec((None, d, tn), lambda l, j: (l, 0, j)),
                  pl.BlockSpec((None, 1, tn), lambda l, j: (l, 0, j))],
        out_specs=pl.BlockSpec((None, 8, tn), lambda l, j: (l, 0, j)),
        compiler_params=_cparams(("parallel", "parallel")),
        name="modvec",
    )(cvec, w_mod, b_mod.reshape(depth, 1, n))


def _mod_row(mod_ref, is_ctx, idx, d):
    return mod_ref[pl.ds(is_ctx, 1), idx * d:(idx + 1) * d]


def _norm_mod(x, gain, shift, scale):
    y = x * lax.rsqrt(jnp.mean(x * x, axis=-1, keepdims=True) + EPS)
    return (y * gain) * (1.0 + scale) + shift


def _norm1_kernel(x_ref, g_ref, mod_ref, o_ref, *, d, n_lat_tiles):
    is_ctx = (pl.program_id(0) >= n_lat_tiles).astype(jnp.int32)
    h = _norm_mod(x_ref[...], g_ref[...], _mod_row(mod_ref, is_ctx, 0, d),
                  _mod_row(mod_ref, is_ctx, 1, d))
    o_ref[...] = h.astype(BF16)


def norm1(cfg, x, gain, mod):
    tr = cfg.row_tile
    t, d = x.shape
    return pl.pallas_call(
        functools.partial(_norm1_kernel, d=d, n_lat_tiles=cfg.seq // tr),
        out_shape=jax.ShapeDtypeStruct((t, d), BF16),
        grid=(t // tr,),
        in_specs=[pl.BlockSpec((tr, d), lambda i: (i, 0)),
                  pl.BlockSpec((1, d), lambda i: (0, 0)),
                  pl.BlockSpec((8, 6 * d), lambda i: (0, 0))],
        out_specs=pl.BlockSpec((tr, d), lambda i: (i, 0)),
        compiler_params=_cparams(("parallel",)),
        name="norm1",
    )(x, gain.reshape(1, d), mod)


def _route(sel_t, sc_t):
    per = N_EXPERTS // N_GROUPS
    grp_score = []
    for g in range(N_GROUPS):
        v = sel_t[g * per:(g + 1) * per]
        best = None
        for a in range(per):
            for b in range(a + 1, per):
                s = v[a] + v[b]
                best = s if best is None else jnp.maximum(best, s)
        grp_score.append(best)
    best_s, best_g = grp_score[0], jnp.zeros_like(grp_score[0])
    for g in range(1, N_GROUPS):
        better = grp_score[g] > best_s
        best_s = jnp.where(better, grp_score[g], best_s)
        best_g = jnp.where(better, float(g), best_g)
    picked = []
    for e in range(N_EXPERTS):
        g, i = divmod(e, per)
        rank = jnp.zeros_like(best_s)
        for j in range(per):
            if j == i:
                continue
            o = sel_t[g * per + j]
            ahead = (o >= sel_t[e]) if j < i else (o > sel_t[e])
            rank = rank + ahead.astype(F32)
        picked.append(jnp.where((best_g == float(g)) & (rank < 2.0), sc_t[e], 0.0))
    den = picked[0]
    for e in range(1, N_EXPERTS):
        den = den + picked[e]
    inv = 1.0 / den
    return [p * inv for p in picked], best_g


def _norm2_kernel(x_ref, g_ref, mod_ref, wr_ref, rb_ref, ex_ref, o_ref, gate_ref, *, d,
                  n_lat_tiles):
    is_ctx = (pl.program_id(0) >= n_lat_tiles).astype(jnp.int32)
    h = _norm_mod(x_ref[...], g_ref[...], _mod_row(mod_ref, is_ctx, 3, d),
                  _mod_row(mod_ref, is_ctx, 4, d))
    o_ref[...] = h.astype(BF16)
    logits = _dot_3(h, wr_ref[...])
    lt = jnp.transpose(logits)
    sc = jax.nn.sigmoid(lt[0:N_EXPERTS, :])
    sel = sc + rb_ref[...]
    gate_rows, _ = _route([sel[e:e + 1, :] for e in range(N_EXPERTS)],
                          [sc[e:e + 1, :] for e in range(N_EXPERTS)])
    rows = lax.broadcasted_iota(jnp.int32, lt.shape, 0)
    gt = jnp.zeros(lt.shape, F32)
    for e in range(N_EXPERTS):
        gt = jnp.where(rows == e, gate_rows[e], gt)
    gate = jnp.transpose(gt)
    gate_ref[...] = _dot_x_sel(gate, ex_ref[...])


def norm2_route(cfg, x, gain, mod, w_router_pad, rbias_col, expand):
    tr = cfg.row_tile
    t, d = x.shape
    return pl.pallas_call(
        functools.partial(_norm2_kernel, d=d, n_lat_tiles=cfg.seq // tr),
        out_shape=(jax.ShapeDtypeStruct((t, d), BF16),
                   jax.ShapeDtypeStruct((t, N_EXPERTS * LANES), F32)),
        grid=(t // tr,),
        in_specs=[pl.BlockSpec((tr, d), lambda i: (i, 0)),
                  pl.BlockSpec((1, d), lambda i: (0, 0)),
                  pl.BlockSpec((8, 6 * d), lambda i: (0, 0)),
                  pl.BlockSpec((d, LANES), lambda i: (0, 0)),
                  pl.BlockSpec((N_EXPERTS, 1), lambda i: (0, 0)),
                  pl.BlockSpec((LANES, N_EXPERTS * LANES), lambda i: (0, 0))],
        out_specs=(pl.BlockSpec((tr, d), lambda i: (i, 0)),
                   pl.BlockSpec((tr, N_EXPERTS * LANES), lambda i: (i, 0))),
        compiler_params=_cparams(("parallel",)),
        name="norm2_route",
    )(x, gain.reshape(1, d), mod, w_router_pad, rbias_col, expand)


def _final_norm_kernel(x_ref, g_ref, o_ref):
    x = x_ref[...]
    o_ref[...] = (x * lax.rsqrt(jnp.mean(x * x, axis=-1, keepdims=True) + EPS)) * g_ref[...]


def final_norm(x, gain, n_rows, tr):
    d = x.shape[1]
    return pl.pallas_call(
        _final_norm_kernel,
        out_shape=jax.ShapeDtypeStruct((n_rows, d), F32),
        grid=(n_rows // tr,),
        in_specs=[pl.BlockSpec((tr, d), lambda i: (i, 0)),
                  pl.BlockSpec((1, d), lambda i: (0, 0))],
        out_specs=pl.BlockSpec((tr, d), lambda i: (i, 0)),
        compiler_params=_cparams(("parallel",)),
        name="final_norm",
    )(x, gain.reshape(1, d))


def _mm_plain_kernel(a_ref, w_ref, o_ref):
    o_ref[...] = _dot(a_ref[...], w_ref[...].astype(BF16)).astype(o_ref.dtype)


def matmul_plain(a, w, layer, *, tm, tn, n_cols=None, col0=0, out_dtype=F32, name="mm"):
    t, k = a.shape
    n_cols = w.shape[2] - col0 if n_cols is None else n_cols
    off = col0 // tn
    return pl.pallas_call(
        _mm_plain_kernel,
        out_shape=jax.ShapeDtypeStruct((t, n_cols), out_dtype),
        grid=(n_cols // tn, t // tm),
        in_specs=[pl.BlockSpec((tm, k), lambda j, i: (i, 0)),
                  pl.BlockSpec((None, k, tn), lambda j, i: (layer, 0, j + off))],
        out_specs=pl.BlockSpec((tm, tn), lambda j, i: (i, j)),
        compiler_params=_cparams(("parallel", "parallel")),
        name=name,
    )(a, w)


def _mm_nt_kernel(a_ref, w_ref, o_ref):
    nt = (((1,), (1,)), ((), ()))
    o_ref[...] = lax.dot_general(a_ref[...], w_ref[...].astype(BF16), nt,
                                 preferred_element_type=F32).astype(o_ref.dtype)


def matmul_nt(a, w_t, layer, *, tm, tn, out_dtype=F32, name="mm_nt", row0=0, n_rows=None):
    k = a.shape[1]
    t = a.shape[0] if n_rows is None else n_rows
    n = w_t.shape[1]
    rb0 = row0 // tm
    return pl.pallas_call(
        _mm_nt_kernel,
        out_shape=jax.ShapeDtypeStruct((t, n), out_dtype),
        grid=(n // tn, t // tm),
        in_specs=[pl.BlockSpec((tm, k), lambda j, i: (i + rb0, 0)),
                  pl.BlockSpec((None, tn, k), lambda j, i: (layer, j, 0))],
        out_specs=pl.BlockSpec((tm, tn), lambda j, i: (i, j)),
        compiler_params=_cparams(("parallel", "parallel")),
        name=name,
    )(a, w_t)


def _cast_rows_kernel(w_ref, o_ref):
    o_ref[...] = w_ref[...].astype(o_ref.dtype)


def cast_rows(w_t, row0, n_rows):
    l, _, k = w_t.shape
    tr = math.gcd(row0, n_rows) if row0 else n_rows
    while tr > 512 and tr % 2 == 0:
        tr //= 2
    assert tr % 16 == 0
    off = row0 // tr
    return pl.pallas_call(
        _cast_rows_kernel,
        out_shape=jax.ShapeDtypeStruct((l, n_rows, k), BF16),
        grid=(l, n_rows // tr),
        in_specs=[pl.BlockSpec((None, tr, k), lambda i, j: (i, j + off, 0))],
        out_specs=pl.BlockSpec((None, tr, k), lambda i, j: (i, j, 0)),
        compiler_params=_cparams(("parallel", "parallel")),
        name="cast_rows",
    )(w_t)


def _mm_resid_kernel(a_ref, w_ref, r_ref, mod_ref, o_ref, *, tm, n_lat):
    row = pl.program_id(1) * tm + lax.broadcasted_iota(jnp.int32, (tm, 1), 0)
    gate = jnp.where(row < n_lat, mod_ref[0:1, :], mod_ref[1:2, :])
    o_ref[...] = r_ref[...] + gate * _dot(a_ref[...], w_ref[...])


def matmul_resid(cfg, a, w, layer, resid, mod, idx, *, tn, name):
    t, k = a.shape
    d = w.shape[2]
    tm = cfg.mm_tm
    return pl.pallas_call(
        functools.partial(_mm_resid_kernel, tm=tm, n_lat=cfg.seq),
        out_shape=jax.ShapeDtypeStruct((t, d), F32),
        grid=(d // tn, t // tm),
        in_specs=[pl.BlockSpec((tm, k), lambda j, i: (i, 0)),
                  pl.BlockSpec((None, k, tn), lambda j, i: (layer, 0, j)),
                  pl.BlockSpec((tm, tn), lambda j, i: (i, j)),
                  pl.BlockSpec((8, tn), lambda j, i: (0, idx * (d // tn) + j))],
        out_specs=pl.BlockSpec((tm, tn), lambda j, i: (i, j)),
        input_output_aliases={2: 0},
        compiler_params=_cparams(("parallel", "parallel")),
        name=name,
    )(a, w, resid, mod)


def _mm_moe_act_kernel(a_ref, w1_ref, w3_ref, g_ref, o_ref, *, de, n_e):
    a = a_ref[...]
    for e in range(n_e):
        up = _dot(a, w1_ref[e])
        lin = _dot(a, w3_ref[e])
        g = g_ref[:, e * LANES:(e + 1) * LANES]
        g = jnp.concatenate([g] * (de // LANES), axis=1)
        o_ref[:, e * de:(e + 1) * de] = (_silu(up) * lin * g).astype(o_ref.dtype)


def matmul_moe_act(cfg, h, w1, w3, layer, gate_rep, *, n_e=2):
    t, k = h.shape
    de = cfg.d_expert
    tm = cfg.mm_tm
    wspec = pl.BlockSpec((None, n_e, k, de), lambda j, i: (layer, j, 0, 0))
    return pl.pallas_call(
        functools.partial(_mm_moe_act_kernel, de=de, n_e=n_e),
        out_shape=jax.ShapeDtypeStruct((t, N_EXPERTS * de), BF16),
        grid=(N_EXPERTS // n_e, t // tm),
        in_specs=[pl.BlockSpec((tm, k), lambda j, i: (i, 0)), wspec, wspec,
                  pl.BlockSpec((tm, n_e * LANES), lambda j, i: (i, j))],
        out_specs=pl.BlockSpec((tm, n_e * de), lambda j, i: (i, j)),
        compiler_params=_cparams(("parallel", "parallel")),
        name="moe_up",
    )(h, w1, w3, gate_rep)


def _merge_kernel(a0_ref, a1_ref, a2_ref, zg_ref, wb_ref, mu_ref, mb_ref, o_ref):
    zg = zg_ref[...].astype(BF16)
    acc = None
    for n, a_ref in enumerate((a0_ref, a1_ref, a2_ref)):
        y = _dot(a_ref[...], wb_ref[n])
        gate = jax.nn.sigmoid(_dot(zg, mu_ref[n]) + mb_ref[n])
        acc = gate * y if acc is None else acc + gate * y
    o_ref[...] = acc.astype(o_ref.dtype)


def merge_branches(cfg, outs, z_small, w_branch, merge_up, merge_b, layer, *, tn):
    t, bw = outs[0].shape
    d = w_branch.shape[3]
    r = cfg.merge_rank
    tm = cfg.mm_tm
    a_spec = pl.BlockSpec((tm, bw), lambda j, i: (i, 0))
    return pl.pallas_call(
        _merge_kernel,
        out_shape=jax.ShapeDtypeStruct((t, d), BF16),
        grid=(d // tn, t // tm),
        in_specs=[a_spec, a_spec, a_spec,
                  pl.BlockSpec((tm, r), lambda j, i: (i, 0)),
                  pl.BlockSpec((None, 3, bw, tn), lambda j, i: (layer, 0, 0, j)),
                  pl.BlockSpec((None, 3, r, tn), lambda j, i: (layer, 0, 0, j)),
                  pl.BlockSpec((None, 3, 1, tn), lambda j, i: (layer, 0, 0, j))],
        out_specs=pl.BlockSpec((tm, tn), lambda j, i: (i, j)),
        compiler_params=_cparams(("parallel", "parallel")),
        name="merge",
    )(outs[0], outs[1], outs[2], z_small, w_branch, merge_up, merge_b)


HALO_ROWS = 16


def _halo_specs(tr, width, col_block, n_rows):
    rb = tr // HALO_ROWS
    last = n_rows // HALO_ROWS - 1
    main = pl.BlockSpec((tr, width), lambda i: (i, col_block))
    prev = pl.BlockSpec((HALO_ROWS, width), lambda i: (jnp.maximum(i * rb - 1, 0), col_block))
    nxt = pl.BlockSpec((HALO_ROWS, width), lambda i: (jnp.minimum((i + 1) * rb, last), col_block))
    return main, prev, nxt


def _conv3(z_ref, prev_ref, next_ref, w_ref, n_lat_tiles):
    i = pl.program_id(0)
    x = z_ref[...].astype(F32)
    tr = x.shape[0]
    has_prev = jnp.logical_and(i != 0, i != n_lat_tiles).astype(F32)
    has_next = jnp.logical_and(i != n_lat_tiles - 1, i != n_lat_tiles).astype(F32)
    row = lax.broadcasted_iota(jnp.int32, (tr, 1), 0)
    halo_prev = prev_ref[HALO_ROWS - 1:HALO_ROWS, :].astype(F32) * has_prev
    halo_next = next_ref[0:1, :].astype(F32) * has_next
    x_prev = jnp.where(row == 0, halo_prev, pltpu.roll(x, 1, axis=0))
    x_next = jnp.where(row == tr - 1, halo_next, pltpu.roll(x, tr - 1, axis=0))
    return x_prev * w_ref[0:1, :] + x * w_ref[1:2, :] + x_next * w_ref[2:3, :]


def _tri_masks(n):
    r = lax.broadcasted_iota(jnp.int32, (n, n), 0)
    c = lax.broadcasted_iota(jnp.int32, (n, n), 1)
    return r, c


def _gdn_prep_kernel(z_ref, zp_ref, zn_ref, s_ref, cw_ref, alog_ref, dtb_ref,
                     q_ref, k_ref, v_ref, bb_ref, gcb_ref, gct_ref, *, n_heads, n_lat_tiles):
    tr = z_ref.shape[0]
    hw = n_heads * 128
    y = _silu(_conv3(z_ref, zp_ref, zn_ref, cw_ref, n_lat_tiles))
    for h in range(n_heads):
        q = y[:, h * 128:(h + 1) * 128]
        k = y[:, hw + h * 128:hw + (h + 1) * 128]
        q = q * (lax.rsqrt(jnp.sum(q * q, axis=-1, keepdims=True) + EPS) * (128.0 ** -0.5))
        k = k * lax.rsqrt(jnp.sum(k * k, axis=-1, keepdims=True) + EPS)
        q_ref[h] = q.astype(BF16)
        k_ref[h] = k.astype(BF16)
        v_ref[h] = y[:, 2 * hw + h * 128:2 * hw + (h + 1) * 128].astype(BF16)
    s = s_ref[...]
    nh2 = 2 * n_heads
    beta = jax.nn.sigmoid(s)
    g = -jnp.exp(alog_ref[...]) * jax.nn.softplus(s + dtb_ref[...])
    r, c = _tri_masks(tr)
    incl_lo = (c <= r).astype(BF16)
    incl_up = (c >= r).astype(BF16)
    lane = lax.broadcasted_iota(jnp.int32, (tr, LANES), 1)
    fwd_lane = lane < nh2 + n_heads
    gc = jnp.where(fwd_lane, _dot_sel(incl_lo, g), _dot_sel(incl_up, g))
    for ch in range(nh2):
        bb_ref[ch] = jnp.broadcast_to(beta[:, ch:ch + 1], (tr, LANES))
        gcb_ref[ch] = jnp.broadcast_to(gc[:, nh2 + ch:nh2 + ch + 1], (tr, LANES))
    gct_ref[...] = jnp.transpose(gc)[nh2:2 * nh2, :]


def gdn_prep(cfg, z_big, z_small, conv_w, alog_row, dtb_row):
    tr = cfg.row_tile
    t = z_big.shape[0]
    nh = cfg.gdn_h
    hw = nh * 128
    n_tiles = t // tr
    main, prev, nxt = _halo_specs(tr, 3 * hw, 0, t)
    head_out = jax.ShapeDtypeStruct((nh, t, 128), BF16)
    head_spec = pl.BlockSpec((nh, tr, 128), lambda i: (0, i, 0))
    col_out = jax.ShapeDtypeStruct((2 * nh, t, LANES), F32)
    col_spec = pl.BlockSpec((2 * nh, tr, LANES), lambda i: (0, i, 0))
    return pl.pallas_call(
        functools.partial(_gdn_prep_kernel, n_heads=nh, n_lat_tiles=cfg.seq // tr),
        out_shape=(head_out, head_out, head_out, col_out, col_out,
                   jax.ShapeDtypeStruct((n_tiles, 2 * nh, tr), F32)),
        grid=(n_tiles,),
        in_specs=[main, prev, nxt,
                  pl.BlockSpec((tr, LANES), lambda i: (i, cfg.merge_rank // LANES)),
                  pl.BlockSpec((SHORT_CONV, 3 * hw), lambda i: (0, 0)),
                  pl.BlockSpec((1, LANES), lambda i: (0, 0)),
                  pl.BlockSpec((1, LANES), lambda i: (0, 0))],
        out_specs=(head_spec, head_spec, head_spec, col_spec, col_spec,
                   pl.BlockSpec((None, 2 * nh, tr), lambda i: (i, 0, 0))),
        compiler_params=_cparams(("parallel",)),
        name="gdn_prep",
    )(z_big, z_big, z_big, z_small, conv_w, alog_row, dtb_row)


def _unit_tri_inverses(n_mats, r, c):
    n = n_mats[0].shape[0]
    eye = (r == c).astype(F32)

    def same_block(b):
        sh = int(math.log2(b))
        return (r >> sh) == (c >> sh)

    blk = same_block(8)
    pfs = [jnp.where(blk, -m, 0.0) for m in n_mats]
    ps = [pf.astype(BF16) for pf in pfs]
    p2 = [_dot(p, p).astype(BF16) for p in ps]
    p4 = [_dot(x, x).astype(BF16) for x in p2]
    ts = [eye + pf for pf in pfs]
    ts = [t + _dot(t.astype(BF16), x) for t, x in zip(ts, p2)]
    ts = [t + _dot(t.astype(BF16), x) for t, x in zip(ts, p4)]
    b = 8
    while b < n:
        sel = jnp.logical_and(same_block(2 * b), jnp.logical_not(same_block(b)))
        offs = [jnp.where(sel, m, 0.0).astype(BF16) for m in n_mats]
        tbs = [t.astype(BF16) for t in ts]
        xs = [_dot(tb, off).astype(BF16) for tb, off in zip(tbs, offs)]
        ts = [t - _dot(x, tb) for t, x, tb in zip(ts, xs, tbs)]
        b *= 2
    return ts, eye


GDN_HEADS_PER_STEP = 4


def _gdn_chunk_kernel(q_ref, k_ref, v_ref, bf_ref, bb_ref, gf_ref, gb_ref, gct_ref,
                      uw_ref, aq_ref, qk_ref, gl_ref, *, n_heads):
    hps = q_ref.shape[0]
    h0 = pl.program_id(1) * hps
    tr = q_ref.shape[1]
    r, c = _tri_masks(tr)
    nt = (((1,), (1,)), ((), ()))
    masks = (((c <= r), (c < r)), ((c >= r), (c > r)))
    chains = [(j, d) for j in range(hps) for d in range(2)]
    k_b = [k_ref[j] for j in range(hps)]
    k_f = [x.astype(F32) for x in k_b]
    a_qk = [lax.dot_general(q_ref[j], k_b[j], nt, preferred_element_type=F32) for j in range(hps)]
    beta, gc, gam, kb, n_mats = {}, {}, {}, {}, []
    for j, d in chains:
        beta[j, d] = (bf_ref, bb_ref)[d][j]
        gc[j, d] = (gf_ref, gb_ref)[d][j]
        gc_row = gct_ref[pl.ds(d * n_heads + h0 + j, 1), :]
        gc_col = jnp.concatenate([gc[j, d]] * (tr // LANES), axis=1)
        gam[j, d] = jnp.exp(jnp.where(masks[d][0], gc_col - gc_row, -jnp.inf))
        kb[j, d] = k_f[j] * beta[j, d]
        a_kk = lax.dot_general(kb[j, d].astype(BF16), k_b[j], nt, preferred_element_type=F32)
        n_mats.append(jnp.where(masks[d][1], a_kk * gam[j, d], 0.0))
    t_invs, eye = _unit_tri_inverses(n_mats, r, c)
    for (j, d), t_inv in zip(chains, t_invs):
        e = jnp.exp(gc[j, d])
        rhs = jnp.concatenate([v_ref[j].astype(F32) * beta[j, d], kb[j, d] * e], axis=1)
        sol = rhs + _dot((t_inv - eye).astype(BF16), rhs.astype(BF16))
        g_last = gc[j, d][tr - 1:tr, :] if d == 0 else gc[j, d][0:1, :]
        uw_ref[d, j] = sol.astype(BF16)
        aq_ref[d, j] = (a_qk[j] * gam[j, d]).astype(BF16)
        qk_ref[d, j] = jnp.concatenate([q_ref[j].astype(F32) * e,
                                        k_f[j] * jnp.exp(g_last - gc[j, d])], axis=1).astype(BF16)
        gl_ref[d, j] = jnp.broadcast_to(jnp.exp(g_last), (8, LANES))


def gdn_chunks(cfg, qn, kn, vs, bb, gcb, gct):
    tr = cfg.row_tile
    nh, t, _ = qn.shape
    hps = min(GDN_HEADS_PER_STEP, nh)
    assert nh % hps == 0
    n_tiles = t // tr
    head = pl.BlockSpec((hps, tr, 128), lambda i, h: (h, i, 0))
    head_b = pl.BlockSpec((hps, tr, 128), lambda i, h: (h + nh // hps, i, 0))
    big = jax.ShapeDtypeStruct((2, nh, t, 2 * 128), BF16)
    aq = jax.ShapeDtypeStruct((2, nh, t, tr), BF16)
    return pl.pallas_call(
        functools.partial(_gdn_chunk_kernel, n_heads=nh),
        out_shape=(big, aq, big, jax.ShapeDtypeStruct((2, nh, n_tiles * 8, LANES), F32)),
        grid=(n_tiles, nh // hps),
        in_specs=[head, head, head, head, head_b, head, head_b,
                  pl.BlockSpec((None, 2 * nh, tr), lambda i, h: (i, 0, 0))],
        out_specs=(pl.BlockSpec((2, hps, tr, 256), lambda i, h: (0, h, i, 0)),
                   pl.BlockSpec((2, hps, tr, tr), lambda i, h: (0, h, i, 0)),
                   pl.BlockSpec((2, hps, tr, 256), lambda i, h: (0, h, i, 0)),
                   pl.BlockSpec((2, hps, 8, LANES), lambda i, h: (0, h, i, 0))),
        compiler_params=_cparams(("parallel", "parallel")),
        name="gdn_chunks",
    )(qn, kn, vs, bb, bb, gcb, gcb, gct)


def _gdn_scan_kernel(uwf_ref, aqf_ref, qkf_ref, glf_ref, uwb_ref, aqb_ref, qkb_ref, glb_ref,
                     of_ref, ob_ref, s_ref, *, n_heads):
    @pl.when(pl.program_id(0) == 0)
    def _():
        s_ref[...] = jnp.zeros_like(s_ref)

    tn = (((0,), (0,)), ((), ()))
    for d, (uw_ref, aq_ref, qk_ref, gl_ref, o_ref) in enumerate(
            ((uwf_ref, aqf_ref, qkf_ref, glf_ref, of_ref), (uwb_ref, aqb_ref, qkb_ref, glb_ref, ob_ref))):
        for h in range(n_heads):
            st = s_ref[d, h]
            st_b = st.astype(BF16)
            uw = uw_ref[h]
            qk = qk_ref[h]
            v_new = uw[:, 0:128].astype(F32) - _dot(uw[:, 128:256], st_b)
            v_new_b = v_new.astype(BF16)
            o_ref[:, h * 128:(h + 1) * 128] = (
                _dot(qk[:, 0:128], st_b) + _dot(aq_ref[h], v_new_b)).astype(o_ref.dtype)
            s_ref[d, h] = gl_ref[h][0:1, :] * st + lax.dot_general(
                qk[:, 128:256], v_new_b, tn, preferred_element_type=F32)


def gdn_scan(cfg, uw, aq, qk, gl):
    tr = cfg.row_tile
    _, nh, t, _ = uw.shape
    n_tiles = t // tr
    last = n_tiles - 1

    def fwd(s):
        return jnp.where(s == 0, last, s - 1)

    def bwd(s):
        return jnp.where(s == 0, last, last - s)

    def specs(d, order):
        return [pl.BlockSpec((None, nh, tr, 256), lambda s: (d, 0, order(s), 0)),
                pl.BlockSpec((None, nh, tr, tr), lambda s: (d, 0, order(s), 0)),
                pl.BlockSpec((None, nh, tr, 256), lambda s: (d, 0, order(s), 0)),
                pl.BlockSpec((None, nh, 8, LANES), lambda s: (d, 0, order(s), 0))]

    out = jax.ShapeDtypeStruct((t, nh * 128), BF16)
    return pl.pallas_call(
        functools.partial(_gdn_scan_kernel, n_heads=nh),
        out_shape=(out, out),
        grid=(n_tiles,),
        in_specs=specs(0, fwd) + specs(1, bwd),
        out_specs=(pl.BlockSpec((tr, nh * 128), lambda s: (fwd(s), 0)),
                   pl.BlockSpec((tr, nh * 128), lambda s: (bwd(s), 0))),
        scratch_shapes=[pltpu.VMEM((2, nh, 128, 128), F32)],
        compiler_params=_cparams(("arbitrary",)),
        name="gdn_scan",
    )(uw, aq, qk, gl, uw, aq, qk, gl)


def _head_norm_kernel(of_ref, ob_ref, gate_ref, w_ref, o_ref, *, n_heads, dv):
    for h in range(n_heads):
        sl = slice(h * dv, (h + 1) * dv)
        o = of_ref[:, sl].astype(F32) + ob_ref[:, sl].astype(F32)
        o = o * lax.rsqrt(jnp.mean(o * o, axis=-1, keepdims=True) + EPS) * w_ref[...]
        o_ref[:, sl] = (o * _silu(gate_ref[:, sl].astype(F32))).astype(o_ref.dtype)


def head_norm(o_f, o_b, z_big, gate_col_block, norm_w, n_heads, dv, tr, gate_row0=0):
    t, w = o_f.shape
    spec = pl.BlockSpec((tr, w), lambda i: (i, 0))
    rb0 = gate_row0 // tr
    return pl.pallas_call(
        functools.partial(_head_norm_kernel, n_heads=n_heads, dv=dv),
        out_shape=jax.ShapeDtypeStruct((t, w), BF16),
        grid=(t // tr,),
        in_specs=[spec, spec, pl.BlockSpec((tr, w), lambda i: (i + rb0, gate_col_block)),
                  pl.BlockSpec((1, dv), lambda i: (0, 0))],
        out_specs=spec,
        compiler_params=_cparams(("parallel",)),
        name="head_norm",
    )(o_f, o_b, z_big, norm_w.reshape(1, dv))


GLA_TILE = 128
GLA_LR_LANE0 = 32


def _gla_direction(d, qk_ref, v_ref, sm_ref, wup_ref, gb_ref, o_ref, st_ref, n_heads):
    tr = GLA_TILE
    qkw = n_heads * 128
    r, c = _tri_masks(tr)
    sh = int(math.log2(GLA_CHUNK))
    same = (r >> sh) == (c >> sh)
    mask = jnp.logical_and(same, (c <= r) if d == 0 else (c >= r))
    logit = _dot_3(sm_ref[...], wup_ref[:, d * qkw:(d + 1) * qkw]) + gb_ref[:, d * qkw:(d + 1) * qkw]
    g = jax.nn.log_sigmoid(logit) * (1.0 / GLA_GATE_NORM)
    b = _dot_sel(jnp.where(mask, 1.0, 0.0).astype(BF16), g)
    nt = (((1,), (1,)), ((), ()))
    tn = (((0,), (0,)), ((), ()))
    chunks = range(tr // GLA_CHUNK) if d == 0 else range(tr // GLA_CHUNK - 1, -1, -1)
    for h in range(n_heads):
        bq = b[:, h * 128:(h + 1) * 128]
        q = qk_ref[:, h * 128:(h + 1) * 128].astype(F32)
        k = qk_ref[:, qkw + h * 128:qkw + (h + 1) * 128].astype(F32)
        v_b = v_ref[:, h * 256:(h + 1) * 256].astype(BF16)
        qe = (q * jnp.exp(bq) * (128.0 ** -0.5)).astype(BF16)
        kinv = (k * jnp.exp(-bq)).astype(BF16)
        a = jnp.where(mask, lax.dot_general(qe, kinv, nt, preferred_element_type=F32), 0.0)
        o_intra = _dot(a.astype(BF16), v_b)
        for ci in chunks:
            lo = ci * GLA_CHUNK
            last = lo + GLA_CHUNK - 1 if d == 0 else lo
            b_last = bq[last:last + 1, :]
            kdec = (k[lo:lo + GLA_CHUNK] * jnp.exp(b_last - bq[lo:lo + GLA_CHUNK])).astype(BF16)
            ds = lax.dot_general(kdec, v_b[lo:lo + GLA_CHUNK], tn, preferred_element_type=F32)
            st = st_ref[d, h]
            o_ref[lo:lo + GLA_CHUNK, h * 256:(h + 1) * 256] = (
                o_intra[lo:lo + GLA_CHUNK] + _dot(qe[lo:lo + GLA_CHUNK], st.astype(BF16))).astype(o_ref.dtype)
            dec = jnp.transpose(jnp.broadcast_to(jnp.exp(b_last), (128, 128)))
            st_ref[d, h] = jnp.concatenate([dec, dec], axis=1) * st + ds


def _gla_kernel(qkf_ref, vf_ref, smf_ref, qkb_ref, vb_ref, smb_ref, wup_ref, gb_ref, s0_ref,
                of_ref, ob_ref, sfin_ref, st_ref, *, n_heads):
    @pl.when(pl.program_id(0) == 0)
    def _():
        st_ref[...] = s0_ref[...]

    _gla_direction(0, qkf_ref, vf_ref, smf_ref, wup_ref, gb_ref, of_ref, st_ref, n_heads)
    _gla_direction(1, qkb_ref, vb_ref, smb_ref, wup_ref, gb_ref, ob_ref, st_ref, n_heads)

    @pl.when(pl.program_id(0) == pl.num_programs(0) - 1)
    def _():
        sfin_ref[...] = st_ref[...]


def _gla_call(cfg, n_steps, arrays, spec_fn, out_rows_shape, out_spec_fn, wup, gbias, s0, name):
    nh = cfg.gla_h
    fwd = lambda s: s
    bwd = lambda s: n_steps - 1 - s
    full = lambda shape: pl.BlockSpec(shape, lambda s: (0,) * len(shape))
    z_view, zs_view = arrays
    st_shape = (2, nh, 128, 256)
    out = jax.ShapeDtypeStruct(out_rows_shape, BF16)
    return pl.pallas_call(
        functools.partial(_gla_kernel, n_heads=nh),
        out_shape=(out, out, jax.ShapeDtypeStruct(st_shape, F32)),
        grid=(n_steps,),
        in_specs=spec_fn(fwd) + spec_fn(bwd) + [full(wup.shape), full(gbias.shape), full(st_shape)],
        out_specs=(out_spec_fn(fwd), out_spec_fn(bwd), full(st_shape)),
        scratch_shapes=[pltpu.VMEM(st_shape, F32)],
        compiler_params=_cparams(("arbitrary",)),
        name=name,
    )(z_view, z_view, zs_view, z_view, z_view, zs_view, wup, gbias, s0)


def gla(cfg, z_cm, z_ctx, zs_cm, zs_ctx, wup, gbias):
    nh = cfg.gla_h
    qk2, vw = 2 * nh * 128, nh * 256
    w, rows, _ = z_cm.shape
    assert rows == GLA_TILE and cfg.ctx % GLA_TILE == 0 and qk2 == vw
    s0 = jnp.zeros((2, nh, 128, 256), F32)

    ctx_specs = lambda order: [
        pl.BlockSpec((GLA_TILE, qk2), lambda s: (order(s), 0)),
        pl.BlockSpec((GLA_TILE, vw), lambda s: (order(s), 1)),
        pl.BlockSpec((GLA_TILE, LANES), lambda s: (order(s), 0))]
    ctx_out = lambda order: pl.BlockSpec((GLA_TILE, vw), lambda s: (order(s), 0))
    ocf, ocb, s_ctx = _gla_call(cfg, cfg.ctx // GLA_TILE, (z_ctx, zs_ctx), ctx_specs,
                                (cfg.ctx, vw), ctx_out, wup, gbias, s0, "gla_ctx")

    lat_specs = lambda order: [
        pl.BlockSpec((None, GLA_TILE, qk2), lambda s: (order(s), 0, 0)),
        pl.BlockSpec((None, GLA_TILE, vw), lambda s: (order(s), 0, 1)),
        pl.BlockSpec((None, GLA_TILE, LANES), lambda s: (order(s), 0, 0))]
    lat_out = lambda order: pl.BlockSpec((None, GLA_TILE, vw), lambda s: (order(s), 0, 0))
    olf, olb, _ = _gla_call(cfg, w, (z_cm, zs_cm), lat_specs, (w, rows, vw), lat_out,
                            wup, gbias, s_ctx, "gla_lat")
    return (olf, olb), (ocf, ocb)


def column_major_perm(cfg, n_rows=16):
    w = cfg.grid_w
    p = np.zeros((n_rows * w, n_rows * w), np.float32)
    r, c = np.meshgrid(np.arange(n_rows), np.arange(w), indexing="ij")
    p[(c * n_rows + r).ravel(), (r * w + c).ravel()] = 1.0
    return jnp.asarray(p, BF16)


def _mm_nt_cm_kernel(a_ref, w_ref, p_ref, o_ref):
    nt = (((1,), (1,)), ((), ()))
    z = lax.dot_general(a_ref[...], w_ref[...], nt, preferred_element_type=F32).astype(BF16)
    o_ref[...] = _dot(p_ref[...], z).astype(o_ref.dtype).reshape(o_ref.shape)


def matmul_nt_cm(cfg, a, w_t, layer, perm, *, tn, name):
    k = a.shape[1]
    n = w_t.shape[1]
    tm = perm.shape[0]
    gw = cfg.grid_w
    rows = cfg.seq // gw
    return pl.pallas_call(
        _mm_nt_cm_kernel,
        out_shape=jax.ShapeDtypeStruct((gw, rows, n), BF16),
        grid=(n // tn, cfg.seq // tm),
        in_specs=[pl.BlockSpec((tm, k), lambda j, i: (i, 0)),
                  pl.BlockSpec((None, tn, k), lambda j, i: (layer, j, 0)),
                  pl.BlockSpec((tm, tm), lambda j, i: (0, 0))],
        out_specs=pl.BlockSpec((gw, tm // gw, tn), lambda j, i: (0, i, j)),
        compiler_params=_cparams(("parallel", "parallel")),
        name=name,
    )(a, w_t, perm)


def _head_norm_cm_kernel(of_ref, ob_ref, gate_ref, w_ref, p_ref, o_ref, *, n_heads, dv):
    tm = o_ref.shape[0]
    parts = []
    for h in range(n_heads):
        sl = slice(h * dv, (h + 1) * dv)
        o = (of_ref[:, :, sl].astype(F32) + ob_ref[:, :, sl].astype(F32)).reshape(tm, dv)
        o = o * lax.rsqrt(jnp.mean(o * o, axis=-1, keepdims=True) + EPS) * w_ref[...]
        parts.append((o * _silu(gate_ref[:, :, sl].astype(F32).reshape(tm, dv))).astype(BF16))
    nt = (((0,), (0,)), ((), ()))
    y = jnp.concatenate(parts, axis=1)
    o_ref[...] = lax.dot_general(p_ref[...], y, nt, preferred_element_type=F32).astype(o_ref.dtype)


def head_norm_cm(cfg, o_f, o_b, z_cm, gate_col_block, norm_w, n_heads, dv, perm):
    gw, rows, vw = o_f.shape
    tm = perm.shape[0]
    nr = tm // gw
    spec = pl.BlockSpec((gw, nr, vw), lambda i: (0, i, 0))
    return pl.pallas_call(
        functools.partial(_head_norm_cm_kernel, n_heads=n_heads, dv=dv),
        out_shape=jax.ShapeDtypeStruct((gw * rows, vw), BF16),
        grid=(rows // nr,),
        in_specs=[spec, spec, pl.BlockSpec((gw, nr, vw), lambda i: (0, i, gate_col_block)),
                  pl.BlockSpec((1, dv), lambda i: (0, 0)),
                  pl.BlockSpec((tm, tm), lambda i: (0, 0))],
        out_specs=pl.BlockSpec((tm, vw), lambda i: (i, 0)),
        compiler_params=_cparams(("parallel",)),
        name="head_norm_cm",
    )(o_f, o_b, z_cm, norm_w.reshape(1, dv), perm)


def _hy_prep_kernel(z_ref, zp_ref, zn_ref, w_ref, b_ref, v_ref, x1_ref, x2_ref, *, n_lat_tiles, hw):
    y = _conv3(z_ref, zp_ref, zn_ref, w_ref, n_lat_tiles) + b_ref[...]
    for p, o_ref in enumerate((v_ref, x1_ref, x2_ref)):
        o_ref[...] = y[:, p * hw:(p + 1) * hw]


def hyena_prep(cfg, z_hy, conv_w, conv_b):
    tr, hw = cfg.row_tile, cfg.hy_w
    t = z_hy.shape[0]
    main, prev, nxt = _halo_specs(tr, 3 * hw, 0, t)
    out = jax.ShapeDtypeStruct((t, hw), F32)
    ospec = pl.BlockSpec((tr, hw), lambda i: (i, 0))
    return pl.pallas_call(
        functools.partial(_hy_prep_kernel, n_lat_tiles=cfg.seq // tr, hw=hw),
        out_shape=(out, out, out),
        grid=(t // tr,),
        in_specs=[main, prev, nxt,
                  pl.BlockSpec((SHORT_CONV, 3 * hw), lambda i: (0, 0)),
                  pl.BlockSpec((1, 3 * hw), lambda i: (0, 0))],
        out_specs=(ospec, ospec, ospec),
        compiler_params=_cparams(("parallel",)),
        name="hyena_prep",
    )(z_hy, z_hy, z_hy, conv_w, conv_b.reshape(1, 3 * hw))


HY_HALF = LANES // 2


def _hy_taps_kernel(f_ref, w1_ref, b1_ref, w2_ref, b2_ref, fr_ref, w3a0_ref, w3b0_ref, w3a1_ref,
                    w3b1_ref, rate_ref, o0_ref, o1_ref, *, length, tt):
    th = tt // 2
    feats = f_ref[...]
    hid = jnp.sin(fr_ref[0:1, :] * (_dot_3(feats, w1_ref[...]) + b1_ref[...]))
    hid = jnp.sin(fr_ref[1:2, :] * (_dot_3(hid, w2_ref[...]) + b2_ref[...]))
    row = pl.program_id(0) * tt + lax.broadcasted_iota(jnp.int32, (th, 1), 0)
    for part, lane0 in ((0, 0), (1, HY_HALF)):
        n = row + part * th
        window = jnp.where(n == length, 0.0, jnp.exp(-feats[:, lane0:lane0 + 1] * rate_ref[...]))
        for w3_ref, o_ref in (((w3a0_ref, w3b0_ref)[part], o0_ref), ((w3a1_ref, w3b1_ref)[part], o1_ref)):
            o_ref[part * th:(part + 1) * th, :] = (_dot_3(hid, w3_ref[...]) * window).astype(o_ref.dtype)


def hyena_taps(feats, w1p, b1p, w2p, b2p, freqp, w3a, w3b, rates, length, hw):
    tt = min(512, length)
    n_half = length // tt
    out = jax.ShapeDtypeStruct((2 * length, hw), F32)
    ospec = pl.BlockSpec((tt, hw), lambda j: (j, 0))
    sq = pl.BlockSpec((LANES, LANES), lambda j: (0, 0))
    row = pl.BlockSpec((1, LANES), lambda j: (0, 0))
    w3spec = lambda order: pl.BlockSpec((LANES, hw), lambda j: (0, 2 * order + j // n_half))
    return pl.pallas_call(
        functools.partial(_hy_taps_kernel, length=length, tt=tt),
        out_shape=(out, out),
        grid=(2 * n_half,),
        in_specs=[pl.BlockSpec((tt // 2, LANES), lambda j: (j, 0)), sq, row, sq, row,
                  pl.BlockSpec((2, LANES), lambda j: (0, 0)),
                  w3spec(0), w3spec(0), w3spec(1), w3spec(1),
                  pl.BlockSpec((1, hw), lambda j: (0, 0))],
        out_specs=(ospec, ospec),
        compiler_params=_cparams(("parallel",)),
        name="hyena_taps",
    )(feats, w1p, b1p, w2p, b2p, freqp, w3a, w3b, w3a, w3b, rates)


FFT_BT = 8


def _taps_fft_a_kernel(f_ref, w1_ref, b1_ref, w2_ref, b2_ref, fr_ref, w3a0_ref, w3b0_ref, w3a1_ref,
                       w3b1_ref, rate_ref, l_ref, yr0_ref, yi0_ref, yr1_ref, yi1_ref):
    ph, bt, c = yr0_ref.shape
    feats = f_ref[...]
    hid = jnp.sin(fr_ref[0:1, :] * (_dot_3(feats, w1_ref[...]) + b1_ref[...]))
    hid = jnp.sin(fr_ref[1:2, :] * (_dot_3(hid, w2_ref[...]) + b2_ref[...]))
    row = lax.broadcasted_iota(jnp.int32, (feats.shape[0], 1), 0)
    is_row_l = jnp.logical_and(row == 0, pl.program_id(0) == 0)
    win_a = jnp.exp(-feats[:, 0:1] * rate_ref[...])
    win_b = jnp.where(is_row_l, 0.0, jnp.exp(-feats[:, HY_HALF:HY_HALF + 1] * rate_ref[...]))
    for w3a_ref, w3b_ref, yr_ref, yi_ref in ((w3a0_ref, w3b0_ref, yr0_ref, yi0_ref),
                                             (w3a1_ref, w3b1_ref, yr1_ref, yi1_ref)):
        taps = jnp.concatenate([_dot_3(hid, w3a_ref[...]) * win_a, _dot_3(hid, w3b_ref[...]) * win_b],
                               axis=0).astype(BF16)
        y = _dot(l_ref[...], taps)
        yr_ref[...] = y[0:ph * bt].reshape(ph, bt, c)
        yi_ref[...] = y[ph * bt:2 * ph * bt].reshape(ph, bt, c)


def taps_fft_a(feats_band, w1p, b1p, w2p, b2p, freqp, w3a, w3b, rates, lhs, ph, hw):
    nb, rows, _ = feats_band.shape
    out = jax.ShapeDtypeStruct((ph, LANES, hw), F32)
    ospec = pl.BlockSpec((ph, FFT_BT, hw), lambda j: (0, j, 0))
    sq = pl.BlockSpec((LANES, LANES), lambda j: (0, 0))
    row = pl.BlockSpec((1, LANES), lambda j: (0, 0))
    w3spec = lambda blk: pl.BlockSpec((LANES, hw), lambda j: (0, blk))
    return pl.pallas_call(
        _taps_fft_a_kernel,
        out_shape=(out, out, out, out),
        grid=(nb,),
        in_specs=[pl.BlockSpec((None, rows, LANES), lambda j: (j, 0, 0)), sq, row, sq, row,
                  pl.BlockSpec((2, LANES), lambda j: (0, 0)),
                  w3spec(0), w3spec(1), w3spec(2), w3spec(3),
                  pl.BlockSpec((1, hw), lambda j: (0, 0)),
                  pl.BlockSpec(lhs.shape, lambda j: (0, 0))],
        out_specs=(ospec, ospec, ospec, ospec),
        compiler_params=_cparams(("parallel",)),
        name="taps_fft_a",
    )(feats_band, w1p, b1p, w2p, b2p, freqp, w3a, w3b, w3a, w3b, rates, lhs)


def _fft_a_kernel(x_ref, l_ref, yr_ref, yi_ref):
    a, bt, c = x_ref.shape
    ph = yr_ref.shape[0]
    y = _dot(l_ref[...], x_ref[...].reshape(a * bt, c).astype(BF16))
    yr_ref[...] = y[0:ph * bt].reshape(ph, bt, c)
    yi_ref[...] = y[ph * bt:2 * ph * bt].reshape(ph, bt, c)


def fft_a(x, lhs, n_pages, ph):
    c = x.shape[1]
    x3 = x.reshape(x.shape[0] // LANES, LANES, c)
    out = jax.ShapeDtypeStruct((ph, LANES, c), F32)
    ospec = pl.BlockSpec((ph, FFT_BT, c), lambda j: (0, j, 0))
    return pl.pallas_call(
        _fft_a_kernel,
        out_shape=(out, out),
        grid=(LANES // FFT_BT,),
        in_specs=[pl.BlockSpec((n_pages, FFT_BT, c), lambda j: (0, j, 0)),
                  pl.BlockSpec(lhs.shape, lambda j: (0, 0))],
        out_specs=(ospec, ospec),
        compiler_params=_cparams(("parallel",)),
        name="fft_a",
    )(x3, lhs)


FFT_GM = 4


def _spec_mul_kernel(yr_ref, yi_ref, hr_ref, hi_ref, m2_ref, ma_ref, vr_ref, vi_ref):
    for g in range(FFT_GM):
        z = _dot(m2_ref[g], jnp.concatenate([yr_ref[g], yi_ref[g]], axis=0).astype(BF16))
        h = _dot(m2_ref[g], jnp.concatenate([hr_ref[g], hi_ref[g]], axis=0).astype(BF16))
        zr, zi = z[0:LANES], z[LANES:2 * LANES]
        hr, hi = h[0:LANES], h[LANES:2 * LANES]
        s = jnp.concatenate([zr * hr - zi * hi, zr * hi + zi * hr], axis=0).astype(BF16)
        v = _dot(ma_ref[g], s)
        vr_ref[g] = v[0:LANES]
        vi_ref[g] = v[LANES:2 * LANES]


def spectrum_multiply(yr, yi, hr, hi, m2, ma):
    ph, _, c = yr.shape
    out = jax.ShapeDtypeStruct((ph, LANES, c), F32)
    spec = pl.BlockSpec((FFT_GM, LANES, c), lambda j: (j, 0, 0))
    mspec = pl.BlockSpec((FFT_GM, 2 * LANES, 2 * LANES), lambda j: (j, 0, 0))
    return pl.pallas_call(
        _spec_mul_kernel,
        out_shape=(out, out),
        grid=(ph // FFT_GM,),
        in_specs=[spec, spec, spec, spec, mspec, mspec],
        out_specs=(spec, spec),
        compiler_params=_cparams(("parallel",)),
        name="spectrum_multiply",
    )(yr, yi, hr, hi, m2, ma)


def _ifft_b_kernel(vr_ref, vi_ref, l_ref, x_ref, u_ref, sk_ref, o_ref):
    ph, bt, c = vr_ref.shape
    s = jnp.concatenate([vr_ref[...].reshape(ph * bt, c), vi_ref[...].reshape(ph * bt, c)],
                        axis=0).astype(BF16)
    y = _dot(l_ref[...], s).reshape(o_ref.shape)
    o_ref[...] = x_ref[...] * (y + sk_ref[...] * u_ref[...])


def ifft_b_gate(vr, vi, lhs, gate_x, u, skip_row):
    ph, _, c = vr.shape
    a_out = lhs.shape[0] // FFT_BT
    x3 = gate_x.reshape(gate_x.shape[0] // LANES, LANES, c)
    u3 = u.reshape(u.shape[0] // LANES, LANES, c)
    vspec = pl.BlockSpec((ph, FFT_BT, c), lambda j: (0, j, 0))
    tspec = pl.BlockSpec((a_out, FFT_BT, c), lambda j: (0, j, 0))
    out = pl.pallas_call(
        _ifft_b_kernel,
        out_shape=jax.ShapeDtypeStruct((a_out, LANES, c), F32),
        grid=(LANES // FFT_BT,),
        in_specs=[vspec, vspec, pl.BlockSpec(lhs.shape, lambda j: (0, 0)), tspec, tspec,
                  pl.BlockSpec((1, 1, c), lambda j: (0, 0, 0))],
        out_specs=tspec,
        compiler_params=_cparams(("parallel",)),
        name="ifft_b",
    )(vr, vi, lhs, x3, u3, skip_row.reshape(1, 1, c))
    return out.reshape(a_out * LANES, c)


def _hy_ctx_kernel(v_ref, x1_ref, x2_ref, t0_ref, t1_ref, sk_ref, f_ref, g_ref, o_ref, *, n):
    def conv(u, taps_ref):
        us = _dot(f_ref[:, 0:n], u.astype(BF16))
        hs = _dot(f_ref[...], taps_ref[...].astype(BF16))
        ur, ui, hr, hi = us[0:2 * n], us[2 * n:4 * n], hs[0:2 * n], hs[2 * n:4 * n]
        prod = jnp.concatenate([ur * hr - ui * hi, ur * hi + ui * hr], axis=0).astype(BF16)
        return _dot(g_ref[...], prod)

    v = v_ref[...]
    y = x1_ref[...] * (conv(v, t0_ref) + sk_ref[0:1, :] * v)
    o_ref[...] = x2_ref[...] * (conv(y, t1_ref) + sk_ref[1:2, :] * y)


def hyena_ctx(cfg, vxx, taps0, taps1, skip, fmat, gmat):
    n, hw = cfg.ctx, cfg.hy_w
    cb = 256
    rb = cfg.seq // n
    part = pl.BlockSpec((n, cb), lambda j: (rb, j))
    tspec = pl.BlockSpec((2 * n, cb), lambda j: (0, j))
    return pl.pallas_call(
        functools.partial(_hy_ctx_kernel, n=n),
        out_shape=jax.ShapeDtypeStruct((n, hw), F32),
        grid=(hw // cb,),
        in_specs=[part, part, part, tspec, tspec,
                  pl.BlockSpec((2, cb), lambda j: (0, j)),
                  pl.BlockSpec((4 * n, 2 * n), lambda j: (0, 0)),
                  pl.BlockSpec((n, 4 * n), lambda j: (0, 0))],
        out_specs=pl.BlockSpec((n, cb), lambda j: (0, j)),
        compiler_params=_cparams(("parallel",)),
        name="hyena_ctx",
    )(vxx[0], vxx[1], vxx[2], taps0, taps1, skip, fmat, gmat)


class HyenaConsts(NamedTuple):
    feats: jax.Array
    rates: jax.Array
    ph: int
    la_data: jax.Array
    la_taps: jax.Array
    m2: jax.Array
    ma: jax.Array
    lb: jax.Array
    feats_ctx: jax.Array
    f_ctx: jax.Array
    g_ctx: jax.Array


def _feature_rows(length):
    n = np.arange(2 * length, dtype=np.float64)
    pos = np.where(n < length, n, 2 * length - n)
    bands = (HY_EMB - 1) // 2
    f = np.linspace(1e-4, bands - 1, bands)
    omega = (2.0 * math.pi / length) * pos
    feats = np.zeros((2 * length, HY_HALF), np.float64)
    feats[:, 0] = pos / (length - 1)
    feats[:, 1:1 + bands] = np.cos(omega[:, None] * f[None, :])
    feats[:, 1 + bands:1 + 2 * bands] = -np.sin(omega[:, None] * f[None, :])
    return feats


def _features(length):
    feats = _feature_rows(length)
    tt = min(512, length)
    tiles = feats.reshape(2 * length // tt, 2, tt // 2, HY_HALF)
    packed = np.concatenate([tiles[:, 0], tiles[:, 1]], axis=-1).reshape(length, LANES)
    return jnp.asarray(packed, F32)


def _features_bands(length):
    feats = _feature_rows(length)
    half_pages = length // LANES
    by_time = feats.reshape(2, half_pages, LANES // FFT_BT, FFT_BT, HY_HALF)
    packed = np.concatenate([by_time[0], by_time[1]], axis=-1)
    return jnp.asarray(packed.transpose(1, 0, 2, 3).reshape(LANES // FFT_BT, half_pages * FFT_BT, LANES), F32)


def hyena_consts(cfg):
    length, n, hw = cfg.seq, cfg.ctx, cfg.hy_w
    big_n = 2 * length
    p = big_n // LANES
    rates = np.abs(np.linspace(math.log(HY_TARGET) / HY_FAST_DECAY, math.log(HY_TARGET) / HY_SLOW_DECAY, hw))
    ph = p // 2 + 8
    k1 = np.arange(ph)
    kept = (k1 <= p // 2).astype(np.float64)
    a = np.arange(p)
    ang_a = 2.0 * math.pi * ((k1[:, None] * a[None, :]) % p) / p
    dft_a = np.concatenate([np.cos(ang_a), -np.sin(ang_a)], axis=0) * np.tile(kept, 2)[:, None]
    eye = np.eye(FFT_BT)
    la_taps = np.kron(dft_a, eye)
    la_data = np.kron(dft_a[:, :p // 2], eye)
    weight = kept * np.where((k1 == 0) | (k1 == p // 2), 1.0, 2.0) / big_n
    inv_a = (np.concatenate([np.cos(ang_a), -np.sin(ang_a)], axis=0) * np.tile(weight, 2)[:, None]).T
    lb = np.kron(inv_a[:p // 2], eye)
    b = np.arange(LANES)
    ang_b = 2.0 * math.pi * (((b[:, None] * b[None, :]) % LANES) / LANES)[None] \
        + 2.0 * math.pi * (k1[:, None, None] * b[None, None, :]) / big_n
    fr, fi = np.cos(ang_b), -np.sin(ang_b)
    m2 = np.concatenate([np.concatenate([fr, -fi], axis=2), np.concatenate([fi, fr], axis=2)], axis=1)
    gr, gi = fr.transpose(0, 2, 1), -fi.transpose(0, 2, 1)
    ma = np.concatenate([np.concatenate([gr, -gi], axis=2), np.concatenate([gi, gr], axis=2)], axis=1)
    kk = np.arange(2 * n)
    ac = 2.0 * math.pi * ((kk[:, None] * kk[None, :]) % (2 * n)) / (2 * n)
    f_ctx = np.concatenate([np.cos(ac), -np.sin(ac)], axis=0)
    g_ctx = np.concatenate([np.cos(ac), -np.sin(ac)], axis=1)[:n] / (2 * n)
    bf = lambda x: jnp.asarray(x, BF16)
    return HyenaConsts(_features_bands(length), jnp.asarray(rates[None, :], F32), ph, bf(la_data), bf(la_taps),
                       bf(m2), bf(ma), bf(lb), _features(n), bf(f_ctx), bf(g_ctx))


def hyena(cfg, hc, vxx, filt_w, skip, with_ctx=True):
    length, n, hw = cfg.seq, cfg.ctx, cfg.hy_w
    half = length // LANES
    y_lat = vxx[0]
    taps_y = taps_fft_a(hc.feats, *filt_w, hc.rates, hc.la_taps, hc.ph, hw)
    for order in range(2):
        yr, yi = fft_a(y_lat, hc.la_data, half, hc.ph)
        vr, vi = spectrum_multiply(yr, yi, taps_y[2 * order], taps_y[2 * order + 1], hc.m2, hc.ma)
        y_lat = ifft_b_gate(vr, vi, hc.lb, vxx[1 + order], y_lat, skip[order:order + 1])
    if not with_ctx:
        return y_lat, jnp.zeros((n, hw), F32)
    taps_c = hyena_taps(hc.feats_ctx, *filt_w, hc.rates, n, hw)
    y_ctx = hyena_ctx(cfg, vxx, taps_c[0], taps_c[1], skip, hc.f_ctx, hc.g_ctx)
    return y_lat, y_ctx


def expert_expand(n=N_EXPERTS):
    m = np.zeros((LANES, n * LANES), np.float32)
    for e in range(n):
        m[e, e * LANES:(e + 1) * LANES] = 1.0
    return jnp.asarray(m, BF16)


def prep_gla_gate(gate_up, gate_b):
    l, _, r, qk = gate_up.shape
    w = jnp.zeros((l, LANES, 2 * qk), F32)
    for z in range(2):
        w = w.at[:, GLA_LR_LANE0 + z * r:GLA_LR_LANE0 + (z + 1) * r, z * qk:(z + 1) * qk].set(gate_up[:, z])
    return w, gate_b.reshape(l, 1, 2 * qk)


def prep_hyena_filter(w1, b1, w2, b2, w3, freq):
    e, hdim = w1.shape
    assert hdim == HY_HALF
    h = HY_HALF
    w1p = jnp.zeros((LANES, LANES), F32).at[:e, :h].set(w1).at[h:h + e, h:].set(w1)
    w2p = jnp.zeros((LANES, LANES), F32).at[:h, :h].set(w2).at[h:, h:].set(w2)
    zero = jnp.zeros_like(w3)
    w3a = jnp.concatenate([w3, zero], axis=0)
    w3b = jnp.concatenate([zero, w3], axis=0)
    twice = lambda v: jnp.concatenate([v, v], axis=-1).reshape(-1, LANES)
    return w1p, twice(b1), w2p, twice(b2), twice(freq), w3a, w3b


def prep_w2(w2):
    l, e, f, d = w2.shape
    return w2.reshape(l, e * f, d).astype(BF16)


def kernel(x, c, ctx, c_ctx, norm1_g, norm2_g, w_mod, b_mod, w_in, gdn_conv, gdn_a_log, gdn_dt_bias, gdn_norm, gla_gate_up, gla_gate_b, gla_norm, hy_conv_w, hy_conv_b, hy_w1, hy_b1, hy_w2, hy_b2, hy_w3, hy_freq, hy_skip, merge_up, merge_b, w_branch, w_out, w_router, router_bias, moe_w1, moe_w3, moe_w2, final_g):
    cfg = Cfg(d=4096, seq=8192, ctx=256, grid_w=64, gdn_h=8, gla_h=4, hy_w=1024, merge_rank=256,
              d_expert=256, row_tile=256, mm_tm=768)
    return forward(cfg, x, c, ctx, c_ctx, norm1_g, norm2_g, w_mod, b_mod, w_in, gdn_conv, gdn_a_log,
                   gdn_dt_bias, gdn_norm, gla_gate_up, gla_gate_b, gla_norm, hy_conv_w, hy_conv_b, hy_w1,
                   hy_b1, hy_w2, hy_b2, hy_w3, hy_freq, hy_skip, merge_up, merge_b, w_branch, w_out,
                   w_router, router_bias, moe_w1, moe_w3, moe_w2, final_g)


def split_w_in(cfg, w_in):
    gw4 = 4 * cfg.gdn_w
    gdn_small = 4 * cfg.gdn_h
    gla0 = gw4 + gdn_small
    gla_w = 2 * cfg.gla_qk + 2 * cfg.gla_v
    hy0 = gla0 + gla_w + 32
    hy_w = 3 * cfg.hy_w
    w_t = jnp.swapaxes(w_in, 1, 2)
    l, _, k = w_t.shape
    zeros = lambda n: jnp.zeros((l, n, k), w_t.dtype)
    small = jnp.concatenate([w_t[:, hy0 + hy_w:hy0 + hy_w + cfg.merge_rank],
                             w_t[:, gw4:gla0], zeros(GLA_LR_LANE0 - gdn_small),
                             w_t[:, gla0 + gla_w:hy0], zeros(LANES - GLA_LR_LANE0 - 32)], axis=1)
    return cast_rows(w_t, 0, gw4), cast_rows(w_t, gla0, gla_w), cast_rows(w_t, hy0, hy_w), small


def forward(cfg, x, c, ctx, c_ctx, norm1_g, norm2_g, w_mod, b_mod, w_in, gdn_conv, gdn_a_log, gdn_dt_bias,
            gdn_norm, gla_gate_up, gla_gate_b, gla_norm, hy_conv_w, hy_conv_b, hy_w1, hy_b1, hy_w2, hy_b2,
            hy_w3, hy_freq, hy_skip, merge_up, merge_b, w_branch, w_out, w_router, router_bias, moe_w1,
            moe_w3, moe_w2, final_g):
    d, tr, tm = cfg.d, cfg.row_tile, cfg.mm_tm
    depth = w_in.shape[0]
    nh = cfg.gdn_h
    tn = min(1024, d)
    w_gdn, w_gla, w_hy, w_small = split_w_in(cfg, w_in)
    w1b, w3b = moe_w1.astype(BF16), moe_w3.astype(BF16)
    w2f = prep_w2(moe_w2)
    wb, mu, wo = w_branch.astype(BF16), merge_up.astype(BF16), w_out.astype(BF16)
    mb = merge_b.reshape(depth, 3, 1, d)
    wr_pad = jnp.zeros((d, LANES), F32).at[:, :N_EXPERTS].set(w_router)
    rb_col = router_bias.reshape(N_EXPERTS, 1)
    expand = expert_expand()
    wup, gbias = prep_gla_gate(gla_gate_up, gla_gate_b)
    lane0 = 2 * nh
    alog_rows = jnp.zeros((depth, 1, LANES), F32).at[:, 0, lane0:2 * lane0].set(gdn_a_log.reshape(depth, -1))
    dtb_rows = jnp.zeros((depth, 1, LANES), F32).at[:, 0, lane0:2 * lane0].set(gdn_dt_bias.reshape(depth, -1))
    hc = hyena_consts(cfg)
    perm = column_major_perm(cfg)
    gla_gate_blk = (2 * cfg.gla_qk + cfg.gla_v) // cfg.gla_v
    z_proj = functools.partial(matmul_nt, tm=tm, tn=tn, out_dtype=BF16)

    lat = jnp.concatenate([x[0], ctx[0]], axis=0)
    cvec = jnp.zeros((8, d), F32).at[0].set(c[0]).at[1].set(c_ctx)
    mods = modvec(cvec, w_mod, b_mod)

    for l in range(depth):
        with_ctx = l < depth - 1
        mod = mods[l]
        h = norm1(cfg, lat, norm1_g[l], mod)
        z_gdn = z_proj(h, w_gdn, l, name="w_in_gdn")
        z_gla = matmul_nt_cm(cfg, h, w_gla, l, perm, tn=tn, name="w_in_gla")
        z_gla_ctx = matmul_nt(h, w_gla, l, tm=cfg.ctx, tn=tn, out_dtype=BF16, name="w_in_gla_ctx",
                              row0=cfg.seq, n_rows=cfg.ctx)
        z_hy = z_proj(h, w_hy, l, name="w_in_hy")
        z_small = matmul_nt(h, w_small, l, tm=tm, tn=w_small.shape[1], name="w_in_small")
        qn, kn, vs, bb, gcb, gct = gdn_prep(cfg, z_gdn, z_small, gdn_conv[l], alog_rows[l], dtb_rows[l])
        o_f, o_b = gdn_scan(cfg, *gdn_chunks(cfg, qn, kn, vs, bb, gcb, gct))
        a_all = head_norm(o_f, o_b, z_gdn, 3, gdn_norm[l], nh, 128, tr)
        lr = z_small[:, cfg.merge_rank:cfg.merge_rank + LANES]
        lr_cm = jnp.swapaxes(lr[:cfg.seq].reshape(cfg.seq // cfg.grid_w, cfg.grid_w, LANES), 0, 1)
        (olf, olb), (ocf, ocb) = gla(cfg, z_gla, z_gla_ctx, lr_cm, lr[cfg.seq:], wup[l], gbias[l])
        b_lat = head_norm_cm(cfg, olf, olb, z_gla, gla_gate_blk, gla_norm[l], cfg.gla_h, 256, perm)
        b_ctx = head_norm(ocf, ocb, z_gla_ctx, gla_gate_blk, gla_norm[l], cfg.gla_h, 256, tr)
        b_all = jnp.concatenate([b_lat, b_ctx], axis=0)
        vxx = hyena_prep(cfg, z_hy, hy_conv_w[l], hy_conv_b[l])
        filt_w = prep_hyena_filter(hy_w1[l], hy_b1[l], hy_w2[l], hy_b2[l], hy_w3[l], hy_freq[l])
        c_lat, c_ctx_out = hyena(cfg, hc, vxx, filt_w, hy_skip[l], with_ctx)
        c_all = jnp.concatenate([c_lat, c_ctx_out], axis=0).astype(BF16)
        s = merge_branches(cfg, (a_all, b_all, c_all), z_small, wb, mu, mb, l, tn=tn)
        lat = matmul_resid(cfg, s, wo, l, lat, mod, 2, tn=tn, name="w_out")
        h2, gate_rep = norm2_route(cfg, lat, norm2_g[l], mod, wr_pad, rb_col, expand)
        act = matmul_moe_act(cfg, h2, w1b, w3b, l, gate_rep)
        lat = matmul_resid(cfg, act, w2f, l, lat, mod, 5, tn=tn, name="moe_down")
    return final_norm(lat, final_g, cfg.seq, tr)[None]
```

```python
import functools
import math
from typing import NamedTuple

import numpy as np
import jax
import jax.numpy as jnp
from jax import lax
from jax.experimental import pallas as pl
from jax.experimental.pallas import tpu as pltpu

F32 = jnp.float32
BF16 = jnp.bfloat16

EPS = 1e-6
LANES = 128
V7X_VMEM_BYTES = 64 * 1024 * 1024
VMEM_LIMIT = (V7X_VMEM_BYTES * 13) // 16
SHORT_CONV = 3
GLA_CHUNK = 64
GLA_GATE_NORM = 16.0
N_EXPERTS = 16
N_GROUPS = 4
HY_EMB = 33
HY_FAST_DECAY = 0.3
HY_SLOW_DECAY = 1.5
HY_TARGET = 1e-2


class Cfg(NamedTuple):
    d: int
    seq: int
    ctx: int
    grid_w: int
    gdn_h: int
    gla_h: int
    hy_w: int
    merge_rank: int
    d_expert: int
    row_tile: int
    mm_tm: int

    @property
    def t(self):
        return self.seq + self.ctx

    @property
    def gdn_w(self):
        return self.gdn_h * 128

    @property
    def gla_qk(self):
        return self.gla_h * 128

    @property
    def gla_v(self):
        return self.gla_h * 256


def _cparams(sem):
    return pltpu.CompilerParams(dimension_semantics=sem, vmem_limit_bytes=VMEM_LIMIT)


def _split3(x):
    hi = x.astype(BF16)
    r1 = x - hi.astype(F32)
    mid = r1.astype(BF16)
    lo = (r1 - mid.astype(F32)).astype(BF16)
    return hi, mid, lo


def _dot(a, b):
    return jnp.dot(a, b, preferred_element_type=F32)


def _dot_sel(sel_bf16, x):
    hi, mid, lo = _split3(x)
    return _dot(sel_bf16, hi) + _dot(sel_bf16, mid) + _dot(sel_bf16, lo)


def _dot_x_sel(x, sel_bf16):
    hi, mid, lo = _split3(x)
    return _dot(hi, sel_bf16) + _dot(mid, sel_bf16) + _dot(lo, sel_bf16)


def _dot_hi(a, b):
    a1, a2, a3 = _split3(a)
    b1, b2, b3 = _split3(b)
    return (_dot(a1, b1) + (_dot(a1, b2) + _dot(a2, b1))
            + (_dot(a2, b2) + _dot(a1, b3) + _dot(a3, b1)))


def _dot_3(a, b):
    a1 = a.astype(BF16)
    a2 = (a - a1.astype(F32)).astype(BF16)
    b1 = b.astype(BF16)
    b2 = (b - b1.astype(F32)).astype(BF16)
    return _dot(a1, b1) + (_dot(a1, b2) + _dot(a2, b1))


def _silu(x):
    return x * jax.nn.sigmoid(x)


def _modvec_kernel(x_ref, w_ref, b_ref, o_ref):
    o_ref[...] = _dot(_silu(x_ref[...]), w_ref[...]) + b_ref[...]


def modvec(cvec, w_mod, b_mod, tn=512):
    depth, d, n = w_mod.shape
    return pl.pallas_call(
        _modvec_kernel,
        out_shape=jax.ShapeDtypeStruct((depth, 8, n), F32),
        grid=(depth, n // tn),
        in_specs=[pl.BlockSpec((8, d), lambda l, j: (0, 0)),
                  pl.BlockSpec((None, d, tn), lambda l, j: (l, 0, j)),
                  pl.BlockSpec((None, 1, tn), lambda l, j: (l, 0, j))],
        out_specs=pl.BlockSpec((None, 8, tn), lambda l, j: (l, 0, j)),
        compiler_params=_cparams(("parallel", "parallel")),
        name="modvec",
    )(cvec, w_mod, b_mod.reshape(depth, 1, n))


def _mod_row(mod_ref, is_ctx, idx, d):
    return mod_ref[pl.ds(is_ctx, 1), idx * d:(idx + 1) * d]


def _norm_mod(x, gain, shift, scale):
    y = x * lax.rsqrt(jnp.mean(x * x, axis=-1, keepdims=True) + EPS)
    return (y * gain) * (1.0 + scale) + shift


def _norm1_kernel(x_ref, g_ref, mod_ref, o_ref, *, d, n_lat_tiles):
    is_ctx = (pl.program_id(0) >= n_lat_tiles).astype(jnp.int32)
    h = _norm_mod(x_ref[...], g_ref[...], _mod_row(mod_ref, is_ctx, 0, d),
                  _mod_row(mod_ref, is_ctx, 1, d))
    o_ref[...] = h.astype(BF16)


def norm1(cfg, x, gain, mod):
    tr = cfg.row_tile
    t, d = x.shape
    return pl.pallas_call(
        functools.partial(_norm1_kernel, d=d, n_lat_tiles=cfg.seq // tr),
        out_shape=jax.ShapeDtypeStruct((t, d), BF16),
        grid=(t // tr,),
        in_specs=[pl.BlockSpec((tr, d), lambda i: (i, 0)),
                  pl.BlockSpec((1, d), lambda i: (0, 0)),
                  pl.BlockSpec((8, 6 * d), lambda i: (0, 0))],
        out_specs=pl.BlockSpec((tr, d), lambda i: (i, 0)),
        compiler_params=_cparams(("parallel",)),
        name="norm1",
    )(x, gain.reshape(1, d), mod)


def _route(sel_t, sc_t):
    per = N_EXPERTS // N_GROUPS
    grp_score = []
    for g in range(N_GROUPS):
        v = sel_t[g * per:(g + 1) * per]
        best = None
        for a in range(per):
            for b in range(a + 1, per):
                s = v[a] + v[b]
                best = s if best is None else jnp.maximum(best, s)
        grp_score.append(best)
    best_s, best_g = grp_score[0], jnp.zeros_like(grp_score[0])
    for g in range(1, N_GROUPS):
        better = grp_score[g] > best_s
        best_s = jnp.where(better, grp_score[g], best_s)
        best_g = jnp.where(better, float(g), best_g)
    picked = []
    for e in range(N_EXPERTS):
        g, i = divmod(e, per)
        rank = jnp.zeros_like(best_s)
        for j in range(per):
            if j == i:
                continue
            o = sel_t[g * per + j]
            ahead = (o >= sel_t[e]) if j < i else (o > sel_t[e])
            rank = rank + ahead.astype(F32)
        picked.append(jnp.where((best_g == float(g)) & (rank < 2.0), sc_t[e], 0.0))
    den = picked[0]
    for e in range(1, N_EXPERTS):
        den = den + picked[e]
    inv = 1.0 / den
    return [p * inv for p in picked], best_g


def _norm2_kernel(x_ref, g_ref, mod_ref, wr_ref, rb_ref, ex_ref, o_ref, gate_ref, *, d,
                  n_lat_tiles):
    is_ctx = (pl.program_id(0) >= n_lat_tiles).astype(jnp.int32)
    h = _norm_mod(x_ref[...], g_ref[...], _mod_row(mod_ref, is_ctx, 3, d),
                  _mod_row(mod_ref, is_ctx, 4, d))
    o_ref[...] = h.astype(BF16)
    logits = _dot_3(h, wr_ref[...])
    lt = jnp.transpose(logits)
    sc = jax.nn.sigmoid(lt[0:N_EXPERTS, :])
    sel = sc + rb_ref[...]
    gate_rows, _ = _route([sel[e:e + 1, :] for e in range(N_EXPERTS)],
                          [sc[e:e + 1, :] for e in range(N_EXPERTS)])
    rows = lax.broadcasted_iota(jnp.int32, lt.shape, 0)
    gt = jnp.zeros(lt.shape, F32)
    for e in range(N_EXPERTS):
        gt = jnp.where(rows == e, gate_rows[e], gt)
    gate = jnp.transpose(gt)
    gate_ref[...] = _dot_x_sel(gate, ex_ref[...])


def norm2_route(cfg, x, gain, mod, w_router_pad, rbias_col, expand):
    tr = cfg.row_tile
    t, d = x.shape
    return pl.pallas_call(
        functools.partial(_norm2_kernel, d=d, n_lat_tiles=cfg.seq // tr),
        out_shape=(jax.ShapeDtypeStruct((t, d), BF16),
                   jax.ShapeDtypeStruct((t, N_EXPERTS * LANES), F32)),
        grid=(t // tr,),
        in_specs=[pl.BlockSpec((tr, d), lambda i: (i, 0)),
                  pl.BlockSpec((1, d), lambda i: (0, 0)),
                  pl.BlockSpec((8, 6 * d), lambda i: (0, 0)),
                  pl.BlockSpec((d, LANES), lambda i: (0, 0)),
                  pl.BlockSpec((N_EXPERTS, 1), lambda i: (0, 0)),
                  pl.BlockSpec((LANES, N_EXPERTS * LANES), lambda i: (0, 0))],
        out_specs=(pl.BlockSpec((tr, d), lambda i: (i, 0)),
                   pl.BlockSpec((tr, N_EXPERTS * LANES), lambda i: (i, 0))),
        compiler_params=_cparams(("parallel",)),
        name="norm2_route",
    )(x, gain.reshape(1, d), mod, w_router_pad, rbias_col, expand)


def _final_norm_kernel(x_ref, g_ref, o_ref):
    x = x_ref[...]
    o_ref[...] = (x * lax.rsqrt(jnp.mean(x * x, axis=-1, keepdims=True) + EPS)) * g_ref[...]


def final_norm(x, gain, n_rows, tr):
    d = x.shape[1]
    return pl.pallas_call(
        _final_norm_kernel,
        out_shape=jax.ShapeDtypeStruct((n_rows, d), F32),
        grid=(n_rows // tr,),
        in_specs=[pl.BlockSpec((tr, d), lambda i: (i, 0)),
                  pl.BlockSpec((1, d), lambda i: (0, 0))],
        out_specs=pl.BlockSpec((tr, d), lambda i: (i, 0)),
        compiler_params=_cparams(("parallel",)),
        name="final_norm",
    )(x, gain.reshape(1, d))


def _mm_plain_kernel(a_ref, w_ref, o_ref):
    o_ref[...] = _dot(a_ref[...], w_ref[...].astype(BF16)).astype(o_ref.dtype)


def matmul_plain(a, w, layer, *, tm, tn, n_cols=None, col0=0, out_dtype=F32, name="mm"):
    t, k = a.shape
    n_cols = w.shape[2] - col0 if n_cols is None else n_cols
    off = col0 // tn
    return pl.pallas_call(
        _mm_plain_kernel,
        out_shape=jax.ShapeDtypeStruct((t, n_cols), out_dtype),
        grid=(n_cols // tn, t // tm),
        in_specs=[pl.BlockSpec((tm, k), lambda j, i: (i, 0)),
                  pl.BlockSpec((None, k, tn), lambda j, i: (layer, 0, j + off))],
        out_specs=pl.BlockSpec((tm, tn), lambda j, i: (i, j)),
        compiler_params=_cparams(("parallel", "parallel")),
        name=name,
    )(a, w)


def _mm_nt_kernel(a_ref, w_ref, o_ref):
    nt = (((1,), (1,)), ((), ()))
    o_ref[...] = lax.dot_general(a_ref[...], w_ref[...].astype(BF16), nt,
                                 preferred_element_type=F32).astype(o_ref.dtype)


def matmul_nt(a, w_t, layer, *, tm, tn, out_dtype=F32, name="mm_nt", row0=0, n_rows=None):
    k = a.shape[1]
    t = a.shape[0] if n_rows is None else n_rows
    n = w_t.shape[1]
    rb0 = row0 // tm
    return pl.pallas_call(
        _mm_nt_kernel,
        out_shape=jax.ShapeDtypeStruct((t, n), out_dtype),
        grid=(n // tn, t // tm),
        in_specs=[pl.BlockSpec((tm, k), lambda j, i: (i + rb0, 0)),
                  pl.BlockSpec((None, tn, k), lambda j, i: (layer, j, 0))],
        out_specs=pl.BlockSpec((tm, tn), lambda j, i: (i, j)),
        compiler_params=_cparams(("parallel", "parallel")),
        name=name,
    )(a, w_t)


def _cast_rows_kernel(w_ref, o_ref):
    o_ref[...] = w_ref[...].astype(o_ref.dtype)


def cast_rows(w_t, row0, n_rows):
    l, _, k = w_t.shape
    tr = math.gcd(row0, n_rows) if row0 else n_rows
    while tr > 512 and tr % 2 == 0:
        tr //= 2
    assert tr % 16 == 0
    off = row0 // tr
    return pl.pallas_call(
        _cast_rows_kernel,
        out_shape=jax.ShapeDtypeStruct((l, n_rows, k), BF16),
        grid=(l, n_rows // tr),
        in_specs=[pl.BlockSpec((None, tr, k), lambda i, j: (i, j + off, 0))],
        out_specs=pl.BlockSpec((None, tr, k), lambda i, j: (i, j, 0)),
        compiler_params=_cparams(("parallel", "parallel")),
        name="cast_rows",
    )(w_t)


def _mm_resid_kernel(a_ref, w_ref, r_ref, mod_ref, o_ref, *, tm, n_lat):
    row = pl.program_id(1) * tm + lax.broadcasted_iota(jnp.int32, (tm, 1), 0)
    gate = jnp.where(row < n_lat, mod_ref[0:1, :], mod_ref[1:2, :])
    o_ref[...] = r_ref[...] + gate * _dot(a_ref[...], w_ref[...])


def matmul_resid(cfg, a, w, layer, resid, mod, idx, *, tn, name):
    t, k = a.shape
    d = w.shape[2]
    tm = cfg.mm_tm
    return pl.pallas_call(
        functools.partial(_mm_resid_kernel, tm=tm, n_lat=cfg.seq),
        out_shape=jax.ShapeDtypeStruct((t, d), F32),
        grid=(d // tn, t // tm),
        in_specs=[pl.BlockSpec((tm, k), lambda j, i: (i, 0)),
                  pl.BlockSpec((None, k, tn), lambda j, i: (layer, 0, j)),
                  pl.BlockSpec((tm, tn), lambda j, i: (i, j)),
                  pl.BlockSpec((8, tn), lambda j, i: (0, idx * (d // tn) + j))],
        out_specs=pl.BlockSpec((tm, tn), lambda j, i: (i, j)),
        input_output_aliases={2: 0},
        compiler_params=_cparams(("parallel", "parallel")),
        name=name,
    )(a, w, resid, mod)


def _mm_moe_act_kernel(a_ref, w1_ref, w3_ref, g_ref, o_ref, *, de, n_e):
    a = a_ref[...]
    for e in range(n_e):
        up = _dot(a, w1_ref[e])
        lin = _dot(a, w3_ref[e])
        g = g_ref[:, e * LANES:(e + 1) * LANES]
        g = jnp.concatenate([g] * (de // LANES), axis=1)
        o_ref[:, e * de:(e + 1) * de] = (_silu(up) * lin * g).astype(o_ref.dtype)


def matmul_moe_act(cfg, h, w1, w3, layer, gate_rep, *, n_e=2):
    t, k = h.shape
    de = cfg.d_expert
    tm = cfg.mm_tm
    wspec = pl.BlockSpec((None, n_e, k, de), lambda j, i: (layer, j, 0, 0))
    return pl.pallas_call(
        functools.partial(_mm_moe_act_kernel, de=de, n_e=n_e),
        out_shape=jax.ShapeDtypeStruct((t, N_EXPERTS * de), BF16),
        grid=(N_EXPERTS // n_e, t // tm),
        in_specs=[pl.BlockSpec((tm, k), lambda j, i: (i, 0)), wspec, wspec,
                  pl.BlockSpec((tm, n_e * LANES), lambda j, i: (i, j))],
        out_specs=pl.BlockSpec((tm, n_e * de), lambda j, i: (i, j)),
        compiler_params=_cparams(("parallel", "parallel")),
        name="moe_up",
    )(h, w1, w3, gate_rep)


def _merge_kernel(a0_ref, a1_ref, a2_ref, zg_ref, wb_ref, mu_ref, mb_ref, o_ref):
    zg = zg_ref[...].astype(BF16)
    acc = None
    for n, a_ref in enumerate((a0_ref, a1_ref, a2_ref)):
        y = _dot(a_ref[...], wb_ref[n])
        gate = jax.nn.sigmoid(_dot(zg, mu_ref[n]) + mb_ref[n])
        acc = gate * y if acc is None else acc + gate * y
    o_ref[...] = acc.astype(o_ref.dtype)


def merge_branches(cfg, outs, z_small, w_branch, merge_up, merge_b, layer, *, tn):
    t, bw = outs[0].shape
    d = w_branch.shape[3]
    r = cfg.merge_rank
    tm = cfg.mm_tm
    a_spec = pl.BlockSpec((tm, bw), lambda j, i: (i, 0))
    return pl.pallas_call(
        _merge_kernel,
        out_shape=jax.ShapeDtypeStruct((t, d), BF16),
        grid=(d // tn, t // tm),
        in_specs=[a_spec, a_spec, a_spec,
                  pl.BlockSpec((tm, r), lambda j, i: (i, 0)),
                  pl.BlockSpec((None, 3, bw, tn), lambda j, i: (layer, 0, 0, j)),
                  pl.BlockSpec((None, 3, r, tn), lambda j, i: (layer, 0, 0, j)),
                  pl.BlockSpec((None, 3, 1, tn), lambda j, i: (layer, 0, 0, j))],
        out_specs=pl.BlockSpec((tm, tn), lambda j, i: (i, j)),
        compiler_params=_cparams(("parallel", "parallel")),
        name="merge",
    )(outs[0], outs[1], outs[2], z_small, w_branch, merge_up, merge_b)


HALO_ROWS = 16


def _halo_specs(tr, width, col_block, n_rows):
    rb = tr // HALO_ROWS
    last = n_rows // HALO_ROWS - 1
    main = pl.BlockSpec((tr, width), lambda i: (i, col_block))
    prev = pl.BlockSpec((HALO_ROWS, width), lambda i: (jnp.maximum(i * rb - 1, 0), col_block))
    nxt = pl.BlockSpec((HALO_ROWS, width), lambda i: (jnp.minimum((i + 1) * rb, last), col_block))
    return main, prev, nxt


def _conv3(z_ref, prev_ref, next_ref, w_ref, n_lat_tiles):
    i = pl.program_id(0)
    x = z_ref[...].astype(F32)
    tr = x.shape[0]
    has_prev = jnp.logical_and(i != 0, i != n_lat_tiles).astype(F32)
    has_next = jnp.logical_and(i != n_lat_tiles - 1, i != n_lat_tiles).astype(F32)
    row = lax.broadcasted_iota(jnp.int32, (tr, 1), 0)
    halo_prev = prev_ref[HALO_ROWS - 1:HALO_ROWS, :].astype(F32) * has_prev
    halo_next = next_ref[0:1, :].astype(F32) * has_next
    x_prev = jnp.where(row == 0, halo_prev, pltpu.roll(x, 1, axis=0))
    x_next = jnp.where(row == tr - 1, halo_next, pltpu.roll(x, tr - 1, axis=0))
    return x_prev * w_ref[0:1, :] + x * w_ref[1:2, :] + x_next * w_ref[2:3, :]


def _tri_masks(n):
    r = lax.broadcasted_iota(jnp.int32, (n, n), 0)
    c = lax.broadcasted_iota(jnp.int32, (n, n), 1)
    return r, c


def _gdn_prep_kernel(z_ref, zp_ref, zn_ref, s_ref, cw_ref, alog_ref, dtb_ref,
                     q_ref, k_ref, v_ref, bb_ref, gcb_ref, gct_ref, *, n_heads, n_lat_tiles):
    tr = z_ref.shape[0]
    hw = n_heads * 128
    y = _silu(_conv3(z_ref, zp_ref, zn_ref, cw_ref, n_lat_tiles))
    for h in range(n_heads):
        q = y[:, h * 128:(h + 1) * 128]
        k = y[:, hw + h * 128:hw + (h + 1) * 128]
        q = q * (lax.rsqrt(jnp.sum(q * q, axis=-1, keepdims=True) + EPS) * (128.0 ** -0.5))
        k = k * lax.rsqrt(jnp.sum(k * k, axis=-1, keepdims=True) + EPS)
        q_ref[h] = q.astype(BF16)
        k_ref[h] = k.astype(BF16)
        v_ref[h] = y[:, 2 * hw + h * 128:2 * hw + (h + 1) * 128].astype(BF16)
    s = s_ref[...]
    nh2 = 2 * n_heads
    beta = jax.nn.sigmoid(s)
    g = -jnp.exp(alog_ref[...]) * jax.nn.softplus(s + dtb_ref[...])
    r, c = _tri_masks(tr)
    incl_lo = (c <= r).astype(BF16)
    incl_up = (c >= r).astype(BF16)
    lane = lax.broadcasted_iota(jnp.int32, (tr, LANES), 1)
    fwd_lane = lane < nh2 + n_heads
    gc = jnp.where(fwd_lane, _dot_sel(incl_lo, g), _dot_sel(incl_up, g))
    for ch in range(nh2):
        bb_ref[ch] = jnp.broadcast_to(beta[:, ch:ch + 1], (tr, LANES))
        gcb_ref[ch] = jnp.broadcast_to(gc[:, nh2 + ch:nh2 + ch + 1], (tr, LANES))
    gct_ref[...] = jnp.transpose(gc)[nh2:2 * nh2, :]


def gdn_prep(cfg, z_big, z_small, conv_w, alog_row, dtb_row):
    tr = cfg.row_tile
    t = z_big.shape[0]
    nh = cfg.gdn_h
    hw = nh * 128
    n_tiles = t // tr
    main, prev, nxt = _halo_specs(tr, 3 * hw, 0, t)
    head_out = jax.ShapeDtypeStruct((nh, t, 128), BF16)
    head_spec = pl.BlockSpec((nh, tr, 128), lambda i: (0, i, 0))
    col_out = jax.ShapeDtypeStruct((2 * nh, t, LANES), F32)
    col_spec = pl.BlockSpec((2 * nh, tr, LANES), lambda i: (0, i, 0))
    return pl.pallas_call(
        functools.partial(_gdn_prep_kernel, n_heads=nh, n_lat_tiles=cfg.seq // tr),
        out_shape=(head_out, head_out, head_out, col_out, col_out,
                   jax.ShapeDtypeStruct((n_tiles, 2 * nh, tr), F32)),
        grid=(n_tiles,),
        in_specs=[main, prev, nxt,
                  pl.BlockSpec((tr, LANES), lambda i: (i, cfg.merge_rank // LANES)),
                  pl.BlockSpec((SHORT_CONV, 3 * hw), lambda i: (0, 0)),
                  pl.BlockSpec((1, LANES), lambda i: (0, 0)),
                  pl.BlockSpec((1, LANES), lambda i: (0, 0))],
        out_specs=(head_spec, head_spec, head_spec, col_spec, col_spec,
                   pl.BlockSpec((None, 2 * nh, tr), lambda i: (i, 0, 0))),
        compiler_params=_cparams(("parallel",)),
        name="gdn_prep",
    )(z_big, z_big, z_big, z_small, conv_w, alog_row, dtb_row)


def _unit_tri_inverses(n_mats, r, c):
    n = n_mats[0].shape[0]
    eye = (r == c).astype(F32)

    def same_block(b):
        sh = int(math.log2(b))
        return (r >> sh) == (c >> sh)

    blk = same_block(8)
    pfs = [jnp.where(blk, -m, 0.0) for m in n_mats]
    ps = [pf.astype(BF16) for pf in pfs]
    p2 = [_dot(p, p).astype(BF16) for p in ps]
    p4 = [_dot(x, x).astype(BF16) for x in p2]
    ts = [eye + pf for pf in pfs]
    ts = [t + _dot(t.astype(BF16), x) for t, x in zip(ts, p2)]
    ts = [t + _dot(t.astype(BF16), x) for t, x in zip(ts, p4)]
    b = 8
    while b < n:
        sel = jnp.logical_and(same_block(2 * b), jnp.logical_not(same_block(b)))
        offs = [jnp.where(sel, m, 0.0).astype(BF16) for m in n_mats]
        tbs = [t.astype(BF16) for t in ts]
        xs = [_dot(tb, off).astype(BF16) for tb, off in zip(tbs, offs)]
        ts = [t - _dot(x, tb) for t, x, tb in zip(ts, xs, tbs)]
        b *= 2
    return ts, eye


GDN_HEADS_PER_STEP = 4


def _gdn_chunk_kernel(q_ref, k_ref, v_ref, bf_ref, bb_ref, gf_ref, gb_ref, gct_ref,
                      uw_ref, aq_ref, qk_ref, gl_ref, *, n_heads):
    hps = q_ref.shape[0]
    h0 = pl.program_id(1) * hps
    tr = q_ref.shape[1]
    r, c = _tri_masks(tr)
    nt = (((1,), (1,)), ((), ()))
    masks = (((c <= r), (c < r)), ((c >= r), (c > r)))
    chains = [(j, d) for j in range(hps) for d in range(2)]
    k_b = [k_ref[j] for j in range(hps)]
    k_f = [x.astype(F32) for x in k_b]
    a_qk = [lax.dot_general(q_ref[j], k_b[j], nt, preferred_element_type=F32) for j in range(hps)]
    beta, gc, gam, kb, n_mats = {}, {}, {}, {}, []
    for j, d in chains:
        beta[j, d] = (bf_ref, bb_ref)[d][j]
        gc[j, d] = (gf_ref, gb_ref)[d][j]
        gc_row = gct_ref[pl.ds(d * n_heads + h0 + j, 1), :]
        gc_col = jnp.concatenate([gc[j, d]] * (tr // LANES), axis=1)
        gam[j, d] = jnp.exp(jnp.where(masks[d][0], gc_col - gc_row, -jnp.inf))
        kb[j, d] = k_f[j] * beta[j, d]
        a_kk = lax.dot_general(kb[j, d].astype(BF16), k_b[j], nt, preferred_element_type=F32)
        n_mats.append(jnp.where(masks[d][1], a_kk * gam[j, d], 0.0))
    t_invs, eye = _unit_tri_inverses(n_mats, r, c)
    for (j, d), t_inv in zip(chains, t_invs):
        e = jnp.exp(gc[j, d])
        rhs = jnp.concatenate([v_ref[j].astype(F32) * beta[j, d], kb[j, d] * e], axis=1)
        sol = rhs + _dot((t_inv - eye).astype(BF16), rhs.astype(BF16))
        g_last = gc[j, d][tr - 1:tr, :] if d == 0 else gc[j, d][0:1, :]
        uw_ref[d, j] = sol.astype(BF16)
        aq_ref[d, j] = (a_qk[j] * gam[j, d]).astype(BF16)
        qk_ref[d, j] = jnp.concatenate([q_ref[j].astype(F32) * e,
                                        k_f[j] * jnp.exp(g_last - gc[j, d])], axis=1).astype(BF16)
        gl_ref[d, j] = jnp.broadcast_to(jnp.exp(g_last), (8, LANES))


def gdn_chunks(cfg, qn, kn, vs, bb, gcb, gct):
    tr = cfg.row_tile
    nh, t, _ = qn.shape
    hps = min(GDN_HEADS_PER_STEP, nh)
    assert nh % hps == 0
    n_tiles = t // tr
    head = pl.BlockSpec((hps, tr, 128), lambda i, h: (h, i, 0))
    head_b = pl.BlockSpec((hps, tr, 128), lambda i, h: (h + nh // hps, i, 0))
    big = jax.ShapeDtypeStruct((2, nh, t, 2 * 128), BF16)
    aq = jax.ShapeDtypeStruct((2, nh, t, tr), BF16)
    return pl.pallas_call(
        functools.partial(_gdn_chunk_kernel, n_heads=nh),
        out_shape=(big, aq, big, jax.ShapeDtypeStruct((2, nh, n_tiles * 8, LANES), F32)),
        grid=(n_tiles, nh // hps),
        in_specs=[head, head, head, head, head_b, head, head_b,
                  pl.BlockSpec((None, 2 * nh, tr), lambda i, h: (i, 0, 0))],
        out_specs=(pl.BlockSpec((2, hps, tr, 256), lambda i, h: (0, h, i, 0)),
                   pl.BlockSpec((2, hps, tr, tr), lambda i, h: (0, h, i, 0)),
                   pl.BlockSpec((2, hps, tr, 256), lambda i, h: (0, h, i, 0)),
                   pl.BlockSpec((2, hps, 8, LANES), lambda i, h: (0, h, i, 0))),
        compiler_params=_cparams(("parallel", "parallel")),
        name="gdn_chunks",
    )(qn, kn, vs, bb, bb, gcb, gcb, gct)


def _gdn_scan_kernel(uwf_ref, aqf_ref, qkf_ref, glf_ref, uwb_ref, aqb_ref, qkb_ref, glb_ref,
                     of_ref, ob_ref, s_ref, *, n_heads):
    @pl.when(pl.program_id(0) == 0)
    def _():
        s_ref[...] = jnp.zeros_like(s_ref)

    tn = (((0,), (0,)), ((), ()))
    tr = of_ref.shape[0]
    r, c = _tri_masks(2 * 128)
    diag = (r >> 7) == (c >> 7)
    cat = lambda x, y: jnp.concatenate([x, y], axis=1)
    for d, (uw_ref, aq_ref, qk_ref, gl_ref, o_ref) in enumerate(
            ((uwf_ref, aqf_ref, qkf_ref, glf_ref, of_ref), (uwb_ref, aqb_ref, qkb_ref, glb_ref, ob_ref))):
        for p in range(n_heads // 2):
            h0, h1 = 2 * p, 2 * p + 1
            st = s_ref[d, p]
            uw0, uw1, qk0, qk1 = uw_ref[h0], uw_ref[h1], qk_ref[h0], qk_ref[h1]
            lhs = jnp.concatenate([cat(uw0[:, 128:256], uw1[:, 128:256]),
                                   cat(qk0[:, 0:128], qk1[:, 0:128])], axis=0)
            prod = _dot(lhs, st.astype(BF16))
            v_new = cat(uw0[:, 0:128], uw1[:, 0:128]).astype(F32) - prod[0:tr]
            v_new_b = v_new.astype(BF16)
            o0 = prod[tr:2 * tr, 0:128] + _dot(aq_ref[h0], v_new_b[:, 0:128])
            o1 = prod[tr:2 * tr, 128:256] + _dot(aq_ref[h1], v_new_b[:, 128:256])
            o_ref[:, h0 * 128:(h1 + 1) * 128] = cat(o0, o1).astype(o_ref.dtype)
            upd = lax.dot_general(cat(qk0[:, 128:256], qk1[:, 128:256]), v_new_b, tn,
                                  preferred_element_type=F32)
            decay = cat(gl_ref[h0][0:1, :], gl_ref[h1][0:1, :])
            s_ref[d, p] = decay * st + jnp.where(diag, upd, 0.0)


def gdn_scan(cfg, uw, aq, qk, gl):
    tr = cfg.row_tile
    _, nh, t, _ = uw.shape
    n_tiles = t // tr
    last = n_tiles - 1

    def fwd(s):
        return jnp.where(s == 0, last, s - 1)

    def bwd(s):
        return jnp.where(s == 0, last, last - s)

    def specs(d, order):
        return [pl.BlockSpec((None, nh, tr, 256), lambda s: (d, 0, order(s), 0)),
                pl.BlockSpec((None, nh, tr, tr), lambda s: (d, 0, order(s), 0)),
                pl.BlockSpec((None, nh, tr, 256), lambda s: (d, 0, order(s), 0)),
                pl.BlockSpec((None, nh, 8, LANES), lambda s: (d, 0, order(s), 0))]

    out = jax.ShapeDtypeStruct((t, nh * 128), BF16)
    return pl.pallas_call(
        functools.partial(_gdn_scan_kernel, n_heads=nh),
        out_shape=(out, out),
        grid=(n_tiles,),
        in_specs=specs(0, fwd) + specs(1, bwd),
        out_specs=(pl.BlockSpec((tr, nh * 128), lambda s: (fwd(s), 0)),
                   pl.BlockSpec((tr, nh * 128), lambda s: (bwd(s), 0))),
        scratch_shapes=[pltpu.VMEM((2, nh // 2, 256, 256), F32)],
        compiler_params=_cparams(("arbitrary",)),
        name="gdn_scan",
    )(uw, aq, qk, gl, uw, aq, qk, gl)


def _head_norm_kernel(of_ref, ob_ref, gate_ref, w_ref, o_ref, *, n_heads, dv):
    for h in range(n_heads):
        sl = slice(h * dv, (h + 1) * dv)
        o = of_ref[:, sl].astype(F32) + ob_ref[:, sl].astype(F32)
        o = o * lax.rsqrt(jnp.mean(o * o, axis=-1, keepdims=True) + EPS) * w_ref[...]
        o_ref[:, sl] = (o * _silu(gate_ref[:, sl].astype(F32))).astype(o_ref.dtype)


def head_norm(o_f, o_b, z_big, gate_col_block, norm_w, n_heads, dv, tr, gate_row0=0):
    t, w = o_f.shape
    spec = pl.BlockSpec((tr, w), lambda i: (i, 0))
    rb0 = gate_row0 // tr
    return pl.pallas_call(
        functools.partial(_head_norm_kernel, n_heads=n_heads, dv=dv),
        out_shape=jax.ShapeDtypeStruct((t, w), BF16),
        grid=(t // tr,),
        in_specs=[spec, spec, pl.BlockSpec((tr, w), lambda i: (i + rb0, gate_col_block)),
                  pl.BlockSpec((1, dv), lambda i: (0, 0))],
        out_specs=spec,
        compiler_params=_cparams(("parallel",)),
        name="head_norm",
    )(o_f, o_b, z_big, norm_w.reshape(1, dv))


GLA_TILE = 128
GLA_LR_LANE0 = 32


def _gla_direction(d, qk_ref, v_ref, sm_ref, wup_ref, gb_ref, o_ref, st_ref, n_heads):
    tr = GLA_TILE
    qkw = n_heads * 128
    r, c = _tri_masks(tr)
    sh = int(math.log2(GLA_CHUNK))
    same = (r >> sh) == (c >> sh)
    mask = jnp.logical_and(same, (c <= r) if d == 0 else (c >= r))
    logit = _dot_3(sm_ref[...], wup_ref[:, d * qkw:(d + 1) * qkw]) + gb_ref[:, d * qkw:(d + 1) * qkw]
    g = jax.nn.log_sigmoid(logit) * (1.0 / GLA_GATE_NORM)
    b = _dot_sel(jnp.where(mask, 1.0, 0.0).astype(BF16), g)
    nt = (((1,), (1,)), ((), ()))
    tn = (((0,), (0,)), ((), ()))
    chunks = range(tr // GLA_CHUNK) if d == 0 else range(tr // GLA_CHUNK - 1, -1, -1)
    for h in range(n_heads):
        bq = b[:, h * 128:(h + 1) * 128]
        q = qk_ref[:, h * 128:(h + 1) * 128].astype(F32)
        k = qk_ref[:, qkw + h * 128:qkw + (h + 1) * 128].astype(F32)
        v_b = v_ref[:, h * 256:(h + 1) * 256].astype(BF16)
        qe = (q * jnp.exp(bq) * (128.0 ** -0.5)).astype(BF16)
        kinv = (k * jnp.exp(-bq)).astype(BF16)
        a = jnp.where(mask, lax.dot_general(qe, kinv, nt, preferred_element_type=F32), 0.0)
        o_intra = _dot(a.astype(BF16), v_b)
        for ci in chunks:
            lo = ci * GLA_CHUNK
            last = lo + GLA_CHUNK - 1 if d == 0 else lo
            b_last = bq[last:last + 1, :]
            kdec = (k[lo:lo + GLA_CHUNK] * jnp.exp(b_last - bq[lo:lo + GLA_CHUNK])).astype(BF16)
            ds = lax.dot_general(kdec, v_b[lo:lo + GLA_CHUNK], tn, preferred_element_type=F32)
            st = st_ref[d, h]
            o_ref[lo:lo + GLA_CHUNK, h * 256:(h + 1) * 256] = (
                o_intra[lo:lo + GLA_CHUNK] + _dot(qe[lo:lo + GLA_CHUNK], st.astype(BF16))).astype(o_ref.dtype)
            dec = jnp.transpose(jnp.broadcast_to(jnp.exp(b_last), (128, 128)))
            st_ref[d, h] = jnp.concatenate([dec, dec], axis=1) * st + ds


def _gla_kernel(qkf_ref, vf_ref, smf_ref, qkb_ref, vb_ref, smb_ref, wup_ref, gb_ref, s0_ref,
                of_ref, ob_ref, sfin_ref, st_ref, *, n_heads):
    @pl.when(pl.program_id(0) == 0)
    def _():
        st_ref[...] = s0_ref[...]

    _gla_direction(0, qkf_ref, vf_ref, smf_ref, wup_ref, gb_ref, of_ref, st_ref, n_heads)
    _gla_direction(1, qkb_ref, vb_ref, smb_ref, wup_ref, gb_ref, ob_ref, st_ref, n_heads)

    @pl.when(pl.program_id(0) == pl.num_programs(0) - 1)
    def _():
        sfin_ref[...] = st_ref[...]


def _gla_call(cfg, n_steps, arrays, spec_fn, out_rows_shape, out_spec_fn, wup, gbias, s0, name):
    nh = cfg.gla_h
    fwd = lambda s: s
    bwd = lambda s: n_steps - 1 - s
    full = lambda shape: pl.BlockSpec(shape, lambda s: (0,) * len(shape))
    z_view, zs_view = arrays
    st_shape = (2, nh, 128, 256)
    out = jax.ShapeDtypeStruct(out_rows_shape, BF16)
    return pl.pallas_call(
        functools.partial(_gla_kernel, n_heads=nh),
        out_shape=(out, out, jax.ShapeDtypeStruct(st_shape, F32)),
        grid=(n_steps,),
        in_specs=spec_fn(fwd) + spec_fn(bwd) + [full(wup.shape), full(gbias.shape), full(st_shape)],
        out_specs=(out_spec_fn(fwd), out_spec_fn(bwd), full(st_shape)),
        scratch_shapes=[pltpu.VMEM(st_shape, F32)],
        compiler_params=_cparams(("arbitrary",)),
        name=name,
    )(z_view, z_view, zs_view, z_view, z_view, zs_view, wup, gbias, s0)


def gla(cfg, z_cm, z_ctx, zs_cm, zs_ctx, wup, gbias):
    nh = cfg.gla_h
    qk2, vw = 2 * nh * 128, nh * 256
    w, rows, _ = z_cm.shape
    assert rows == GLA_TILE and cfg.ctx % GLA_TILE == 0 and qk2 == vw
    s0 = jnp.zeros((2, nh, 128, 256), F32)

    ctx_specs = lambda order: [
        pl.BlockSpec((GLA_TILE, qk2), lambda s: (order(s), 0)),
        pl.BlockSpec((GLA_TILE, vw), lambda s: (order(s), 1)),
        pl.BlockSpec((GLA_TILE, LANES), lambda s: (order(s), 0))]
    ctx_out = lambda order: pl.BlockSpec((GLA_TILE, vw), lambda s: (order(s), 0))
    ocf, ocb, s_ctx = _gla_call(cfg, cfg.ctx // GLA_TILE, (z_ctx, zs_ctx), ctx_specs,
                                (cfg.ctx, vw), ctx_out, wup, gbias, s0, "gla_ctx")

    lat_specs = lambda order: [
        pl.BlockSpec((None, GLA_TILE, qk2), lambda s: (order(s), 0, 0)),
        pl.BlockSpec((None, GLA_TILE, vw), lambda s: (order(s), 0, 1)),
        pl.BlockSpec((None, GLA_TILE, LANES), lambda s: (order(s), 0, 0))]
    lat_out = lambda order: pl.BlockSpec((None, GLA_TILE, vw), lambda s: (order(s), 0, 0))
    olf, olb, _ = _gla_call(cfg, w, (z_cm, zs_cm), lat_specs, (w, rows, vw), lat_out,
                            wup, gbias, s_ctx, "gla_lat")
    return (olf, olb), (ocf, ocb)


def column_major_perm(cfg, n_rows=16):
    w = cfg.grid_w
    p = np.zeros((n_rows * w, n_rows * w), np.float32)
    r, c = np.meshgrid(np.arange(n_rows), np.arange(w), indexing="ij")
    p[(c * n_rows + r).ravel(), (r * w + c).ravel()] = 1.0
    return jnp.asarray(p, BF16)


def _mm_nt_cm_kernel(a_ref, w_ref, p_ref, o_ref):
    nt = (((1,), (1,)), ((), ()))
    z = lax.dot_general(a_ref[...], w_ref[...], nt, preferred_element_type=F32).astype(BF16)
    o_ref[...] = _dot(p_ref[...], z).astype(o_ref.dtype).reshape(o_ref.shape)


def matmul_nt_cm(cfg, a, w_t, layer, perm, *, tn, name):
    k = a.shape[1]
    n = w_t.shape[1]
    tm = perm.shape[0]
    gw = cfg.grid_w
    rows = cfg.seq // gw
    return pl.pallas_call(
        _mm_nt_cm_kernel,
        out_shape=jax.ShapeDtypeStruct((gw, rows, n), BF16),
        grid=(n // tn, cfg.seq // tm),
        in_specs=[pl.BlockSpec((tm, k), lambda j, i: (i, 0)),
                  pl.BlockSpec((None, tn, k), lambda j, i: (layer, j, 0)),
                  pl.BlockSpec((tm, tm), lambda j, i: (0, 0))],
        out_specs=pl.BlockSpec((gw, tm // gw, tn), lambda j, i: (0, i, j)),
        compiler_params=_cparams(("parallel", "parallel")),
        name=name,
    )(a, w_t, perm)


def _head_norm_cm_kernel(of_ref, ob_ref, gate_ref, w_ref, p_ref, o_ref, *, n_heads, dv):
    tm = o_ref.shape[0]
    parts = []
    for h in range(n_heads):
        sl = slice(h * dv, (h + 1) * dv)
        o = (of_ref[:, :, sl].astype(F32) + ob_ref[:, :, sl].astype(F32)).reshape(tm, dv)
        o = o * lax.rsqrt(jnp.mean(o * o, axis=-1, keepdims=True) + EPS) * w_ref[...]
        parts.append((o * _silu(gate_ref[:, :, sl].astype(F32).reshape(tm, dv))).astype(BF16))
    nt = (((0,), (0,)), ((), ()))
    y = jnp.concatenate(parts, axis=1)
    o_ref[...] = lax.dot_general(p_ref[...], y, nt, preferred_element_type=F32).astype(o_ref.dtype)


def head_norm_cm(cfg, o_f, o_b, z_cm, gate_col_block, norm_w, n_heads, dv, perm):
    gw, rows, vw = o_f.shape
    tm = perm.shape[0]
    nr = tm // gw
    spec = pl.BlockSpec((gw, nr, vw), lambda i: (0, i, 0))
    return pl.pallas_call(
        functools.partial(_head_norm_cm_kernel, n_heads=n_heads, dv=dv),
        out_shape=jax.ShapeDtypeStruct((gw * rows, vw), BF16),
        grid=(rows // nr,),
        in_specs=[spec, spec, pl.BlockSpec((gw, nr, vw), lambda i: (0, i, gate_col_block)),
                  pl.BlockSpec((1, dv), lambda i: (0, 0)),
                  pl.BlockSpec((tm, tm), lambda i: (0, 0))],
        out_specs=pl.BlockSpec((tm, vw), lambda i: (i, 0)),
        compiler_params=_cparams(("parallel",)),
        name="head_norm_cm",
    )(o_f, o_b, z_cm, norm_w.reshape(1, dv), perm)


def _hy_prep_kernel(z_ref, zp_ref, zn_ref, w_ref, b_ref, v_ref, x1_ref, x2_ref, *, n_lat_tiles, hw):
    y = _conv3(z_ref, zp_ref, zn_ref, w_ref, n_lat_tiles) + b_ref[...]
    for p, o_ref in enumerate((v_ref, x1_ref, x2_ref)):
        o_ref[...] = y[:, p * hw:(p + 1) * hw]


def hyena_prep(cfg, z_hy, conv_w, conv_b):
    tr, hw = cfg.row_tile, cfg.hy_w
    t = z_hy.shape[0]
    main, prev, nxt = _halo_specs(tr, 3 * hw, 0, t)
    out = jax.ShapeDtypeStruct((t, hw), F32)
    ospec = pl.BlockSpec((tr, hw), lambda i: (i, 0))
    return pl.pallas_call(
        functools.partial(_hy_prep_kernel, n_lat_tiles=cfg.seq // tr, hw=hw),
        out_shape=(out, out, out),
        grid=(t // tr,),
        in_specs=[main, prev, nxt,
                  pl.BlockSpec((SHORT_CONV, 3 * hw), lambda i: (0, 0)),
                  pl.BlockSpec((1, 3 * hw), lambda i: (0, 0))],
        out_specs=(ospec, ospec, ospec),
        compiler_params=_cparams(("parallel",)),
        name="hyena_prep",
    )(z_hy, z_hy, z_hy, conv_w, conv_b.reshape(1, 3 * hw))


HY_HALF = LANES // 2


def _hy_taps_kernel(f_ref, w1_ref, b1_ref, w2_ref, b2_ref, fr_ref, w3a0_ref, w3b0_ref, w3a1_ref,
                    w3b1_ref, rate_ref, o0_ref, o1_ref, *, length, tt):
    th = tt // 2
    feats = f_ref[...]
    hid = jnp.sin(fr_ref[0:1, :] * (_dot_3(feats, w1_ref[...]) + b1_ref[...]))
    hid = jnp.sin(fr_ref[1:2, :] * (_dot_3(hid, w2_ref[...]) + b2_ref[...]))
    row = pl.program_id(0) * tt + lax.broadcasted_iota(jnp.int32, (th, 1), 0)
    for part, lane0 in ((0, 0), (1, HY_HALF)):
        n = row + part * th
        window = jnp.where(n == length, 0.0, jnp.exp(-feats[:, lane0:lane0 + 1] * rate_ref[...]))
        for w3_ref, o_ref in (((w3a0_ref, w3b0_ref)[part], o0_ref), ((w3a1_ref, w3b1_ref)[part], o1_ref)):
            o_ref[part * th:(part + 1) * th, :] = (_dot_3(hid, w3_ref[...]) * window).astype(o_ref.dtype)


def hyena_taps(feats, w1p, b1p, w2p, b2p, freqp, w3a, w3b, rates, length, hw):
    tt = min(512, length)
    n_half = length // tt
    out = jax.ShapeDtypeStruct((2 * length, hw), F32)
    ospec = pl.BlockSpec((tt, hw), lambda j: (j, 0))
    sq = pl.BlockSpec((LANES, LANES), lambda j: (0, 0))
    row = pl.BlockSpec((1, LANES), lambda j: (0, 0))
    w3spec = lambda order: pl.BlockSpec((LANES, hw), lambda j: (0, 2 * order + j // n_half))
    return pl.pallas_call(
        functools.partial(_hy_taps_kernel, length=length, tt=tt),
        out_shape=(out, out),
        grid=(2 * n_half,),
        in_specs=[pl.BlockSpec((tt // 2, LANES), lambda j: (j, 0)), sq, row, sq, row,
                  pl.BlockSpec((2, LANES), lambda j: (0, 0)),
                  w3spec(0), w3spec(0), w3spec(1), w3spec(1),
                  pl.BlockSpec((1, hw), lambda j: (0, 0))],
        out_specs=(ospec, ospec),
        compiler_params=_cparams(("parallel",)),
        name="hyena_taps",
    )(feats, w1p, b1p, w2p, b2p, freqp, w3a, w3b, w3a, w3b, rates)


FFT_BT = 8


def _taps_fft_a_kernel(f_ref, w1_ref, b1_ref, w2_ref, b2_ref, fr_ref, w3a0_ref, w3b0_ref, w3a1_ref,
                       w3b1_ref, rate_ref, l_ref, yr0_ref, yi0_ref, yr1_ref, yi1_ref):
    ph, bt, c = yr0_ref.shape
    feats = f_ref[...]
    hid = jnp.sin(fr_ref[0:1, :] * (_dot_3(feats, w1_ref[...]) + b1_ref[...]))
    hid = jnp.sin(fr_ref[1:2, :] * (_dot_3(hid, w2_ref[...]) + b2_ref[...]))
    row = lax.broadcasted_iota(jnp.int32, (feats.shape[0], 1), 0)
    is_row_l = jnp.logical_and(row == 0, pl.program_id(0) == 0)
    win_a = jnp.exp(-feats[:, 0:1] * rate_ref[...])
    win_b = jnp.where(is_row_l, 0.0, jnp.exp(-feats[:, HY_HALF:HY_HALF + 1] * rate_ref[...]))
    for w3a_ref, w3b_ref, yr_ref, yi_ref in ((w3a0_ref, w3b0_ref, yr0_ref, yi0_ref),
                                             (w3a1_ref, w3b1_ref, yr1_ref, yi1_ref)):
        taps = jnp.concatenate([_dot_3(hid, w3a_ref[...]) * win_a, _dot_3(hid, w3b_ref[...]) * win_b],
                               axis=0).astype(BF16)
        y = _dot(l_ref[...], taps)
        yr_ref[...] = y[0:ph * bt].reshape(ph, bt, c)
        yi_ref[...] = y[ph * bt:2 * ph * bt].reshape(ph, bt, c)


def taps_fft_a(feats_band, w1p, b1p, w2p, b2p, freqp, w3a, w3b, rates, lhs, ph, hw):
    nb, rows, _ = feats_band.shape
    out = jax.ShapeDtypeStruct((ph, LANES, hw), F32)
    ospec = pl.BlockSpec((ph, FFT_BT, hw), lambda j: (0, j, 0))
    sq = pl.BlockSpec((LANES, LANES), lambda j: (0, 0))
    row = pl.BlockSpec((1, LANES), lambda j: (0, 0))
    w3spec = lambda blk: pl.BlockSpec((LANES, hw), lambda j: (0, blk))
    return pl.pallas_call(
        _taps_fft_a_kernel,
        out_shape=(out, out, out, out),
        grid=(nb,),
        in_specs=[pl.BlockSpec((None, rows, LANES), lambda j: (j, 0, 0)), sq, row, sq, row,
                  pl.BlockSpec((2, LANES), lambda j: (0, 0)),
                  w3spec(0), w3spec(1), w3spec(2), w3spec(3),
                  pl.BlockSpec((1, hw), lambda j: (0, 0)),
                  pl.BlockSpec(lhs.shape, lambda j: (0, 0))],
        out_specs=(ospec, ospec, ospec, ospec),
        compiler_params=_cparams(("parallel",)),
        name="taps_fft_a",
    )(feats_band, w1p, b1p, w2p, b2p, freqp, w3a, w3b, w3a, w3b, rates, lhs)


def _fft_a_kernel(x_ref, l_ref, yr_ref, yi_ref):
    a, bt, c = x_ref.shape
    ph = yr_ref.shape[0]
    y = _dot(l_ref[...], x_ref[...].reshape(a * bt, c).astype(BF16))
    yr_ref[...] = y[0:ph * bt].reshape(ph, bt, c)
    yi_ref[...] = y[ph * bt:2 * ph * bt].reshape(ph, bt, c)


def fft_a(x, lhs, n_pages, ph):
    c = x.shape[1]
    x3 = x.reshape(x.shape[0] // LANES, LANES, c)
    out = jax.ShapeDtypeStruct((ph, LANES, c), F32)
    ospec = pl.BlockSpec((ph, FFT_BT, c), lambda j: (0, j, 0))
    return pl.pallas_call(
        _fft_a_kernel,
        out_shape=(out, out),
        grid=(LANES // FFT_BT,),
        in_specs=[pl.BlockSpec((n_pages, FFT_BT, c), lambda j: (0, j, 0)),
                  pl.BlockSpec(lhs.shape, lambda j: (0, 0))],
        out_specs=(ospec, ospec),
        compiler_params=_cparams(("parallel",)),
        name="fft_a",
    )(x3, lhs)


FFT_GM = 4


def _spec_mul_kernel(yr_ref, yi_ref, hr_ref, hi_ref, m2_ref, ma_ref, vr_ref, vi_ref):
    for g in range(FFT_GM):
        z = _dot(m2_ref[g], jnp.concatenate([yr_ref[g], yi_ref[g]], axis=0).astype(BF16))
        h = _dot(m2_ref[g], jnp.concatenate([hr_ref[g], hi_ref[g]], axis=0).astype(BF16))
        zr, zi = z[0:LANES], z[LANES:2 * LANES]
        hr, hi = h[0:LANES], h[LANES:2 * LANES]
        s = jnp.concatenate([zr * hr - zi * hi, zr * hi + zi * hr], axis=0).astype(BF16)
        v = _dot(ma_ref[g], s)
        vr_ref[g] = v[0:LANES]
        vi_ref[g] = v[LANES:2 * LANES]


def spectrum_multiply(yr, yi, hr, hi, m2, ma):
    ph, _, c = yr.shape
    out = jax.ShapeDtypeStruct((ph, LANES, c), F32)
    spec = pl.BlockSpec((FFT_GM, LANES, c), lambda j: (j, 0, 0))
    mspec = pl.BlockSpec((FFT_GM, 2 * LANES, 2 * LANES), lambda j: (j, 0, 0))
    return pl.pallas_call(
        _spec_mul_kernel,
        out_shape=(out, out),
        grid=(ph // FFT_GM,),
        in_specs=[spec, spec, spec, spec, mspec, mspec],
        out_specs=(spec, spec),
        compiler_params=_cparams(("parallel",)),
        name="spectrum_multiply",
    )(yr, yi, hr, hi, m2, ma)


def _ifft_b_kernel(vr_ref, vi_ref, l_ref, x_ref, u_ref, sk_ref, o_ref):
    ph, bt, c = vr_ref.shape
    s = jnp.concatenate([vr_ref[...].reshape(ph * bt, c), vi_ref[...].reshape(ph * bt, c)],
                        axis=0).astype(BF16)
    y = _dot(l_ref[...], s).reshape(o_ref.shape)
    o_ref[...] = x_ref[...] * (y + sk_ref[...] * u_ref[...])


def ifft_b_gate(vr, vi, lhs, gate_x, u, skip_row):
    ph, _, c = vr.shape
    a_out = lhs.shape[0] // FFT_BT
    x3 = gate_x.reshape(gate_x.shape[0] // LANES, LANES, c)
    u3 = u.reshape(u.shape[0] // LANES, LANES, c)
    vspec = pl.BlockSpec((ph, FFT_BT, c), lambda j: (0, j, 0))
    tspec = pl.BlockSpec((a_out, FFT_BT, c), lambda j: (0, j, 0))
    out = pl.pallas_call(
        _ifft_b_kernel,
        out_shape=jax.ShapeDtypeStruct((a_out, LANES, c), F32),
        grid=(LANES // FFT_BT,),
        in_specs=[vspec, vspec, pl.BlockSpec(lhs.shape, lambda j: (0, 0)), tspec, tspec,
                  pl.BlockSpec((1, 1, c), lambda j: (0, 0, 0))],
        out_specs=tspec,
        compiler_params=_cparams(("parallel",)),
        name="ifft_b",
    )(vr, vi, lhs, x3, u3, skip_row.reshape(1, 1, c))
    return out.reshape(a_out * LANES, c)


def _hy_ctx_kernel(v_ref, x1_ref, x2_ref, t0_ref, t1_ref, sk_ref, f_ref, g_ref, o_ref, *, n):
    def conv(u, taps_ref):
        us = _dot(f_ref[:, 0:n], u.astype(BF16))
        hs = _dot(f_ref[...], taps_ref[...].astype(BF16))
        ur, ui, hr, hi = us[0:2 * n], us[2 * n:4 * n], hs[0:2 * n], hs[2 * n:4 * n]
        prod = jnp.concatenate([ur * hr - ui * hi, ur * hi + ui * hr], axis=0).astype(BF16)
        return _dot(g_ref[...], prod)

    v = v_ref[...]
    y = x1_ref[...] * (conv(v, t0_ref) + sk_ref[0:1, :] * v)
    o_ref[...] = x2_ref[...] * (conv(y, t1_ref) + sk_ref[1:2, :] * y)


def hyena_ctx(cfg, vxx, taps0, taps1, skip, fmat, gmat):
    n, hw = cfg.ctx, cfg.hy_w
    cb = 256
    rb = cfg.seq // n
    part = pl.BlockSpec((n, cb), lambda j: (rb, j))
    tspec = pl.BlockSpec((2 * n, cb), lambda j: (0, j))
    return pl.pallas_call(
        functools.partial(_hy_ctx_kernel, n=n),
        out_shape=jax.ShapeDtypeStruct((n, hw), F32),
        grid=(hw // cb,),
        in_specs=[part, part, part, tspec, tspec,
                  pl.BlockSpec((2, cb), lambda j: (0, j)),
                  pl.BlockSpec((4 * n, 2 * n), lambda j: (0, 0)),
                  pl.BlockSpec((n, 4 * n), lambda j: (0, 0))],
        out_specs=pl.BlockSpec((n, cb), lambda j: (0, j)),
        compiler_params=_cparams(("parallel",)),
        name="hyena_ctx",
    )(vxx[0], vxx[1], vxx[2], taps0, taps1, skip, fmat, gmat)


class HyenaConsts(NamedTuple):
    feats: jax.Array
    rates: jax.Array
    ph: int
    la_data: jax.Array
    la_taps: jax.Array
    m2: jax.Array
    ma: jax.Array
    lb: jax.Array
    feats_ctx: jax.Array
    f_ctx: jax.Array
    g_ctx: jax.Array


def _feature_rows(length):
    n = np.arange(2 * length, dtype=np.float64)
    pos = np.where(n < length, n, 2 * length - n)
    bands = (HY_EMB - 1) // 2
    f = np.linspace(1e-4, bands - 1, bands)
    omega = (2.0 * math.pi / length) * pos
    feats = np.zeros((2 * length, HY_HALF), np.float64)
    feats[:, 0] = pos / (length - 1)
    feats[:, 1:1 + bands] = np.cos(omega[:, None] * f[None, :])
    feats[:, 1 + bands:1 + 2 * bands] = -np.sin(omega[:, None] * f[None, :])
    return feats


def _features(length):
    feats = _feature_rows(length)
    tt = min(512, length)
    tiles = feats.reshape(2 * length // tt, 2, tt // 2, HY_HALF)
    packed = np.concatenate([tiles[:, 0], tiles[:, 1]], axis=-1).reshape(length, LANES)
    return jnp.asarray(packed, F32)


def _features_bands(length):
    feats = _feature_rows(length)
    half_pages = length // LANES
    by_time = feats.reshape(2, half_pages, LANES // FFT_BT, FFT_BT, HY_HALF)
    packed = np.concatenate([by_time[0], by_time[1]], axis=-1)
    return jnp.asarray(packed.transpose(1, 0, 2, 3).reshape(LANES // FFT_BT, half_pages * FFT_BT, LANES), F32)


def hyena_consts(cfg):
    length, n, hw = cfg.seq, cfg.ctx, cfg.hy_w
    big_n = 2 * length
    p = big_n // LANES
    rates = np.abs(np.linspace(math.log(HY_TARGET) / HY_FAST_DECAY, math.log(HY_TARGET) / HY_SLOW_DECAY, hw))
    ph = p // 2 + 8
    k1 = np.arange(ph)
    kept = (k1 <= p // 2).astype(np.float64)
    a = np.arange(p)
    ang_a = 2.0 * math.pi * ((k1[:, None] * a[None, :]) % p) / p
    dft_a = np.concatenate([np.cos(ang_a), -np.sin(ang_a)], axis=0) * np.tile(kept, 2)[:, None]
    eye = np.eye(FFT_BT)
    la_taps = np.kron(dft_a, eye)
    la_data = np.kron(dft_a[:, :p // 2], eye)
    weight = kept * np.where((k1 == 0) | (k1 == p // 2), 1.0, 2.0) / big_n
    inv_a = (np.concatenate([np.cos(ang_a), -np.sin(ang_a)], axis=0) * np.tile(weight, 2)[:, None]).T
    lb = np.kron(inv_a[:p // 2], eye)
    b = np.arange(LANES)
    ang_b = 2.0 * math.pi * (((b[:, None] * b[None, :]) % LANES) / LANES)[None] \
        + 2.0 * math.pi * (k1[:, None, None] * b[None, None, :]) / big_n
    fr, fi = np.cos(ang_b), -np.sin(ang_b)
    m2 = np.concatenate([np.concatenate([fr, -fi], axis=2), np.concatenate([fi, fr], axis=2)], axis=1)
    gr, gi = fr.transpose(0, 2, 1), -fi.transpose(0, 2, 1)
    ma = np.concatenate([np.concatenate([gr, -gi], axis=2), np.concatenate([gi, gr], axis=2)], axis=1)
    kk = np.arange(2 * n)
    ac = 2.0 * math.pi * ((kk[:, None] * kk[None, :]) % (2 * n)) / (2 * n)
    f_ctx = np.concatenate([np.cos(ac), -np.sin(ac)], axis=0)
    g_ctx = np.concatenate([np.cos(ac), -np.sin(ac)], axis=1)[:n] / (2 * n)
    bf = lambda x: jnp.asarray(x, BF16)
    return HyenaConsts(_features_bands(length), jnp.asarray(rates[None, :], F32), ph, bf(la_data), bf(la_taps),
                       bf(m2), bf(ma), bf(lb), _features(n), bf(f_ctx), bf(g_ctx))


def hyena(cfg, hc, vxx, filt_w, skip, with_ctx=True):
    length, n, hw = cfg.seq, cfg.ctx, cfg.hy_w
    half = length // LANES
    y_lat = vxx[0]
    taps_y = taps_fft_a(hc.feats, *filt_w, hc.rates, hc.la_taps, hc.ph, hw)
    for order in range(2):
        yr, yi = fft_a(y_lat, hc.la_data, half, hc.ph)
        vr, vi = spectrum_multiply(yr, yi, taps_y[2 * order], taps_y[2 * order + 1], hc.m2, hc.ma)
        y_lat = ifft_b_gate(vr, vi, hc.lb, vxx[1 + order], y_lat, skip[order:order + 1])
    if not with_ctx:
        return y_lat, jnp.zeros((n, hw), F32)
    taps_c = hyena_taps(hc.feats_ctx, *filt_w, hc.rates, n, hw)
    y_ctx = hyena_ctx(cfg, vxx, taps_c[0], taps_c[1], skip, hc.f_ctx, hc.g_ctx)
    return y_lat, y_ctx


def expert_expand(n=N_EXPERTS):
    m = np.zeros((LANES, n * LANES), np.float32)
    for e in range(n):
        m[e, e * LANES:(e + 1) * LANES] = 1.0
    return jnp.asarray(m, BF16)


def prep_gla_gate(gate_up, gate_b):
    l, _, r, qk = gate_up.shape
    w = jnp.zeros((l, LANES, 2 * qk), F32)
    for z in range(2):
        w = w.at[:, GLA_LR_LANE0 + z * r:GLA_LR_LANE0 + (z + 1) * r, z * qk:(z + 1) * qk].set(gate_up[:, z])
    return w, gate_b.reshape(l, 1, 2 * qk)


def prep_hyena_filter(w1, b1, w2, b2, w3, freq):
    e, hdim = w1.shape
    assert hdim == HY_HALF
    h = HY_HALF
    w1p = jnp.zeros((LANES, LANES), F32).at[:e, :h].set(w1).at[h:h + e, h:].set(w1)
    w2p = jnp.zeros((LANES, LANES), F32).at[:h, :h].set(w2).at[h:, h:].set(w2)
    zero = jnp.zeros_like(w3)
    w3a = jnp.concatenate([w3, zero], axis=0)
    w3b = jnp.concatenate([zero, w3], axis=0)
    twice = lambda v: jnp.concatenate([v, v], axis=-1).reshape(-1, LANES)
    return w1p, twice(b1), w2p, twice(b2), twice(freq), w3a, w3b


def prep_w2(w2):
    l, e, f, d = w2.shape
    return w2.reshape(l, e * f, d).astype(BF16)


def kernel(x, c, ctx, c_ctx, norm1_g, norm2_g, w_mod, b_mod, w_in, gdn_conv, gdn_a_log, gdn_dt_bias, gdn_norm, gla_gate_up, gla_gate_b, gla_norm, hy_conv_w, hy_conv_b, hy_w1, hy_b1, hy_w2, hy_b2, hy_w3, hy_freq, hy_skip, merge_up, merge_b, w_branch, w_out, w_router, router_bias, moe_w1, moe_w3, moe_w2, final_g):
    cfg = Cfg(d=4096, seq=8192, ctx=256, grid_w=64, gdn_h=8, gla_h=4, hy_w=1024, merge_rank=256,
              d_expert=256, row_tile=256, mm_tm=768)
    return forward(cfg, x, c, ctx, c_ctx, norm1_g, norm2_g, w_mod, b_mod, w_in, gdn_conv, gdn_a_log,
                   gdn_dt_bias, gdn_norm, gla_gate_up, gla_gate_b, gla_norm, hy_conv_w, hy_conv_b, hy_w1,
                   hy_b1, hy_w2, hy_b2, hy_w3, hy_freq, hy_skip, merge_up, merge_b, w_branch, w_out,
                   w_router, router_bias, moe_w1, moe_w3, moe_w2, final_g)


def split_w_in(cfg, w_in):
    gw4 = 4 * cfg.gdn_w
    gdn_small = 4 * cfg.gdn_h
    gla0 = gw4 + gdn_small
    gla_w = 2 * cfg.gla_qk + 2 * cfg.gla_v
    hy0 = gla0 + gla_w + 32
    hy_w = 3 * cfg.hy_w
    w_t = jnp.swapaxes(w_in, 1, 2)
    l, _, k = w_t.shape
    zeros = lambda n: jnp.zeros((l, n, k), w_t.dtype)
    small = jnp.concatenate([w_t[:, hy0 + hy_w:hy0 + hy_w + cfg.merge_rank],
                             w_t[:, gw4:gla0], zeros(GLA_LR_LANE0 - gdn_small),
                             w_t[:, gla0 + gla_w:hy0], zeros(LANES - GLA_LR_LANE0 - 32)], axis=1)
    return cast_rows(w_t, 0, gw4), cast_rows(w_t, gla0, gla_w), cast_rows(w_t, hy0, hy_w), small


def forward(cfg, x, c, ctx, c_ctx, norm1_g, norm2_g, w_mod, b_mod, w_in, gdn_conv, gdn_a_log, gdn_dt_bias,
            gdn_norm, gla_gate_up, gla_gate_b, gla_norm, hy_conv_w, hy_conv_b, hy_w1, hy_b1, hy_w2, hy_b2,
            hy_w3, hy_freq, hy_skip, merge_up, merge_b, w_branch, w_out, w_router, router_bias, moe_w1,
            moe_w3, moe_w2, final_g):
    d, tr, tm = cfg.d, cfg.row_tile, cfg.mm_tm
    depth = w_in.shape[0]
    nh = cfg.gdn_h
    tn = min(1024, d)
    w_gdn, w_gla, w_hy, w_small = split_w_in(cfg, w_in)
    w1b, w3b = moe_w1.astype(BF16), moe_w3.astype(BF16)
    w2f = prep_w2(moe_w2)
    wb, mu, wo = w_branch.astype(BF16), merge_up.astype(BF16), w_out.astype(BF16)
    mb = merge_b.reshape(depth, 3, 1, d)
    wr_pad = jnp.zeros((d, LANES), F32).at[:, :N_EXPERTS].set(w_router)
    rb_col = router_bias.reshape(N_EXPERTS, 1)
    expand = expert_expand()
    wup, gbias = prep_gla_gate(gla_gate_up, gla_gate_b)
    lane0 = 2 * nh
    alog_rows = jnp.zeros((depth, 1, LANES), F32).at[:, 0, lane0:2 * lane0].set(gdn_a_log.reshape(depth, -1))
    dtb_rows = jnp.zeros((depth, 1, LANES), F32).at[:, 0, lane0:2 * lane0].set(gdn_dt_bias.reshape(depth, -1))
    hc = hyena_consts(cfg)
    perm = column_major_perm(cfg)
    gla_gate_blk = (2 * cfg.gla_qk + cfg.gla_v) // cfg.gla_v
    z_proj = functools.partial(matmul_nt, tm=tm, tn=tn, out_dtype=BF16)

    lat = jnp.concatenate([x[0], ctx[0]], axis=0)
    cvec = jnp.zeros((8, d), F32).at[0].set(c[0]).at[1].set(c_ctx)
    mods = modvec(cvec, w_mod, b_mod)

    for l in range(depth):
        with_ctx = l < depth - 1
        mod = mods[l]
        h = norm1(cfg, lat, norm1_g[l], mod)
        z_gdn = z_proj(h, w_gdn, l, name="w_in_gdn")
        z_gla = matmul_nt_cm(cfg, h, w_gla, l, perm, tn=tn, name="w_in_gla")
        z_gla_ctx = matmul_nt(h, w_gla, l, tm=cfg.ctx, tn=tn, out_dtype=BF16, name="w_in_gla_ctx",
                              row0=cfg.seq, n_rows=cfg.ctx)
        z_hy = z_proj(h, w_hy, l, name="w_in_hy")
        z_small = matmul_nt(h, w_small, l, tm=tm, tn=w_small.shape[1], name="w_in_small")
        qn, kn, vs, bb, gcb, gct = gdn_prep(cfg, z_gdn, z_small, gdn_conv[l], alog_rows[l], dtb_rows[l])
        o_f, o_b = gdn_scan(cfg, *gdn_chunks(cfg, qn, kn, vs, bb, gcb, gct))
        a_all = head_norm(o_f, o_b, z_gdn, 3, gdn_norm[l], nh, 128, tr)
        lr = z_small[:, cfg.merge_rank:cfg.merge_rank + LANES]
        lr_cm = jnp.swapaxes(lr[:cfg.seq].reshape(cfg.seq // cfg.grid_w, cfg.grid_w, LANES), 0, 1)
        (olf, olb), (ocf, ocb) = gla(cfg, z_gla, z_gla_ctx, lr_cm, lr[cfg.seq:], wup[l], gbias[l])
        b_lat = head_norm_cm(cfg, olf, olb, z_gla, gla_gate_blk, gla_norm[l], cfg.gla_h, 256, perm)
        b_ctx = head_norm(ocf, ocb, z_gla_ctx, gla_gate_blk, gla_norm[l], cfg.gla_h, 256, tr)
        b_all = jnp.concatenate([b_lat, b_ctx], axis=0)
        vxx = hyena_prep(cfg, z_hy, hy_conv_w[l], hy_conv_b[l])
        filt_w = prep_hyena_filter(hy_w1[l], hy_b1[l], hy_w2[l], hy_b2[l], hy_w3[l], hy_freq[l])
        c_lat, c_ctx_out = hyena(cfg, hc, vxx, filt_w, hy_skip[l], with_ctx)
        c_all = jnp.concatenate([c_lat, c_ctx_out], axis=0).astype(BF16)
        s = merge_branches(cfg, (a_all, b_all, c_all), z_small, wb, mu, mb, l, tn=tn)
        lat = matmul_resid(cfg, s, wo, l, lat, mod, 2, tn=tn, name="w_out")
        h2, gate_rep = norm2_route(cfg, lat, norm2_g[l], mod, wr_pad, rb_col, expand)
        act = matmul_moe_act(cfg, h2, w1b, w3b, l, gate_rep)
        lat = matmul_resid(cfg, act, w2f, l, lat, mod, 5, tn=tn, name="moe_down")
    return final_norm(lat, final_g, cfg.seq, tr)[None]
```
